```python
import math
import jax, jax.numpy as jnp
from jax import lax
import numpy as np

D_MODEL = 4096
BATCH = 4
SEQ = 4096
DEPTH = 1

MEM_LEN = 256
D_MIX = D_MODEL
D_CONV = D_MIX // 2
CONV_WIDTH = 31
N_HEADS = 16
HEAD_DIM = (D_MIX - D_CONV) // N_HEADS
D_ATTN = N_HEADS * HEAD_DIM
KV_RANK = 512
H_IDX = 32
D_IDX = 64
TOPK_MAX = 256
Q_BLOCK = 128
N_BUCKETS = 32
MAX_DIST = 128
MEM_HEADS = 4
MEM_HEAD_DIM = 128
D_MEM = MEM_HEADS * MEM_HEAD_DIM
N_GROUPS = 8
EXP_PER_GROUP = 8
N_EXPERTS = N_GROUPS * EXP_PER_GROUP
TOP_K_EXP = 2
D_FF = 512
MOE_BLOCK = 128
ALPHA = (2.0 * DEPTH) ** 0.25
BETA = (8.0 * DEPTH) ** -0.25
LN_EPS = 1e-5

C_GLU = 2 * D_CONV
C_Q = D_ATTN
C_KV = KV_RANK
C_QI = H_IDX * D_IDX
C_KI = D_IDX
C_WI = H_IDX
D_IN = C_GLU + C_Q + C_KV + C_QI + C_KI + C_WI
SPLITS = [C_GLU, C_GLU + C_Q, C_GLU + C_Q + C_KV, C_GLU + C_Q + C_KV + C_QI, C_GLU + C_Q + C_KV + C_QI + C_KI]

kernel_name = "hybrid_conv_dsa_hmoe_block"


def layer_norm(x, g, b):
    xf = x.astype(jnp.float32)
    mu = jnp.mean(xf, axis=-1, keepdims=True)
    var = jnp.mean(jnp.square(xf - mu), axis=-1, keepdims=True)
    return ((xf - mu) * lax.rsqrt(var + LN_EPS)).astype(x.dtype) * g + b


def rms_norm(x, g):
    xf = x.astype(jnp.float32)
    return (xf * lax.rsqrt(jnp.mean(xf * xf, axis=-1, keepdims=True) + LN_EPS)).astype(x.dtype) * g


def t5_bucket(dist):
    n = jnp.maximum(dist, 0)
    max_exact = N_BUCKETS // 2
    nf = jnp.maximum(n, 1).astype(jnp.float32)
    large = max_exact + (jnp.log(nf / max_exact) / math.log(MAX_DIST / max_exact)
                         * (N_BUCKETS - max_exact)).astype(jnp.int32)
    large = jnp.minimum(large, N_BUCKETS - 1)
    return jnp.where(n < max_exact, n, large)


def conformer_conv(u, conv_w, conv_b, ln_g, ln_b):
    a, gate = jnp.split(u, 2, axis=-1)
    h = a * jax.nn.sigmoid(gate)
    h = lax.conv_general_dilated(h, conv_w[:, None, :], window_strides=(1,),
                                 padding=[(CONV_WIDTH - 1, 0)],
                                 dimension_numbers=("NWC", "WIO", "NWC"),
                                 feature_group_count=D_CONV) + conv_b
    return jax.nn.silu(layer_norm(h, ln_g, ln_b))


def dsa_attention(q, ckv, q_idx, k_idx, w_idx, w_uk, w_uv, rel_bias):
    B, S = q.shape[0], q.shape[1]
    topk = min(TOPK_MAX, S // 4)
    n_blocks = S // Q_BLOCK
    key_pos = jnp.arange(S, dtype=jnp.int32)
    scale = HEAD_DIM ** -0.5

    def block(qb):
        start = qb * Q_BLOCK
        qpos = start + jnp.arange(Q_BLOCK, dtype=jnp.int32)
        qi = lax.dynamic_slice_in_dim(q_idx, start, Q_BLOCK, axis=1)
        wi = lax.dynamic_slice_in_dim(w_idx, start, Q_BLOCK, axis=1)
        qh = lax.dynamic_slice_in_dim(q, start, Q_BLOCK, axis=1)
        score = jax.nn.relu(jnp.einsum("bqhd,bsd->bqhs", qi, k_idx))
        score = jnp.einsum("bqh,bqhs->bqs", wi, score).astype(jnp.float32)
        causal = key_pos[None, :] <= qpos[:, None]
        score = jnp.where(causal[None], score, -jnp.inf)
        _, sel = lax.top_k(score, topk)
        ckv_sel = jax.vmap(lambda c, i: c[i])(ckv, sel)
        q_lat = jnp.einsum("bqhd,hrd->bqhr", qh, w_uk)
        logits = jnp.einsum("bqhr,bqkr->bqhk", q_lat, ckv_sel).astype(jnp.float32) * scale
        dist = qpos[None, :, None] - sel
        bias = rel_bias[t5_bucket(dist)]
        logits = logits + jnp.swapaxes(bias, -1, -2).astype(jnp.float32)
        logits = jnp.where((dist >= 0)[:, :, None, :], logits, -jnp.inf)
        p = jax.nn.softmax(logits, axis=-1).astype(ckv.dtype)
        o_lat = jnp.einsum("bqhk,bqkr->bqhr", p, ckv_sel)
        o = jnp.einsum("bqhr,hrd->bqhd", o_lat, w_uv)
        return o.reshape(B, Q_BLOCK, D_ATTN)

    out = lax.map(block, jnp.arange(n_blocks, dtype=jnp.int32))
    return jnp.moveaxis(out, 0, 1).reshape(B, S, D_ATTN)


def hybrid_mixer(h, w_in, conv_w, conv_b, conv_ln_g, conv_ln_b, kv_norm_g, w_uk, w_uv,
                 rel_bias, conv_out_g, attn_out_g, w_out):
    B, S, _ = h.shape
    proj = h @ w_in
    u_glu, q, ckv, qi, ki, wi = jnp.split(proj, SPLITS, axis=-1)
    conv_out = conformer_conv(u_glu, conv_w, conv_b, conv_ln_g, conv_ln_b)
    attn_out = dsa_attention(q.reshape(B, S, N_HEADS, HEAD_DIM), rms_norm(ckv, kv_norm_g),
                             qi.reshape(B, S, H_IDX, D_IDX), ki, wi, w_uk, w_uv, rel_bias)
    mix = jnp.concatenate([rms_norm(conv_out, conv_out_g), rms_norm(attn_out, attn_out_g)], axis=-1)
    return mix @ w_out


def memory_attention(h, mem, w_mq, w_mk, w_mv, w_mo):
    B, S, _ = h.shape
    M = mem.shape[1]
    q = (h @ w_mq).reshape(B, S, MEM_HEADS, MEM_HEAD_DIM)
    k = (mem @ w_mk).reshape(B, M, MEM_HEADS, MEM_HEAD_DIM)
    v = (mem @ w_mv).reshape(B, M, MEM_HEADS, MEM_HEAD_DIM)
    logits = jnp.einsum("bqhd,bmhd->bhqm", q, k).astype(jnp.float32) * MEM_HEAD_DIM ** -0.5
    p = jax.nn.softmax(logits, axis=-1).astype(h.dtype)
    o = jnp.einsum("bhqm,bmhd->bqhd", p, v).reshape(B, S, D_MEM)
    return o @ w_mo


def hierarchical_moe(h, w_router_grp, w_router_exp, w_gate, w_up, w_down):
    B, S, D = h.shape
    T = B * S
    xf = h.reshape(T, D)
    grp_prob = jax.nn.softmax((xf @ w_router_grp).astype(jnp.float32), axis=-1)
    g_p, g_sel = lax.top_k(grp_prob, 1)
    exp_logits = (xf @ w_router_exp).astype(jnp.float32).reshape(T, N_GROUPS, EXP_PER_GROUP)
    exp_logits = jnp.take_along_axis(exp_logits, g_sel[:, :, None], axis=1)[:, 0]
    e_p, e_sel = lax.top_k(jax.nn.softmax(exp_logits, axis=-1), TOP_K_EXP)
    gate = g_p * (e_p / jnp.sum(e_p, axis=-1, keepdims=True))
    expert = g_sel * EXP_PER_GROUP + e_sel
    A = T * TOP_K_EXP
    e_flat = expert.reshape(A)
    tok_flat = jnp.repeat(jnp.arange(T, dtype=jnp.int32), TOP_K_EXP)
    gate_flat = gate.reshape(A)
    order = jnp.argsort(e_flat)
    e_sorted = e_flat[order]
    counts = jnp.bincount(e_flat, length=N_EXPERTS)
    padded = (counts + MOE_BLOCK - 1) // MOE_BLOCK * MOE_BLOCK
    seg_start = jnp.cumsum(counts) - counts
    pad_end = jnp.cumsum(padded)
    pad_start = pad_end - padded
    dest = pad_start[e_sorted] + (jnp.arange(A, dtype=jnp.int32) - seg_start[e_sorted])
    n_blocks = (A + N_EXPERTS * (MOE_BLOCK - 1) + MOE_BLOCK - 1) // MOE_BLOCK
    P = n_blocks * MOE_BLOCK
    tok_pad = jnp.zeros((P,), jnp.int32).at[dest].set(tok_flat[order])
    gate_pad = jnp.zeros((P,), jnp.float32).at[dest].set(gate_flat[order])
    block_pos = jnp.arange(n_blocks, dtype=jnp.int32) * MOE_BLOCK
    block_expert = jnp.minimum(jnp.searchsorted(pad_end, block_pos, side="right"), N_EXPERTS - 1)

    def run_block(args):
        e, toks = args
        xb = xf[toks]
        hb = jax.nn.silu(xb @ w_gate[e]) * (xb @ w_up[e])
        return hb @ w_down[e]

    y_pad = lax.map(run_block, (block_expert, tok_pad.reshape(n_blocks, MOE_BLOCK)))
    y_pad = y_pad.reshape(P, D) * gate_pad[:, None].astype(h.dtype)
    y = jax.ops.segment_sum(y_pad, tok_pad, num_segments=T)
    return y.reshape(B, S, D)


def setup_inputs(seed: int = 0) -> dict:
    key = jax.random.key(seed)
    ks = jax.random.split(key, 32)
    n = jax.random.normal
    f32 = jnp.float32

    def gain(k, shape):
        return 1.0 + 0.1 * n(k, shape, f32)

    def small(k, shape):
        return 0.01 * n(k, shape, f32)

    return {
        "x": n(ks[0], (BATCH, SEQ, D_MODEL), f32),
        "mem": n(ks[1], (BATCH, MEM_LEN, D_MODEL), f32),
        "w_in": n(ks[2], (DEPTH, D_MODEL, D_IN), f32) * D_MODEL ** -0.5,
        "conv_w": n(ks[3], (DEPTH, CONV_WIDTH, D_CONV), f32) * CONV_WIDTH ** -0.5,
        "conv_b": small(ks[4], (DEPTH, D_CONV)),
        "conv_ln_g": gain(ks[5], (DEPTH, D_CONV)),
        "conv_ln_b": small(ks[6], (DEPTH, D_CONV)),
        "kv_norm_g": gain(ks[7], (DEPTH, KV_RANK)),
        "w_uk": n(ks[8], (DEPTH, N_HEADS, KV_RANK, HEAD_DIM), f32) * KV_RANK ** -0.5,
        "w_uv": n(ks[9], (DEPTH, N_HEADS, KV_RANK, HEAD_DIM), f32) * (KV_RANK ** -0.5 * BETA),
        "rel_bias": 0.2 * n(ks[10], (N_BUCKETS, N_HEADS), f32),
        "conv_out_g": gain(ks[11], (DEPTH, D_CONV)),
        "attn_out_g": gain(ks[12], (DEPTH, D_ATTN)),
        "w_out": n(ks[13], (DEPTH, D_MIX, D_MODEL), f32) * (D_MIX ** -0.5 * BETA),
        "ln1_g": gain(ks[14], (DEPTH, D_MODEL)),
        "ln1_b": small(ks[15], (DEPTH, D_MODEL)),
        "w_mq": n(ks[16], (DEPTH, D_MODEL, D_MEM), f32) * D_MODEL ** -0.5,
        "w_mk": n(ks[17], (DEPTH, D_MODEL, D_MEM), f32) * D_MODEL ** -0.5,
        "w_mv": n(ks[18], (DEPTH, D_MODEL, D_MEM), f32) * (D_MODEL ** -0.5 * BETA),
        "w_mo": n(ks[19], (DEPTH, D_MEM, D_MODEL), f32) * (D_MEM ** -0.5 * BETA),
        "ln2_g": gain(ks[20], (DEPTH, D_MODEL)),
        "ln2_b": small(ks[21], (DEPTH, D_MODEL)),
        "w_router_grp": n(ks[22], (DEPTH, D_MODEL, N_GROUPS), f32) * D_MODEL ** -0.5,
        "w_router_exp": n(ks[23], (DEPTH, D_MODEL, N_EXPERTS), f32) * D_MODEL ** -0.5,
        "w_gate": n(ks[24], (DEPTH, N_EXPERTS, D_MODEL, D_FF), f32) * D_MODEL ** -0.5,
        "w_up": n(ks[25], (DEPTH, N_EXPERTS, D_MODEL, D_FF), f32) * D_MODEL ** -0.5,
        "w_down": n(ks[26], (DEPTH, N_EXPERTS, D_FF, D_MODEL), f32) * (D_FF ** -0.5 * BETA),
        "ln3_g": gain(ks[27], (DEPTH, D_MODEL)),
        "ln3_b": small(ks[28], (DEPTH, D_MODEL)),
    }


def reference(x, mem, w_in, conv_w, conv_b, conv_ln_g, conv_ln_b, kv_norm_g, w_uk, w_uv,
              rel_bias, conv_out_g, attn_out_g, w_out, ln1_g, ln1_b, w_mq, w_mk, w_mv, w_mo,
              ln2_g, ln2_b, w_router_grp, w_router_exp, w_gate, w_up, w_down, ln3_g, ln3_b):
    for l in range(DEPTH):
        mix = hybrid_mixer(x, w_in[l], conv_w[l], conv_b[l], conv_ln_g[l], conv_ln_b[l],
                           kv_norm_g[l], w_uk[l], w_uv[l], rel_bias, conv_out_g[l],
                           attn_out_g[l], w_out[l])
        x = layer_norm(ALPHA * x + mix, ln1_g[l], ln1_b[l])
        x = layer_norm(ALPHA * x + memory_attention(x, mem, w_mq[l], w_mk[l], w_mv[l], w_mo[l]),
                       ln2_g[l], ln2_b[l])
        x = layer_norm(ALPHA * x + hierarchical_moe(x, w_router_grp[l], w_router_exp[l],
                                                    w_gate[l], w_up[l], w_down[l]),
                       ln3_g[l], ln3_b[l])
    return x
```

```python
import functools
import math

import jax
import jax.numpy as jnp
from jax import lax
from jax.experimental import pallas as pl
from jax.experimental.pallas import tpu as pltpu

F32 = jnp.float32
BF16 = jnp.bfloat16
I32 = jnp.int32

DEPTH = 1
CONV_WIDTH = 31
N_HEADS = 16
HEAD_DIM = 128
KV_RANK = 512
H_IDX = 32
D_IDX = 64
TOPK_MAX = 256
N_BUCKETS = 32
MAX_DIST = 128
MEM_HEADS = 4
MEM_HEAD_DIM = 128
N_GROUPS = 8
EXP_PER_GROUP = 8
N_EXPERTS = N_GROUPS * EXP_PER_GROUP
MOE_BLOCK = 128
ALPHA = (2.0 * DEPTH) ** 0.25
LN_EPS = 1e-5

LANES = 128
V7X_VMEM_BYTES = 64 * 1024 * 1024
VMEM_LIMIT = 56 * 1024 * 1024
NEG = -1e30
INT_MIN = -(2 ** 31)

ATT_BLOCK = 256
CONV_TS = 256
CONV_HALO = 32
CONV_CC = 256
CONV_RC = 32


def _cparams(sem):
    return pltpu.CompilerParams(dimension_semantics=sem, vmem_limit_bytes=VMEM_LIMIT)


def _dot(a, b):
    return jnp.dot(a, b, preferred_element_type=F32)


def _dot_nt(a, b):
    return lax.dot_general(a, b, (((1,), (1,)), ((), ())), preferred_element_type=F32)


def _mm_kernel(a_ref, b_ref, o_ref):
    o_ref[...] = _dot(a_ref[...], b_ref[...]).astype(o_ref.dtype)


def _matmul(a, b, out_dtype, tm, tn, name):
    m, k = a.shape
    n = b.shape[1]
    return pl.pallas_call(
        _mm_kernel,
        grid=(m // tm, n // tn),
        in_specs=[pl.BlockSpec((tm, k), lambda i, j: (i, 0)),
                  pl.BlockSpec((k, tn), lambda i, j: (0, j))],
        out_specs=pl.BlockSpec((tm, tn), lambda i, j: (i, j)),
        out_shape=jax.ShapeDtypeStruct((m, n), out_dtype),
        compiler_params=_cparams(("parallel", "parallel")),
        name=name,
    )(a, b)


def _mm_heads_kernel(a_ref, b_ref, o_ref):
    r = _dot(a_ref[...], b_ref[...])
    for p in range(o_ref.shape[0]):
        o_ref[p] = r[:, p * LANES:(p + 1) * LANES].astype(o_ref.dtype)


def _matmul_heads(a, b, out_dtype, tm, tn, name):
    m, k = a.shape
    n = b.shape[1]
    return pl.pallas_call(
        _mm_heads_kernel,
        grid=(m // tm, n // tn),
        in_specs=[pl.BlockSpec((tm, k), lambda i, j: (i, 0)),
                  pl.BlockSpec((k, tn), lambda i, j: (0, j))],
        out_specs=pl.BlockSpec((tn // LANES, tm, LANES), lambda i, j: (j, i, 0)),
        out_shape=jax.ShapeDtypeStruct((n // LANES, m, LANES), out_dtype),
        compiler_params=_cparams(("parallel", "parallel")),
        name=name,
    )(a, b)


def _mm2_res_kernel(a1_ref, a2_ref, w1_ref, w2_ref, r_ref, o_ref):
    o_ref[...] = (ALPHA * r_ref[...] + _dot(a1_ref[...], w1_ref[...])
                  + _dot(a2_ref[...], w2_ref[...]))


def _matmul2_residual(a1, a2, w1, w2, res, tm, tn, name):
    m, k1 = a1.shape
    k2 = a2.shape[1]
    n = w1.shape[1]
    return pl.pallas_call(
        _mm2_res_kernel,
        grid=(m // tm, n // tn),
        in_specs=[pl.BlockSpec((tm, k1), lambda i, j: (i, 0)),
                  pl.BlockSpec((tm, k2), lambda i, j: (i, 0)),
                  pl.BlockSpec((k1, tn), lambda i, j: (0, j)),
                  pl.BlockSpec((k2, tn), lambda i, j: (0, j)),
                  pl.BlockSpec((tm, tn), lambda i, j: (i, j))],
        out_specs=pl.BlockSpec((tm, tn), lambda i, j: (i, j)),
        out_shape=jax.ShapeDtypeStruct((m, n), F32),
        compiler_params=_cparams(("parallel", "parallel")),
        name=name,
    )(a1, a2, w1, w2, res)


def _ln_rows(x, g, b):
    mu = jnp.mean(x, axis=-1, keepdims=True)
    xc = x - mu
    var = jnp.mean(xc * xc, axis=-1, keepdims=True)
    return xc * lax.rsqrt(var + LN_EPS) * g + b


def _ln_kernel(x_ref, g_ref, b_ref, o_ref):
    o_ref[...] = _ln_rows(x_ref[...], g_ref[...], b_ref[...])


def _layer_norm(x, g, b, tm, name):
    m, d = x.shape
    return pl.pallas_call(
        _ln_kernel,
        grid=(m // tm,),
        in_specs=[pl.BlockSpec((tm, d), lambda i: (i, 0)),
                  pl.BlockSpec((1, d), lambda i: (0, 0)),
                  pl.BlockSpec((1, d), lambda i: (0, 0))],
        out_specs=pl.BlockSpec((tm, d), lambda i: (i, 0)),
        out_shape=jax.ShapeDtypeStruct((m, d), F32),
        compiler_params=_cparams(("parallel",)),
        name=name,
    )(x, g.reshape(1, d), b.reshape(1, d))


def _conv_kernel(a_ref, g_ref, cw_ref, cb_ref, lg_ref, lb_ref, og_ref, o_ref, hbuf, ybuf):
    ts = a_ref.shape[0]
    nch = hbuf.shape[0]
    cc = hbuf.shape[2]
    d_conv = nch * cc

    @pl.when(pl.program_id(1) == 0)
    def _():
        hbuf[:, 0:CONV_HALO, :] = jnp.zeros((nch, CONV_HALO, cc), F32)

    for c in range(nch):
        a = a_ref[:, c * cc:(c + 1) * cc].astype(F32)
        g = g_ref[:, c * cc:(c + 1) * cc].astype(F32)
        hbuf[c, CONV_HALO:CONV_HALO + ts, :] = a * jax.nn.sigmoid(g)

    first = CONV_HALO - (CONV_WIDTH - 1)

    def chunk_body(c, carry):
        for r0 in range(0, ts, CONV_RC):
            acc = jnp.zeros((CONV_RC, cc), F32)
            for j in range(CONV_WIDTH):
                acc = acc + cw_ref[c, j:j + 1, :] * hbuf[c, r0 + first + j:r0 + first + j + CONV_RC, :]
            ybuf[c, r0:r0 + CONV_RC, :] = acc + cb_ref[c]
        hbuf[c, 0:CONV_HALO, :] = hbuf[c, ts:ts + CONV_HALO, :]
        return carry

    lax.fori_loop(0, nch, chunk_body, 0)

    s1 = jnp.zeros((ts, 1), F32)
    for c in range(nch):
        s1 = s1 + jnp.sum(ybuf[c], axis=1, keepdims=True)
    mu = s1 * (1.0 / d_conv)
    s2 = jnp.zeros((ts, 1), F32)
    for c in range(nch):
        yc = ybuf[c] - mu
        s2 = s2 + jnp.sum(yc * yc, axis=1, keepdims=True)
    rstd = lax.rsqrt(s2 * (1.0 / d_conv) + LN_EPS)
    s3 = jnp.zeros((ts, 1), F32)
    for c in range(nch):
        z = (ybuf[c] - mu) * rstd * lg_ref[:, c * cc:(c + 1) * cc] + lb_ref[:, c * cc:(c + 1) * cc]
        z = z * jax.nn.sigmoid(z)
        ybuf[c] = z
        s3 = s3 + jnp.sum(z * z, axis=1, keepdims=True)
    rr = lax.rsqrt(s3 * (1.0 / d_conv) + LN_EPS)
    for c in range(nch):
        o_ref[:, c * cc:(c + 1) * cc] = (ybuf[c] * rr * og_ref[:, c * cc:(c + 1) * cc]).astype(o_ref.dtype)


def _conformer_conv(u, bsz, seq, conv_w, conv_b, ln_g, ln_b, out_g):
    d_conv = u.shape[1] // 2
    ts = min(CONV_TS, seq)
    nch = d_conv // CONV_CC
    nt = seq // ts
    cw = conv_w.reshape(CONV_WIDTH, nch, CONV_CC).transpose(1, 0, 2)
    cb = conv_b.reshape(nch, 1, CONV_CC)
    vec = pl.BlockSpec((1, d_conv), lambda b, i: (0, 0))
    return pl.pallas_call(
        _conv_kernel,
        grid=(bsz, nt),
        in_specs=[pl.BlockSpec((ts, d_conv), lambda b, i: (b * nt + i, 0)),
                  pl.BlockSpec((ts, d_conv), lambda b, i: (b * nt + i, 1)),
                  pl.BlockSpec((nch, CONV_WIDTH, CONV_CC), lambda b, i: (0, 0, 0)),
                  pl.BlockSpec((nch, 1, CONV_CC), lambda b, i: (0, 0, 0)),
                  vec, vec, vec],
        out_specs=pl.BlockSpec((ts, d_conv), lambda b, i: (b * nt + i, 0)),
        out_shape=jax.ShapeDtypeStruct((bsz * seq, d_conv), BF16),
        scratch_shapes=[pltpu.VMEM((nch, CONV_HALO + ts, CONV_CC), F32),
                        pltpu.VMEM((nch, ts, CONV_CC), F32)],
        compiler_params=_cparams(("arbitrary", "arbitrary")),
        name="conformer_conv",
    )(u, u, cw, cb, ln_g.reshape(1, d_conv), ln_b.reshape(1, d_conv), out_g.reshape(1, d_conv))


def _prep_kernel(t_ref, g_ref, ckv_ref, kia_ref, kib_ref, kw_ref):
    ckv = t_ref[:, 0:KV_RANK]
    ms = jnp.mean(ckv * ckv, axis=-1, keepdims=True)
    ckv_ref[...] = (ckv * lax.rsqrt(ms + LN_EPS) * g_ref[...]).astype(ckv_ref.dtype)
    kw = t_ref[:, KV_RANK:KV_RANK + LANES]
    kw_ref[...] = kw
    lane = lax.broadcasted_iota(I32, kw.shape, 1)
    kia_ref[...] = jnp.where(lane < D_IDX, kw, 0.0).astype(kia_ref.dtype)
    kib_ref[...] = jnp.where(lane >= D_IDX, pltpu.roll(kw, D_IDX, 1), 0.0).astype(kib_ref.dtype)


def _prep_latent(tail, kv_norm_g, tm):
    m, w = tail.shape
    return pl.pallas_call(
        _prep_kernel,
        grid=(m // tm,),
        in_specs=[pl.BlockSpec((tm, w), lambda i: (i, 0)),
                  pl.BlockSpec((1, KV_RANK), lambda i: (0, 0))],
        out_specs=[pl.BlockSpec((tm, KV_RANK), lambda i: (i, 0)),
                   pl.BlockSpec((tm, LANES), lambda i: (i, 0)),
                   pl.BlockSpec((tm, LANES), lambda i: (i, 0)),
                   pl.BlockSpec((tm, LANES), lambda i: (i, 0))],
        out_shape=[jax.ShapeDtypeStruct((m, KV_RANK), BF16),
                   jax.ShapeDtypeStruct((m, LANES), BF16),
                   jax.ShapeDtypeStruct((m, LANES), BF16),
                   jax.ShapeDtypeStruct((m, LANES), F32)],
        compiler_params=_cparams(("parallel",)),
        name="prep_latent",
    )(tail, kv_norm_g.reshape(1, KV_RANK))


def _indexer_kernel(qi_ref, kw_ref, kia_ref, kib_ref, o_ref, keybuf, wb, *, topk):
    i = pl.program_id(1)
    npairs, tq, _ = qi_ref.shape
    nk = o_ref.shape[1]
    tk = o_ref.shape[3]
    half = tk // 2
    kf = float(topk)
    group = 4

    for h in range(H_IDX):
        wb[h] = jnp.broadcast_to(kw_ref[:, D_IDX + h:D_IDX + h + 1], (tq, LANES))

    row = lax.broadcasted_iota(I32, (tq, tk), 0) + i * tq
    col0 = lax.broadcasted_iota(I32, (tq, tk), 1)

    def score_chunk(j, carry):
        k0 = pl.multiple_of(j * tk, tk)
        kd = jnp.concatenate([kia_ref[pl.ds(k0, tk), :], kib_ref[pl.ds(k0, tk), :]], axis=0)
        acc = jnp.zeros((tq, tk), F32)
        for p0 in range(0, npairs, group):
            lhs = qi_ref[p0:p0 + group].reshape(group * tq, LANES)
            zz = _dot_nt(lhs, kd)
            for p in range(group):
                h = 2 * (p0 + p)
                z = zz[p * tq:(p + 1) * tq]
                we = jnp.concatenate([wb[h]] * (tk // LANES), axis=1)
                wo = jnp.concatenate([wb[h + 1]] * (tk // LANES), axis=1)
                acc = acc + we * jnp.maximum(z[:, 0:tk], 0.0) + wo * jnp.maximum(z[:, tk:2 * tk], 0.0)
        bits = pltpu.bitcast(acc, I32)
        key = jnp.where(bits >= 0, bits, bits ^ jnp.int32(0x7FFFFFFF))
        keybuf[j] = jnp.where(col0 + j * tk <= row, key, INT_MIN)
        return carry

    lax.fori_loop(0, i + 1, score_chunk, 0)

    lane = lax.broadcasted_iota(I32, (tq, LANES), 1)

    def count(pred):
        def body(j, c):
            for s in range(tk // LANES):
                kk = keybuf[j, :, s * LANES:(s + 1) * LANES]
                c = c + jnp.where(pred(kk, lane + (j * tk + s * LANES)), 1.0, 0.0)
            return c
        c = lax.fori_loop(0, i + 1, body, jnp.zeros((tq, LANES), F32))
        return jnp.sum(c, axis=1, keepdims=True)

    def bcast(v):
        return jnp.broadcast_to(v, (tq, LANES))

    def count_ge(cand):
        cb = bcast(cand)
        return count(lambda kk, col: kk >= cb)

    tau = jnp.where(count_ge(jnp.zeros((tq, 1), I32)) >= kf, 0, INT_MIN).astype(I32)

    def bit_body(it, tau):
        cand = tau | jnp.left_shift(jnp.int32(1), 30 - it)
        return jnp.where(count_ge(cand) >= kf, cand, tau)

    tau = lax.fori_loop(0, 31, bit_body, tau)
    tau_l = bcast(tau)
    n_ge = count(lambda kk, col: kk >= tau_l)
    n_gt = count(lambda kk, col: kk > tau_l)

    def tie_cut():
        need = kf - n_gt

        def cut_body(it, cut):
            cand = cut + jnp.left_shift(jnp.int32(1), 30 - it)
            cb = bcast(cand)
            below = count(lambda kk, col: (kk == tau_l) & (col < cb))
            return jnp.where(below < need, cand, cut)

        return lax.fori_loop(0, 31, cut_body, jnp.zeros((tq, 1), I32))

    has_ties = jnp.max(n_ge) > kf
    cut = lax.cond(has_ties, tie_cut, lambda: jnp.full((tq, 1), 2 ** 30, I32))
    taub = jnp.broadcast_to(tau, (tq, tk))
    cutb = jnp.broadcast_to(cut, (tq, tk))

    def write_chunk(j, carry):
        kk = keybuf[j]
        sel = ((kk > taub) | ((kk == taub) & (col0 + j * tk <= cutb))) & (kk != INT_MIN)
        o_ref[0, j] = jnp.where(sel, 0.0, NEG).astype(o_ref.dtype)
        return carry

    lax.fori_loop(0, i + 1, write_chunk, 0)

    def write_rest(j, carry):
        o_ref[0, j] = jnp.full((tq, tk), NEG, o_ref.dtype)
        return carry

    lax.fori_loop(i + 1, nk, write_rest, 0)


def _indexer_mask(qi_hm, kw, kia, kib, bsz, seq, topk):
    tq = min(ATT_BLOCK, seq)
    nq = seq // tq
    npairs = qi_hm.shape[0]
    return pl.pallas_call(
        functools.partial(_indexer_kernel, topk=topk),
        grid=(bsz, nq),
        in_specs=[pl.BlockSpec((npairs, tq, LANES), lambda b, i: (0, b * nq + i, 0)),
                  pl.BlockSpec((tq, LANES), lambda b, i: (b * nq + i, 0)),
                  pl.BlockSpec((seq, LANES), lambda b, i: (b, 0)),
                  pl.BlockSpec((seq, LANES), lambda b, i: (b, 0))],
        out_specs=pl.BlockSpec((1, nq, tq, tq), lambda b, i: (b * nq + i, 0, 0, 0)),
        out_shape=jax.ShapeDtypeStruct((bsz * nq, nq, tq, tq), BF16),
        scratch_shapes=[pltpu.VMEM((nq, tq, tq), I32),
                        pltpu.VMEM((H_IDX, tq, LANES), F32)],
        compiler_params=_cparams(("parallel", "parallel")),
        name="indexer_mask",
    )(qi_hm, kw, kia, kib)


def _t5_bucket(dist):
    n = jnp.maximum(dist, 0)
    max_exact = N_BUCKETS // 2
    nf = jnp.maximum(n, 1).astype(F32)
    large = max_exact + (jnp.log(nf / max_exact) / math.log(MAX_DIST / max_exact)
                         * (N_BUCKETS - max_exact)).astype(I32)
    large = jnp.minimum(large, N_BUCKETS - 1)
    return jnp.where(n < max_exact, n, large)


def _attn_kernel(relb_ref, q_ref, ckv_ref, mask_ref, wuk_ref, wuv_ref, og_ref, o_ref,
                 qlat, acc, m_s, l_s, btab, obuf):
    b = pl.program_id(0)
    i = pl.program_id(1)
    nh, tq, _ = q_ref.shape
    tk = mask_ref.shape[3]
    scale = HEAD_DIM ** -0.5

    @pl.when((b == 0) & (i == 0))
    def _():
        r = lax.broadcasted_iota(I32, (tq, tk), 0)
        c = lax.broadcasted_iota(I32, (tq, tk), 1)
        for t in range(2):
            bk = _t5_bucket(r - c + t * tk)

            def fill(h, carry):
                v = jnp.zeros((tq, tk), F32)
                for k in range(N_BUCKETS):
                    v = jnp.where(bk == k, relb_ref[k, h], v)
                btab[t, h] = v
                return carry

            lax.fori_loop(0, nh, fill, 0)

    def init(h, carry):
        qlat[h] = (_dot_nt(q_ref[h], wuk_ref[h]) * scale).astype(qlat.dtype)
        acc[h] = jnp.zeros(acc.shape[1:], F32)
        m_s[h] = jnp.full((tq, 1), NEG, F32)
        l_s[h] = jnp.zeros((tq, 1), F32)
        return carry

    lax.fori_loop(0, nh, init, 0)

    def chunk(j, bias_fn):
        kc = ckv_ref[pl.ds(pl.multiple_of(j * tk, tk), tk), :]
        mk = mask_ref[0, j].astype(F32)

        def head(h, carry):
            s = _dot_nt(qlat[h], kc) + mk + bias_fn(h)
            m_old = m_s[h]
            m_new = jnp.maximum(m_old, jnp.max(s, axis=1, keepdims=True))
            a = jnp.exp(m_old - m_new)
            p = jnp.exp(s - m_new)
            l_s[h] = a * l_s[h] + jnp.sum(p, axis=1, keepdims=True)
            acc[h] = a * acc[h] + _dot(p.astype(kc.dtype), kc)
            m_s[h] = m_new
            return carry

        lax.fori_loop(0, nh, head, 0)

    def far_chunk(j, carry):
        chunk(j, lambda h: relb_ref[N_BUCKETS - 1, h])
        return carry

    def near_chunk(j, carry):
        chunk(j, lambda h: btab[i - j, h])
        return carry

    lax.fori_loop(0, jnp.maximum(i - 1, 0), far_chunk, 0)
    lax.fori_loop(jnp.maximum(i - 1, 0), i + 1, near_chunk, 0)

    def finish(h, ssq):
        o_lat = (acc[h] / l_s[h]).astype(wuv_ref.dtype)
        o = _dot(o_lat, wuv_ref[h])
        obuf[h] = o
        return ssq + jnp.sum(o * o, axis=1, keepdims=True)

    ssq = lax.fori_loop(0, nh, finish, jnp.zeros((tq, 1), F32))
    rr = lax.rsqrt(ssq * (1.0 / (nh * HEAD_DIM)) + LN_EPS)
    for h in range(nh):
        o_ref[:, h * HEAD_DIM:(h + 1) * HEAD_DIM] = (
            obuf[h] * rr * og_ref[:, h * HEAD_DIM:(h + 1) * HEAD_DIM]).astype(o_ref.dtype)


def _latent_attention(q_hm, ckv_n, mask, w_uk, w_uv, rel_bias, out_g, bsz, seq):
    nh = q_hm.shape[0]
    tq = mask.shape[2]
    nq = seq // tq
    d_attn = nh * HEAD_DIM
    assert tq == mask.shape[3] and tq >= MAX_DIST
    return pl.pallas_call(
        _attn_kernel,
        grid=(bsz, nq),
        in_specs=[pl.BlockSpec(memory_space=pltpu.SMEM),
                  pl.BlockSpec((nh, tq, HEAD_DIM), lambda b, i: (0, b * nq + i, 0)),
                  pl.BlockSpec((seq, KV_RANK), lambda b, i: (b, 0)),
                  pl.BlockSpec((1, nq, tq, tq), lambda b, i: (b * nq + i, 0, 0, 0)),
                  pl.BlockSpec((nh, KV_RANK, HEAD_DIM), lambda b, i: (0, 0, 0)),
                  pl.BlockSpec((nh, KV_RANK, HEAD_DIM), lambda b, i: (0, 0, 0)),
                  pl.BlockSpec((1, d_attn), lambda b, i: (0, 0))],
        out_specs=pl.BlockSpec((tq, d_attn), lambda b, i: (b * nq + i, 0)),
        out_shape=jax.ShapeDtypeStruct((bsz * seq, d_attn), BF16),
        scratch_shapes=[pltpu.VMEM((nh, tq, KV_RANK), BF16),
                        pltpu.VMEM((nh, tq, KV_RANK), F32),
                        pltpu.VMEM((nh, tq, 1), F32),
                        pltpu.VMEM((nh, tq, 1), F32),
                        pltpu.VMEM((2, nh, tq, tq), F32),
                        pltpu.VMEM((nh, tq, HEAD_DIM), F32)],
        compiler_params=_cparams(("arbitrary", "arbitrary")),
        name="latent_attention",
    )(rel_bias, q_hm, ckv_n, mask, w_uk, w_uv, out_g.reshape(1, d_attn))


def _memattn_kernel(x_ref, kv_ref, wq_ref, wo_ref, g_ref, b_ref, wr_ref, o_ref, lg_ref):
    x = x_ref[...]
    d_mem = MEM_HEADS * MEM_HEAD_DIM
    q = (_dot(x.astype(BF16), wq_ref[...]) * (MEM_HEAD_DIM ** -0.5)).astype(BF16)
    outs = []
    for h in range(MEM_HEADS):
        lo = h * MEM_HEAD_DIM
        k = kv_ref[:, lo:lo + MEM_HEAD_DIM]
        v = kv_ref[:, d_mem + lo:d_mem + lo + MEM_HEAD_DIM]
        s = _dot_nt(q[:, lo:lo + MEM_HEAD_DIM], k)
        p = jnp.exp(s - jnp.max(s, axis=1, keepdims=True))
        p = p / jnp.sum(p, axis=1, keepdims=True)
        outs.append(_dot(p.astype(BF16), v).astype(BF16))
    o = jnp.concatenate(outs, axis=1)
    x2 = _ln_rows(ALPHA * x + _dot(o, wo_ref[...]), g_ref[...], b_ref[...])
    o_ref[...] = x2
    lg_ref[...] = _dot(x2.astype(BF16), wr_ref[...])


def _memory_attention(x1, kv, w_mq, w_mo, g, b, w_router, bsz, seq, tm):
    t, d = x1.shape
    nt = seq // tm
    mem_len = kv.shape[0] // bsz
    d_mem = w_mq.shape[1]
    return pl.pallas_call(
        _memattn_kernel,
        grid=(bsz, nt),
        in_specs=[pl.BlockSpec((tm, d), lambda bi, i: (bi * nt + i, 0)),
                  pl.BlockSpec((mem_len, 2 * d_mem), lambda bi, i: (bi, 0)),
                  pl.BlockSpec((d, d_mem), lambda bi, i: (0, 0)),
                  pl.BlockSpec((d_mem, d), lambda bi, i: (0, 0)),
                  pl.BlockSpec((1, d), lambda bi, i: (0, 0)),
                  pl.BlockSpec((1, d), lambda bi, i: (0, 0)),
                  pl.BlockSpec((d, LANES), lambda bi, i: (0, 0))],
        out_specs=[pl.BlockSpec((tm, d), lambda bi, i: (bi * nt + i, 0)),
                   pl.BlockSpec((tm, LANES), lambda bi, i: (bi * nt + i, 0))],
        out_shape=[jax.ShapeDtypeStruct((t, d), F32),
                   jax.ShapeDtypeStruct((t, LANES), F32)],
        compiler_params=_cparams(("parallel", "parallel")),
        name="memory_attention",
    )(x1, kv, w_mq, w_mo, g.reshape(1, d), b.reshape(1, d), w_router)


def _router_kernel(lg_ref, e1_ref, e2_ref, g1_ref, g2_ref):
    x = lg_ref[...]
    lane = lax.broadcasted_iota(I32, x.shape, 1)
    lane_f = lane.astype(F32)

    def argmax(mask):
        v = jnp.where(mask, x, -jnp.inf)
        mx = jnp.max(v, axis=1, keepdims=True)
        idx = jnp.min(jnp.where(mask & (v == mx), lane_f, float(LANES)), axis=1, keepdims=True)
        return mx, idx.astype(I32)

    gmask = lane < N_GROUPS
    gmax, gsel = argmax(gmask)
    gsum = jnp.sum(jnp.where(gmask, jnp.exp(x - gmax), 0.0), axis=1, keepdims=True)
    g_p = 1.0 / gsum
    lo = N_GROUPS + gsel * EXP_PER_GROUP
    emask = (lane >= lo) & (lane < lo + EXP_PER_GROUP)
    m1, i1 = argmax(emask)
    m2, i2 = argmax(emask & (lane != i1))
    esum = jnp.sum(jnp.where(emask, jnp.exp(x - m1), 0.0), axis=1, keepdims=True)
    p1 = 1.0 / esum
    p2 = jnp.exp(m2 - m1) / esum
    e1_ref[...] = i1 - N_GROUPS
    e2_ref[...] = i2 - N_GROUPS
    g1_ref[...] = g_p * (p1 / (p1 + p2))
    g2_ref[...] = g_p * (p2 / (p1 + p2))


def _router(logits, tm):
    t = logits.shape[0]
    col = pl.BlockSpec((tm, 1), lambda i: (i, 0))
    return pl.pallas_call(
        _router_kernel,
        grid=(t // tm,),
        in_specs=[pl.BlockSpec((tm, LANES), lambda i: (i, 0))],
        out_specs=[col, col, col, col],
        out_shape=[jax.ShapeDtypeStruct((t, 1), I32), jax.ShapeDtypeStruct((t, 1), I32),
                   jax.ShapeDtypeStruct((t, 1), F32), jax.ShapeDtypeStruct((t, 1), F32)],
        compiler_params=_cparams(("parallel",)),
        name="moe_router",
    )(logits)


def _moe_kernel(be_ref, nused_ref, tok_ref, dst_ref, x_hbm, wg_ref, wu_ref, wd_ref, y_hbm,
                xbuf, ybuf, gsem, ssem):
    i = pl.program_id(0)
    blk = xbuf.shape[0]

    def gather(r):
        return pltpu.make_async_copy(x_hbm.at[pl.ds(tok_ref[0, 0, r], 1)], xbuf.at[pl.ds(r, 1)], gsem)

    def scatter(r):
        return pltpu.make_async_copy(ybuf.at[pl.ds(r, 1)],
                                     y_hbm.at[pl.ds(jnp.maximum(dst_ref[0, 0, r], 0), 1)], ssem)

    @pl.when(i < nused_ref[0])
    def _():
        def g_start(r, c):
            gather(r).start()
            return c

        def g_wait(r, c):
            gather(r).wait()
            return c

        lax.fori_loop(0, blk, g_start, 0)
        lax.fori_loop(0, blk, g_wait, 0)
        xb = xbuf[...].astype(BF16)
        g = _dot(xb, wg_ref[0])
        u = _dot(xb, wu_ref[0])
        hmid = (g * jax.nn.sigmoid(g) * u).astype(BF16)
        ybuf[...] = _dot(hmid, wd_ref[0])

        def s_start(r, c):
            @pl.when(dst_ref[0, 0, r] >= 0)
            def _():
                scatter(r).start()
            return c

        def s_wait(r, c):
            @pl.when(dst_ref[0, 0, r] >= 0)
            def _():
                scatter(r).wait()
            return c

        lax.fori_loop(0, blk, s_start, 0)
        lax.fori_loop(0, blk, s_wait, 0)


def _moe_experts(x2, w_gate, w_up, w_down, block_expert, n_used, tok_pad, dst_pad):
    t, d = x2.shape
    nb = block_expert.shape[0]
    ff = w_gate.shape[2]
    grid_spec = pltpu.PrefetchScalarGridSpec(
        num_scalar_prefetch=2,
        grid=(nb,),
        in_specs=[pl.BlockSpec((1, 1, MOE_BLOCK), lambda i, be, nu: (i, 0, 0), memory_space=pltpu.SMEM),
                  pl.BlockSpec((1, 1, MOE_BLOCK), lambda i, be, nu: (i, 0, 0), memory_space=pltpu.SMEM),
                  pl.BlockSpec(memory_space=pl.ANY),
                  pl.BlockSpec((1, d, ff), lambda i, be, nu: (be[i], 0, 0)),
                  pl.BlockSpec((1, d, ff), lambda i, be, nu: (be[i], 0, 0)),
                  pl.BlockSpec((1, ff, d), lambda i, be, nu: (be[i], 0, 0))],
        out_specs=pl.BlockSpec(memory_space=pl.ANY),
        scratch_shapes=[pltpu.VMEM((MOE_BLOCK, d), F32),
                        pltpu.VMEM((MOE_BLOCK, d), F32),
                        pltpu.SemaphoreType.DMA(()),
                        pltpu.SemaphoreType.DMA(())],
    )
    return pl.pallas_call(
        _moe_kernel,
        grid_spec=grid_spec,
        out_shape=jax.ShapeDtypeStruct((2 * t, d), F32),
        compiler_params=_cparams(("arbitrary",)),
        name="moe_experts",
    )(block_expert, n_used, tok_pad.reshape(nb, 1, MOE_BLOCK), dst_pad.reshape(nb, 1, MOE_BLOCK),
      x2, w_gate, w_up, w_down)


def _combine_kernel(x_ref, y1_ref, y2_ref, g1_ref, g2_ref, g_ref, b_ref, o_ref):
    y = y1_ref[0] * g1_ref[...] + y2_ref[0] * g2_ref[...]
    o_ref[...] = _ln_rows(ALPHA * x_ref[...] + y, g_ref[...], b_ref[...])


def _moe_combine(x2, y, g1, g2, g, b, tm):
    t, d = x2.shape
    y3 = y.reshape(2, t, d)
    return pl.pallas_call(
        _combine_kernel,
        grid=(t // tm,),
        in_specs=[pl.BlockSpec((tm, d), lambda i: (i, 0)),
                  pl.BlockSpec((1, tm, d), lambda i: (0, i, 0)),
                  pl.BlockSpec((1, tm, d), lambda i: (1, i, 0)),
                  pl.BlockSpec((tm, 1), lambda i: (i, 0)),
                  pl.BlockSpec((tm, 1), lambda i: (i, 0)),
                  pl.BlockSpec((1, d), lambda i: (0, 0)),
                  pl.BlockSpec((1, d), lambda i: (0, 0))],
        out_specs=pl.BlockSpec((tm, d), lambda i: (i, 0)),
        out_shape=jax.ShapeDtypeStruct((t, d), F32),
        compiler_params=_cparams(("parallel",)),
        name="moe_combine",
    )(x2, y3, y3, g1, g2, g.reshape(1, d), b.reshape(1, d))


def _moe_dispatch(e1, e2, t):
    a = 2 * t
    e_flat = jnp.concatenate([e1, e2], axis=1).reshape(a)
    order = jnp.argsort(e_flat).astype(I32)
    e_sorted = e_flat[order]
    counts = jnp.bincount(e_flat, length=N_EXPERTS).astype(I32)
    padded = (counts + MOE_BLOCK - 1) // MOE_BLOCK * MOE_BLOCK
    seg_start = jnp.cumsum(counts) - counts
    pad_end = jnp.cumsum(padded)
    pad_start = pad_end - padded
    dest = pad_start[e_sorted] + (jnp.arange(a, dtype=I32) - seg_start[e_sorted])
    nb = (a + N_EXPERTS * (MOE_BLOCK - 1) + MOE_BLOCK - 1) // MOE_BLOCK
    p = nb * MOE_BLOCK
    tok = order // 2
    slot = order % 2
    tok_pad = jnp.zeros((p,), I32).at[dest].set(tok)
    dst_pad = jnp.full((p,), -1, I32).at[dest].set(slot * t + tok)
    block_pos = jnp.arange(nb, dtype=I32) * MOE_BLOCK
    block_expert = jnp.minimum(jnp.searchsorted(pad_end, block_pos, side="right"), N_EXPERTS - 1).astype(I32)
    n_used = (pad_end[-1:] // MOE_BLOCK).astype(I32)
    return block_expert, n_used, tok_pad, dst_pad


def _tile(n, pref):
    return pref if n % pref == 0 else n


def _layer(x, mem, w_in, conv_w, conv_b, conv_ln_g, conv_ln_b, kv_norm_g, w_uk, w_uv, rel_bias,
           conv_out_g, attn_out_g, w_out, ln1_g, ln1_b, w_mq, w_mk, w_mv, w_mo, ln2_g, ln2_b,
           w_router_grp, w_router_exp, w_gate, w_up, w_down, ln3_g, ln3_b):
    bsz, seq, d = x.shape
    t = bsz * seq
    d_conv = conv_w.shape[1]
    d_attn = N_HEADS * HEAD_DIM
    c_glu = 2 * d_conv
    c_qi = H_IDX * D_IDX
    o_q, o_kv = c_glu, c_glu + d_attn
    o_qi = o_kv + KV_RANK
    o_ki = o_qi + c_qi
    topk = min(TOPK_MAX, seq // 4)

    xf = x.reshape(t, d)
    xb = xf.astype(BF16)
    w_inb = w_in.astype(BF16)
    tail_w = jnp.concatenate([w_inb[:, o_kv:o_qi], w_inb[:, o_ki:]], axis=1)
    tail_w = jnp.pad(tail_w, ((0, 0), (0, KV_RANK + LANES - tail_w.shape[1])))
    tm = _tile(t, 1024)

    u = _matmul(xb, w_inb[:, :c_glu], BF16, tm, 512, "proj_glu")
    q_hm = _matmul_heads(xb, w_inb[:, o_q:o_kv], BF16, tm, 512, "proj_q")
    qi_hm = _matmul_heads(xb, w_inb[:, o_qi:o_ki], BF16, tm, 512, "proj_qidx")
    tail = _matmul(xb, tail_w, F32, tm, KV_RANK + LANES, "proj_tail")

    conv_n = _conformer_conv(u, bsz, seq, conv_w, conv_b, conv_ln_g, conv_ln_b, conv_out_g)
    ckv_n, kia, kib, kw = _prep_latent(tail, kv_norm_g, tm)
    mask = _indexer_mask(qi_hm, kw, kia, kib, bsz, seq, topk)
    w_ukb = w_uk.astype(BF16)
    w_uvb = w_uv.astype(BF16)
    attn_n = _latent_attention(q_hm, ckv_n, mask, w_ukb, w_uvb, rel_bias, attn_out_g, bsz, seq)

    w_outb = w_out.astype(BF16)
    pre1 = _matmul2_residual(conv_n, attn_n, w_outb[:d_conv], w_outb[d_conv:], xf, tm, 512, "out_proj")
    x1 = _layer_norm(pre1, ln1_g, ln1_b, 256, "ln1")

    mem_len = mem.shape[1]
    memb = mem.reshape(bsz * mem_len, d).astype(BF16)
    w_kv = jnp.concatenate([w_mk, w_mv], axis=1).astype(BF16)
    kv = _matmul(memb, w_kv, BF16, _tile(bsz * mem_len, 512), 512, "mem_kv")
    w_router = jnp.concatenate([w_router_grp, w_router_exp], axis=1)
    w_router = jnp.pad(w_router, ((0, 0), (0, LANES - w_router.shape[1]))).astype(BF16)
    x2, logits = _memory_attention(x1, kv, w_mq.astype(BF16), w_mo.astype(BF16), ln2_g, ln2_b,
                                   w_router, bsz, seq, 256)

    e1, e2, g1, g2 = _router(logits, _tile(t, 1024))
    block_expert, n_used, tok_pad, dst_pad = _moe_dispatch(e1, e2, t)
    y = _moe_experts(x2, w_gate.astype(BF16), w_up.astype(BF16), w_down.astype(BF16),
                     block_expert, n_used, tok_pad, dst_pad)
    x3 = _moe_combine(x2, y, g1, g2, ln3_g, ln3_b, 256)
    return x3.reshape(bsz, seq, d)


def kernel(x, mem, w_in, conv_w, conv_b, conv_ln_g, conv_ln_b, kv_norm_g, w_uk, w_uv, rel_bias, conv_out_g, attn_out_g, w_out, ln1_g, ln1_b, w_mq, w_mk, w_mv, w_mo, ln2_g, ln2_b, w_router_grp, w_router_exp, w_gate, w_up, w_down, ln3_g, ln3_b):
    for l in range(w_in.shape[0]):
        x = _layer(x, mem, w_in[l], conv_w[l], conv_b[l], conv_ln_g[l], conv_ln_b[l], kv_norm_g[l],
                   w_uk[l], w_uv[l], rel_bias, conv_out_g[l], attn_out_g[l], w_out[l], ln1_g[l], ln1_b[l],
                   w_mq[l], w_mk[l], w_mv[l], w_mo[l], ln2_g[l], ln2_b[l], w_router_grp[l],
                   w_router_exp[l], w_gate[l], w_up[l], w_down[l], ln3_g[l], ln3_b[l])
    return x
```

```python
import functools
import math

import jax
import jax.numpy as jnp
from jax import lax
from jax.experimental import pallas as pl
from jax.experimental.pallas import tpu as pltpu

F32 = jnp.float32
BF16 = jnp.bfloat16
I32 = jnp.int32

DEPTH = 1
CONV_WIDTH = 31
N_HEADS = 16
HEAD_DIM = 128
KV_RANK = 512
H_IDX = 32
D_IDX = 64
TOPK_MAX = 256
N_BUCKETS = 32
MAX_DIST = 128
MEM_HEADS = 4
MEM_HEAD_DIM = 128
N_GROUPS = 8
EXP_PER_GROUP = 8
N_EXPERTS = N_GROUPS * EXP_PER_GROUP
MOE_BLOCK = 128
ALPHA = (2.0 * DEPTH) ** 0.25
LN_EPS = 1e-5

LANES = 128
SUBLANES = 8
V7X_VMEM_BYTES = 64 * 1024 * 1024
VMEM_LIMIT = 56 * 1024 * 1024
NEG = -1e30
INT_MIN = -(2 ** 31)

LOG2E = 1.4426950408889634

ATT_BLOCK = 256
ATT_HEAD_GROUP = 4
CONV_TS = 256
CONV_HALO = 32
CONV_CC = 256
CONV_RC = 32


def _cparams(sem):
    return pltpu.CompilerParams(dimension_semantics=sem, vmem_limit_bytes=VMEM_LIMIT)


def _dot(a, b):
    return jnp.dot(a, b, preferred_element_type=F32)


def _dot_nt(a, b):
    return lax.dot_general(a, b, (((1,), (1,)), ((), ())), preferred_element_type=F32)


def _mm_kernel(a_ref, b_ref, o_ref):
    o_ref[...] = _dot(a_ref[...], b_ref[...]).astype(o_ref.dtype)


def _matmul(a, b, out_dtype, tm, tn, name):
    m, k = a.shape
    n = b.shape[1]
    return pl.pallas_call(
        _mm_kernel,
        grid=(m // tm, n // tn),
        in_specs=[pl.BlockSpec((tm, k), lambda i, j: (i, 0)),
                  pl.BlockSpec((k, tn), lambda i, j: (0, j))],
        out_specs=pl.BlockSpec((tm, tn), lambda i, j: (i, j)),
        out_shape=jax.ShapeDtypeStruct((m, n), out_dtype),
        compiler_params=_cparams(("parallel", "parallel")),
        name=name,
    )(a, b)


def _mm_heads_kernel(a_ref, b_ref, o_ref):
    r = _dot(a_ref[...], b_ref[...])
    for p in range(o_ref.shape[0]):
        o_ref[p] = r[:, p * LANES:(p + 1) * LANES].astype(o_ref.dtype)


def _matmul_heads(a, b, out_dtype, tm, tn, name):
    m, k = a.shape
    n = b.shape[1]
    return pl.pallas_call(
        _mm_heads_kernel,
        grid=(m // tm, n // tn),
        in_specs=[pl.BlockSpec((tm, k), lambda i, j: (i, 0)),
                  pl.BlockSpec((k, tn), lambda i, j: (0, j))],
        out_specs=pl.BlockSpec((tn // LANES, tm, LANES), lambda i, j: (j, i, 0)),
        out_shape=jax.ShapeDtypeStruct((n // LANES, m, LANES), out_dtype),
        compiler_params=_cparams(("parallel", "parallel")),
        name=name,
    )(a, b)


def _mm2_res_kernel(a1_ref, a2_ref, w1_ref, w2_ref, r_ref, o_ref):
    o_ref[...] = (ALPHA * r_ref[...] + _dot(a1_ref[...], w1_ref[...])
                  + _dot(a2_ref[...], w2_ref[...]))


def _matmul2_residual(a1, a2, w1, w2, res, tm, tn, name):
    m, k1 = a1.shape
    k2 = a2.shape[1]
    n = w1.shape[1]
    return pl.pallas_call(
        _mm2_res_kernel,
        grid=(m // tm, n // tn),
        in_specs=[pl.BlockSpec((tm, k1), lambda i, j: (i, 0)),
                  pl.BlockSpec((tm, k2), lambda i, j: (i, 0)),
                  pl.BlockSpec((k1, tn), lambda i, j: (0, j)),
                  pl.BlockSpec((k2, tn), lambda i, j: (0, j)),
                  pl.BlockSpec((tm, tn), lambda i, j: (i, j))],
        out_specs=pl.BlockSpec((tm, tn), lambda i, j: (i, j)),
        out_shape=jax.ShapeDtypeStruct((m, n), F32),
        compiler_params=_cparams(("parallel", "parallel")),
        name=name,
    )(a1, a2, w1, w2, res)


def _ln_rows(x, g, b):
    mu = jnp.mean(x, axis=-1, keepdims=True)
    xc = x - mu
    var = jnp.mean(xc * xc, axis=-1, keepdims=True)
    return xc * lax.rsqrt(var + LN_EPS) * g + b


def _ln_kernel(x_ref, g_ref, b_ref, o_ref):
    o_ref[...] = _ln_rows(x_ref[...], g_ref[...], b_ref[...])


def _layer_norm(x, g, b, tm, name):
    m, d = x.shape
    return pl.pallas_call(
        _ln_kernel,
        grid=(m // tm,),
        in_specs=[pl.BlockSpec((tm, d), lambda i: (i, 0)),
                  pl.BlockSpec((1, d), lambda i: (0, 0)),
                  pl.BlockSpec((1, d), lambda i: (0, 0))],
        out_specs=pl.BlockSpec((tm, d), lambda i: (i, 0)),
        out_shape=jax.ShapeDtypeStruct((m, d), F32),
        compiler_params=_cparams(("parallel",)),
        name=name,
    )(x, g.reshape(1, d), b.reshape(1, d))


def _conv_kernel(a_ref, g_ref, cw_ref, cb_ref, lg_ref, lb_ref, og_ref, o_ref, hbuf, ybuf, hs):
    ts = a_ref.shape[0]
    nch = hbuf.shape[0]
    cc = hbuf.shape[2]
    d_conv = nch * cc

    nrow = CONV_HALO + ts

    @pl.when(pl.program_id(1) == 0)
    def _():
        hbuf[:, 0:CONV_HALO, :] = jnp.zeros((nch, CONV_HALO, cc), F32)
        hbuf[:, nrow:nrow + SUBLANES, :] = jnp.zeros((nch, SUBLANES, cc), F32)

    for c in range(nch):
        a = a_ref[:, c * cc:(c + 1) * cc].astype(F32)
        g = g_ref[:, c * cc:(c + 1) * cc].astype(F32)
        hbuf[c, CONV_HALO:nrow, :] = a * jax.nn.sigmoid(g)

    first = CONV_HALO - (CONV_WIDTH - 1)

    def chunk_body(c, carry):
        for o in range(1, SUBLANES):
            hs[o - 1] = hbuf[c, o:o + nrow, :]
        for r0 in range(0, ts, CONV_RC):
            acc = jnp.zeros((CONV_RC, cc), F32)
            for j in range(CONV_WIDTH):
                o = (first + j) % SUBLANES
                base = r0 + first + j - o
                rows = hbuf[c, base:base + CONV_RC, :] if o == 0 else hs[o - 1, base:base + CONV_RC, :]
                acc = acc + cw_ref[c, j:j + 1, :] * rows
            ybuf[c, r0:r0 + CONV_RC, :] = acc + cb_ref[c]
        hbuf[c, 0:CONV_HALO, :] = hbuf[c, ts:nrow, :]
        return carry

    lax.fori_loop(0, nch, chunk_body, 0)

    s1 = jnp.zeros((ts, 1), F32)
    for c in range(nch):
        s1 = s1 + jnp.sum(ybuf[c], axis=1, keepdims=True)
    mu = s1 * (1.0 / d_conv)
    s2 = jnp.zeros((ts, 1), F32)
    for c in range(nch):
        yc = ybuf[c] - mu
        s2 = s2 + jnp.sum(yc * yc, axis=1, keepdims=True)
    rstd = lax.rsqrt(s2 * (1.0 / d_conv) + LN_EPS)
    s3 = jnp.zeros((ts, 1), F32)
    for c in range(nch):
        z = (ybuf[c] - mu) * rstd * lg_ref[:, c * cc:(c + 1) * cc] + lb_ref[:, c * cc:(c + 1) * cc]
        z = z * jax.nn.sigmoid(z)
        ybuf[c] = z
        s3 = s3 + jnp.sum(z * z, axis=1, keepdims=True)
    rr = lax.rsqrt(s3 * (1.0 / d_conv) + LN_EPS)
    for c in range(nch):
        o_ref[:, c * cc:(c + 1) * cc] = (ybuf[c] * rr * og_ref[:, c * cc:(c + 1) * cc]).astype(o_ref.dtype)


def _conformer_conv(u, bsz, seq, conv_w, conv_b, ln_g, ln_b, out_g):
    d_conv = u.shape[1] // 2
    ts = min(CONV_TS, seq)
    nch = d_conv // CONV_CC
    nt = seq // ts
    cw = conv_w.reshape(CONV_WIDTH, nch, CONV_CC).transpose(1, 0, 2)
    cb = conv_b.reshape(nch, 1, CONV_CC)
    vec = pl.BlockSpec((1, d_conv), lambda b, i: (0, 0))
    return pl.pallas_call(
        _conv_kernel,
        grid=(bsz, nt),
        in_specs=[pl.BlockSpec((ts, d_conv), lambda b, i: (b * nt + i, 0)),
                  pl.BlockSpec((ts, d_conv), lambda b, i: (b * nt + i, 1)),
                  pl.BlockSpec((nch, CONV_WIDTH, CONV_CC), lambda b, i: (0, 0, 0)),
                  pl.BlockSpec((nch, 1, CONV_CC), lambda b, i: (0, 0, 0)),
                  vec, vec, vec],
        out_specs=pl.BlockSpec((ts, d_conv), lambda b, i: (b * nt + i, 0)),
        out_shape=jax.ShapeDtypeStruct((bsz * seq, d_conv), BF16),
        scratch_shapes=[pltpu.VMEM((nch, CONV_HALO + ts + SUBLANES, CONV_CC), F32),
                        pltpu.VMEM((nch, ts, CONV_CC), F32),
                        pltpu.VMEM((SUBLANES - 1, CONV_HALO + ts, CONV_CC), F32)],
        compiler_params=_cparams(("arbitrary", "arbitrary")),
        name="conformer_conv",
    )(u, u, cw, cb, ln_g.reshape(1, d_conv), ln_b.reshape(1, d_conv), out_g.reshape(1, d_conv))


def _prep_kernel(t_ref, g_ref, ckv_ref, kia_ref, kib_ref, kw_ref):
    ckv = t_ref[:, 0:KV_RANK]
    ms = jnp.mean(ckv * ckv, axis=-1, keepdims=True)
    ckv_ref[...] = (ckv * lax.rsqrt(ms + LN_EPS) * g_ref[...]).astype(ckv_ref.dtype)
    kw = t_ref[:, KV_RANK:KV_RANK + LANES]
    kw_ref[...] = kw
    lane = lax.broadcasted_iota(I32, kw.shape, 1)
    kia_ref[...] = jnp.where(lane < D_IDX, kw, 0.0).astype(kia_ref.dtype)
    kib_ref[...] = jnp.where(lane >= D_IDX, pltpu.roll(kw, D_IDX, 1), 0.0).astype(kib_ref.dtype)


def _prep_latent(tail, kv_norm_g, tm):
    m, w = tail.shape
    return pl.pallas_call(
        _prep_kernel,
        grid=(m // tm,),
        in_specs=[pl.BlockSpec((tm, w), lambda i: (i, 0)),
                  pl.BlockSpec((1, KV_RANK), lambda i: (0, 0))],
        out_specs=[pl.BlockSpec((tm, KV_RANK), lambda i: (i, 0)),
                   pl.BlockSpec((tm, LANES), lambda i: (i, 0)),
                   pl.BlockSpec((tm, LANES), lambda i: (i, 0)),
                   pl.BlockSpec((tm, LANES), lambda i: (i, 0))],
        out_shape=[jax.ShapeDtypeStruct((m, KV_RANK), BF16),
                   jax.ShapeDtypeStruct((m, LANES), BF16),
                   jax.ShapeDtypeStruct((m, LANES), BF16),
                   jax.ShapeDtypeStruct((m, LANES), F32)],
        compiler_params=_cparams(("parallel",)),
        name="prep_latent",
    )(tail, kv_norm_g.reshape(1, KV_RANK))


def _indexer_kernel(qi_ref, kw_ref, kia_ref, kib_ref, o_ref, keybuf, wb, *, topk):
    i = pl.program_id(1)
    npairs, tq, _ = qi_ref.shape
    nk = o_ref.shape[1]
    tk = o_ref.shape[3]
    half = tk // 2
    kf = float(topk)
    group = 4

    for h in range(H_IDX):
        wb[h] = jnp.broadcast_to(kw_ref[:, D_IDX + h:D_IDX + h + 1], (tq, LANES))

    row = lax.broadcasted_iota(I32, (tq, tk), 0) + i * tq
    col0 = lax.broadcasted_iota(I32, (tq, tk), 1)

    def score_chunk(j, carry):
        k0 = pl.multiple_of(j * tk, tk)
        kd = jnp.concatenate([kia_ref[pl.ds(k0, tk), :], kib_ref[pl.ds(k0, tk), :]], axis=0)
        acc = jnp.zeros((tq, tk), F32)
        for p0 in range(0, npairs, group):
            lhs = qi_ref[p0:p0 + group].reshape(group * tq, LANES)
            zz = _dot_nt(lhs, kd)
            for p in range(group):
                h = 2 * (p0 + p)
                z = zz[p * tq:(p + 1) * tq]
                we = jnp.concatenate([wb[h]] * (tk // LANES), axis=1)
                wo = jnp.concatenate([wb[h + 1]] * (tk // LANES), axis=1)
                acc = acc + we * jnp.maximum(z[:, 0:tk], 0.0) + wo * jnp.maximum(z[:, tk:2 * tk], 0.0)
        bits = pltpu.bitcast(acc, I32)
        key = jnp.where(bits >= 0, bits, bits ^ jnp.int32(0x7FFFFFFF))
        keybuf[j] = jnp.where(col0 + j * tk <= row, key, INT_MIN)
        return carry

    lax.fori_loop(0, i + 1, score_chunk, 0)

    rows_per_pass = 128
    lane = lax.broadcasted_iota(I32, (rows_per_pass, LANES), 1)

    def count(pred):
        parts = []
        for r0 in range(0, tq, rows_per_pass):
            rows = slice(r0, r0 + rows_per_pass)

            def body(j, c):
                for s in range(tk // LANES):
                    kk = keybuf[j, rows, s * LANES:(s + 1) * LANES]
                    c = c + jnp.where(pred(kk, lane + (j * tk + s * LANES), rows), 1.0, 0.0)
                return c

            parts.append(lax.fori_loop(0, i + 1, body, jnp.zeros((rows_per_pass, LANES), F32)))
        return jnp.concatenate(
            [jnp.broadcast_to(jnp.sum(c, axis=1, keepdims=True), (rows_per_pass, LANES)) for c in parts], axis=0)

    def count_ge(cand):
        return count(lambda kk, col, rows: kk >= cand[rows])

    tau = jnp.where(count_ge(jnp.zeros((tq, LANES), I32)) >= kf, 0, INT_MIN).astype(I32)

    def bit_body(it, tau):
        cand = tau | jnp.left_shift(jnp.int32(1), 30 - it)
        return jnp.where(count_ge(cand) >= kf, cand, tau)

    tau = lax.fori_loop(0, 31, bit_body, tau)
    n_ge = count_ge(tau)
    n_gt = count(lambda kk, col, rows: kk > tau[rows])

    def tie_cut():
        need = kf - n_gt

        def cut_body(it, cut):
            cand = cut + jnp.left_shift(jnp.int32(1), 30 - it)
            below = count(lambda kk, col, rows: (kk == tau[rows]) & (col < cand[rows]))
            return jnp.where(below < need, cand, cut)

        return lax.fori_loop(0, 31, cut_body, jnp.zeros((tq, LANES), I32))

    has_ties = jnp.max(n_ge) > kf
    cut = lax.cond(has_ties, tie_cut, lambda: jnp.full((tq, LANES), 2 ** 30, I32))
    taub = jnp.concatenate([tau] * (tk // LANES), axis=1)
    cutb = jnp.concatenate([cut] * (tk // LANES), axis=1)

    def write_chunk(j, carry):
        kk = keybuf[j]
        sel = ((kk > taub) | ((kk == taub) & (col0 + j * tk <= cutb))) & (kk != INT_MIN)
        o_ref[0, j] = jnp.where(sel, 0.0, NEG).astype(o_ref.dtype)
        return carry

    lax.fori_loop(0, i + 1, write_chunk, 0)

    def write_rest(j, carry):
        o_ref[0, j] = jnp.full((tq, tk), NEG, o_ref.dtype)
        return carry

    lax.fori_loop(i + 1, nk, write_rest, 0)


def _indexer_mask(qi_hm, kw, kia, kib, bsz, seq, topk):
    tq = min(ATT_BLOCK, seq)
    nq = seq // tq
    npairs = qi_hm.shape[0]
    return pl.pallas_call(
        functools.partial(_indexer_kernel, topk=topk),
        grid=(bsz, nq),
        in_specs=[pl.BlockSpec((npairs, tq, LANES), lambda b, i: (0, b * nq + i, 0)),
                  pl.BlockSpec((tq, LANES), lambda b, i: (b * nq + i, 0)),
                  pl.BlockSpec((seq, LANES), lambda b, i: (b, 0)),
                  pl.BlockSpec((seq, LANES), lambda b, i: (b, 0))],
        out_specs=pl.BlockSpec((1, nq, tq, tq), lambda b, i: (b * nq + i, 0, 0, 0)),
        out_shape=jax.ShapeDtypeStruct((bsz * nq, nq, tq, tq), BF16),
        scratch_shapes=[pltpu.VMEM((nq, tq, tq), I32),
                        pltpu.VMEM((H_IDX, tq, LANES), F32)],
        compiler_params=_cparams(("parallel", "parallel")),
        name="indexer_mask",
    )(qi_hm, kw, kia, kib)


def _t5_bucket(dist):
    n = jnp.maximum(dist, 0)
    max_exact = N_BUCKETS // 2
    nf = jnp.maximum(n, 1).astype(F32)
    large = max_exact + (jnp.log(nf / max_exact) / math.log(MAX_DIST / max_exact)
                         * (N_BUCKETS - max_exact)).astype(I32)
    large = jnp.minimum(large, N_BUCKETS - 1)
    return jnp.where(n < max_exact, n, large)


def _attn_kernel(relb_ref, q_ref, ckv_ref, mask_ref, wuk_ref, wuv_ref, og_ref, o_ref,
                 qlat, acc, m_s, l_s, btab, obuf):
    b = pl.program_id(0)
    i = pl.program_id(1)
    nh, tq, _ = q_ref.shape
    tk = mask_ref.shape[3]
    r_lat = ckv_ref.shape[1]
    grp = ATT_HEAD_GROUP
    ng = nh // grp
    scale = HEAD_DIM ** -0.5 * LOG2E

    @pl.when((b == 0) & (i == 0))
    def _():
        r = lax.broadcasted_iota(I32, (tq, tk), 0)
        c = lax.broadcasted_iota(I32, (tq, tk), 1)
        for t in range(2):
            bk = _t5_bucket(r - c + t * tk)

            def fill(h, carry):
                far = relb_ref[N_BUCKETS - 1, h]
                v = jnp.zeros((tq, tk), F32)
                for k in range(N_BUCKETS):
                    v = jnp.where(bk == k, (relb_ref[k, h] - far) * LOG2E, v)
                btab[t, h] = v
                return carry

            lax.fori_loop(0, nh, fill, 0)

    def init(h, carry):
        qlat[h] = (_dot_nt(q_ref[h], wuk_ref[h]) * scale).astype(qlat.dtype)
        acc[h] = jnp.zeros(acc.shape[1:], F32)
        m_s[h] = jnp.full((tq, LANES), NEG, F32)
        l_s[h] = jnp.zeros((tq, LANES), F32)
        return carry

    lax.fori_loop(0, nh, init, 0)

    def lane_tile(v, n):
        return jnp.concatenate([v] * n, axis=1)

    def chunk(j, near):
        kc = ckv_ref[pl.ds(pl.multiple_of(j * tk, tk), tk), :]
        mk = mask_ref[0, j].astype(F32)

        def group(g):
            hs = slice(g * grp, (g + 1) * grp)
            s = _dot_nt(qlat[hs].reshape(grp * tq, r_lat), kc).reshape(grp, tq, tk) + mk
            if near:
                s = s + btab[i - j, hs]
            s = s.reshape(grp * tq, tk)
            m_old = m_s[hs].reshape(grp * tq, LANES)
            m_new = jnp.maximum(m_old, jnp.max(s, axis=1, keepdims=True))
            a = jnp.exp2(m_old - m_new)
            p = jnp.exp2(s - lane_tile(m_new, tk // LANES))
            l_new = a * l_s[hs].reshape(grp * tq, LANES) + jnp.sum(p, axis=1, keepdims=True)
            pv = _dot(p.astype(kc.dtype), kc)
            acc_new = lane_tile(a, r_lat // LANES) * acc[hs].reshape(grp * tq, r_lat) + pv
            acc[hs] = acc_new.reshape(grp, tq, r_lat)
            l_s[hs] = l_new.reshape(grp, tq, LANES)
            m_s[hs] = m_new.reshape(grp, tq, LANES)

        for g in range(ng):
            group(g)

    def far_chunk(j, carry):
        chunk(j, False)
        return carry

    def near_chunk(j, carry):
        chunk(j, True)
        return carry

    lax.fori_loop(0, jnp.maximum(i - 1, 0), far_chunk, 0)
    lax.fori_loop(jnp.maximum(i - 1, 0), i + 1, near_chunk, 0)

    def finish(h, ssq):
        o_lat = (acc[h] / lane_tile(l_s[h], r_lat // LANES)).astype(wuv_ref.dtype)
        o = _dot(o_lat, wuv_ref[h])
        obuf[h] = o
        return ssq + jnp.sum(o * o, axis=1, keepdims=True)

    ssq = lax.fori_loop(0, nh, finish, jnp.zeros((tq, 1), F32))
    rr = lax.rsqrt(ssq * (1.0 / (nh * HEAD_DIM)) + LN_EPS)
    for h in range(nh):
        o_ref[:, h * HEAD_DIM:(h + 1) * HEAD_DIM] = (
            obuf[h] * rr * og_ref[:, h * HEAD_DIM:(h + 1) * HEAD_DIM]).astype(o_ref.dtype)


def _latent_attention(q_hm, ckv_n, mask, w_uk, w_uv, rel_bias, out_g, bsz, seq):
    nh = q_hm.shape[0]
    tq = mask.shape[2]
    nq = seq // tq
    d_attn = nh * HEAD_DIM
    assert tq == mask.shape[3] and tq >= MAX_DIST
    return pl.pallas_call(
        _attn_kernel,
        grid=(bsz, nq),
        in_specs=[pl.BlockSpec(memory_space=pltpu.SMEM),
                  pl.BlockSpec((nh, tq, HEAD_DIM), lambda b, i: (0, b * nq + i, 0)),
                  pl.BlockSpec((seq, KV_RANK), lambda b, i: (b, 0)),
                  pl.BlockSpec((1, nq, tq, tq), lambda b, i: (b * nq + i, 0, 0, 0)),
                  pl.BlockSpec((nh, KV_RANK, HEAD_DIM), lambda b, i: (0, 0, 0)),
                  pl.BlockSpec((nh, KV_RANK, HEAD_DIM), lambda b, i: (0, 0, 0)),
                  pl.BlockSpec((1, d_attn), lambda b, i: (0, 0))],
        out_specs=pl.BlockSpec((tq, d_attn), lambda b, i: (b * nq + i, 0)),
        out_shape=jax.ShapeDtypeStruct((bsz * seq, d_attn), BF16),
        scratch_shapes=[pltpu.VMEM((nh, tq, KV_RANK), BF16),
                        pltpu.VMEM((nh, tq, KV_RANK), F32),
                        pltpu.VMEM((nh, tq, LANES), F32),
                        pltpu.VMEM((nh, tq, LANES), F32),
                        pltpu.VMEM((2, nh, tq, tq), F32),
                        pltpu.VMEM((nh, tq, HEAD_DIM), F32)],
        compiler_params=_cparams(("arbitrary", "arbitrary")),
        name="latent_attention",
    )(rel_bias, q_hm, ckv_n, mask, w_uk, w_uv, out_g.reshape(1, d_attn))


def _memattn_kernel(x_ref, kv_ref, wq_ref, wo_ref, g_ref, b_ref, wr_ref, o_ref, lg_ref):
    x = x_ref[...]
    d_mem = MEM_HEADS * MEM_HEAD_DIM
    q = (_dot(x.astype(BF16), wq_ref[...]) * (MEM_HEAD_DIM ** -0.5)).astype(BF16)
    outs = []
    for h in range(MEM_HEADS):
        lo = h * MEM_HEAD_DIM
        k = kv_ref[:, lo:lo + MEM_HEAD_DIM]
        v = kv_ref[:, d_mem + lo:d_mem + lo + MEM_HEAD_DIM]
        s = _dot_nt(q[:, lo:lo + MEM_HEAD_DIM], k)
        p = jnp.exp(s - jnp.max(s, axis=1, keepdims=True))
        p = p / jnp.sum(p, axis=1, keepdims=True)
        outs.append(_dot(p.astype(BF16), v).astype(BF16))
    o = jnp.concatenate(outs, axis=1)
    x2 = _ln_rows(ALPHA * x + _dot(o, wo_ref[...]), g_ref[...], b_ref[...])
    o_ref[...] = x2
    lg_ref[...] = _dot(x2.astype(BF16), wr_ref[...])


def _memory_attention(x1, kv, w_mq, w_mo, g, b, w_router, bsz, seq, tm):
    t, d = x1.shape
    nt = seq // tm
    mem_len = kv.shape[0] // bsz
    d_mem = w_mq.shape[1]
    return pl.pallas_call(
        _memattn_kernel,
        grid=(bsz, nt),
        in_specs=[pl.BlockSpec((tm, d), lambda bi, i: (bi * nt + i, 0)),
                  pl.BlockSpec((mem_len, 2 * d_mem), lambda bi, i: (bi, 0)),
                  pl.BlockSpec((d, d_mem), lambda bi, i: (0, 0)),
                  pl.BlockSpec((d_mem, d), lambda bi, i: (0, 0)),
                  pl.BlockSpec((1, d), lambda bi, i: (0, 0)),
                  pl.BlockSpec((1, d), lambda bi, i: (0, 0)),
                  pl.BlockSpec((d, LANES), lambda bi, i: (0, 0))],
        out_specs=[pl.BlockSpec((tm, d), lambda bi, i: (bi * nt + i, 0)),
                   pl.BlockSpec((tm, LANES), lambda bi, i: (bi * nt + i, 0))],
        out_shape=[jax.ShapeDtypeStruct((t, d), F32),
                   jax.ShapeDtypeStruct((t, LANES), F32)],
        compiler_params=_cparams(("parallel", "parallel")),
        name="memory_attention",
    )(x1, kv, w_mq, w_mo, g.reshape(1, d), b.reshape(1, d), w_router)


def _router_kernel(lg_ref, e1_ref, e2_ref, g1_ref, g2_ref):
    x = lg_ref[...]
    lane = lax.broadcasted_iota(I32, x.shape, 1)
    lane_f = lane.astype(F32)

    def argmax(mask):
        v = jnp.where(mask, x, -jnp.inf)
        mx = jnp.max(v, axis=1, keepdims=True)
        idx = jnp.min(jnp.where(mask & (v == mx), lane_f, float(LANES)), axis=1, keepdims=True)
        return mx, idx.astype(I32)

    gmask = lane < N_GROUPS
    gmax, gsel = argmax(gmask)
    gsum = jnp.sum(jnp.where(gmask, jnp.exp(x - gmax), 0.0), axis=1, keepdims=True)
    g_p = 1.0 / gsum
    lo = N_GROUPS + gsel * EXP_PER_GROUP
    emask = (lane >= lo) & (lane < lo + EXP_PER_GROUP)
    m1, i1 = argmax(emask)
    m2, i2 = argmax(emask & (lane != i1))
    esum = jnp.sum(jnp.where(emask, jnp.exp(x - m1), 0.0), axis=1, keepdims=True)
    p1 = 1.0 / esum
    p2 = jnp.exp(m2 - m1) / esum
    e1_ref[...] = i1 - N_GROUPS
    e2_ref[...] = i2 - N_GROUPS
    g1_ref[...] = g_p * (p1 / (p1 + p2))
    g2_ref[...] = g_p * (p2 / (p1 + p2))


def _router(logits, tm):
    t = logits.shape[0]
    col = pl.BlockSpec((tm, 1), lambda i: (i, 0))
    return pl.pallas_call(
        _router_kernel,
        grid=(t // tm,),
        in_specs=[pl.BlockSpec((tm, LANES), lambda i: (i, 0))],
        out_specs=[col, col, col, col],
        out_shape=[jax.ShapeDtypeStruct((t, 1), I32), jax.ShapeDtypeStruct((t, 1), I32),
                   jax.ShapeDtypeStruct((t, 1), F32), jax.ShapeDtypeStruct((t, 1), F32)],
        compiler_params=_cparams(("parallel",)),
        name="moe_router",
    )(logits)


HI16 = -65536


def _pack_halves(x):
    d = x.shape[1] // 2
    lo = pltpu.bitcast(x[:, :d].astype(BF16).astype(F32), I32)
    hi = pltpu.bitcast(x[:, d:].astype(BF16).astype(F32), I32)
    return lax.shift_right_logical(lo, 16) | (hi & HI16)


def _unpack_halves(u):
    return pltpu.bitcast(lax.shift_left(u, 16), F32), pltpu.bitcast(u & HI16, F32)


def _moe_rank_kernel(e1_ref, e2_ref, r1_ref, r2_ref, cnt_ref, base):
    i = pl.program_id(0)
    tm = e1_ref.shape[0]

    @pl.when(i == 0)
    def _():
        base[...] = jnp.zeros(base.shape, F32)

    lane = lax.broadcasted_iota(I32, (tm, LANES), 1)
    rr = lax.broadcasted_iota(I32, (tm, tm), 0)
    cc = lax.broadcasted_iota(I32, (tm, tm), 1)
    earlier = jnp.where(cc < rr, 1.0, 0.0).astype(BF16)
    for slot, (e_ref, r_ref) in enumerate(((e1_ref, r1_ref), (e2_ref, r2_ref))):
        oh = jnp.where(lane == e_ref[...], 1.0, 0.0)
        before = _dot(earlier, oh.astype(BF16)) + base[slot:slot + 1, :]
        r_ref[...] = jnp.sum(oh * before, axis=1, keepdims=True).astype(I32)
        base[slot:slot + 1, :] = base[slot:slot + 1, :] + jnp.sum(oh, axis=0, keepdims=True)
    cnt_ref[...] = base[...]


def _moe_rank(e1, e2, tm):
    t = e1.shape[0]
    col = pl.BlockSpec((tm, 1), lambda i: (i, 0))
    return pl.pallas_call(
        _moe_rank_kernel,
        grid=(t // tm,),
        in_specs=[col, col],
        out_specs=[col, col, pl.BlockSpec((8, LANES), lambda i: (0, 0))],
        out_shape=[jax.ShapeDtypeStruct((t, 1), I32), jax.ShapeDtypeStruct((t, 1), I32),
                   jax.ShapeDtypeStruct((8, LANES), F32)],
        scratch_shapes=[pltpu.VMEM((8, LANES), F32)],
        compiler_params=_cparams(("arbitrary",)),
        name="moe_rank",
    )(e1, e2)


def _moe_place_kernel(e1_ref, e2_ref, r1_ref, r2_ref, cnt_ref, p1_ref, p2_ref, be_ref, nu_ref):
    tm = e1_ref.shape[0]
    nbp = be_ref.shape[0]
    lane8 = lax.broadcasted_iota(I32, (8, LANES), 1)
    cnt = cnt_ref[...].astype(I32)
    c0 = jnp.broadcast_to(cnt[0:1], (8, LANES))
    c1 = jnp.broadcast_to(cnt[1:2], (8, LANES))
    blk_shift = MOE_BLOCK.bit_length() - 1
    padded = lax.shift_left(lax.shift_right_logical(c0 + c1 + (MOE_BLOCK - 1), blk_shift), blk_shift)
    pad_end = padded
    s = 1
    while s < LANES:
        pad_end = pad_end + jnp.where(lane8 >= s, pltpu.roll(pad_end, s, 1), 0)
        s *= 2
    start0 = (pad_end - padded).astype(F32)
    start1 = (pad_end - padded + c0).astype(F32)

    lane = lax.broadcasted_iota(I32, (tm, LANES), 1)
    for e_ref, r_ref, p_ref, start in ((e1_ref, r1_ref, p1_ref, start0), (e2_ref, r2_ref, p2_ref, start1)):
        seg = jnp.sum(jnp.where(lane == e_ref[...], start[0:1], 0.0), axis=1, keepdims=True)
        p_ref[...] = seg.astype(I32) + r_ref[...]

    block_row = lax.broadcasted_iota(I32, (nbp, LANES), 0) * MOE_BLOCK
    lane_b = lax.broadcasted_iota(I32, (nbp, LANES), 1)
    ended = (pad_end[0:1] <= block_row) & (lane_b < N_EXPERTS)
    be = jnp.sum(jnp.where(ended, 1.0, 0.0), axis=1, keepdims=True)
    be_ref[...] = jnp.minimum(be, N_EXPERTS - 1.0).astype(I32)
    total = jnp.max(pad_end, axis=1, keepdims=True)
    nu_ref[...] = jnp.broadcast_to(lax.shift_right_logical(total, blk_shift), (8, LANES))


def _moe_place(e1, e2, r1, r2, cnt, nb, tm):
    t = e1.shape[0]
    col = pl.BlockSpec((tm, 1), lambda i: (i, 0))
    return pl.pallas_call(
        _moe_place_kernel,
        grid=(t // tm,),
        in_specs=[col, col, col, col, pl.BlockSpec((8, LANES), lambda i: (0, 0))],
        out_specs=[col, col, pl.BlockSpec((nb, 1), lambda i: (0, 0)),
                   pl.BlockSpec((8, LANES), lambda i: (0, 0))],
        out_shape=[jax.ShapeDtypeStruct((t, 1), I32), jax.ShapeDtypeStruct((t, 1), I32),
                   jax.ShapeDtypeStruct((nb, 1), I32), jax.ShapeDtypeStruct((8, LANES), I32)],
        compiler_params=_cparams(("arbitrary",)),
        name="moe_place",
    )(e1, e2, r1, r2, cnt)


def _moe_dispatch_kernel(p1_ref, p2_ref, x_ref, xs_in, xs_hbm, pk, sem):
    del xs_in
    i = pl.program_id(0)
    n = pl.num_programs(0)
    slot = lax.rem(i, 2)
    blk = x_ref.shape[0]

    def row_copy(s, r, pos):
        return pltpu.make_async_copy(pk.at[s, pl.ds(r, 1)], xs_hbm.at[pl.ds(pos, 1)], sem.at[s])

    def drain(s):
        for _ in range(2):
            pltpu.make_async_copy(pk.at[s], xs_hbm.at[pl.ds(0, blk)], sem.at[s]).wait()

    @pl.when(i >= 2)
    def _():
        drain(slot)

    pk[slot] = _pack_halves(x_ref[...])

    def issue(r, c):
        row_copy(slot, r, p1_ref[0, 0, r]).start()
        row_copy(slot, r, p2_ref[0, 0, r]).start()
        return c

    lax.fori_loop(0, blk, issue, 0, unroll=8)

    @pl.when(i == n - 1)
    def _():
        drain(slot)

    @pl.when((i == n - 1) & (i >= 1))
    def _():
        drain(1 - slot)


def _moe_dispatch(x2, p1, p2, nb):
    t, d = x2.shape
    nt = t // MOE_BLOCK
    rows = nb * MOE_BLOCK
    pos = pl.BlockSpec((1, 1, MOE_BLOCK), lambda i: (i, 0, 0), memory_space=pltpu.SMEM)
    return pl.pallas_call(
        _moe_dispatch_kernel,
        grid=(nt,),
        in_specs=[pos, pos, pl.BlockSpec((MOE_BLOCK, d), lambda i: (i, 0)),
                  pl.BlockSpec(memory_space=pl.ANY)],
        out_specs=pl.BlockSpec(memory_space=pl.ANY),
        out_shape=jax.ShapeDtypeStruct((rows, d // 2), I32),
        input_output_aliases={3: 0},
        scratch_shapes=[pltpu.VMEM((2, MOE_BLOCK, d // 2), I32),
                        pltpu.SemaphoreType.DMA((2,))],
        compiler_params=_cparams(("arbitrary",)),
        name="moe_dispatch",
    )(p1.reshape(nt, 1, MOE_BLOCK), p2.reshape(nt, 1, MOE_BLOCK), x2, jnp.zeros((rows, d // 2), I32))


def _moe_ffn_kernel(be_ref, nu_ref, xs_ref, wg_ref, wu_ref, wd_ref, ys_ref):
    i = pl.program_id(0)

    @pl.when(i < nu_ref[0])
    def _():
        lo, hi = _unpack_halves(xs_ref[...])
        lo = lo.astype(BF16)
        hi = hi.astype(BF16)
        d2 = lo.shape[1]
        g = _dot(lo, wg_ref[0, :d2]) + _dot(hi, wg_ref[0, d2:])
        u = _dot(lo, wu_ref[0, :d2]) + _dot(hi, wu_ref[0, d2:])
        hmid = (g * jax.nn.sigmoid(g) * u).astype(BF16)
        ys_ref[...] = _pack_halves(_dot(hmid, wd_ref[0]))

    @pl.when(i >= nu_ref[0])
    def _():
        ys_ref[...] = jnp.zeros(ys_ref.shape, ys_ref.dtype)


def _moe_ffn(xs, w_gate, w_up, w_down, block_expert, n_used):
    rows, d2 = xs.shape
    nb = rows // MOE_BLOCK
    d = 2 * d2
    ff = w_gate.shape[2]
    grid_spec = pltpu.PrefetchScalarGridSpec(
        num_scalar_prefetch=2,
        grid=(nb,),
        in_specs=[pl.BlockSpec((MOE_BLOCK, d2), lambda i, be, nu: (i, 0)),
                  pl.BlockSpec((1, d, ff), lambda i, be, nu: (be[i], 0, 0)),
                  pl.BlockSpec((1, d, ff), lambda i, be, nu: (be[i], 0, 0)),
                  pl.BlockSpec((1, ff, d), lambda i, be, nu: (be[i], 0, 0))],
        out_specs=pl.BlockSpec((MOE_BLOCK, d2), lambda i, be, nu: (i, 0)),
    )
    return pl.pallas_call(
        _moe_ffn_kernel,
        grid_spec=grid_spec,
        out_shape=jax.ShapeDtypeStruct((rows, d2), I32),
        compiler_params=_cparams(("arbitrary",)),
        name="moe_ffn",
    )(block_expert, n_used, xs, w_gate, w_up, w_down)


def _moe_combine_kernel(p1_ref, p2_ref, q1_ref, q2_ref, x_ref, g1_ref, g2_ref, g_ref, b_ref, ys_hbm,
                        o_ref, yb, sem):
    i = pl.program_id(0)
    n = pl.num_programs(0)
    slot = lax.rem(i, 2)
    blk = x_ref.shape[0]

    def row_copy(s, k, r, pos):
        return pltpu.make_async_copy(ys_hbm.at[pl.ds(pos, 1)], yb.at[s, k, pl.ds(r, 1)], sem.at[s])

    def fetch(s, a_ref, b_ref2):
        def body(r, c):
            row_copy(s, 0, r, a_ref[0, 0, r]).start()
            row_copy(s, 1, r, b_ref2[0, 0, r]).start()
            return c
        lax.fori_loop(0, blk, body, 0, unroll=8)

    @pl.when(i == 0)
    def _():
        fetch(0, p1_ref, p2_ref)

    @pl.when(i + 1 < n)
    def _():
        fetch(1 - slot, q1_ref, q2_ref)

    for k in range(2):
        pltpu.make_async_copy(ys_hbm.at[pl.ds(0, blk)], yb.at[slot, k], sem.at[slot]).wait()
    y1 = jnp.concatenate(_unpack_halves(yb[slot, 0]), axis=1)
    y2 = jnp.concatenate(_unpack_halves(yb[slot, 1]), axis=1)
    y = y1 * g1_ref[...] + y2 * g2_ref[...]
    o_ref[...] = _ln_rows(ALPHA * x_ref[...] + y, g_ref[...], b_ref[...])


def _moe_combine(x2, ys, p1, p2, g1, g2, g, b):
    t, d = x2.shape
    nt = t // MOE_BLOCK
    p1 = p1.reshape(nt, 1, MOE_BLOCK)
    p2 = p2.reshape(nt, 1, MOE_BLOCK)
    pos = pl.BlockSpec((1, 1, MOE_BLOCK), lambda i: (i, 0, 0), memory_space=pltpu.SMEM)
    nxt = pl.BlockSpec((1, 1, MOE_BLOCK), lambda i: (jnp.minimum(i + 1, nt - 1), 0, 0),
                       memory_space=pltpu.SMEM)
    col = pl.BlockSpec((MOE_BLOCK, 1), lambda i: (i, 0))
    vec = pl.BlockSpec((1, d), lambda i: (0, 0))
    return pl.pallas_call(
        _moe_combine_kernel,
        grid=(nt,),
        in_specs=[pos, pos, nxt, nxt, pl.BlockSpec((MOE_BLOCK, d), lambda i: (i, 0)), col, col, vec, vec,
                  pl.BlockSpec(memory_space=pl.ANY)],
        out_specs=pl.BlockSpec((MOE_BLOCK, d), lambda i: (i, 0)),
        out_shape=jax.ShapeDtypeStruct((t, d), F32),
        scratch_shapes=[pltpu.VMEM((2, 2, MOE_BLOCK, d // 2), I32),
                        pltpu.SemaphoreType.DMA((2,))],
        compiler_params=_cparams(("arbitrary",)),
        name="moe_combine",
    )(p1, p2, p1, p2, x2, g1, g2, g.reshape(1, d), b.reshape(1, d), ys)


def _tile(n, pref):
    return pref if n % pref == 0 else n


def _layer(x, mem, w_in, conv_w, conv_b, conv_ln_g, conv_ln_b, kv_norm_g, w_uk, w_uv, rel_bias,
           conv_out_g, attn_out_g, w_out, ln1_g, ln1_b, w_mq, w_mk, w_mv, w_mo, ln2_g, ln2_b,
           w_router_grp, w_router_exp, w_gate, w_up, w_down, ln3_g, ln3_b):
    bsz, seq, d = x.shape
    t = bsz * seq
    d_conv = conv_w.shape[1]
    d_attn = N_HEADS * HEAD_DIM
    c_glu = 2 * d_conv
    c_qi = H_IDX * D_IDX
    o_q, o_kv = c_glu, c_glu + d_attn
    o_qi = o_kv + KV_RANK
    o_ki = o_qi + c_qi
    topk = min(TOPK_MAX, seq // 4)

    xf = x.reshape(t, d)
    xb = xf.astype(BF16)
    w_inb = w_in.astype(BF16)
    tail_w = jnp.concatenate([w_inb[:, o_kv:o_qi], w_inb[:, o_ki:]], axis=1)
    tail_w = jnp.pad(tail_w, ((0, 0), (0, KV_RANK + LANES - tail_w.shape[1])))
    tm = _tile(t, 1024)

    u = _matmul(xb, w_inb[:, :c_glu], BF16, tm, 512, "proj_glu")
    q_hm = _matmul_heads(xb, w_inb[:, o_q:o_kv], BF16, tm, 512, "proj_q")
    qi_hm = _matmul_heads(xb, w_inb[:, o_qi:o_ki], BF16, tm, 512, "proj_qidx")
    tail = _matmul(xb, tail_w, F32, tm, KV_RANK + LANES, "proj_tail")

    conv_n = _conformer_conv(u, bsz, seq, conv_w, conv_b, conv_ln_g, conv_ln_b, conv_out_g)
    ckv_n, kia, kib, kw = _prep_latent(tail, kv_norm_g, tm)
    mask = _indexer_mask(qi_hm, kw, kia, kib, bsz, seq, topk)
    w_ukb = w_uk.astype(BF16)
    w_uvb = w_uv.astype(BF16)
    attn_n = _latent_attention(q_hm, ckv_n, mask, w_ukb, w_uvb, rel_bias, attn_out_g, bsz, seq)

    w_outb = w_out.astype(BF16)
    pre1 = _matmul2_residual(conv_n, attn_n, w_outb[:d_conv], w_outb[d_conv:], xf, tm, 512, "out_proj")
    x1 = _layer_norm(pre1, ln1_g, ln1_b, 256, "ln1")

    mem_len = mem.shape[1]
    memb = mem.reshape(bsz * mem_len, d).astype(BF16)
    w_kv = jnp.concatenate([w_mk, w_mv], axis=1).astype(BF16)
    kv = _matmul(memb, w_kv, BF16, _tile(bsz * mem_len, 512), 512, "mem_kv")
    w_router = jnp.concatenate([w_router_grp, w_router_exp], axis=1)
    w_router = jnp.pad(w_router, ((0, 0), (0, LANES - w_router.shape[1]))).astype(BF16)
    x2, logits = _memory_attention(x1, kv, w_mq.astype(BF16), w_mo.astype(BF16), ln2_g, ln2_b,
                                   w_router, bsz, seq, 256)

    e1, e2, g1, g2 = _router(logits, _tile(t, 1024))
    nb = (2 * t + N_EXPERTS * (MOE_BLOCK - 1) + MOE_BLOCK - 1) // MOE_BLOCK
    r1, r2, cnt = _moe_rank(e1, e2, 512)
    p1, p2, block_expert, n_used = _moe_place(e1, e2, r1, r2, cnt, nb, 512)
    xs = _moe_dispatch(x2, p1, p2, nb)
    ys = _moe_ffn(xs, w_gate.astype(BF16), w_up.astype(BF16), w_down.astype(BF16),
                  block_expert.reshape(nb), n_used[0, 0:1])
    x3 = _moe_combine(x2, ys, p1, p2, g1, g2, ln3_g, ln3_b)
    return x3.reshape(bsz, seq, d)


def kernel(x, mem, w_in, conv_w, conv_b, conv_ln_g, conv_ln_b, kv_norm_g, w_uk, w_uv, rel_bias, conv_out_g, attn_out_g, w_out, ln1_g, ln1_b, w_mq, w_mk, w_mv, w_mo, ln2_g, ln2_b, w_router_grp, w_router_exp, w_gate, w_up, w_down, ln3_g, ln3_b):
    for l in range(w_in.shape[0]):
        x = _layer(x, mem, w_in[l], conv_w[l], conv_b[l], conv_ln_g[l], conv_ln_b[l], kv_norm_g[l],
                   w_uk[l], w_uv[l], rel_bias, conv_out_g[l], attn_out_g[l], w_out[l], ln1_g[l], ln1_b[l],
                   w_mq[l], w_mk[l], w_mv[l], w_mo[l], ln2_g[l], ln2_b[l], w_router_grp[l],
                   w_router_exp[l], w_gate[l], w_up[l], w_down[l], ln3_g[l], ln3_b[l])
    return x
```

```python
import functools
import math

import jax
import jax.numpy as jnp
from jax import lax
from jax.experimental import pallas as pl
from jax.experimental.pallas import tpu as pltpu

F32 = jnp.float32
BF16 = jnp.bfloat16
I32 = jnp.int32

DEPTH = 1
CONV_WIDTH = 31
N_HEADS = 16
HEAD_DIM = 128
KV_RANK = 512
H_IDX = 32
D_IDX = 64
TOPK_MAX = 256
N_BUCKETS = 32
MAX_DIST = 128
MEM_HEADS = 4
MEM_HEAD_DIM = 128
N_GROUPS = 8
EXP_PER_GROUP = 8
N_EXPERTS = N_GROUPS * EXP_PER_GROUP
MOE_BLOCK = 128
ALPHA = (2.0 * DEPTH) ** 0.25
LN_EPS = 1e-5

LANES = 128
SUBLANES = 8
V7X_VMEM_BYTES = 64 * 1024 * 1024
VMEM_LIMIT = 56 * 1024 * 1024
NEG = -1e30
INT_MIN = -(2 ** 31)

LOG2E = 1.4426950408889634

ATT_BLOCK = 256
ATT_HEAD_GROUP = 1
CONV_TS = 256
CONV_HALO = 32
CONV_CC = 256
CONV_RC = 32


def _cparams(sem):
    return pltpu.CompilerParams(dimension_semantics=sem, vmem_limit_bytes=VMEM_LIMIT)


def _dot(a, b):
    return jnp.dot(a, b, preferred_element_type=F32)


def _dot_nt(a, b):
    return lax.dot_general(a, b, (((1,), (1,)), ((), ())), preferred_element_type=F32)


def _mm_kernel(a_ref, b_ref, o_ref):
    o_ref[...] = _dot(a_ref[...], b_ref[...]).astype(o_ref.dtype)


def _matmul(a, b, out_dtype, tm, tn, name):
    m, k = a.shape
    n = b.shape[1]
    return pl.pallas_call(
        _mm_kernel,
        grid=(m // tm, n // tn),
        in_specs=[pl.BlockSpec((tm, k), lambda i, j: (i, 0)),
                  pl.BlockSpec((k, tn), lambda i, j: (0, j))],
        out_specs=pl.BlockSpec((tm, tn), lambda i, j: (i, j)),
        out_shape=jax.ShapeDtypeStruct((m, n), out_dtype),
        compiler_params=_cparams(("parallel", "parallel")),
        name=name,
    )(a, b)


def _mm_heads_kernel(a_ref, b_ref, o_ref):
    r = _dot(a_ref[...], b_ref[...])
    for p in range(o_ref.shape[0]):
        o_ref[p] = r[:, p * LANES:(p + 1) * LANES].astype(o_ref.dtype)


def _matmul_heads(a, b, out_dtype, tm, tn, name):
    m, k = a.shape
    n = b.shape[1]
    return pl.pallas_call(
        _mm_heads_kernel,
        grid=(m // tm, n // tn),
        in_specs=[pl.BlockSpec((tm, k), lambda i, j: (i, 0)),
                  pl.BlockSpec((k, tn), lambda i, j: (0, j))],
        out_specs=pl.BlockSpec((tn // LANES, tm, LANES), lambda i, j: (j, i, 0)),
        out_shape=jax.ShapeDtypeStruct((n // LANES, m, LANES), out_dtype),
        compiler_params=_cparams(("parallel", "parallel")),
        name=name,
    )(a, b)


def _mm2_res_kernel(a1_ref, a2_ref, w1_ref, w2_ref, r_ref, o_ref):
    o_ref[...] = (ALPHA * r_ref[...] + _dot(a1_ref[...], w1_ref[...])
                  + _dot(a2_ref[...], w2_ref[...]))


def _matmul2_residual(a1, a2, w1, w2, res, tm, tn, name):
    m, k1 = a1.shape
    k2 = a2.shape[1]
    n = w1.shape[1]
    return pl.pallas_call(
        _mm2_res_kernel,
        grid=(m // tm, n // tn),
        in_specs=[pl.BlockSpec((tm, k1), lambda i, j: (i, 0)),
                  pl.BlockSpec((tm, k2), lambda i, j: (i, 0)),
                  pl.BlockSpec((k1, tn), lambda i, j: (0, j)),
                  pl.BlockSpec((k2, tn), lambda i, j: (0, j)),
                  pl.BlockSpec((tm, tn), lambda i, j: (i, j))],
        out_specs=pl.BlockSpec((tm, tn), lambda i, j: (i, j)),
        out_shape=jax.ShapeDtypeStruct((m, n), F32),
        compiler_params=_cparams(("parallel", "parallel")),
        name=name,
    )(a1, a2, w1, w2, res)


def _ln_rows(x, g, b):
    mu = jnp.mean(x, axis=-1, keepdims=True)
    xc = x - mu
    var = jnp.mean(xc * xc, axis=-1, keepdims=True)
    return xc * lax.rsqrt(var + LN_EPS) * g + b


def _ln_kernel(x_ref, g_ref, b_ref, o_ref):
    o_ref[...] = _ln_rows(x_ref[...], g_ref[...], b_ref[...])


def _layer_norm(x, g, b, tm, name):
    m, d = x.shape
    return pl.pallas_call(
        _ln_kernel,
        grid=(m // tm,),
        in_specs=[pl.BlockSpec((tm, d), lambda i: (i, 0)),
                  pl.BlockSpec((1, d), lambda i: (0, 0)),
                  pl.BlockSpec((1, d), lambda i: (0, 0))],
        out_specs=pl.BlockSpec((tm, d), lambda i: (i, 0)),
        out_shape=jax.ShapeDtypeStruct((m, d), F32),
        compiler_params=_cparams(("parallel",)),
        name=name,
    )(x, g.reshape(1, d), b.reshape(1, d))


def _conv_kernel(a_ref, g_ref, cw_ref, cb_ref, lg_ref, lb_ref, og_ref, o_ref, hbuf, ybuf, hs):
    ts = a_ref.shape[0]
    nch = hbuf.shape[0]
    cc = hbuf.shape[2]
    d_conv = nch * cc

    nrow = CONV_HALO + ts

    @pl.when(pl.program_id(1) == 0)
    def _():
        hbuf[:, 0:CONV_HALO, :] = jnp.zeros((nch, CONV_HALO, cc), F32)
        hbuf[:, nrow:nrow + SUBLANES, :] = jnp.zeros((nch, SUBLANES, cc), F32)

    for c in range(nch):
        a = a_ref[:, c * cc:(c + 1) * cc].astype(F32)
        g = g_ref[:, c * cc:(c + 1) * cc].astype(F32)
        hbuf[c, CONV_HALO:nrow, :] = a * jax.nn.sigmoid(g)

    first = CONV_HALO - (CONV_WIDTH - 1)

    def chunk_body(c, carry):
        for o in range(1, SUBLANES):
            hs[o - 1] = hbuf[c, o:o + nrow, :]
        for r0 in range(0, ts, CONV_RC):
            acc = jnp.zeros((CONV_RC, cc), F32)
            for j in range(CONV_WIDTH):
                o = (first + j) % SUBLANES
                base = r0 + first + j - o
                rows = hbuf[c, base:base + CONV_RC, :] if o == 0 else hs[o - 1, base:base + CONV_RC, :]
                acc = acc + cw_ref[c, j:j + 1, :] * rows
            ybuf[c, r0:r0 + CONV_RC, :] = acc + cb_ref[c]
        hbuf[c, 0:CONV_HALO, :] = hbuf[c, ts:nrow, :]
        return carry

    lax.fori_loop(0, nch, chunk_body, 0)

    s1 = jnp.zeros((ts, 1), F32)
    for c in range(nch):
        s1 = s1 + jnp.sum(ybuf[c], axis=1, keepdims=True)
    mu = s1 * (1.0 / d_conv)
    s2 = jnp.zeros((ts, 1), F32)
    for c in range(nch):
        yc = ybuf[c] - mu
        s2 = s2 + jnp.sum(yc * yc, axis=1, keepdims=True)
    rstd = lax.rsqrt(s2 * (1.0 / d_conv) + LN_EPS)
    s3 = jnp.zeros((ts, 1), F32)
    for c in range(nch):
        z = (ybuf[c] - mu) * rstd * lg_ref[:, c * cc:(c + 1) * cc] + lb_ref[:, c * cc:(c + 1) * cc]
        z = z * jax.nn.sigmoid(z)
        ybuf[c] = z
        s3 = s3 + jnp.sum(z * z, axis=1, keepdims=True)
    rr = lax.rsqrt(s3 * (1.0 / d_conv) + LN_EPS)
    for c in range(nch):
        o_ref[:, c * cc:(c + 1) * cc] = (ybuf[c] * rr * og_ref[:, c * cc:(c + 1) * cc]).astype(o_ref.dtype)


def _conformer_conv(u, bsz, seq, conv_w, conv_b, ln_g, ln_b, out_g):
    d_conv = u.shape[1] // 2
    ts = min(CONV_TS, seq)
    nch = d_conv // CONV_CC
    nt = seq // ts
    cw = conv_w.reshape(CONV_WIDTH, nch, CONV_CC).transpose(1, 0, 2)
    cb = conv_b.reshape(nch, 1, CONV_CC)
    vec = pl.BlockSpec((1, d_conv), lambda b, i: (0, 0))
    return pl.pallas_call(
        _conv_kernel,
        grid=(bsz, nt),
        in_specs=[pl.BlockSpec((ts, d_conv), lambda b, i: (b * nt + i, 0)),
                  pl.BlockSpec((ts, d_conv), lambda b, i: (b * nt + i, 1)),
                  pl.BlockSpec((nch, CONV_WIDTH, CONV_CC), lambda b, i: (0, 0, 0)),
                  pl.BlockSpec((nch, 1, CONV_CC), lambda b, i: (0, 0, 0)),
                  vec, vec, vec],
        out_specs=pl.BlockSpec((ts, d_conv), lambda b, i: (b * nt + i, 0)),
        out_shape=jax.ShapeDtypeStruct((bsz * seq, d_conv), BF16),
        scratch_shapes=[pltpu.VMEM((nch, CONV_HALO + ts + SUBLANES, CONV_CC), F32),
                        pltpu.VMEM((nch, ts, CONV_CC), F32),
                        pltpu.VMEM((SUBLANES - 1, CONV_HALO + ts, CONV_CC), F32)],
        compiler_params=_cparams(("arbitrary", "arbitrary")),
        name="conformer_conv",
    )(u, u, cw, cb, ln_g.reshape(1, d_conv), ln_b.reshape(1, d_conv), out_g.reshape(1, d_conv))


def _prep_kernel(t_ref, g_ref, ckv_ref, ckvt_ref, kia_ref, kib_ref, kw_ref):
    ckv = t_ref[:, 0:KV_RANK]
    ms = jnp.mean(ckv * ckv, axis=-1, keepdims=True)
    ckv_n = ckv * lax.rsqrt(ms + LN_EPS) * g_ref[...]
    ckv_ref[...] = ckv_n.astype(ckv_ref.dtype)
    tk = ckvt_ref.shape[3]
    for c in range(ckvt_ref.shape[1]):
        ckvt_ref[0, c] = ckv_n[c * tk:(c + 1) * tk, :].T.astype(ckvt_ref.dtype)
    kw = t_ref[:, KV_RANK:KV_RANK + LANES]
    kw_ref[...] = kw
    lane = lax.broadcasted_iota(I32, kw.shape, 1)
    kia_ref[...] = jnp.where(lane < D_IDX, kw, 0.0).astype(kia_ref.dtype)
    kib_ref[...] = jnp.where(lane >= D_IDX, pltpu.roll(kw, D_IDX, 1), 0.0).astype(kib_ref.dtype)


def _prep_latent(tail, kv_norm_g, tm, tk):
    m, w = tail.shape
    return pl.pallas_call(
        _prep_kernel,
        grid=(m // tm,),
        in_specs=[pl.BlockSpec((tm, w), lambda i: (i, 0)),
                  pl.BlockSpec((1, KV_RANK), lambda i: (0, 0))],
        out_specs=[pl.BlockSpec((tm, KV_RANK), lambda i: (i, 0)),
                   pl.BlockSpec((1, tm // tk, KV_RANK, tk), lambda i: (i, 0, 0, 0)),
                   pl.BlockSpec((tm, LANES), lambda i: (i, 0)),
                   pl.BlockSpec((tm, LANES), lambda i: (i, 0)),
                   pl.BlockSpec((tm, LANES), lambda i: (i, 0))],
        out_shape=[jax.ShapeDtypeStruct((m, KV_RANK), BF16),
                   jax.ShapeDtypeStruct((m // tm, tm // tk, KV_RANK, tk), BF16),
                   jax.ShapeDtypeStruct((m, LANES), BF16),
                   jax.ShapeDtypeStruct((m, LANES), BF16),
                   jax.ShapeDtypeStruct((m, LANES), F32)],
        compiler_params=_cparams(("parallel",)),
        name="prep_latent",
    )(tail, kv_norm_g.reshape(1, KV_RANK))


def _indexer_kernel(qi_ref, kw_ref, kia_ref, kib_ref, o_ref, keybuf, wb, *, topk):
    i = pl.program_id(1)
    npairs, tq, _ = qi_ref.shape
    nk = o_ref.shape[1]
    tk = o_ref.shape[3]
    half = tk // 2
    kf = float(topk)
    group = 4

    for h in range(H_IDX):
        wb[h] = jnp.broadcast_to(kw_ref[:, D_IDX + h:D_IDX + h + 1], (tq, LANES))

    row = lax.broadcasted_iota(I32, (tq, tk), 0) + i * tq
    col0 = lax.broadcasted_iota(I32, (tq, tk), 1)

    def score_chunk(j, carry):
        k0 = pl.multiple_of(j * tk, tk)
        kd = jnp.concatenate([kia_ref[pl.ds(k0, tk), :], kib_ref[pl.ds(k0, tk), :]], axis=0)
        acc = jnp.zeros((tq, tk), F32)
        for p0 in range(0, npairs, group):
            lhs = qi_ref[p0:p0 + group].reshape(group * tq, LANES)
            zz = _dot_nt(lhs, kd)
            for p in range(group):
                h = 2 * (p0 + p)
                z = zz[p * tq:(p + 1) * tq]
                we = jnp.concatenate([wb[h]] * (tk // LANES), axis=1)
                wo = jnp.concatenate([wb[h + 1]] * (tk // LANES), axis=1)
                acc = acc + we * jnp.maximum(z[:, 0:tk], 0.0) + wo * jnp.maximum(z[:, tk:2 * tk], 0.0)
        bits = pltpu.bitcast(acc, I32)
        key = jnp.where(bits >= 0, bits, bits ^ jnp.int32(0x7FFFFFFF))
        keybuf[j] = jnp.where(col0 + j * tk <= row, key, INT_MIN)
        return carry

    lax.fori_loop(0, i + 1, score_chunk, 0)

    rows_per_pass = 128
    lane = lax.broadcasted_iota(I32, (rows_per_pass, LANES), 1)

    def count(pred):
        parts = []
        for r0 in range(0, tq, rows_per_pass):
            rows = slice(r0, r0 + rows_per_pass)

            def body(j, c):
                for s in range(tk // LANES):
                    kk = keybuf[j, rows, s * LANES:(s + 1) * LANES]
                    c = c + jnp.where(pred(kk, lane + (j * tk + s * LANES), rows), 1.0, 0.0)
                return c

            parts.append(lax.fori_loop(0, i + 1, body, jnp.zeros((rows_per_pass, LANES), F32)))
        return jnp.concatenate(
            [jnp.broadcast_to(jnp.sum(c, axis=1, keepdims=True), (rows_per_pass, LANES)) for c in parts], axis=0)

    def count_ge(cand):
        return count(lambda kk, col, rows: kk >= cand[rows])

    tau = jnp.where(count_ge(jnp.zeros((tq, LANES), I32)) >= kf, 0, INT_MIN).astype(I32)

    def bit_body(it, tau):
        cand = tau | jnp.left_shift(jnp.int32(1), 30 - it)
        return jnp.where(count_ge(cand) >= kf, cand, tau)

    tau = lax.fori_loop(0, 31, bit_body, tau)
    n_ge = count_ge(tau)
    n_gt = count(lambda kk, col, rows: kk > tau[rows])

    def tie_cut():
        need = kf - n_gt

        def cut_body(it, cut):
            cand = cut + jnp.left_shift(jnp.int32(1), 30 - it)
            below = count(lambda kk, col, rows: (kk == tau[rows]) & (col < cand[rows]))
            return jnp.where(below < need, cand, cut)

        return lax.fori_loop(0, 31, cut_body, jnp.zeros((tq, LANES), I32))

    has_ties = jnp.max(n_ge) > kf
    cut = lax.cond(has_ties, tie_cut, lambda: jnp.full((tq, LANES), 2 ** 30, I32))
    taub = jnp.concatenate([tau] * (tk // LANES), axis=1)
    cutb = jnp.concatenate([cut] * (tk // LANES), axis=1)

    def write_chunk(j, carry):
        kk = keybuf[j]
        sel = ((kk > taub) | ((kk == taub) & (col0 + j * tk <= cutb))) & (kk != INT_MIN)
        o_ref[0, j] = jnp.where(sel, 0.0, NEG).astype(o_ref.dtype)
        return carry

    lax.fori_loop(0, i + 1, write_chunk, 0)

    def write_rest(j, carry):
        o_ref[0, j] = jnp.full((tq, tk), NEG, o_ref.dtype)
        return carry

    lax.fori_loop(i + 1, nk, write_rest, 0)


def _indexer_mask(qi_hm, kw, kia, kib, bsz, seq, topk):
    tq = min(ATT_BLOCK, seq)
    nq = seq // tq
    npairs = qi_hm.shape[0]
    return pl.pallas_call(
        functools.partial(_indexer_kernel, topk=topk),
        grid=(bsz, nq),
        in_specs=[pl.BlockSpec((npairs, tq, LANES), lambda b, i: (0, b * nq + i, 0)),
                  pl.BlockSpec((tq, LANES), lambda b, i: (b * nq + i, 0)),
                  pl.BlockSpec((seq, LANES), lambda b, i: (b, 0)),
                  pl.BlockSpec((seq, LANES), lambda b, i: (b, 0))],
        out_specs=pl.BlockSpec((1, nq, tq, tq), lambda b, i: (b * nq + i, 0, 0, 0)),
        out_shape=jax.ShapeDtypeStruct((bsz * nq, nq, tq, tq), BF16),
        scratch_shapes=[pltpu.VMEM((nq, tq, tq), I32),
                        pltpu.VMEM((H_IDX, tq, LANES), F32)],
        compiler_params=_cparams(("parallel", "parallel")),
        name="indexer_mask",
    )(qi_hm, kw, kia, kib)


def _t5_bucket(dist):
    n = jnp.maximum(dist, 0)
    max_exact = N_BUCKETS // 2
    nf = jnp.maximum(n, 1).astype(F32)
    large = max_exact + (jnp.log(nf / max_exact) / math.log(MAX_DIST / max_exact)
                         * (N_BUCKETS - max_exact)).astype(I32)
    large = jnp.minimum(large, N_BUCKETS - 1)
    return jnp.where(n < max_exact, n, large)


def _attn_kernel(relb_ref, q_ref, ckv_ref, ckvt_ref, mask_ref, wuk_ref, wuv_ref, og_ref, o_ref,
                 qlat, acc, m_s, l_s, btab, obuf):
    b = pl.program_id(0)
    i = pl.program_id(1)
    nh, tq, _ = q_ref.shape
    tk = mask_ref.shape[3]
    r_lat = ckv_ref.shape[1]
    grp = ATT_HEAD_GROUP
    ng = nh // grp
    scale = HEAD_DIM ** -0.5 * LOG2E

    @pl.when((b == 0) & (i == 0))
    def _():
        r = lax.broadcasted_iota(I32, (tq, tk), 0)
        c = lax.broadcasted_iota(I32, (tq, tk), 1)
        for t in range(2):
            bk = _t5_bucket(r - c + t * tk)

            def fill(h, carry):
                far = relb_ref[N_BUCKETS - 1, h]
                v = jnp.zeros((tq, tk), F32)
                for k in range(N_BUCKETS):
                    v = jnp.where(bk == k, (relb_ref[k, h] - far) * LOG2E, v)
                btab[t, h] = v
                return carry

            lax.fori_loop(0, nh, fill, 0)

    def init(h, carry):
        qlat[h] = (_dot_nt(q_ref[h], wuk_ref[h]) * scale).astype(qlat.dtype)
        acc[h] = jnp.zeros(acc.shape[1:], F32)
        m_s[h] = jnp.full((tq, LANES), NEG, F32)
        l_s[h] = jnp.zeros((tq, LANES), F32)
        return carry

    lax.fori_loop(0, nh, init, 0)

    def lane_tile(v, n):
        return jnp.concatenate([v] * n, axis=1)

    def chunk(j, near):
        kc = ckv_ref[pl.ds(pl.multiple_of(j * tk, tk), tk), :]
        kct = ckvt_ref[0, j]
        mk = mask_ref[0, j].astype(F32)

        def group(g):
            hs = slice(g * grp, (g + 1) * grp)
            s = _dot(qlat[hs].reshape(grp * tq, r_lat), kct).reshape(grp, tq, tk) + mk
            if near:
                s = s + btab[i - j, hs]
            s = s.reshape(grp * tq, tk)
            m_old = m_s[hs].reshape(grp * tq, LANES)
            m_new = jnp.maximum(m_old, jnp.max(s, axis=1, keepdims=True))
            a = jnp.exp2(m_old - m_new)
            p = jnp.exp2(s - lane_tile(m_new, tk // LANES))
            l_new = a * l_s[hs].reshape(grp * tq, LANES) + jnp.sum(p, axis=1, keepdims=True)
            pv = _dot(p.astype(kc.dtype), kc)
            acc_new = lane_tile(a, r_lat // LANES) * acc[hs].reshape(grp * tq, r_lat) + pv
            acc[hs] = acc_new.reshape(grp, tq, r_lat)
            l_s[hs] = l_new.reshape(grp, tq, LANES)
            m_s[hs] = m_new.reshape(grp, tq, LANES)

        for g in range(ng):
            group(g)

    def far_chunk(j, carry):
        chunk(j, False)
        return carry

    def near_chunk(j, carry):
        chunk(j, True)
        return carry

    lax.fori_loop(0, jnp.maximum(i - 1, 0), far_chunk, 0)
    lax.fori_loop(jnp.maximum(i - 1, 0), i + 1, near_chunk, 0)

    def finish(h, ssq):
        o_lat = (acc[h] / lane_tile(l_s[h], r_lat // LANES)).astype(wuv_ref.dtype)
        o = _dot(o_lat, wuv_ref[h])
        obuf[h] = o
        return ssq + jnp.sum(o * o, axis=1, keepdims=True)

    ssq = lax.fori_loop(0, nh, finish, jnp.zeros((tq, 1), F32))
    rr = lax.rsqrt(ssq * (1.0 / (nh * HEAD_DIM)) + LN_EPS)
    for h in range(nh):
        o_ref[:, h * HEAD_DIM:(h + 1) * HEAD_DIM] = (
            obuf[h] * rr * og_ref[:, h * HEAD_DIM:(h + 1) * HEAD_DIM]).astype(o_ref.dtype)


def _latent_attention(q_hm, ckv_n, ckv_t, mask, w_uk, w_uv, rel_bias, out_g, bsz, seq):
    nh = q_hm.shape[0]
    tq = mask.shape[2]
    nq = seq // tq
    d_attn = nh * HEAD_DIM
    assert tq == mask.shape[3] and tq >= MAX_DIST
    once = pl.Buffered(1)
    return pl.pallas_call(
        _attn_kernel,
        grid=(bsz, nq),
        in_specs=[pl.BlockSpec(memory_space=pltpu.SMEM),
                  pl.BlockSpec((nh, tq, HEAD_DIM), lambda b, i: (0, b * nq + i, 0)),
                  pl.BlockSpec((seq, KV_RANK), lambda b, i: (b, 0), pipeline_mode=once),
                  pl.BlockSpec((1, nq, KV_RANK, tq), lambda b, i: (b, 0, 0, 0), pipeline_mode=once),
                  pl.BlockSpec((1, nq, tq, tq), lambda b, i: (b * nq + i, 0, 0, 0)),
                  pl.BlockSpec((nh, KV_RANK, HEAD_DIM), lambda b, i: (0, 0, 0), pipeline_mode=once),
                  pl.BlockSpec((nh, KV_RANK, HEAD_DIM), lambda b, i: (0, 0, 0), pipeline_mode=once),
                  pl.BlockSpec((1, d_attn), lambda b, i: (0, 0))],
        out_specs=pl.BlockSpec((tq, d_attn), lambda b, i: (b * nq + i, 0)),
        out_shape=jax.ShapeDtypeStruct((bsz * seq, d_attn), BF16),
        scratch_shapes=[pltpu.VMEM((nh, tq, KV_RANK), BF16),
                        pltpu.VMEM((nh, tq, KV_RANK), F32),
                        pltpu.VMEM((nh, tq, LANES), F32),
                        pltpu.VMEM((nh, tq, LANES), F32),
                        pltpu.VMEM((2, nh, tq, tq), F32),
                        pltpu.VMEM((nh, tq, HEAD_DIM), F32)],
        compiler_params=_cparams(("arbitrary", "arbitrary")),
        name="latent_attention",
    )(rel_bias, q_hm, ckv_n, ckv_t, mask, w_uk, w_uv, out_g.reshape(1, d_attn))


def _memattn_kernel(x_ref, g1_ref, b1_ref, kv_ref, wq_ref, wo_ref, g_ref, b_ref, wr_ref, o_ref, lg_ref):
    x = _ln_rows(x_ref[...], g1_ref[...], b1_ref[...])
    d_mem = MEM_HEADS * MEM_HEAD_DIM
    q = (_dot(x.astype(BF16), wq_ref[...]) * (MEM_HEAD_DIM ** -0.5)).astype(BF16)
    outs = []
    for h in range(MEM_HEADS):
        lo = h * MEM_HEAD_DIM
        k = kv_ref[:, lo:lo + MEM_HEAD_DIM]
        v = kv_ref[:, d_mem + lo:d_mem + lo + MEM_HEAD_DIM]
        s = _dot_nt(q[:, lo:lo + MEM_HEAD_DIM], k)
        p = jnp.exp(s - jnp.max(s, axis=1, keepdims=True))
        p = p / jnp.sum(p, axis=1, keepdims=True)
        outs.append(_dot(p.astype(BF16), v).astype(BF16))
    o = jnp.concatenate(outs, axis=1)
    x2 = _ln_rows(ALPHA * x + _dot(o, wo_ref[...]), g_ref[...], b_ref[...])
    o_ref[...] = x2
    lg_ref[...] = _dot(x2.astype(BF16), wr_ref[...])


def _memory_attention(pre1, g1, b1, kv, w_mq, w_mo, g, b, w_router, bsz, seq, tm):
    t, d = pre1.shape
    nt = seq // tm
    mem_len = kv.shape[0] // bsz
    d_mem = w_mq.shape[1]
    return pl.pallas_call(
        _memattn_kernel,
        grid=(bsz, nt),
        in_specs=[pl.BlockSpec((tm, d), lambda bi, i: (bi * nt + i, 0)),
                  pl.BlockSpec((1, d), lambda bi, i: (0, 0)),
                  pl.BlockSpec((1, d), lambda bi, i: (0, 0)),
                  pl.BlockSpec((mem_len, 2 * d_mem), lambda bi, i: (bi, 0)),
                  pl.BlockSpec((d, d_mem), lambda bi, i: (0, 0)),
                  pl.BlockSpec((d_mem, d), lambda bi, i: (0, 0)),
                  pl.BlockSpec((1, d), lambda bi, i: (0, 0)),
                  pl.BlockSpec((1, d), lambda bi, i: (0, 0)),
                  pl.BlockSpec((d, LANES), lambda bi, i: (0, 0))],
        out_specs=[pl.BlockSpec((tm, d), lambda bi, i: (bi * nt + i, 0)),
                   pl.BlockSpec((tm, LANES), lambda bi, i: (bi * nt + i, 0))],
        out_shape=[jax.ShapeDtypeStruct((t, d), F32),
                   jax.ShapeDtypeStruct((t, LANES), F32)],
        compiler_params=_cparams(("parallel", "parallel")),
        name="memory_attention",
    )(pre1, g1.reshape(1, d), b1.reshape(1, d), kv, w_mq, w_mo, g.reshape(1, d), b.reshape(1, d), w_router)


def _router_kernel(lg_ref, e1_ref, e2_ref, g1_ref, g2_ref):
    x = lg_ref[...]
    lane = lax.broadcasted_iota(I32, x.shape, 1)
    lane_f = lane.astype(F32)

    def argmax(mask):
        v = jnp.where(mask, x, -jnp.inf)
        mx = jnp.max(v, axis=1, keepdims=True)
        idx = jnp.min(jnp.where(mask & (v == mx), lane_f, float(LANES)), axis=1, keepdims=True)
        return mx, idx.astype(I32)

    gmask = lane < N_GROUPS
    gmax, gsel = argmax(gmask)
    gsum = jnp.sum(jnp.where(gmask, jnp.exp(x - gmax), 0.0), axis=1, keepdims=True)
    g_p = 1.0 / gsum
    lo = N_GROUPS + gsel * EXP_PER_GROUP
    emask = (lane >= lo) & (lane < lo + EXP_PER_GROUP)
    m1, i1 = argmax(emask)
    m2, i2 = argmax(emask & (lane != i1))
    esum = jnp.sum(jnp.where(emask, jnp.exp(x - m1), 0.0), axis=1, keepdims=True)
    p1 = 1.0 / esum
    p2 = jnp.exp(m2 - m1) / esum
    e1_ref[...] = i1 - N_GROUPS
    e2_ref[...] = i2 - N_GROUPS
    g1_ref[...] = g_p * (p1 / (p1 + p2))
    g2_ref[...] = g_p * (p2 / (p1 + p2))


def _router(logits, tm):
    t = logits.shape[0]
    col = pl.BlockSpec((tm, 1), lambda i: (i, 0))
    return pl.pallas_call(
        _router_kernel,
        grid=(t // tm,),
        in_specs=[pl.BlockSpec((tm, LANES), lambda i: (i, 0))],
        out_specs=[col, col, col, col],
        out_shape=[jax.ShapeDtypeStruct((t, 1), I32), jax.ShapeDtypeStruct((t, 1), I32),
                   jax.ShapeDtypeStruct((t, 1), F32), jax.ShapeDtypeStruct((t, 1), F32)],
        compiler_params=_cparams(("parallel",)),
        name="moe_router",
    )(logits)


HI16 = -65536


def _pack_halves(x):
    d = x.shape[1] // 2
    lo = pltpu.bitcast(x[:, :d].astype(BF16).astype(F32), I32)
    hi = pltpu.bitcast(x[:, d:].astype(BF16).astype(F32), I32)
    return lax.shift_right_logical(lo, 16) | (hi & HI16)


def _unpack_halves(u):
    return pltpu.bitcast(lax.shift_left(u, 16), F32), pltpu.bitcast(u & HI16, F32)


def _moe_rank_kernel(e1_ref, e2_ref, r1_ref, r2_ref, cnt_ref, base):
    i = pl.program_id(0)
    tm = e1_ref.shape[0]

    @pl.when(i == 0)
    def _():
        base[...] = jnp.zeros(base.shape, F32)

    lane = lax.broadcasted_iota(I32, (tm, LANES), 1)
    rr = lax.broadcasted_iota(I32, (tm, tm), 0)
    cc = lax.broadcasted_iota(I32, (tm, tm), 1)
    earlier = jnp.where(cc < rr, 1.0, 0.0).astype(BF16)
    for slot, (e_ref, r_ref) in enumerate(((e1_ref, r1_ref), (e2_ref, r2_ref))):
        oh = jnp.where(lane == e_ref[...], 1.0, 0.0)
        before = _dot(earlier, oh.astype(BF16)) + base[slot:slot + 1, :]
        r_ref[...] = jnp.sum(oh * before, axis=1, keepdims=True).astype(I32)
        base[slot:slot + 1, :] = base[slot:slot + 1, :] + jnp.sum(oh, axis=0, keepdims=True)
    cnt_ref[...] = base[...]


def _moe_rank(e1, e2, tm):
    t = e1.shape[0]
    col = pl.BlockSpec((tm, 1), lambda i: (i, 0))
    return pl.pallas_call(
        _moe_rank_kernel,
        grid=(t // tm,),
        in_specs=[col, col],
        out_specs=[col, col, pl.BlockSpec((8, LANES), lambda i: (0, 0))],
        out_shape=[jax.ShapeDtypeStruct((t, 1), I32), jax.ShapeDtypeStruct((t, 1), I32),
                   jax.ShapeDtypeStruct((8, LANES), F32)],
        scratch_shapes=[pltpu.VMEM((8, LANES), F32)],
        compiler_params=_cparams(("arbitrary",)),
        name="moe_rank",
    )(e1, e2)


def _moe_place_kernel(e1_ref, e2_ref, r1_ref, r2_ref, cnt_ref, p1_ref, p2_ref, be_ref, nu_ref):
    tm = e1_ref.shape[0]
    nbp = be_ref.shape[0]
    lane8 = lax.broadcasted_iota(I32, (8, LANES), 1)
    cnt = cnt_ref[...].astype(I32)
    c0 = jnp.broadcast_to(cnt[0:1], (8, LANES))
    c1 = jnp.broadcast_to(cnt[1:2], (8, LANES))
    blk_shift = MOE_BLOCK.bit_length() - 1
    padded = lax.shift_left(lax.shift_right_logical(c0 + c1 + (MOE_BLOCK - 1), blk_shift), blk_shift)
    pad_end = padded
    s = 1
    while s < LANES:
        pad_end = pad_end + jnp.where(lane8 >= s, pltpu.roll(pad_end, s, 1), 0)
        s *= 2
    start0 = (pad_end - padded).astype(F32)
    start1 = (pad_end - padded + c0).astype(F32)

    lane = lax.broadcasted_iota(I32, (tm, LANES), 1)
    for e_ref, r_ref, p_ref, start in ((e1_ref, r1_ref, p1_ref, start0), (e2_ref, r2_ref, p2_ref, start1)):
        seg = jnp.sum(jnp.where(lane == e_ref[...], start[0:1], 0.0), axis=1, keepdims=True)
        p_ref[...] = seg.astype(I32) + r_ref[...]

    block_row = lax.broadcasted_iota(I32, (nbp, LANES), 0) * MOE_BLOCK
    lane_b = lax.broadcasted_iota(I32, (nbp, LANES), 1)
    ended = (pad_end[0:1] <= block_row) & (lane_b < N_EXPERTS)
    be = jnp.sum(jnp.where(ended, 1.0, 0.0), axis=1, keepdims=True)
    be_ref[...] = jnp.minimum(be, N_EXPERTS - 1.0).astype(I32)
    total = jnp.max(pad_end, axis=1, keepdims=True)
    nu_ref[...] = jnp.broadcast_to(lax.shift_right_logical(total, blk_shift), (8, LANES))


def _moe_place(e1, e2, r1, r2, cnt, nb, tm):
    t = e1.shape[0]
    col = pl.BlockSpec((tm, 1), lambda i: (i, 0))
    return pl.pallas_call(
        _moe_place_kernel,
        grid=(t // tm,),
        in_specs=[col, col, col, col, pl.BlockSpec((8, LANES), lambda i: (0, 0))],
        out_specs=[col, col, pl.BlockSpec((nb, 1), lambda i: (0, 0)),
                   pl.BlockSpec((8, LANES), lambda i: (0, 0))],
        out_shape=[jax.ShapeDtypeStruct((t, 1), I32), jax.ShapeDtypeStruct((t, 1), I32),
                   jax.ShapeDtypeStruct((nb, 1), I32), jax.ShapeDtypeStruct((8, LANES), I32)],
        compiler_params=_cparams(("arbitrary",)),
        name="moe_place",
    )(e1, e2, r1, r2, cnt)


def _moe_dispatch_kernel(p1_ref, p2_ref, x_ref, xs_in, xs_hbm, pk, sem):
    del xs_in
    i = pl.program_id(0)
    n = pl.num_programs(0)
    slot = lax.rem(i, 2)
    blk = x_ref.shape[0]

    def row_copy(s, r, pos):
        return pltpu.make_async_copy(pk.at[s, pl.ds(r, 1)], xs_hbm.at[pl.ds(pos, 1)], sem.at[s])

    def drain(s):
        for _ in range(2):
            pltpu.make_async_copy(pk.at[s], xs_hbm.at[pl.ds(0, blk)], sem.at[s]).wait()

    @pl.when(i >= 2)
    def _():
        drain(slot)

    pk[slot] = _pack_halves(x_ref[...])

    def issue(r, c):
        row_copy(slot, r, p1_ref[0, 0, r]).start()
        row_copy(slot, r, p2_ref[0, 0, r]).start()
        return c

    lax.fori_loop(0, blk, issue, 0, unroll=8)

    @pl.when(i == n - 1)
    def _():
        drain(slot)

    @pl.when((i == n - 1) & (i >= 1))
    def _():
        drain(1 - slot)


def _moe_dispatch(x2, p1, p2, nb):
    t, d = x2.shape
    nt = t // MOE_BLOCK
    rows = nb * MOE_BLOCK
    pos = pl.BlockSpec((1, 1, MOE_BLOCK), lambda i: (i, 0, 0), memory_space=pltpu.SMEM)
    return pl.pallas_call(
        _moe_dispatch_kernel,
        grid=(nt,),
        in_specs=[pos, pos, pl.BlockSpec((MOE_BLOCK, d), lambda i: (i, 0)),
                  pl.BlockSpec(memory_space=pl.ANY)],
        out_specs=pl.BlockSpec(memory_space=pl.ANY),
        out_shape=jax.ShapeDtypeStruct((rows, d // 2), I32),
        input_output_aliases={3: 0},
        scratch_shapes=[pltpu.VMEM((2, MOE_BLOCK, d // 2), I32),
                        pltpu.SemaphoreType.DMA((2,))],
        compiler_params=_cparams(("arbitrary",)),
        name="moe_dispatch",
    )(p1.reshape(nt, 1, MOE_BLOCK), p2.reshape(nt, 1, MOE_BLOCK), x2, jnp.zeros((rows, d // 2), I32))


def _moe_ffn_kernel(be_ref, nu_ref, xs_ref, wg_ref, wu_ref, wd_ref, ys_ref, wgb, wub, wdb):
    i = pl.program_id(0)
    used = i < nu_ref[0]

    @pl.when(used & ((i == 0) | (be_ref[i] != be_ref[jnp.maximum(i - 1, 0)])))
    def _():
        wgb[...] = wg_ref[0].astype(BF16)
        wub[...] = wu_ref[0].astype(BF16)
        wdb[...] = wd_ref[0].astype(BF16)

    @pl.when(used)
    def _():
        lo, hi = _unpack_halves(xs_ref[...])
        lo = lo.astype(BF16)
        hi = hi.astype(BF16)
        d2 = lo.shape[1]
        g = _dot(lo, wgb[:d2]) + _dot(hi, wgb[d2:])
        u = _dot(lo, wub[:d2]) + _dot(hi, wub[d2:])
        hmid = (g * jax.nn.sigmoid(g) * u).astype(BF16)
        ys_ref[...] = _pack_halves(_dot(hmid, wdb[...]))

    @pl.when(i >= nu_ref[0])
    def _():
        ys_ref[...] = jnp.zeros(ys_ref.shape, ys_ref.dtype)


def _moe_ffn(xs, w_gate, w_up, w_down, block_expert, n_used):
    rows, d2 = xs.shape
    nb = rows // MOE_BLOCK
    d = 2 * d2
    ff = w_gate.shape[2]
    grid_spec = pltpu.PrefetchScalarGridSpec(
        num_scalar_prefetch=2,
        grid=(nb,),
        in_specs=[pl.BlockSpec((MOE_BLOCK, d2), lambda i, be, nu: (i, 0)),
                  pl.BlockSpec((1, d, ff), lambda i, be, nu: (be[i], 0, 0), pipeline_mode=pl.Buffered(1)),
                  pl.BlockSpec((1, d, ff), lambda i, be, nu: (be[i], 0, 0), pipeline_mode=pl.Buffered(1)),
                  pl.BlockSpec((1, ff, d), lambda i, be, nu: (be[i], 0, 0), pipeline_mode=pl.Buffered(1))],
        out_specs=pl.BlockSpec((MOE_BLOCK, d2), lambda i, be, nu: (i, 0)),
        scratch_shapes=[pltpu.VMEM((d, ff), BF16), pltpu.VMEM((d, ff), BF16), pltpu.VMEM((ff, d), BF16)],
    )
    return pl.pallas_call(
        _moe_ffn_kernel,
        grid_spec=grid_spec,
        out_shape=jax.ShapeDtypeStruct((rows, d2), I32),
        compiler_params=_cparams(("arbitrary",)),
        name="moe_ffn",
    )(block_expert, n_used, xs, w_gate, w_up, w_down)


def _moe_combine_kernel(p1_ref, p2_ref, q1_ref, q2_ref, x_ref, g1_ref, g2_ref, g_ref, b_ref, ys_hbm,
                        o_ref, yb, sem):
    i = pl.program_id(0)
    n = pl.num_programs(0)
    slot = lax.rem(i, 2)
    blk = x_ref.shape[0]

    def row_copy(s, k, r, pos):
        return pltpu.make_async_copy(ys_hbm.at[pl.ds(pos, 1)], yb.at[s, k, pl.ds(r, 1)], sem.at[s])

    def fetch(s, a_ref, b_ref2):
        def body(r, c):
            row_copy(s, 0, r, a_ref[0, 0, r]).start()
            row_copy(s, 1, r, b_ref2[0, 0, r]).start()
            return c
        lax.fori_loop(0, blk, body, 0, unroll=8)

    @pl.when(i == 0)
    def _():
        fetch(0, p1_ref, p2_ref)

    @pl.when(i + 1 < n)
    def _():
        fetch(1 - slot, q1_ref, q2_ref)

    for k in range(2):
        pltpu.make_async_copy(ys_hbm.at[pl.ds(0, blk)], yb.at[slot, k], sem.at[slot]).wait()
    y1 = jnp.concatenate(_unpack_halves(yb[slot, 0]), axis=1)
    y2 = jnp.concatenate(_unpack_halves(yb[slot, 1]), axis=1)
    y = y1 * g1_ref[...] + y2 * g2_ref[...]
    o_ref[...] = _ln_rows(ALPHA * x_ref[...] + y, g_ref[...], b_ref[...])


def _moe_combine(x2, ys, p1, p2, g1, g2, g, b):
    t, d = x2.shape
    nt = t // MOE_BLOCK
    p1 = p1.reshape(nt, 1, MOE_BLOCK)
    p2 = p2.reshape(nt, 1, MOE_BLOCK)
    pos = pl.BlockSpec((1, 1, MOE_BLOCK), lambda i: (i, 0, 0), memory_space=pltpu.SMEM)
    nxt = pl.BlockSpec((1, 1, MOE_BLOCK), lambda i: (jnp.minimum(i + 1, nt - 1), 0, 0),
                       memory_space=pltpu.SMEM)
    col = pl.BlockSpec((MOE_BLOCK, 1), lambda i: (i, 0))
    vec = pl.BlockSpec((1, d), lambda i: (0, 0))
    return pl.pallas_call(
        _moe_combine_kernel,
        grid=(nt,),
        in_specs=[pos, pos, nxt, nxt, pl.BlockSpec((MOE_BLOCK, d), lambda i: (i, 0)), col, col, vec, vec,
                  pl.BlockSpec(memory_space=pl.ANY)],
        out_specs=pl.BlockSpec((MOE_BLOCK, d), lambda i: (i, 0)),
        out_shape=jax.ShapeDtypeStruct((t, d), F32),
        scratch_shapes=[pltpu.VMEM((2, 2, MOE_BLOCK, d // 2), I32),
                        pltpu.SemaphoreType.DMA((2,))],
        compiler_params=_cparams(("arbitrary",)),
        name="moe_combine",
    )(p1, p2, p1, p2, x2, g1, g2, g.reshape(1, d), b.reshape(1, d), ys)


def _tile(n, pref):
    return pref if n % pref == 0 else n


def _layer(x, mem, w_in, conv_w, conv_b, conv_ln_g, conv_ln_b, kv_norm_g, w_uk, w_uv, rel_bias,
           conv_out_g, attn_out_g, w_out, ln1_g, ln1_b, w_mq, w_mk, w_mv, w_mo, ln2_g, ln2_b,
           w_router_grp, w_router_exp, w_gate, w_up, w_down, ln3_g, ln3_b):
    bsz, seq, d = x.shape
    t = bsz * seq
    d_conv = conv_w.shape[1]
    d_attn = N_HEADS * HEAD_DIM
    c_glu = 2 * d_conv
    c_qi = H_IDX * D_IDX
    o_q, o_kv = c_glu, c_glu + d_attn
    o_qi = o_kv + KV_RANK
    o_ki = o_qi + c_qi
    topk = min(TOPK_MAX, seq // 4)

    xf = x.reshape(t, d)
    xb = xf.astype(BF16)
    w_inb = w_in.astype(BF16)
    tail_w = jnp.concatenate([w_inb[:, o_kv:o_qi], w_inb[:, o_ki:]], axis=1)
    tail_w = jnp.pad(tail_w, ((0, 0), (0, KV_RANK + LANES - tail_w.shape[1])))
    tm = _tile(t, 1024)

    u = _matmul(xb, w_inb[:, :c_glu], BF16, tm, 512, "proj_glu")
    q_hm = _matmul_heads(xb, w_inb[:, o_q:o_kv], BF16, tm, 512, "proj_q")
    qi_hm = _matmul_heads(xb, w_inb[:, o_qi:o_ki], BF16, tm, 512, "proj_qidx")
    tail = _matmul(xb, tail_w, F32, tm, KV_RANK + LANES, "proj_tail")

    conv_n = _conformer_conv(u, bsz, seq, conv_w, conv_b, conv_ln_g, conv_ln_b, conv_out_g)
    tk = min(ATT_BLOCK, seq)
    ckv_n, ckv_t, kia, kib, kw = _prep_latent(tail, kv_norm_g, tm, tk)
    ckv_t = ckv_t.reshape(bsz, seq // tk, KV_RANK, tk)
    mask = _indexer_mask(qi_hm, kw, kia, kib, bsz, seq, topk)
    w_ukb = w_uk.astype(BF16)
    w_uvb = w_uv.astype(BF16)
    attn_n = _latent_attention(q_hm, ckv_n, ckv_t, mask, w_ukb, w_uvb, rel_bias, attn_out_g, bsz, seq)

    w_outb = w_out.astype(BF16)
    pre1 = _matmul2_residual(conv_n, attn_n, w_outb[:d_conv], w_outb[d_conv:], xf, tm, 512, "out_proj")

    mem_len = mem.shape[1]
    memb = mem.reshape(bsz * mem_len, d).astype(BF16)
    w_kv = jnp.concatenate([w_mk, w_mv], axis=1).astype(BF16)
    kv = _matmul(memb, w_kv, BF16, _tile(bsz * mem_len, 512), 512, "mem_kv")
    w_router = jnp.concatenate([w_router_grp, w_router_exp], axis=1)
    w_router = jnp.pad(w_router, ((0, 0), (0, LANES - w_router.shape[1]))).astype(BF16)
    x2, logits = _memory_attention(pre1, ln1_g, ln1_b, kv, w_mq.astype(BF16), w_mo.astype(BF16),
                                   ln2_g, ln2_b, w_router, bsz, seq, 256)

    e1, e2, g1, g2 = _router(logits, _tile(t, 1024))
    nb = (2 * t + N_EXPERTS * (MOE_BLOCK - 1) + MOE_BLOCK - 1) // MOE_BLOCK
    r1, r2, cnt = _moe_rank(e1, e2, 512)
    p1, p2, block_expert, n_used = _moe_place(e1, e2, r1, r2, cnt, nb, 512)
    xs = _moe_dispatch(x2, p1, p2, nb)
    ys = _moe_ffn(xs, w_gate, w_up, w_down, block_expert.reshape(nb), n_used[0, 0:1])
    x3 = _moe_combine(x2, ys, p1, p2, g1, g2, ln3_g, ln3_b)
    return x3.reshape(bsz, seq, d)


def kernel(x, mem, w_in, conv_w, conv_b, conv_ln_g, conv_ln_b, kv_norm_g, w_uk, w_uv, rel_bias, conv_out_g, attn_out_g, w_out, ln1_g, ln1_b, w_mq, w_mk, w_mv, w_mo, ln2_g, ln2_b, w_router_grp, w_router_exp, w_gate, w_up, w_down, ln3_g, ln3_b):
    for l in range(w_in.shape[0]):
        x = _layer(x, mem, w_in[l], conv_w[l], conv_b[l], conv_ln_g[l], conv_ln_b[l], kv_norm_g[l],
                   w_uk[l], w_uv[l], rel_bias, conv_out_g[l], attn_out_g[l], w_out[l], ln1_g[l], ln1_b[l],
                   w_mq[l], w_mk[l], w_mv[l], w_mo[l], ln2_g[l], ln2_b[l], w_router_grp[l],
                   w_router_exp[l], w_gate[l], w_up[l], w_down[l], ln3_g[l], ln3_b[l])
    return x
```

```python
import functools
import math

import jax
import jax.numpy as jnp
from jax import lax
from jax.experimental import pallas as pl
from jax.experimental.pallas import tpu as pltpu

F32 = jnp.float32
BF16 = jnp.bfloat16
I32 = jnp.int32

DEPTH = 1
CONV_WIDTH = 31
N_HEADS = 16
HEAD_DIM = 128
KV_RANK = 512
H_IDX = 32
D_IDX = 64
TOPK_MAX = 256
N_BUCKETS = 32
MAX_DIST = 128
MEM_HEADS = 4
MEM_HEAD_DIM = 128
N_GROUPS = 8
EXP_PER_GROUP = 8
N_EXPERTS = N_GROUPS * EXP_PER_GROUP
MOE_BLOCK = 128
ALPHA = (2.0 * DEPTH) ** 0.25
LN_EPS = 1e-5

LANES = 128
SUBLANES = 8
V7X_VMEM_BYTES = 64 * 1024 * 1024
VMEM_LIMIT = 56 * 1024 * 1024
NEG = -1e30
INT_MIN = -(2 ** 31)

LOG2E = 1.4426950408889634

ATT_BLOCK = 256
ATT_HEAD_GROUP = 8
CONV_TS = 256
CONV_HALO = 32
CONV_CC = 256
CONV_RC = 32


def _cparams(sem):
    return pltpu.CompilerParams(dimension_semantics=sem, vmem_limit_bytes=VMEM_LIMIT)


def _dot(a, b):
    return jnp.dot(a, b, preferred_element_type=F32)


def _dot_nt(a, b):
    return lax.dot_general(a, b, (((1,), (1,)), ((), ())), preferred_element_type=F32)


def _mm_kernel(a_ref, b_ref, o_ref):
    o_ref[...] = _dot(a_ref[...], b_ref[...]).astype(o_ref.dtype)


def _matmul(a, b, out_dtype, tm, tn, name):
    m, k = a.shape
    n = b.shape[1]
    return pl.pallas_call(
        _mm_kernel,
        grid=(m // tm, n // tn),
        in_specs=[pl.BlockSpec((tm, k), lambda i, j: (i, 0)),
                  pl.BlockSpec((k, tn), lambda i, j: (0, j))],
        out_specs=pl.BlockSpec((tm, tn), lambda i, j: (i, j)),
        out_shape=jax.ShapeDtypeStruct((m, n), out_dtype),
        compiler_params=_cparams(("parallel", "parallel")),
        name=name,
    )(a, b)


def _mm_heads_kernel(a_ref, b_ref, o_ref):
    r = _dot(a_ref[...], b_ref[...])
    for p in range(o_ref.shape[0]):
        o_ref[p] = r[:, p * LANES:(p + 1) * LANES].astype(o_ref.dtype)


def _matmul_heads(a, b, out_dtype, tm, tn, name):
    m, k = a.shape
    n = b.shape[1]
    return pl.pallas_call(
        _mm_heads_kernel,
        grid=(m // tm, n // tn),
        in_specs=[pl.BlockSpec((tm, k), lambda i, j: (i, 0)),
                  pl.BlockSpec((k, tn), lambda i, j: (0, j))],
        out_specs=pl.BlockSpec((tn // LANES, tm, LANES), lambda i, j: (j, i, 0)),
        out_shape=jax.ShapeDtypeStruct((n // LANES, m, LANES), out_dtype),
        compiler_params=_cparams(("parallel", "parallel")),
        name=name,
    )(a, b)


def _mm_value_heads_kernel(a_ref, b_ref, o_ref):
    r = _dot(a_ref[...], b_ref[...])
    tm = r.shape[0]
    for p in range(o_ref.shape[0]):
        o_ref[p, :, 0:LANES] = r[:, p * LANES:(p + 1) * LANES].astype(o_ref.dtype)
        o_ref[p, :, LANES:2 * LANES] = jnp.ones((tm, LANES), o_ref.dtype)


def _matmul_value_heads(a, b, tm, tn, name):
    m, k = a.shape
    n = b.shape[1]
    return pl.pallas_call(
        _mm_value_heads_kernel,
        grid=(m // tm, n // tn),
        in_specs=[pl.BlockSpec((tm, k), lambda i, j: (i, 0)),
                  pl.BlockSpec((k, tn), lambda i, j: (0, j))],
        out_specs=pl.BlockSpec((tn // LANES, tm, 2 * LANES), lambda i, j: (j, i, 0)),
        out_shape=jax.ShapeDtypeStruct((n // LANES, m, 2 * LANES), BF16),
        compiler_params=_cparams(("parallel", "parallel")),
        name=name,
    )(a, b)


def _mm2_res_kernel(a1_ref, a2_ref, g2_ref, w1_ref, w2_ref, r_ref, o_ref, a2n):
    @pl.when(pl.program_id(1) == 0)
    def _():
        x = a2_ref[...].astype(F32)
        ms = jnp.mean(x * x, axis=-1, keepdims=True)
        a2n[...] = (x * lax.rsqrt(ms + LN_EPS) * g2_ref[...]).astype(a2n.dtype)

    o_ref[...] = (ALPHA * r_ref[...] + _dot(a1_ref[...], w1_ref[...])
                  + _dot(a2n[...], w2_ref[...]))


def _matmul2_residual(a1, a2, g2, w1, w2, res, tm, tn, name):
    m, k1 = a1.shape
    k2 = a2.shape[1]
    n = w1.shape[1]
    return pl.pallas_call(
        _mm2_res_kernel,
        grid=(m // tm, n // tn),
        in_specs=[pl.BlockSpec((tm, k1), lambda i, j: (i, 0)),
                  pl.BlockSpec((tm, k2), lambda i, j: (i, 0)),
                  pl.BlockSpec((1, k2), lambda i, j: (0, 0)),
                  pl.BlockSpec((k1, tn), lambda i, j: (0, j)),
                  pl.BlockSpec((k2, tn), lambda i, j: (0, j)),
                  pl.BlockSpec((tm, tn), lambda i, j: (i, j))],
        out_specs=pl.BlockSpec((tm, tn), lambda i, j: (i, j)),
        out_shape=jax.ShapeDtypeStruct((m, n), F32),
        scratch_shapes=[pltpu.VMEM((tm, k2), BF16)],
        compiler_params=_cparams(("parallel", "arbitrary")),
        name=name,
    )(a1, a2, g2.reshape(1, k2), w1, w2, res)


def _ln_rows(x, g, b):
    mu = jnp.mean(x, axis=-1, keepdims=True)
    xc = x - mu
    var = jnp.mean(xc * xc, axis=-1, keepdims=True)
    return xc * lax.rsqrt(var + LN_EPS) * g + b


def _ln_kernel(x_ref, g_ref, b_ref, o_ref):
    o_ref[...] = _ln_rows(x_ref[...], g_ref[...], b_ref[...])


def _layer_norm(x, g, b, tm, name):
    m, d = x.shape
    return pl.pallas_call(
        _ln_kernel,
        grid=(m // tm,),
        in_specs=[pl.BlockSpec((tm, d), lambda i: (i, 0)),
                  pl.BlockSpec((1, d), lambda i: (0, 0)),
                  pl.BlockSpec((1, d), lambda i: (0, 0))],
        out_specs=pl.BlockSpec((tm, d), lambda i: (i, 0)),
        out_shape=jax.ShapeDtypeStruct((m, d), F32),
        compiler_params=_cparams(("parallel",)),
        name=name,
    )(x, g.reshape(1, d), b.reshape(1, d))


def _conv_kernel(a_ref, g_ref, cw_ref, cb_ref, lg_ref, lb_ref, og_ref, o_ref, hbuf, ybuf, hs):
    ts = a_ref.shape[0]
    nch = hbuf.shape[0]
    cc = hbuf.shape[2]
    d_conv = nch * cc

    nrow = CONV_HALO + ts

    @pl.when(pl.program_id(1) == 0)
    def _():
        hbuf[:, 0:CONV_HALO, :] = jnp.zeros((nch, CONV_HALO, cc), F32)
        hbuf[:, nrow:nrow + SUBLANES, :] = jnp.zeros((nch, SUBLANES, cc), F32)

    for c in range(nch):
        a = a_ref[:, c * cc:(c + 1) * cc].astype(F32)
        g = g_ref[:, c * cc:(c + 1) * cc].astype(F32)
        hbuf[c, CONV_HALO:nrow, :] = a * jax.nn.sigmoid(g)

    first = CONV_HALO - (CONV_WIDTH - 1)

    def chunk_body(c, carry):
        for o in range(1, SUBLANES):
            hs[o - 1] = hbuf[c, o:o + nrow, :]
        for r0 in range(0, ts, CONV_RC):
            acc = jnp.zeros((CONV_RC, cc), F32)
            for j in range(CONV_WIDTH):
                o = (first + j) % SUBLANES
                base = r0 + first + j - o
                rows = hbuf[c, base:base + CONV_RC, :] if o == 0 else hs[o - 1, base:base + CONV_RC, :]
                acc = acc + cw_ref[c, j:j + 1, :] * rows
            ybuf[c, r0:r0 + CONV_RC, :] = acc + cb_ref[c]
        hbuf[c, 0:CONV_HALO, :] = hbuf[c, ts:nrow, :]
        return carry

    lax.fori_loop(0, nch, chunk_body, 0)

    s1 = jnp.zeros((ts, 1), F32)
    for c in range(nch):
        s1 = s1 + jnp.sum(ybuf[c], axis=1, keepdims=True)
    mu = s1 * (1.0 / d_conv)
    s2 = jnp.zeros((ts, 1), F32)
    for c in range(nch):
        yc = ybuf[c] - mu
        s2 = s2 + jnp.sum(yc * yc, axis=1, keepdims=True)
    rstd = lax.rsqrt(s2 * (1.0 / d_conv) + LN_EPS)
    s3 = jnp.zeros((ts, 1), F32)
    for c in range(nch):
        z = (ybuf[c] - mu) * rstd * lg_ref[:, c * cc:(c + 1) * cc] + lb_ref[:, c * cc:(c + 1) * cc]
        z = z * jax.nn.sigmoid(z)
        ybuf[c] = z
        s3 = s3 + jnp.sum(z * z, axis=1, keepdims=True)
    rr = lax.rsqrt(s3 * (1.0 / d_conv) + LN_EPS)
    for c in range(nch):
        o_ref[:, c * cc:(c + 1) * cc] = (ybuf[c] * rr * og_ref[:, c * cc:(c + 1) * cc]).astype(o_ref.dtype)


def _conformer_conv(u, bsz, seq, conv_w, conv_b, ln_g, ln_b, out_g):
    d_conv = u.shape[1] // 2
    ts = min(CONV_TS, seq)
    nch = d_conv // CONV_CC
    nt = seq // ts
    cw = conv_w.reshape(CONV_WIDTH, nch, CONV_CC).transpose(1, 0, 2)
    cb = conv_b.reshape(nch, 1, CONV_CC)
    vec = pl.BlockSpec((1, d_conv), lambda b, i: (0, 0))
    return pl.pallas_call(
        _conv_kernel,
        grid=(bsz, nt),
        in_specs=[pl.BlockSpec((ts, d_conv), lambda b, i: (b * nt + i, 0)),
                  pl.BlockSpec((ts, d_conv), lambda b, i: (b * nt + i, 1)),
                  pl.BlockSpec((nch, CONV_WIDTH, CONV_CC), lambda b, i: (0, 0, 0)),
                  pl.BlockSpec((nch, 1, CONV_CC), lambda b, i: (0, 0, 0)),
                  vec, vec, vec],
        out_specs=pl.BlockSpec((ts, d_conv), lambda b, i: (b * nt + i, 0)),
        out_shape=jax.ShapeDtypeStruct((bsz * seq, d_conv), BF16),
        scratch_shapes=[pltpu.VMEM((nch, CONV_HALO + ts + SUBLANES, CONV_CC), F32),
                        pltpu.VMEM((nch, ts, CONV_CC), F32),
                        pltpu.VMEM((SUBLANES - 1, CONV_HALO + ts, CONV_CC), F32)],
        compiler_params=_cparams(("arbitrary", "arbitrary")),
        name="conformer_conv",
    )(u, u, cw, cb, ln_g.reshape(1, d_conv), ln_b.reshape(1, d_conv), out_g.reshape(1, d_conv))


def _prep_kernel(t_ref, g_ref, ckv_ref, ckvt_ref, kia_ref, kib_ref, kw_ref):
    ckv = t_ref[:, 0:KV_RANK]
    ms = jnp.mean(ckv * ckv, axis=-1, keepdims=True)
    ckv_n = ckv * lax.rsqrt(ms + LN_EPS) * g_ref[...]
    ckv_ref[...] = ckv_n.astype(ckv_ref.dtype)
    tk = ckvt_ref.shape[3]
    for c in range(ckvt_ref.shape[1]):
        ckvt_ref[0, c] = ckv_n[c * tk:(c + 1) * tk, :].T.astype(ckvt_ref.dtype)
    kw = t_ref[:, KV_RANK:KV_RANK + LANES]
    kw_ref[...] = kw
    lane = lax.broadcasted_iota(I32, kw.shape, 1)
    kia_ref[...] = jnp.where(lane < D_IDX, kw, 0.0).astype(kia_ref.dtype)
    kib_ref[...] = jnp.where(lane >= D_IDX, pltpu.roll(kw, D_IDX, 1), 0.0).astype(kib_ref.dtype)


def _prep_latent(tail, kv_norm_g, tm, tk):
    m, w = tail.shape
    return pl.pallas_call(
        _prep_kernel,
        grid=(m // tm,),
        in_specs=[pl.BlockSpec((tm, w), lambda i: (i, 0)),
                  pl.BlockSpec((1, KV_RANK), lambda i: (0, 0))],
        out_specs=[pl.BlockSpec((tm, KV_RANK), lambda i: (i, 0)),
                   pl.BlockSpec((1, tm // tk, KV_RANK, tk), lambda i: (i, 0, 0, 0)),
                   pl.BlockSpec((tm, LANES), lambda i: (i, 0)),
                   pl.BlockSpec((tm, LANES), lambda i: (i, 0)),
                   pl.BlockSpec((tm, LANES), lambda i: (i, 0))],
        out_shape=[jax.ShapeDtypeStruct((m, KV_RANK), BF16),
                   jax.ShapeDtypeStruct((m // tm, tm // tk, KV_RANK, tk), BF16),
                   jax.ShapeDtypeStruct((m, LANES), BF16),
                   jax.ShapeDtypeStruct((m, LANES), BF16),
                   jax.ShapeDtypeStruct((m, LANES), F32)],
        compiler_params=_cparams(("parallel",)),
        name="prep_latent",
    )(tail, kv_norm_g.reshape(1, KV_RANK))


def _indexer_kernel(qi_ref, kw_ref, kia_ref, kib_ref, o_ref, keybuf, wb, *, topk):
    i = pl.program_id(1)
    npairs, tq, _ = qi_ref.shape
    nk = o_ref.shape[1]
    tk = o_ref.shape[3]
    half = tk // 2
    kf = float(topk)
    group = 4

    for h in range(H_IDX):
        wb[h] = jnp.broadcast_to(kw_ref[:, D_IDX + h:D_IDX + h + 1], (tq, LANES))

    row = lax.broadcasted_iota(I32, (tq, tk), 0) + i * tq
    col0 = lax.broadcasted_iota(I32, (tq, tk), 1)

    def score_chunk(j, carry):
        k0 = pl.multiple_of(j * tk, tk)
        kd = jnp.concatenate([kia_ref[pl.ds(k0, tk), :], kib_ref[pl.ds(k0, tk), :]], axis=0)
        acc = jnp.zeros((tq, tk), F32)
        for p0 in range(0, npairs, group):
            lhs = qi_ref[p0:p0 + group].reshape(group * tq, LANES)
            zz = _dot_nt(lhs, kd)
            for p in range(group):
                h = 2 * (p0 + p)
                z = zz[p * tq:(p + 1) * tq]
                we = jnp.concatenate([wb[h]] * (tk // LANES), axis=1)
                wo = jnp.concatenate([wb[h + 1]] * (tk // LANES), axis=1)
                acc = acc + we * jnp.maximum(z[:, 0:tk], 0.0) + wo * jnp.maximum(z[:, tk:2 * tk], 0.0)
        bits = pltpu.bitcast(acc, I32)
        key = jnp.where(bits >= 0, bits, bits ^ jnp.int32(0x7FFFFFFF))
        keybuf[j] = jnp.where(col0 + j * tk <= row, key, INT_MIN)
        return carry

    lax.fori_loop(0, i + 1, score_chunk, 0)

    rows_per_pass = 128
    lane = lax.broadcasted_iota(I32, (rows_per_pass, LANES), 1)

    def count(pred):
        parts = []
        for r0 in range(0, tq, rows_per_pass):
            rows = slice(r0, r0 + rows_per_pass)

            def body(j, c):
                for s in range(tk // LANES):
                    kk = keybuf[j, rows, s * LANES:(s + 1) * LANES]
                    c = c + jnp.where(pred(kk, lane + (j * tk + s * LANES), rows), 1.0, 0.0)
                return c

            parts.append(lax.fori_loop(0, i + 1, body, jnp.zeros((rows_per_pass, LANES), F32)))
        return jnp.concatenate(
            [jnp.broadcast_to(jnp.sum(c, axis=1, keepdims=True), (rows_per_pass, LANES)) for c in parts], axis=0)

    def count_ge(cand):
        return count(lambda kk, col, rows: kk >= cand[rows])

    tau = jnp.where(count_ge(jnp.zeros((tq, LANES), I32)) >= kf, 0, INT_MIN).astype(I32)

    def bit_body(it, tau):
        cand = tau | jnp.left_shift(jnp.int32(1), 30 - it)
        return jnp.where(count_ge(cand) >= kf, cand, tau)

    tau = lax.fori_loop(0, 31, bit_body, tau)
    n_ge = count_ge(tau)
    n_gt = count(lambda kk, col, rows: kk > tau[rows])

    def tie_cut():
        need = kf - n_gt

        def cut_body(it, cut):
            cand = cut + jnp.left_shift(jnp.int32(1), 30 - it)
            below = count(lambda kk, col, rows: (kk == tau[rows]) & (col < cand[rows]))
            return jnp.where(below < need, cand, cut)

        return lax.fori_loop(0, 31, cut_body, jnp.zeros((tq, LANES), I32))

    has_ties = jnp.max(n_ge) > kf
    cut = lax.cond(has_ties, tie_cut, lambda: jnp.full((tq, LANES), 2 ** 30, I32))
    taub = jnp.concatenate([tau] * (tk // LANES), axis=1)
    cutb = jnp.concatenate([cut] * (tk // LANES), axis=1)

    def write_chunk(j, carry):
        kk = keybuf[j]
        sel = ((kk > taub) | ((kk == taub) & (col0 + j * tk <= cutb))) & (kk != INT_MIN)
        o_ref[0, j] = jnp.where(sel, 0.0, NEG).astype(o_ref.dtype)
        return carry

    lax.fori_loop(0, i + 1, write_chunk, 0)

    def write_rest(j, carry):
        o_ref[0, j] = jnp.full((tq, tk), NEG, o_ref.dtype)
        return carry

    lax.fori_loop(i + 1, nk, write_rest, 0)


def _indexer_mask(qi_hm, kw, kia, kib, bsz, seq, topk):
    tq = min(ATT_BLOCK, seq)
    nq = seq // tq
    npairs = qi_hm.shape[0]
    return pl.pallas_call(
        functools.partial(_indexer_kernel, topk=topk),
        grid=(bsz, nq),
        in_specs=[pl.BlockSpec((npairs, tq, LANES), lambda b, i: (0, b * nq + i, 0)),
                  pl.BlockSpec((tq, LANES), lambda b, i: (b * nq + i, 0)),
                  pl.BlockSpec((seq, LANES), lambda b, i: (b, 0)),
                  pl.BlockSpec((seq, LANES), lambda b, i: (b, 0))],
        out_specs=pl.BlockSpec((1, nq, tq, tq), lambda b, i: (b * nq + i, 0, 0, 0)),
        out_shape=jax.ShapeDtypeStruct((bsz * nq, nq, tq, tq), BF16),
        scratch_shapes=[pltpu.VMEM((nq, tq, tq), I32),
                        pltpu.VMEM((H_IDX, tq, LANES), F32)],
        compiler_params=_cparams(("parallel", "parallel")),
        name="indexer_mask",
    )(qi_hm, kw, kia, kib)


def _t5_bucket(dist):
    n = jnp.maximum(dist, 0)
    max_exact = N_BUCKETS // 2
    nf = jnp.maximum(n, 1).astype(F32)
    large = max_exact + (jnp.log(nf / max_exact) / math.log(MAX_DIST / max_exact)
                         * (N_BUCKETS - max_exact)).astype(I32)
    large = jnp.minimum(large, N_BUCKETS - 1)
    return jnp.where(n < max_exact, n, large)


def _key_heads_kernel(wt_ref, ct_ref, o_ref):
    r = _dot(wt_ref[...], ct_ref[0, 0]) * (HEAD_DIM ** -0.5 * LOG2E)
    for h in range(o_ref.shape[1]):
        o_ref[0, h, 0] = r[h * HEAD_DIM:(h + 1) * HEAD_DIM].astype(o_ref.dtype)


def _key_heads(w_uk_t, ckv_t):
    bsz, nk, r_lat, tk = ckv_t.shape
    nh = w_uk_t.shape[0] // HEAD_DIM
    return pl.pallas_call(
        _key_heads_kernel,
        grid=(bsz, nk),
        in_specs=[pl.BlockSpec((nh * HEAD_DIM, r_lat), lambda b, j: (0, 0)),
                  pl.BlockSpec((1, 1, r_lat, tk), lambda b, j: (b, j, 0, 0))],
        out_specs=pl.BlockSpec((1, nh, 1, HEAD_DIM, tk), lambda b, j: (b, 0, j, 0, 0)),
        out_shape=jax.ShapeDtypeStruct((bsz, nh, nk, HEAD_DIM, tk), BF16),
        compiler_params=_cparams(("parallel", "parallel")),
        name="key_heads",
    )(w_uk_t, ckv_t)


def _attn_kernel(relb_ref, q_ref, kt_ref, v_ref, mask_ref, o_ref, acc, m_s, btab):
    b = pl.program_id(0)
    g = pl.program_id(1)
    i = pl.program_id(2)
    hg, tq, _ = q_ref.shape
    nh = btab.shape[1]
    tk = mask_ref.shape[3]

    @pl.when((b == 0) & (g == 0) & (i == 0))
    def _():
        r = lax.broadcasted_iota(I32, (tq, tk), 0)
        c = lax.broadcasted_iota(I32, (tq, tk), 1)
        for t in range(2):
            bk = _t5_bucket(r - c + t * tk)

            def fill(h, carry):
                far = relb_ref[N_BUCKETS - 1, h]
                v = jnp.zeros((tq, tk), F32)
                for k in range(N_BUCKETS):
                    v = jnp.where(bk == k, (relb_ref[k, h] - far) * LOG2E, v)
                btab[t, h] = v
                return carry

            lax.fori_loop(0, nh, fill, 0)

    acc[...] = jnp.zeros(acc.shape, F32)
    m_s[...] = jnp.full(m_s.shape, NEG, F32)

    def lane_tile(v, n):
        return jnp.concatenate([v] * n, axis=1)

    def chunk(j, near):
        rows = pl.ds(pl.multiple_of(j * tk, tk), tk)
        mk = mask_ref[0, j].astype(F32)
        for h in range(hg):
            s = _dot(q_ref[h], kt_ref[0, h, j]) + mk
            if near:
                s = s + btab[i - j, g * hg + h]
            m_old = m_s[h]
            m_new = jnp.maximum(m_old, jnp.max(s, axis=1, keepdims=True))
            a = jnp.exp2(m_old - m_new)
            p = jnp.exp2(s - lane_tile(m_new, tk // LANES))
            acc[h] = lane_tile(a, 2) * acc[h] + _dot(p.astype(BF16), v_ref[h, rows, :])
            m_s[h] = m_new

    def far_chunk(j, carry):
        chunk(j, False)
        return carry

    def near_chunk(j, carry):
        chunk(j, True)
        return carry

    lax.fori_loop(0, jnp.maximum(i - 1, 0), far_chunk, 0)
    lax.fori_loop(jnp.maximum(i - 1, 0), i + 1, near_chunk, 0)

    for h in range(hg):
        o_ref[:, h * HEAD_DIM:(h + 1) * HEAD_DIM] = (
            acc[h, :, :HEAD_DIM] / acc[h, :, HEAD_DIM:]).astype(o_ref.dtype)


def _head_attention(q_hm, k_t, v_hm, mask, rel_bias, bsz, seq):
    nh = q_hm.shape[0]
    hg = ATT_HEAD_GROUP
    tq = mask.shape[2]
    nq = seq // tq
    assert tq == mask.shape[3] and tq >= MAX_DIST
    once = pl.Buffered(1)
    return pl.pallas_call(
        _attn_kernel,
        grid=(bsz, nh // hg, nq),
        in_specs=[pl.BlockSpec(memory_space=pltpu.SMEM),
                  pl.BlockSpec((hg, tq, HEAD_DIM), lambda b, g, i: (g, b * nq + i, 0)),
                  pl.BlockSpec((1, hg, nq, HEAD_DIM, tq), lambda b, g, i: (b, g, 0, 0, 0), pipeline_mode=once),
                  pl.BlockSpec((hg, seq, 2 * HEAD_DIM), lambda b, g, i: (g, b, 0), pipeline_mode=once),
                  pl.BlockSpec((1, nq, tq, tq), lambda b, g, i: (b * nq + i, 0, 0, 0))],
        out_specs=pl.BlockSpec((tq, hg * HEAD_DIM), lambda b, g, i: (b * nq + i, g)),
        out_shape=jax.ShapeDtypeStruct((bsz * seq, nh * HEAD_DIM), BF16),
        scratch_shapes=[pltpu.VMEM((hg, tq, 2 * HEAD_DIM), F32),
                        pltpu.VMEM((hg, tq, LANES), F32),
                        pltpu.VMEM((2, nh, tq, tq), F32)],
        compiler_params=_cparams(("arbitrary", "arbitrary", "arbitrary")),
        name="head_attention",
    )(rel_bias, q_hm, k_t, v_hm, mask)


def _memattn_kernel(x_ref, g1_ref, b1_ref, kv_ref, wq_ref, wo_ref, g_ref, b_ref, wr_ref, o_ref, lg_ref):
    x = _ln_rows(x_ref[...], g1_ref[...], b1_ref[...])
    d_mem = MEM_HEADS * MEM_HEAD_DIM
    q = (_dot(x.astype(BF16), wq_ref[...]) * (MEM_HEAD_DIM ** -0.5)).astype(BF16)
    outs = []
    for h in range(MEM_HEADS):
        lo = h * MEM_HEAD_DIM
        k = kv_ref[:, lo:lo + MEM_HEAD_DIM]
        v = kv_ref[:, d_mem + lo:d_mem + lo + MEM_HEAD_DIM]
        s = _dot_nt(q[:, lo:lo + MEM_HEAD_DIM], k)
        p = jnp.exp(s - jnp.max(s, axis=1, keepdims=True))
        p = p / jnp.sum(p, axis=1, keepdims=True)
        outs.append(_dot(p.astype(BF16), v).astype(BF16))
    o = jnp.concatenate(outs, axis=1)
    x2 = _ln_rows(ALPHA * x + _dot(o, wo_ref[...]), g_ref[...], b_ref[...])
    o_ref[...] = x2
    lg_ref[...] = _dot(x2.astype(BF16), wr_ref[...])


def _memory_attention(pre1, g1, b1, kv, w_mq, w_mo, g, b, w_router, bsz, seq, tm):
    t, d = pre1.shape
    nt = seq // tm
    mem_len = kv.shape[0] // bsz
    d_mem = w_mq.shape[1]
    return pl.pallas_call(
        _memattn_kernel,
        grid=(bsz, nt),
        in_specs=[pl.BlockSpec((tm, d), lambda bi, i: (bi * nt + i, 0)),
                  pl.BlockSpec((1, d), lambda bi, i: (0, 0)),
                  pl.BlockSpec((1, d), lambda bi, i: (0, 0)),
                  pl.BlockSpec((mem_len, 2 * d_mem), lambda bi, i: (bi, 0)),
                  pl.BlockSpec((d, d_mem), lambda bi, i: (0, 0)),
                  pl.BlockSpec((d_mem, d), lambda bi, i: (0, 0)),
                  pl.BlockSpec((1, d), lambda bi, i: (0, 0)),
                  pl.BlockSpec((1, d), lambda bi, i: (0, 0)),
                  pl.BlockSpec((d, LANES), lambda bi, i: (0, 0))],
        out_specs=[pl.BlockSpec((tm, d), lambda bi, i: (bi * nt + i, 0)),
                   pl.BlockSpec((tm, LANES), lambda bi, i: (bi * nt + i, 0))],
        out_shape=[jax.ShapeDtypeStruct((t, d), F32),
                   jax.ShapeDtypeStruct((t, LANES), F32)],
        compiler_params=_cparams(("parallel", "parallel")),
        name="memory_attention",
    )(pre1, g1.reshape(1, d), b1.reshape(1, d), kv, w_mq, w_mo, g.reshape(1, d), b.reshape(1, d), w_router)


def _router_kernel(lg_ref, e1_ref, e2_ref, g1_ref, g2_ref):
    x = lg_ref[...]
    lane = lax.broadcasted_iota(I32, x.shape, 1)
    lane_f = lane.astype(F32)

    def argmax(mask):
        v = jnp.where(mask, x, -jnp.inf)
        mx = jnp.max(v, axis=1, keepdims=True)
        idx = jnp.min(jnp.where(mask & (v == mx), lane_f, float(LANES)), axis=1, keepdims=True)
        return mx, idx.astype(I32)

    gmask = lane < N_GROUPS
    gmax, gsel = argmax(gmask)
    gsum = jnp.sum(jnp.where(gmask, jnp.exp(x - gmax), 0.0), axis=1, keepdims=True)
    g_p = 1.0 / gsum
    lo = N_GROUPS + gsel * EXP_PER_GROUP
    emask = (lane >= lo) & (lane < lo + EXP_PER_GROUP)
    m1, i1 = argmax(emask)
    m2, i2 = argmax(emask & (lane != i1))
    esum = jnp.sum(jnp.where(emask, jnp.exp(x - m1), 0.0), axis=1, keepdims=True)
    p1 = 1.0 / esum
    p2 = jnp.exp(m2 - m1) / esum
    e1_ref[...] = i1 - N_GROUPS
    e2_ref[...] = i2 - N_GROUPS
    g1_ref[...] = g_p * (p1 / (p1 + p2))
    g2_ref[...] = g_p * (p2 / (p1 + p2))


def _router(logits, tm):
    t = logits.shape[0]
    col = pl.BlockSpec((tm, 1), lambda i: (i, 0))
    return pl.pallas_call(
        _router_kernel,
        grid=(t // tm,),
        in_specs=[pl.BlockSpec((tm, LANES), lambda i: (i, 0))],
        out_specs=[col, col, col, col],
        out_shape=[jax.ShapeDtypeStruct((t, 1), I32), jax.ShapeDtypeStruct((t, 1), I32),
                   jax.ShapeDtypeStruct((t, 1), F32), jax.ShapeDtypeStruct((t, 1), F32)],
        compiler_params=_cparams(("parallel",)),
        name="moe_router",
    )(logits)


HI16 = -65536


def _pack_halves(x):
    d = x.shape[1] // 2
    lo = pltpu.bitcast(x[:, :d].astype(BF16).astype(F32), I32)
    hi = pltpu.bitcast(x[:, d:].astype(BF16).astype(F32), I32)
    return lax.shift_right_logical(lo, 16) | (hi & HI16)


def _unpack_halves(u):
    return pltpu.bitcast(lax.shift_left(u, 16), F32), pltpu.bitcast(u & HI16, F32)


def _moe_rank_kernel(e1_ref, e2_ref, r1_ref, r2_ref, cnt_ref, base):
    i = pl.program_id(0)
    tm = e1_ref.shape[0]

    @pl.when(i == 0)
    def _():
        base[...] = jnp.zeros(base.shape, F32)

    lane = lax.broadcasted_iota(I32, (tm, LANES), 1)
    rr = lax.broadcasted_iota(I32, (tm, tm), 0)
    cc = lax.broadcasted_iota(I32, (tm, tm), 1)
    earlier = jnp.where(cc < rr, 1.0, 0.0).astype(BF16)
    for slot, (e_ref, r_ref) in enumerate(((e1_ref, r1_ref), (e2_ref, r2_ref))):
        oh = jnp.where(lane == e_ref[...], 1.0, 0.0)
        before = _dot(earlier, oh.astype(BF16)) + base[slot:slot + 1, :]
        r_ref[...] = jnp.sum(oh * before, axis=1, keepdims=True).astype(I32)
        base[slot:slot + 1, :] = base[slot:slot + 1, :] + jnp.sum(oh, axis=0, keepdims=True)
    cnt_ref[...] = base[...]


def _moe_rank(e1, e2, tm):
    t = e1.shape[0]
    col = pl.BlockSpec((tm, 1), lambda i: (i, 0))
    return pl.pallas_call(
        _moe_rank_kernel,
        grid=(t // tm,),
        in_specs=[col, col],
        out_specs=[col, col, pl.BlockSpec((8, LANES), lambda i: (0, 0))],
        out_shape=[jax.ShapeDtypeStruct((t, 1), I32), jax.ShapeDtypeStruct((t, 1), I32),
                   jax.ShapeDtypeStruct((8, LANES), F32)],
        scratch_shapes=[pltpu.VMEM((8, LANES), F32)],
        compiler_params=_cparams(("arbitrary",)),
        name="moe_rank",
    )(e1, e2)


def _moe_place_kernel(e1_ref, e2_ref, r1_ref, r2_ref, cnt_ref, p1_ref, p2_ref, be_ref, nu_ref):
    tm = e1_ref.shape[0]
    nbp = be_ref.shape[0]
    lane8 = lax.broadcasted_iota(I32, (8, LANES), 1)
    cnt = cnt_ref[...].astype(I32)
    c0 = jnp.broadcast_to(cnt[0:1], (8, LANES))
    c1 = jnp.broadcast_to(cnt[1:2], (8, LANES))
    blk_shift = MOE_BLOCK.bit_length() - 1
    padded = lax.shift_left(lax.shift_right_logical(c0 + c1 + (MOE_BLOCK - 1), blk_shift), blk_shift)
    pad_end = padded
    s = 1
    while s < LANES:
        pad_end = pad_end + jnp.where(lane8 >= s, pltpu.roll(pad_end, s, 1), 0)
        s *= 2
    start0 = (pad_end - padded).astype(F32)
    start1 = (pad_end - padded + c0).astype(F32)

    lane = lax.broadcasted_iota(I32, (tm, LANES), 1)
    for e_ref, r_ref, p_ref, start in ((e1_ref, r1_ref, p1_ref, start0), (e2_ref, r2_ref, p2_ref, start1)):
        seg = jnp.sum(jnp.where(lane == e_ref[...], start[0:1], 0.0), axis=1, keepdims=True)
        p_ref[...] = seg.astype(I32) + r_ref[...]

    block_row = lax.broadcasted_iota(I32, (nbp, LANES), 0) * MOE_BLOCK
    lane_b = lax.broadcasted_iota(I32, (nbp, LANES), 1)
    ended = (pad_end[0:1] <= block_row) & (lane_b < N_EXPERTS)
    be = jnp.sum(jnp.where(ended, 1.0, 0.0), axis=1, keepdims=True)
    be_ref[...] = jnp.minimum(be, N_EXPERTS - 1.0).astype(I32)
    total = jnp.max(pad_end, axis=1, keepdims=True)
    nu_ref[...] = jnp.broadcast_to(lax.shift_right_logical(total, blk_shift), (8, LANES))


def _moe_place(e1, e2, r1, r2, cnt, nb, tm):
    t = e1.shape[0]
    col = pl.BlockSpec((tm, 1), lambda i: (i, 0))
    return pl.pallas_call(
        _moe_place_kernel,
        grid=(t // tm,),
        in_specs=[col, col, col, col, pl.BlockSpec((8, LANES), lambda i: (0, 0))],
        out_specs=[col, col, pl.BlockSpec((nb, 1), lambda i: (0, 0)),
                   pl.BlockSpec((8, LANES), lambda i: (0, 0))],
        out_shape=[jax.ShapeDtypeStruct((t, 1), I32), jax.ShapeDtypeStruct((t, 1), I32),
                   jax.ShapeDtypeStruct((nb, 1), I32), jax.ShapeDtypeStruct((8, LANES), I32)],
        compiler_params=_cparams(("arbitrary",)),
        name="moe_place",
    )(e1, e2, r1, r2, cnt)


def _moe_dispatch_kernel(p1_ref, p2_ref, x_ref, xs_in, xs_hbm, pk, sem):
    del xs_in
    i = pl.program_id(0)
    n = pl.num_programs(0)
    slot = lax.rem(i, 2)
    blk = x_ref.shape[0]

    def row_copy(s, r, pos):
        return pltpu.make_async_copy(pk.at[s, pl.ds(r, 1)], xs_hbm.at[pl.ds(pos, 1)], sem.at[s])

    def drain(s):
        for _ in range(2):
            pltpu.make_async_copy(pk.at[s], xs_hbm.at[pl.ds(0, blk)], sem.at[s]).wait()

    @pl.when(i >= 2)
    def _():
        drain(slot)

    pk[slot] = _pack_halves(x_ref[...])

    def issue(r, c):
        row_copy(slot, r, p1_ref[0, 0, r]).start()
        row_copy(slot, r, p2_ref[0, 0, r]).start()
        return c

    lax.fori_loop(0, blk, issue, 0, unroll=8)

    @pl.when(i == n - 1)
    def _():
        drain(slot)

    @pl.when((i == n - 1) & (i >= 1))
    def _():
        drain(1 - slot)


def _moe_dispatch(x2, p1, p2, nb):
    t, d = x2.shape
    nt = t // MOE_BLOCK
    rows = nb * MOE_BLOCK
    pos = pl.BlockSpec((1, 1, MOE_BLOCK), lambda i: (i, 0, 0), memory_space=pltpu.SMEM)
    return pl.pallas_call(
        _moe_dispatch_kernel,
        grid=(nt,),
        in_specs=[pos, pos, pl.BlockSpec((MOE_BLOCK, d), lambda i: (i, 0)),
                  pl.BlockSpec(memory_space=pl.ANY)],
        out_specs=pl.BlockSpec(memory_space=pl.ANY),
        out_shape=jax.ShapeDtypeStruct((rows, d // 2), I32),
        input_output_aliases={3: 0},
        scratch_shapes=[pltpu.VMEM((2, MOE_BLOCK, d // 2), I32),
                        pltpu.SemaphoreType.DMA((2,))],
        compiler_params=_cparams(("arbitrary",)),
        name="moe_dispatch",
    )(p1.reshape(nt, 1, MOE_BLOCK), p2.reshape(nt, 1, MOE_BLOCK), x2, jnp.zeros((rows, d // 2), I32))


MAT_PIECES = 4
N_PIECES = 3 * MAT_PIECES
PIECES_PER_BLOCK = 3
PIECE_RING = 2


def _moe_ffn_kernel(be_ref, nu_ref, xs_ref, wg_hbm, wu_hbm, wd_hbm, ys_ref,
                    wgb, wub, wdb, sa, sb, sem, st):
    i = pl.program_id(0)
    nb = be_ref.shape[0]
    n_used = nu_ref[0]
    ra = wgb.shape[1] // MAT_PIECES
    rb = wdb.shape[1] // MAT_PIECES

    def piece_copy(p, e):
        k = p % PIECE_RING
        m, r = divmod(p, MAT_PIECES)
        if m == 0:
            return pltpu.make_async_copy(wg_hbm.at[e, pl.ds(r * ra, ra)], sa.at[k], sem.at[k])
        if m == 1:
            return pltpu.make_async_copy(wu_hbm.at[e, pl.ds(r * ra, ra)], sa.at[k], sem.at[k])
        return pltpu.make_async_copy(wd_hbm.at[e, pl.ds(r * rb, rb)], sb.at[k], sem.at[k])

    def piece_round(p, slot):
        k = p % PIECE_RING
        m, r = divmod(p, MAT_PIECES)
        if m == 0:
            wgb[slot, pl.ds(r * ra, ra), :] = sa[k].astype(BF16)
        elif m == 1:
            wub[slot, pl.ds(r * ra, ra), :] = sa[k].astype(BF16)
        else:
            wdb[slot, pl.ds(r * rb, rb), :] = sb[k].astype(BF16)

    def start_one():
        e, started, finished = st[1], st[2], st[3]
        can = (started < N_PIECES) & (started - finished < PIECE_RING)
        for p in range(N_PIECES):
            @pl.when(can & (started == p))
            def _():
                piece_copy(p, e).start()
        st[2] = started + can.astype(I32)

    def finish_one(slot):
        e, finished = st[1], st[3]

        @pl.when((e >= 0) & (finished < N_PIECES))
        def _():
            for p in range(N_PIECES):
                @pl.when(finished == p)
                def _():
                    piece_copy(p, e).wait()
                    piece_round(p, slot)
            st[3] = finished + 1
            start_one()

    def prepare(e):
        st[1] = e
        st[2] = 0
        st[3] = 0

        @pl.when(e >= 0)
        def _():
            for _ in range(PIECE_RING):
                start_one()

    @pl.when(i < n_used)
    def _():
        e = be_ref[i]

        @pl.when(i == 0)
        def _():
            st[0] = 1
            prepare(e)

        @pl.when((i == 0) | (e != be_ref[jnp.maximum(i - 1, 0)]))
        def _():
            slot = 1 - st[0]

            def fin(_, c):
                finish_one(slot)
                return c

            lax.fori_loop(0, N_PIECES, fin, 0)
            st[0] = slot
            k = lax.while_loop(lambda k: (k < n_used) & (be_ref[jnp.minimum(k, nb - 1)] == e),
                               lambda k: k + 1, i + 1)
            prepare(jnp.where(k < n_used, be_ref[jnp.minimum(k, nb - 1)], -1))

        def ahead(_, c):
            finish_one(1 - st[0])
            return c

        lax.fori_loop(0, PIECES_PER_BLOCK, ahead, 0)

        slot = st[0]
        lo, hi = _unpack_halves(xs_ref[...])
        lo = lo.astype(BF16)
        hi = hi.astype(BF16)
        d2 = lo.shape[1]
        g = _dot(lo, wgb[slot, :d2]) + _dot(hi, wgb[slot, d2:])
        u = _dot(lo, wub[slot, :d2]) + _dot(hi, wub[slot, d2:])
        hmid = (g * jax.nn.sigmoid(g) * u).astype(BF16)
        ys_ref[...] = _pack_halves(_dot(hmid, wdb[slot]))

    @pl.when(i >= n_used)
    def _():
        ys_ref[...] = jnp.zeros(ys_ref.shape, ys_ref.dtype)


def _moe_ffn(xs, w_gate, w_up, w_down, block_expert, n_used):
    rows, d2 = xs.shape
    nb = rows // MOE_BLOCK
    d = 2 * d2
    ff = w_gate.shape[2]
    grid_spec = pltpu.PrefetchScalarGridSpec(
        num_scalar_prefetch=2,
        grid=(nb,),
        in_specs=[pl.BlockSpec((MOE_BLOCK, d2), lambda i, be, nu: (i, 0)),
                  pl.BlockSpec(memory_space=pl.ANY),
                  pl.BlockSpec(memory_space=pl.ANY),
                  pl.BlockSpec(memory_space=pl.ANY)],
        out_specs=pl.BlockSpec((MOE_BLOCK, d2), lambda i, be, nu: (i, 0)),
        scratch_shapes=[pltpu.VMEM((2, d, ff), BF16), pltpu.VMEM((2, d, ff), BF16),
                        pltpu.VMEM((2, ff, d), BF16),
                        pltpu.VMEM((PIECE_RING, d // MAT_PIECES, ff), F32),
                        pltpu.VMEM((PIECE_RING, ff // MAT_PIECES, d), F32),
                        pltpu.SemaphoreType.DMA((PIECE_RING,)),
                        pltpu.SMEM((4,), I32)],
    )
    return pl.pallas_call(
        _moe_ffn_kernel,
        grid_spec=grid_spec,
        out_shape=jax.ShapeDtypeStruct((rows, d2), I32),
        compiler_params=_cparams(("arbitrary",)),
        name="moe_ffn",
    )(block_expert, n_used, xs, w_gate, w_up, w_down)


def _moe_combine_kernel(p1_ref, p2_ref, q1_ref, q2_ref, x_ref, g1_ref, g2_ref, g_ref, b_ref, ys_hbm,
                        o_ref, yb, sem):
    i = pl.program_id(0)
    n = pl.num_programs(0)
    slot = lax.rem(i, 2)
    blk = x_ref.shape[0]

    def row_copy(s, k, r, pos):
        return pltpu.make_async_copy(ys_hbm.at[pl.ds(pos, 1)], yb.at[s, k, pl.ds(r, 1)], sem.at[s])

    def fetch(s, a_ref, b_ref2):
        def body(r, c):
            row_copy(s, 0, r, a_ref[0, 0, r]).start()
            row_copy(s, 1, r, b_ref2[0, 0, r]).start()
            return c
        lax.fori_loop(0, blk, body, 0, unroll=8)

    @pl.when(i == 0)
    def _():
        fetch(0, p1_ref, p2_ref)

    @pl.when(i + 1 < n)
    def _():
        fetch(1 - slot, q1_ref, q2_ref)

    for k in range(2):
        pltpu.make_async_copy(ys_hbm.at[pl.ds(0, blk)], yb.at[slot, k], sem.at[slot]).wait()
    y1 = jnp.concatenate(_unpack_halves(yb[slot, 0]), axis=1)
    y2 = jnp.concatenate(_unpack_halves(yb[slot, 1]), axis=1)
    y = y1 * g1_ref[...] + y2 * g2_ref[...]
    o_ref[...] = _ln_rows(ALPHA * x_ref[...] + y, g_ref[...], b_ref[...])


def _moe_combine(x2, ys, p1, p2, g1, g2, g, b):
    t, d = x2.shape
    nt = t // MOE_BLOCK
    p1 = p1.reshape(nt, 1, MOE_BLOCK)
    p2 = p2.reshape(nt, 1, MOE_BLOCK)
    pos = pl.BlockSpec((1, 1, MOE_BLOCK), lambda i: (i, 0, 0), memory_space=pltpu.SMEM)
    nxt = pl.BlockSpec((1, 1, MOE_BLOCK), lambda i: (jnp.minimum(i + 1, nt - 1), 0, 0),
                       memory_space=pltpu.SMEM)
    col = pl.BlockSpec((MOE_BLOCK, 1), lambda i: (i, 0))
    vec = pl.BlockSpec((1, d), lambda i: (0, 0))
    return pl.pallas_call(
        _moe_combine_kernel,
        grid=(nt,),
        in_specs=[pos, pos, nxt, nxt, pl.BlockSpec((MOE_BLOCK, d), lambda i: (i, 0)), col, col, vec, vec,
                  pl.BlockSpec(memory_space=pl.ANY)],
        out_specs=pl.BlockSpec((MOE_BLOCK, d), lambda i: (i, 0)),
        out_shape=jax.ShapeDtypeStruct((t, d), F32),
        scratch_shapes=[pltpu.VMEM((2, 2, MOE_BLOCK, d // 2), I32),
                        pltpu.SemaphoreType.DMA((2,))],
        compiler_params=_cparams(("arbitrary",)),
        name="moe_combine",
    )(p1, p2, p1, p2, x2, g1, g2, g.reshape(1, d), b.reshape(1, d), ys)


def _tile(n, pref):
    return pref if n % pref == 0 else n


def _layer(x, mem, w_in, conv_w, conv_b, conv_ln_g, conv_ln_b, kv_norm_g, w_uk, w_uv, rel_bias,
           conv_out_g, attn_out_g, w_out, ln1_g, ln1_b, w_mq, w_mk, w_mv, w_mo, ln2_g, ln2_b,
           w_router_grp, w_router_exp, w_gate, w_up, w_down, ln3_g, ln3_b):
    bsz, seq, d = x.shape
    t = bsz * seq
    d_conv = conv_w.shape[1]
    d_attn = N_HEADS * HEAD_DIM
    c_glu = 2 * d_conv
    c_qi = H_IDX * D_IDX
    o_q, o_kv = c_glu, c_glu + d_attn
    o_qi = o_kv + KV_RANK
    o_ki = o_qi + c_qi
    topk = min(TOPK_MAX, seq // 4)

    xf = x.reshape(t, d)
    xb = xf.astype(BF16)
    w_inb = w_in.astype(BF16)
    tail_w = jnp.concatenate([w_inb[:, o_kv:o_qi], w_inb[:, o_ki:]], axis=1)
    tail_w = jnp.pad(tail_w, ((0, 0), (0, KV_RANK + LANES - tail_w.shape[1])))
    tm = _tile(t, 1024)

    u = _matmul(xb, w_inb[:, :c_glu], BF16, tm, 512, "proj_glu")
    q_hm = _matmul_heads(xb, w_inb[:, o_q:o_kv], BF16, tm, 512, "proj_q")
    qi_hm = _matmul_heads(xb, w_inb[:, o_qi:o_ki], BF16, tm, 512, "proj_qidx")
    tail = _matmul(xb, tail_w, F32, tm, KV_RANK + LANES, "proj_tail")

    conv_n = _conformer_conv(u, bsz, seq, conv_w, conv_b, conv_ln_g, conv_ln_b, conv_out_g)
    tk = min(ATT_BLOCK, seq)
    ckv_n, ckv_t, kia, kib, kw = _prep_latent(tail, kv_norm_g, tm, tk)
    ckv_t = ckv_t.reshape(bsz, seq // tk, KV_RANK, tk)
    mask = _indexer_mask(qi_hm, kw, kia, kib, bsz, seq, topk)
    w_uk_t = w_uk.transpose(0, 2, 1).reshape(d_attn, KV_RANK).astype(BF16)
    w_uv_all = w_uv.transpose(1, 0, 2).reshape(KV_RANK, d_attn).astype(BF16)
    k_t = _key_heads(w_uk_t, ckv_t)
    v_hm = _matmul_value_heads(ckv_n, w_uv_all, tm, 512, "value_heads")
    attn = _head_attention(q_hm, k_t, v_hm, mask, rel_bias, bsz, seq)

    w_outb = w_out.astype(BF16)
    pre1 = _matmul2_residual(conv_n, attn, attn_out_g, w_outb[:d_conv], w_outb[d_conv:], xf, tm, 512,
                             "out_proj")

    mem_len = mem.shape[1]
    memb = mem.reshape(bsz * mem_len, d).astype(BF16)
    w_kv = jnp.concatenate([w_mk, w_mv], axis=1).astype(BF16)
    kv = _matmul(memb, w_kv, BF16, _tile(bsz * mem_len, 512), 512, "mem_kv")
    w_router = jnp.concatenate([w_router_grp, w_router_exp], axis=1)
    w_router = jnp.pad(w_router, ((0, 0), (0, LANES - w_router.shape[1]))).astype(BF16)
    x2, logits = _memory_attention(pre1, ln1_g, ln1_b, kv, w_mq.astype(BF16), w_mo.astype(BF16),
                                   ln2_g, ln2_b, w_router, bsz, seq, 256)

    e1, e2, g1, g2 = _router(logits, _tile(t, 1024))
    nb = (2 * t + N_EXPERTS * (MOE_BLOCK - 1) + MOE_BLOCK - 1) // MOE_BLOCK
    r1, r2, cnt = _moe_rank(e1, e2, 512)
    p1, p2, block_expert, n_used = _moe_place(e1, e2, r1, r2, cnt, nb, 512)
    xs = _moe_dispatch(x2, p1, p2, nb)
    ys = _moe_ffn(xs, w_gate, w_up, w_down, block_expert.reshape(nb), n_used[0, 0:1])
    x3 = _moe_combine(x2, ys, p1, p2, g1, g2, ln3_g, ln3_b)
    return x3.reshape(bsz, seq, d)


def kernel(x, mem, w_in, conv_w, conv_b, conv_ln_g, conv_ln_b, kv_norm_g, w_uk, w_uv, rel_bias, conv_out_g, attn_out_g, w_out, ln1_g, ln1_b, w_mq, w_mk, w_mv, w_mo, ln2_g, ln2_b, w_router_grp, w_router_exp, w_gate, w_up, w_down, ln3_g, ln3_b):
    for l in range(w_in.shape[0]):
        x = _layer(x, mem, w_in[l], conv_w[l], conv_b[l], conv_ln_g[l], conv_ln_b[l], kv_norm_g[l],
                   w_uk[l], w_uv[l], rel_bias, conv_out_g[l], attn_out_g[l], w_out[l], ln1_g[l], ln1_b[l],
                   w_mq[l], w_mk[l], w_mv[l], w_mo[l], ln2_g[l], ln2_b[l], w_router_grp[l],
                   w_router_exp[l], w_gate[l], w_up[l], w_down[l], ln3_g[l], ln3_b[l])
    return x
```

```python
import functools
import math

import jax
import jax.numpy as jnp
from jax import lax
from jax.experimental import pallas as pl
from jax.experimental.pallas import tpu as pltpu

F32 = jnp.float32
BF16 = jnp.bfloat16
I32 = jnp.int32

DEPTH = 1
CONV_WIDTH = 31
N_HEADS = 16
HEAD_DIM = 128
KV_RANK = 512
H_IDX = 32
D_IDX = 64
TOPK_MAX = 256
N_BUCKETS = 32
MAX_DIST = 128
MEM_HEADS = 4
MEM_HEAD_DIM = 128
N_GROUPS = 8
EXP_PER_GROUP = 8
N_EXPERTS = N_GROUPS * EXP_PER_GROUP
MOE_BLOCK = 128
ALPHA = (2.0 * DEPTH) ** 0.25
LN_EPS = 1e-5

LANES = 128
SUBLANES = 8
V7X_VMEM_BYTES = 64 * 1024 * 1024
VMEM_LIMIT = 56 * 1024 * 1024
NEG = -1e30
INT_MIN = -(2 ** 31)

LOG2E = 1.4426950408889634

ATT_BLOCK = 256
ATT_HEAD_GROUP = 8
CONV_TS = 256
CONV_HALO = 32
CONV_CC = 256
CONV_RC = 32


def _cparams(sem):
    return pltpu.CompilerParams(dimension_semantics=sem, vmem_limit_bytes=VMEM_LIMIT)


def _dot(a, b):
    return jnp.dot(a, b, preferred_element_type=F32)


def _dot_nt(a, b):
    return lax.dot_general(a, b, (((1,), (1,)), ((), ())), preferred_element_type=F32)


def _mm_kernel(a_ref, b_ref, o_ref):
    o_ref[...] = _dot(a_ref[...], b_ref[...]).astype(o_ref.dtype)


def _matmul(a, b, out_dtype, tm, tn, name):
    m, k = a.shape
    n = b.shape[1]
    return pl.pallas_call(
        _mm_kernel,
        grid=(m // tm, n // tn),
        in_specs=[pl.BlockSpec((tm, k), lambda i, j: (i, 0)),
                  pl.BlockSpec((k, tn), lambda i, j: (0, j))],
        out_specs=pl.BlockSpec((tm, tn), lambda i, j: (i, j)),
        out_shape=jax.ShapeDtypeStruct((m, n), out_dtype),
        compiler_params=_cparams(("parallel", "parallel")),
        name=name,
    )(a, b)


def _mm_cast_kernel(a_ref, b_ref, o_ref, ab_ref):
    @pl.when(pl.program_id(1) == 0)
    def _():
        ab_ref[...] = a_ref[...].astype(ab_ref.dtype)

    o_ref[...] = _dot(ab_ref[...], b_ref[...]).astype(o_ref.dtype)


def _matmul_cast(a, b, n, out_dtype, tm, tn, name):
    m, k = a.shape
    return pl.pallas_call(
        _mm_cast_kernel,
        grid=(m // tm, n // tn),
        in_specs=[pl.BlockSpec((tm, k), lambda i, j: (i, 0)),
                  pl.BlockSpec((k, tn), lambda i, j: (0, j))],
        out_specs=[pl.BlockSpec((tm, tn), lambda i, j: (i, j)),
                   pl.BlockSpec((tm, k), lambda i, j: (i, 0))],
        out_shape=[jax.ShapeDtypeStruct((m, n), out_dtype), jax.ShapeDtypeStruct((m, k), BF16)],
        compiler_params=_cparams(("parallel", "arbitrary")),
        name=name,
    )(a, b)


def _mm_heads_kernel(a_ref, b_ref, o_ref):
    r = _dot(a_ref[...], b_ref[...])
    for p in range(o_ref.shape[0]):
        o_ref[p] = r[:, p * LANES:(p + 1) * LANES].astype(o_ref.dtype)


def _matmul_heads(a, b, col0, n, out_dtype, tm, tn, name):
    m, k = a.shape
    assert col0 % tn == 0
    return pl.pallas_call(
        _mm_heads_kernel,
        grid=(m // tm, n // tn),
        in_specs=[pl.BlockSpec((tm, k), lambda i, j: (i, 0)),
                  pl.BlockSpec((k, tn), lambda i, j: (0, col0 // tn + j))],
        out_specs=pl.BlockSpec((tn // LANES, tm, LANES), lambda i, j: (j, i, 0)),
        out_shape=jax.ShapeDtypeStruct((n // LANES, m, LANES), out_dtype),
        compiler_params=_cparams(("parallel", "parallel")),
        name=name,
    )(a, b)


def _mm_value_heads_kernel(a_ref, b_ref, o_ref):
    r = _dot(a_ref[...], b_ref[...])
    tm = r.shape[0]
    for p in range(o_ref.shape[0]):
        o_ref[p, :, 0:LANES] = r[:, p * LANES:(p + 1) * LANES].astype(o_ref.dtype)
        o_ref[p, :, LANES:2 * LANES] = jnp.ones((tm, LANES), o_ref.dtype)


def _matmul_value_heads(a, b, tm, tn, name):
    m, k = a.shape
    n = b.shape[1]
    return pl.pallas_call(
        _mm_value_heads_kernel,
        grid=(m // tm, n // tn),
        in_specs=[pl.BlockSpec((tm, k), lambda i, j: (i, 0)),
                  pl.BlockSpec((k, tn), lambda i, j: (0, j))],
        out_specs=pl.BlockSpec((tn // LANES, tm, 2 * LANES), lambda i, j: (j, i, 0)),
        out_shape=jax.ShapeDtypeStruct((n // LANES, m, 2 * LANES), BF16),
        compiler_params=_cparams(("parallel", "parallel")),
        name=name,
    )(a, b)


def _mm2_res_kernel(a1_ref, a2_ref, g2_ref, w1_ref, w2_ref, r_ref, o_ref, a2n):
    @pl.when(pl.program_id(1) == 0)
    def _():
        x = a2_ref[...].astype(F32)
        ms = jnp.mean(x * x, axis=-1, keepdims=True)
        a2n[...] = (x * lax.rsqrt(ms + LN_EPS) * g2_ref[...]).astype(a2n.dtype)

    o_ref[...] = (ALPHA * r_ref[...] + _dot(a1_ref[...], w1_ref[...])
                  + _dot(a2n[...], w2_ref[...]))


def _matmul2_residual(a1, a2, g2, w1, w2, res, tm, tn, name):
    m, k1 = a1.shape
    k2 = a2.shape[1]
    n = w1.shape[1]
    return pl.pallas_call(
        _mm2_res_kernel,
        grid=(m // tm, n // tn),
        in_specs=[pl.BlockSpec((tm, k1), lambda i, j: (i, 0)),
                  pl.BlockSpec((tm, k2), lambda i, j: (i, 0)),
                  pl.BlockSpec((1, k2), lambda i, j: (0, 0)),
                  pl.BlockSpec((k1, tn), lambda i, j: (0, j)),
                  pl.BlockSpec((k2, tn), lambda i, j: (0, j)),
                  pl.BlockSpec((tm, tn), lambda i, j: (i, j))],
        out_specs=pl.BlockSpec((tm, tn), lambda i, j: (i, j)),
        out_shape=jax.ShapeDtypeStruct((m, n), F32),
        scratch_shapes=[pltpu.VMEM((tm, k2), BF16)],
        compiler_params=_cparams(("parallel", "arbitrary")),
        name=name,
    )(a1, a2, g2.reshape(1, k2), w1, w2, res)


def _ln_rows(x, g, b):
    mu = jnp.mean(x, axis=-1, keepdims=True)
    xc = x - mu
    var = jnp.mean(xc * xc, axis=-1, keepdims=True)
    return xc * lax.rsqrt(var + LN_EPS) * g + b


def _conv_kernel(a_ref, g_ref, cw_ref, cb_ref, lg_ref, lb_ref, og_ref, o_ref, hbuf, ybuf, hs):
    ts = a_ref.shape[0]
    nch = hbuf.shape[0]
    cc = hbuf.shape[2]
    d_conv = nch * cc

    nrow = CONV_HALO + ts

    @pl.when(pl.program_id(1) == 0)
    def _():
        hbuf[:, 0:CONV_HALO, :] = jnp.zeros((nch, CONV_HALO, cc), F32)
        hbuf[:, nrow:nrow + SUBLANES, :] = jnp.zeros((nch, SUBLANES, cc), F32)

    for c in range(nch):
        a = a_ref[:, c * cc:(c + 1) * cc].astype(F32)
        g = g_ref[:, c * cc:(c + 1) * cc].astype(F32)
        hbuf[c, CONV_HALO:nrow, :] = a * jax.nn.sigmoid(g)

    first = CONV_HALO - (CONV_WIDTH - 1)

    def chunk_body(c, carry):
        for o in range(1, SUBLANES):
            hs[o - 1] = hbuf[c, o:o + nrow, :]
        for r0 in range(0, ts, CONV_RC):
            acc = jnp.zeros((CONV_RC, cc), F32)
            for j in range(CONV_WIDTH):
                o = (first + j) % SUBLANES
                base = r0 + first + j - o
                rows = hbuf[c, base:base + CONV_RC, :] if o == 0 else hs[o - 1, base:base + CONV_RC, :]
                acc = acc + cw_ref[c, j:j + 1, :] * rows
            ybuf[c, r0:r0 + CONV_RC, :] = acc + cb_ref[c]
        hbuf[c, 0:CONV_HALO, :] = hbuf[c, ts:nrow, :]
        return carry

    lax.fori_loop(0, nch, chunk_body, 0)

    s1 = jnp.zeros((ts, 1), F32)
    for c in range(nch):
        s1 = s1 + jnp.sum(ybuf[c], axis=1, keepdims=True)
    mu = s1 * (1.0 / d_conv)
    s2 = jnp.zeros((ts, 1), F32)
    for c in range(nch):
        yc = ybuf[c] - mu
        s2 = s2 + jnp.sum(yc * yc, axis=1, keepdims=True)
    rstd = lax.rsqrt(s2 * (1.0 / d_conv) + LN_EPS)
    s3 = jnp.zeros((ts, 1), F32)
    for c in range(nch):
        z = (ybuf[c] - mu) * rstd * lg_ref[:, c * cc:(c + 1) * cc] + lb_ref[:, c * cc:(c + 1) * cc]
        z = z * jax.nn.sigmoid(z)
        ybuf[c] = z
        s3 = s3 + jnp.sum(z * z, axis=1, keepdims=True)
    rr = lax.rsqrt(s3 * (1.0 / d_conv) + LN_EPS)
    for c in range(nch):
        o_ref[:, c * cc:(c + 1) * cc] = (ybuf[c] * rr * og_ref[:, c * cc:(c + 1) * cc]).astype(o_ref.dtype)


def _conformer_conv(u, bsz, seq, conv_w, conv_b, ln_g, ln_b, out_g):
    d_conv = u.shape[1] // 2
    ts = min(CONV_TS, seq)
    nch = d_conv // CONV_CC
    nt = seq // ts
    cw = conv_w.reshape(CONV_WIDTH, nch, CONV_CC).transpose(1, 0, 2)
    cb = conv_b.reshape(nch, 1, CONV_CC)
    vec = pl.BlockSpec((1, d_conv), lambda b, i: (0, 0))
    return pl.pallas_call(
        _conv_kernel,
        grid=(bsz, nt),
        in_specs=[pl.BlockSpec((ts, d_conv), lambda b, i: (b * nt + i, 0)),
                  pl.BlockSpec((ts, d_conv), lambda b, i: (b * nt + i, 1)),
                  pl.BlockSpec((nch, CONV_WIDTH, CONV_CC), lambda b, i: (0, 0, 0)),
                  pl.BlockSpec((nch, 1, CONV_CC), lambda b, i: (0, 0, 0)),
                  vec, vec, vec],
        out_specs=pl.BlockSpec((ts, d_conv), lambda b, i: (b * nt + i, 0)),
        out_shape=jax.ShapeDtypeStruct((bsz * seq, d_conv), BF16),
        scratch_shapes=[pltpu.VMEM((nch, CONV_HALO + ts + SUBLANES, CONV_CC), F32),
                        pltpu.VMEM((nch, ts, CONV_CC), F32),
                        pltpu.VMEM((SUBLANES - 1, CONV_HALO + ts, CONV_CC), F32)],
        compiler_params=_cparams(("arbitrary", "arbitrary")),
        name="conformer_conv",
    )(u, u, cw, cb, ln_g.reshape(1, d_conv), ln_b.reshape(1, d_conv), out_g.reshape(1, d_conv))


def _prep_kernel(t_ref, g_ref, ckv_ref, ckvt_ref, kia_ref, kib_ref, kw_ref):
    ckv = t_ref[:, 0:KV_RANK]
    ms = jnp.mean(ckv * ckv, axis=-1, keepdims=True)
    ckv_n = ckv * lax.rsqrt(ms + LN_EPS) * g_ref[...]
    ckv_ref[...] = ckv_n.astype(ckv_ref.dtype)
    tk = ckvt_ref.shape[3]
    for c in range(ckvt_ref.shape[1]):
        ckvt_ref[0, c] = ckv_n[c * tk:(c + 1) * tk, :].T.astype(ckvt_ref.dtype)
    kw = t_ref[:, KV_RANK:KV_RANK + LANES]
    kw_ref[...] = kw
    lane = lax.broadcasted_iota(I32, kw.shape, 1)
    kia_ref[...] = jnp.where(lane < D_IDX, kw, 0.0).astype(kia_ref.dtype)
    kib_ref[...] = jnp.where(lane >= D_IDX, pltpu.roll(kw, D_IDX, 1), 0.0).astype(kib_ref.dtype)


def _prep_latent(tail, kv_norm_g, tm, tk):
    m, w = tail.shape
    return pl.pallas_call(
        _prep_kernel,
        grid=(m // tm,),
        in_specs=[pl.BlockSpec((tm, w), lambda i: (i, 0)),
                  pl.BlockSpec((1, KV_RANK), lambda i: (0, 0))],
        out_specs=[pl.BlockSpec((tm, KV_RANK), lambda i: (i, 0)),
                   pl.BlockSpec((1, tm // tk, KV_RANK, tk), lambda i: (i, 0, 0, 0)),
                   pl.BlockSpec((tm, LANES), lambda i: (i, 0)),
                   pl.BlockSpec((tm, LANES), lambda i: (i, 0)),
                   pl.BlockSpec((tm, LANES), lambda i: (i, 0))],
        out_shape=[jax.ShapeDtypeStruct((m, KV_RANK), BF16),
                   jax.ShapeDtypeStruct((m // tm, tm // tk, KV_RANK, tk), BF16),
                   jax.ShapeDtypeStruct((m, LANES), BF16),
                   jax.ShapeDtypeStruct((m, LANES), BF16),
                   jax.ShapeDtypeStruct((m, LANES), F32)],
        compiler_params=_cparams(("parallel",)),
        name="prep_latent",
    )(tail, kv_norm_g.reshape(1, KV_RANK))


def _indexer_kernel(qi_ref, kw_ref, kia_ref, kib_ref, o_ref, keybuf, wb, *, topk):
    i = pl.program_id(1)
    npairs, tq, _ = qi_ref.shape
    nk = o_ref.shape[1]
    tk = o_ref.shape[3]
    half = tk // 2
    kf = float(topk)
    group = 4

    for h in range(H_IDX):
        wb[h] = jnp.broadcast_to(kw_ref[:, D_IDX + h:D_IDX + h + 1], (tq, LANES))

    row = lax.broadcasted_iota(I32, (tq, tk), 0) + i * tq
    col0 = lax.broadcasted_iota(I32, (tq, tk), 1)

    def score_chunk(j, carry):
        k0 = pl.multiple_of(j * tk, tk)
        kd = jnp.concatenate([kia_ref[pl.ds(k0, tk), :], kib_ref[pl.ds(k0, tk), :]], axis=0)
        acc = jnp.zeros((tq, tk), F32)
        for p0 in range(0, npairs, group):
            lhs = qi_ref[p0:p0 + group].reshape(group * tq, LANES)
            zz = _dot_nt(lhs, kd)
            for p in range(group):
                h = 2 * (p0 + p)
                z = zz[p * tq:(p + 1) * tq]
                we = jnp.concatenate([wb[h]] * (tk // LANES), axis=1)
                wo = jnp.concatenate([wb[h + 1]] * (tk // LANES), axis=1)
                acc = acc + we * jnp.maximum(z[:, 0:tk], 0.0) + wo * jnp.maximum(z[:, tk:2 * tk], 0.0)
        bits = pltpu.bitcast(acc, I32)
        key = jnp.where(bits >= 0, bits, bits ^ jnp.int32(0x7FFFFFFF))
        keybuf[j] = jnp.where(col0 + j * tk <= row, key, INT_MIN)
        return carry

    lax.fori_loop(0, i + 1, score_chunk, 0)

    rows_per_pass = 128
    lane = lax.broadcasted_iota(I32, (rows_per_pass, LANES), 1)

    def count(pred):
        parts = []
        for r0 in range(0, tq, rows_per_pass):
            rows = slice(r0, r0 + rows_per_pass)

            def body(j, c):
                for s in range(tk // LANES):
                    kk = keybuf[j, rows, s * LANES:(s + 1) * LANES]
                    c = c + jnp.where(pred(kk, lane + (j * tk + s * LANES), rows), 1.0, 0.0)
                return c

            parts.append(lax.fori_loop(0, i + 1, body, jnp.zeros((rows_per_pass, LANES), F32)))
        return jnp.concatenate(
            [jnp.broadcast_to(jnp.sum(c, axis=1, keepdims=True), (rows_per_pass, LANES)) for c in parts], axis=0)

    def count_ge(cand):
        return count(lambda kk, col, rows: kk >= cand[rows])

    tau = jnp.where(count_ge(jnp.zeros((tq, LANES), I32)) >= kf, 0, INT_MIN).astype(I32)

    def bit_body(it, tau):
        cand = tau | jnp.left_shift(jnp.int32(1), 30 - it)
        return jnp.where(count_ge(cand) >= kf, cand, tau)

    tau = lax.fori_loop(0, 31, bit_body, tau)
    n_ge = count_ge(tau)
    n_gt = count(lambda kk, col, rows: kk > tau[rows])

    def tie_cut():
        need = kf - n_gt

        def cut_body(it, cut):
            cand = cut + jnp.left_shift(jnp.int32(1), 30 - it)
            below = count(lambda kk, col, rows: (kk == tau[rows]) & (col < cand[rows]))
            return jnp.where(below < need, cand, cut)

        return lax.fori_loop(0, 31, cut_body, jnp.zeros((tq, LANES), I32))

    has_ties = jnp.max(n_ge) > kf
    cut = lax.cond(has_ties, tie_cut, lambda: jnp.full((tq, LANES), 2 ** 30, I32))
    taub = jnp.concatenate([tau] * (tk // LANES), axis=1)
    cutb = jnp.concatenate([cut] * (tk // LANES), axis=1)

    def write_chunk(j, carry):
        kk = keybuf[j]
        sel = ((kk > taub) | ((kk == taub) & (col0 + j * tk <= cutb))) & (kk != INT_MIN)
        o_ref[0, j] = jnp.where(sel, 0.0, NEG).astype(o_ref.dtype)
        return carry

    lax.fori_loop(0, i + 1, write_chunk, 0)

    def write_rest(j, carry):
        o_ref[0, j] = jnp.full((tq, tk), NEG, o_ref.dtype)
        return carry

    lax.fori_loop(i + 1, nk, write_rest, 0)


def _indexer_mask(qi_hm, kw, kia, kib, bsz, seq, topk):
    tq = min(ATT_BLOCK, seq)
    nq = seq // tq
    npairs = qi_hm.shape[0]
    return pl.pallas_call(
        functools.partial(_indexer_kernel, topk=topk),
        grid=(bsz, nq),
        in_specs=[pl.BlockSpec((npairs, tq, LANES), lambda b, i: (0, b * nq + i, 0)),
                  pl.BlockSpec((tq, LANES), lambda b, i: (b * nq + i, 0)),
                  pl.BlockSpec((seq, LANES), lambda b, i: (b, 0)),
                  pl.BlockSpec((seq, LANES), lambda b, i: (b, 0))],
        out_specs=pl.BlockSpec((1, nq, tq, tq), lambda b, i: (b * nq + i, 0, 0, 0)),
        out_shape=jax.ShapeDtypeStruct((bsz * nq, nq, tq, tq), BF16),
        scratch_shapes=[pltpu.VMEM((nq, tq, tq), I32),
                        pltpu.VMEM((H_IDX, tq, LANES), F32)],
        compiler_params=_cparams(("parallel", "parallel")),
        name="indexer_mask",
    )(qi_hm, kw, kia, kib)


def _t5_bucket(dist):
    n = jnp.maximum(dist, 0)
    max_exact = N_BUCKETS // 2
    nf = jnp.maximum(n, 1).astype(F32)
    large = max_exact + (jnp.log(nf / max_exact) / math.log(MAX_DIST / max_exact)
                         * (N_BUCKETS - max_exact)).astype(I32)
    large = jnp.minimum(large, N_BUCKETS - 1)
    return jnp.where(n < max_exact, n, large)


def _key_heads_kernel(wt_ref, ct_ref, o_ref):
    r = _dot(wt_ref[...], ct_ref[0, 0]) * (HEAD_DIM ** -0.5 * LOG2E)
    for h in range(o_ref.shape[1]):
        o_ref[0, h, 0] = r[h * HEAD_DIM:(h + 1) * HEAD_DIM].astype(o_ref.dtype)


def _key_heads(w_uk_t, ckv_t):
    bsz, nk, r_lat, tk = ckv_t.shape
    nh = w_uk_t.shape[0] // HEAD_DIM
    return pl.pallas_call(
        _key_heads_kernel,
        grid=(bsz, nk),
        in_specs=[pl.BlockSpec((nh * HEAD_DIM, r_lat), lambda b, j: (0, 0)),
                  pl.BlockSpec((1, 1, r_lat, tk), lambda b, j: (b, j, 0, 0))],
        out_specs=pl.BlockSpec((1, nh, 1, HEAD_DIM, tk), lambda b, j: (b, 0, j, 0, 0)),
        out_shape=jax.ShapeDtypeStruct((bsz, nh, nk, HEAD_DIM, tk), BF16),
        compiler_params=_cparams(("parallel", "parallel")),
        name="key_heads",
    )(w_uk_t, ckv_t)


def _attn_kernel(relb_ref, q_ref, kt_ref, v_ref, mask_ref, o_ref, acc, m_s, btab):
    b = pl.program_id(0)
    g = pl.program_id(1)
    i = pl.program_id(2)
    hg, tq, _ = q_ref.shape
    nh = btab.shape[1]
    tk = mask_ref.shape[3]

    @pl.when((b == 0) & (g == 0) & (i == 0))
    def _():
        r = lax.broadcasted_iota(I32, (tq, tk), 0)
        c = lax.broadcasted_iota(I32, (tq, tk), 1)
        for t in range(2):
            bk = _t5_bucket(r - c + t * tk)

            def fill(h, carry):
                far = relb_ref[N_BUCKETS - 1, h]
                v = jnp.zeros((tq, tk), F32)
                for k in range(N_BUCKETS):
                    v = jnp.where(bk == k, (relb_ref[k, h] - far) * LOG2E, v)
                btab[t, h] = v
                return carry

            lax.fori_loop(0, nh, fill, 0)

    acc[...] = jnp.zeros(acc.shape, F32)
    m_s[...] = jnp.full(m_s.shape, NEG, F32)

    def lane_tile(v, n):
        return jnp.concatenate([v] * n, axis=1)

    def chunk(j, near):
        rows = pl.ds(pl.multiple_of(j * tk, tk), tk)
        mk = mask_ref[0, j].astype(F32)
        for h in range(hg):
            s = _dot(q_ref[h], kt_ref[0, h, j]) + mk
            if near:
                s = s + btab[i - j, g * hg + h]
            m_old = m_s[h]
            m_new = jnp.maximum(m_old, jnp.max(s, axis=1, keepdims=True))
            a = jnp.exp2(m_old - m_new)
            p = jnp.exp2(s - lane_tile(m_new, tk // LANES))
            acc[h] = lane_tile(a, 2) * acc[h] + _dot(p.astype(BF16), v_ref[h, rows, :])
            m_s[h] = m_new

    def far_chunk(j, carry):
        chunk(j, False)
        return carry

    def near_chunk(j, carry):
        chunk(j, True)
        return carry

    lax.fori_loop(0, jnp.maximum(i - 1, 0), far_chunk, 0)
    lax.fori_loop(jnp.maximum(i - 1, 0), i + 1, near_chunk, 0)

    for h in range(hg):
        o_ref[:, h * HEAD_DIM:(h + 1) * HEAD_DIM] = (
            acc[h, :, :HEAD_DIM] / acc[h, :, HEAD_DIM:]).astype(o_ref.dtype)


def _head_attention(q_hm, k_t, v_hm, mask, rel_bias, bsz, seq):
    nh = q_hm.shape[0]
    hg = ATT_HEAD_GROUP
    tq = mask.shape[2]
    nq = seq // tq
    assert tq == mask.shape[3] and tq >= MAX_DIST
    once = pl.Buffered(1)
    return pl.pallas_call(
        _attn_kernel,
        grid=(bsz, nh // hg, nq),
        in_specs=[pl.BlockSpec(memory_space=pltpu.SMEM),
                  pl.BlockSpec((hg, tq, HEAD_DIM), lambda b, g, i: (g, b * nq + i, 0)),
                  pl.BlockSpec((1, hg, nq, HEAD_DIM, tq), lambda b, g, i: (b, g, 0, 0, 0), pipeline_mode=once),
                  pl.BlockSpec((hg, seq, 2 * HEAD_DIM), lambda b, g, i: (g, b, 0), pipeline_mode=once),
                  pl.BlockSpec((1, nq, tq, tq), lambda b, g, i: (b * nq + i, 0, 0, 0))],
        out_specs=pl.BlockSpec((tq, hg * HEAD_DIM), lambda b, g, i: (b * nq + i, g)),
        out_shape=jax.ShapeDtypeStruct((bsz * seq, nh * HEAD_DIM), BF16),
        scratch_shapes=[pltpu.VMEM((hg, tq, 2 * HEAD_DIM), F32),
                        pltpu.VMEM((hg, tq, LANES), F32),
                        pltpu.VMEM((2, nh, tq, tq), F32)],
        compiler_params=_cparams(("arbitrary", "arbitrary", "arbitrary")),
        name="head_attention",
    )(rel_bias, q_hm, k_t, v_hm, mask)


def _memattn_kernel(x_ref, g1_ref, b1_ref, kv_ref, wq_ref, wo_ref, g_ref, b_ref, wr_ref, o_ref, lg_ref):
    x = _ln_rows(x_ref[...], g1_ref[...], b1_ref[...])
    d_mem = MEM_HEADS * MEM_HEAD_DIM
    q = (_dot(x.astype(BF16), wq_ref[...]) * (MEM_HEAD_DIM ** -0.5)).astype(BF16)
    outs = []
    for h in range(MEM_HEADS):
        lo = h * MEM_HEAD_DIM
        k = kv_ref[:, lo:lo + MEM_HEAD_DIM]
        v = kv_ref[:, d_mem + lo:d_mem + lo + MEM_HEAD_DIM]
        s = _dot_nt(q[:, lo:lo + MEM_HEAD_DIM], k)
        p = jnp.exp(s - jnp.max(s, axis=1, keepdims=True))
        p = p / jnp.sum(p, axis=1, keepdims=True)
        outs.append(_dot(p.astype(BF16), v).astype(BF16))
    o = jnp.concatenate(outs, axis=1)
    x2 = _ln_rows(ALPHA * x + _dot(o, wo_ref[...]), g_ref[...], b_ref[...])
    o_ref[...] = x2
    lg_ref[...] = _dot(x2.astype(BF16), wr_ref[...])


def _memory_attention(pre1, g1, b1, kv, w_mq, w_mo, g, b, w_router, bsz, seq, tm):
    t, d = pre1.shape
    nt = seq // tm
    mem_len = kv.shape[0] // bsz
    d_mem = w_mq.shape[1]
    return pl.pallas_call(
        _memattn_kernel,
        grid=(bsz, nt),
        in_specs=[pl.BlockSpec((tm, d), lambda bi, i: (bi * nt + i, 0)),
                  pl.BlockSpec((1, d), lambda bi, i: (0, 0)),
                  pl.BlockSpec((1, d), lambda bi, i: (0, 0)),
                  pl.BlockSpec((mem_len, 2 * d_mem), lambda bi, i: (bi, 0)),
                  pl.BlockSpec((d, d_mem), lambda bi, i: (0, 0)),
                  pl.BlockSpec((d_mem, d), lambda bi, i: (0, 0)),
                  pl.BlockSpec((1, d), lambda bi, i: (0, 0)),
                  pl.BlockSpec((1, d), lambda bi, i: (0, 0)),
                  pl.BlockSpec((d, LANES), lambda bi, i: (0, 0))],
        out_specs=[pl.BlockSpec((tm, d), lambda bi, i: (bi * nt + i, 0)),
                   pl.BlockSpec((tm, LANES), lambda bi, i: (bi * nt + i, 0))],
        out_shape=[jax.ShapeDtypeStruct((t, d), F32),
                   jax.ShapeDtypeStruct((t, LANES), F32)],
        compiler_params=_cparams(("parallel", "parallel")),
        name="memory_attention",
    )(pre1, g1.reshape(1, d), b1.reshape(1, d), kv, w_mq, w_mo, g.reshape(1, d), b.reshape(1, d), w_router)


def _router_kernel(lg_ref, e1_ref, e2_ref, g1_ref, g2_ref):
    x = lg_ref[...]
    lane = lax.broadcasted_iota(I32, x.shape, 1)
    lane_f = lane.astype(F32)

    def argmax(mask):
        v = jnp.where(mask, x, -jnp.inf)
        mx = jnp.max(v, axis=1, keepdims=True)
        idx = jnp.min(jnp.where(mask & (v == mx), lane_f, float(LANES)), axis=1, keepdims=True)
        return mx, idx.astype(I32)

    gmask = lane < N_GROUPS
    gmax, gsel = argmax(gmask)
    gsum = jnp.sum(jnp.where(gmask, jnp.exp(x - gmax), 0.0), axis=1, keepdims=True)
    g_p = 1.0 / gsum
    lo = N_GROUPS + gsel * EXP_PER_GROUP
    emask = (lane >= lo) & (lane < lo + EXP_PER_GROUP)
    m1, i1 = argmax(emask)
    m2, i2 = argmax(emask & (lane != i1))
    esum = jnp.sum(jnp.where(emask, jnp.exp(x - m1), 0.0), axis=1, keepdims=True)
    p1 = 1.0 / esum
    p2 = jnp.exp(m2 - m1) / esum
    e1_ref[...] = i1 - N_GROUPS
    e2_ref[...] = i2 - N_GROUPS
    g1_ref[...] = g_p * (p1 / (p1 + p2))
    g2_ref[...] = g_p * (p2 / (p1 + p2))


def _router(logits, tm):
    t = logits.shape[0]
    col = pl.BlockSpec((tm, 1), lambda i: (i, 0))
    return pl.pallas_call(
        _router_kernel,
        grid=(t // tm,),
        in_specs=[pl.BlockSpec((tm, LANES), lambda i: (i, 0))],
        out_specs=[col, col, col, col],
        out_shape=[jax.ShapeDtypeStruct((t, 1), I32), jax.ShapeDtypeStruct((t, 1), I32),
                   jax.ShapeDtypeStruct((t, 1), F32), jax.ShapeDtypeStruct((t, 1), F32)],
        compiler_params=_cparams(("parallel",)),
        name="moe_router",
    )(logits)


HI16 = -65536


def _pack_halves(x):
    d = x.shape[1] // 2
    lo = pltpu.bitcast(x[:, :d].astype(BF16).astype(F32), I32)
    hi = pltpu.bitcast(x[:, d:].astype(BF16).astype(F32), I32)
    return lax.shift_right_logical(lo, 16) | (hi & HI16)


def _unpack_halves(u):
    return pltpu.bitcast(lax.shift_left(u, 16), F32), pltpu.bitcast(u & HI16, F32)


def _moe_rank_kernel(e1_ref, e2_ref, r1_ref, r2_ref, cnt_ref, base):
    i = pl.program_id(0)
    tm = e1_ref.shape[0]

    @pl.when(i == 0)
    def _():
        base[...] = jnp.zeros(base.shape, F32)

    lane = lax.broadcasted_iota(I32, (tm, LANES), 1)
    rr = lax.broadcasted_iota(I32, (tm, tm), 0)
    cc = lax.broadcasted_iota(I32, (tm, tm), 1)
    earlier = jnp.where(cc < rr, 1.0, 0.0).astype(BF16)
    for slot, (e_ref, r_ref) in enumerate(((e1_ref, r1_ref), (e2_ref, r2_ref))):
        oh = jnp.where(lane == e_ref[...], 1.0, 0.0)
        before = _dot(earlier, oh.astype(BF16)) + base[slot:slot + 1, :]
        r_ref[...] = jnp.sum(oh * before, axis=1, keepdims=True).astype(I32)
        base[slot:slot + 1, :] = base[slot:slot + 1, :] + jnp.sum(oh, axis=0, keepdims=True)
    cnt_ref[...] = base[...]


def _moe_rank(e1, e2, tm):
    t = e1.shape[0]
    col = pl.BlockSpec((tm, 1), lambda i: (i, 0))
    return pl.pallas_call(
        _moe_rank_kernel,
        grid=(t // tm,),
        in_specs=[col, col],
        out_specs=[col, col, pl.BlockSpec((8, LANES), lambda i: (0, 0))],
        out_shape=[jax.ShapeDtypeStruct((t, 1), I32), jax.ShapeDtypeStruct((t, 1), I32),
                   jax.ShapeDtypeStruct((8, LANES), F32)],
        scratch_shapes=[pltpu.VMEM((8, LANES), F32)],
        compiler_params=_cparams(("arbitrary",)),
        name="moe_rank",
    )(e1, e2)


def _moe_place_kernel(e1_ref, e2_ref, r1_ref, r2_ref, cnt_ref, p1_ref, p2_ref, be_ref, nu_ref):
    tm = e1_ref.shape[0]
    nbp = be_ref.shape[0]
    lane8 = lax.broadcasted_iota(I32, (8, LANES), 1)
    cnt = cnt_ref[...].astype(I32)
    c0 = jnp.broadcast_to(cnt[0:1], (8, LANES))
    c1 = jnp.broadcast_to(cnt[1:2], (8, LANES))
    blk_shift = MOE_BLOCK.bit_length() - 1
    padded = lax.shift_left(lax.shift_right_logical(c0 + c1 + (MOE_BLOCK - 1), blk_shift), blk_shift)
    pad_end = padded
    s = 1
    while s < LANES:
        pad_end = pad_end + jnp.where(lane8 >= s, pltpu.roll(pad_end, s, 1), 0)
        s *= 2
    start0 = (pad_end - padded).astype(F32)
    start1 = (pad_end - padded + c0).astype(F32)

    lane = lax.broadcasted_iota(I32, (tm, LANES), 1)
    for e_ref, r_ref, p_ref, start in ((e1_ref, r1_ref, p1_ref, start0), (e2_ref, r2_ref, p2_ref, start1)):
        seg = jnp.sum(jnp.where(lane == e_ref[...], start[0:1], 0.0), axis=1, keepdims=True)
        p_ref[...] = seg.astype(I32) + r_ref[...]

    block_row = lax.broadcasted_iota(I32, (nbp, LANES), 0) * MOE_BLOCK
    lane_b = lax.broadcasted_iota(I32, (nbp, LANES), 1)
    ended = (pad_end[0:1] <= block_row) & (lane_b < N_EXPERTS)
    be = jnp.sum(jnp.where(ended, 1.0, 0.0), axis=1, keepdims=True)
    be_ref[...] = jnp.minimum(be, N_EXPERTS - 1.0).astype(I32)
    total = jnp.max(pad_end, axis=1, keepdims=True)
    sub = lax.broadcasted_iota(I32, (8, LANES), 0)
    n_used = jnp.broadcast_to(lax.shift_right_logical(total, blk_shift), (8, LANES))
    nu_ref[...] = jnp.where(sub == 0, n_used, jnp.where(sub == 1, pad_end, jnp.where(sub == 2, padded, 0)))


def _moe_place(e1, e2, r1, r2, cnt, nb, tm):
    t = e1.shape[0]
    col = pl.BlockSpec((tm, 1), lambda i: (i, 0))
    return pl.pallas_call(
        _moe_place_kernel,
        grid=(t // tm,),
        in_specs=[col, col, col, col, pl.BlockSpec((8, LANES), lambda i: (0, 0))],
        out_specs=[col, col, pl.BlockSpec((nb, 1), lambda i: (0, 0)),
                   pl.BlockSpec((8, LANES), lambda i: (0, 0))],
        out_shape=[jax.ShapeDtypeStruct((t, 1), I32), jax.ShapeDtypeStruct((t, 1), I32),
                   jax.ShapeDtypeStruct((nb, 1), I32), jax.ShapeDtypeStruct((8, LANES), I32)],
        compiler_params=_cparams(("arbitrary",)),
        name="moe_place",
    )(e1, e2, r1, r2, cnt)


def _moe_dispatch_kernel(seg_ref, p1_ref, p2_ref, x_ref, xs_hbm, pk, zbuf, sem, zsem):
    i = pl.program_id(0)
    n = pl.num_programs(0)
    slot = lax.rem(i, 2)
    blk = x_ref.shape[0]
    nb = xs_hbm.shape[0] // blk

    def row_copy(s, r, pos):
        return pltpu.make_async_copy(pk.at[s, pl.ds(r, 1)], xs_hbm.at[pl.ds(pos, 1)], sem.at[s])

    def drain(s):
        for _ in range(2):
            pltpu.make_async_copy(pk.at[s], xs_hbm.at[pl.ds(0, blk)], sem.at[s]).wait()

    def zero_copy(row0):
        return pltpu.make_async_copy(zbuf, xs_hbm.at[pl.ds(pl.multiple_of(row0, blk), blk)], zsem)

    @pl.when(i == 0)
    def _():
        zbuf[...] = jnp.zeros(zbuf.shape, zbuf.dtype)
        n_used = seg_ref[0, 0]

        def seg_start(e, c):
            @pl.when(seg_ref[2, e] > 0)
            def _():
                zero_copy(seg_ref[1, e] - blk).start()
            return c

        def tail_start(b, c):
            zero_copy(b * blk).start()
            return c

        def seg_wait(e, c):
            @pl.when(seg_ref[2, e] > 0)
            def _():
                zero_copy(0).wait()
            return c

        def tail_wait(b, c):
            zero_copy(0).wait()
            return c

        lax.fori_loop(0, N_EXPERTS, seg_start, 0)
        lax.fori_loop(n_used, nb, tail_start, 0)
        lax.fori_loop(0, N_EXPERTS, seg_wait, 0)
        lax.fori_loop(n_used, nb, tail_wait, 0)

    @pl.when(i >= 2)
    def _():
        drain(slot)

    pk[slot] = _pack_halves(x_ref[...])

    def issue(r, c):
        row_copy(slot, r, p1_ref[0, 0, r]).start()
        row_copy(slot, r, p2_ref[0, 0, r]).start()
        return c

    lax.fori_loop(0, blk, issue, 0, unroll=8)

    @pl.when(i == n - 1)
    def _():
        drain(slot)

    @pl.when((i == n - 1) & (i >= 1))
    def _():
        drain(1 - slot)


def _moe_dispatch(x2, p1, p2, seg, nb):
    t, d = x2.shape
    nt = t // MOE_BLOCK
    rows = nb * MOE_BLOCK
    pos = pl.BlockSpec((1, 1, MOE_BLOCK), lambda i: (i, 0, 0), memory_space=pltpu.SMEM)
    return pl.pallas_call(
        _moe_dispatch_kernel,
        grid=(nt,),
        in_specs=[pl.BlockSpec(memory_space=pltpu.SMEM), pos, pos,
                  pl.BlockSpec((MOE_BLOCK, d), lambda i: (i, 0))],
        out_specs=pl.BlockSpec(memory_space=pl.ANY),
        out_shape=jax.ShapeDtypeStruct((rows, d // 2), I32),
        scratch_shapes=[pltpu.VMEM((2, MOE_BLOCK, d // 2), I32),
                        pltpu.VMEM((MOE_BLOCK, d // 2), I32),
                        pltpu.SemaphoreType.DMA((2,)),
                        pltpu.SemaphoreType.DMA(())],
        compiler_params=_cparams(("arbitrary",)),
        name="moe_dispatch",
    )(seg, p1.reshape(nt, 1, MOE_BLOCK), p2.reshape(nt, 1, MOE_BLOCK), x2)


MAT_PIECES = 4
N_PIECES = 3 * MAT_PIECES
PIECES_PER_BLOCK = 3
PIECE_RING = 4


def _moe_ffn_kernel(be_ref, nu_ref, xs_ref, wg_hbm, wu_hbm, wd_hbm, ys_ref,
                    wgb, wub, wdb, sa, sb, sem, st):
    i = pl.program_id(0)
    nb = be_ref.shape[0]
    n_used = nu_ref[0]
    ra = wgb.shape[1] // MAT_PIECES
    rb = wdb.shape[1] // MAT_PIECES

    def piece_copy(p, e):
        k = p % PIECE_RING
        m, r = divmod(p, MAT_PIECES)
        if m == 0:
            return pltpu.make_async_copy(wg_hbm.at[e, pl.ds(r * ra, ra)], sa.at[k], sem.at[k])
        if m == 1:
            return pltpu.make_async_copy(wu_hbm.at[e, pl.ds(r * ra, ra)], sa.at[k], sem.at[k])
        return pltpu.make_async_copy(wd_hbm.at[e, pl.ds(r * rb, rb)], sb.at[k], sem.at[k])

    def piece_round(p, slot):
        k = p % PIECE_RING
        m, r = divmod(p, MAT_PIECES)
        if m == 0:
            wgb[slot, pl.ds(r * ra, ra), :] = sa[k].astype(BF16)
        elif m == 1:
            wub[slot, pl.ds(r * ra, ra), :] = sa[k].astype(BF16)
        else:
            wdb[slot, pl.ds(r * rb, rb), :] = sb[k].astype(BF16)

    def start_one():
        e, started, finished = st[1], st[2], st[3]
        can = (started < N_PIECES) & (started - finished < PIECE_RING)
        for p in range(N_PIECES):
            @pl.when(can & (started == p))
            def _():
                piece_copy(p, e).start()
        st[2] = started + can.astype(I32)

    def finish_one(slot):
        e, finished = st[1], st[3]

        @pl.when((e >= 0) & (finished < N_PIECES))
        def _():
            for p in range(N_PIECES):
                @pl.when(finished == p)
                def _():
                    piece_copy(p, e).wait()
                    piece_round(p, slot)
            st[3] = finished + 1
            start_one()

    def prepare(e):
        st[1] = e
        st[2] = 0
        st[3] = 0

        @pl.when(e >= 0)
        def _():
            for _ in range(PIECE_RING):
                start_one()

    @pl.when(i < n_used)
    def _():
        e = be_ref[i]

        @pl.when(i == 0)
        def _():
            st[0] = 1
            prepare(e)

        @pl.when((i == 0) | (e != be_ref[jnp.maximum(i - 1, 0)]))
        def _():
            slot = 1 - st[0]

            def fin(_, c):
                finish_one(slot)
                return c

            lax.fori_loop(0, N_PIECES, fin, 0)
            st[0] = slot
            k = lax.while_loop(lambda k: (k < n_used) & (be_ref[jnp.minimum(k, nb - 1)] == e),
                               lambda k: k + 1, i + 1)
            prepare(jnp.where(k < n_used, be_ref[jnp.minimum(k, nb - 1)], -1))

        def ahead(_, c):
            finish_one(1 - st[0])
            return c

        lax.fori_loop(0, PIECES_PER_BLOCK, ahead, 0)

        slot = st[0]
        lo, hi = _unpack_halves(xs_ref[...])
        lo = lo.astype(BF16)
        hi = hi.astype(BF16)
        d2 = lo.shape[1]
        g = _dot(lo, wgb[slot, :d2]) + _dot(hi, wgb[slot, d2:])
        u = _dot(lo, wub[slot, :d2]) + _dot(hi, wub[slot, d2:])
        hmid = (g * jax.nn.sigmoid(g) * u).astype(BF16)
        ys_ref[...] = _pack_halves(_dot(hmid, wdb[slot]))

    @pl.when(i >= n_used)
    def _():
        ys_ref[...] = jnp.zeros(ys_ref.shape, ys_ref.dtype)


def _moe_ffn(xs, w_gate, w_up, w_down, block_expert, n_used):
    rows, d2 = xs.shape
    nb = rows // MOE_BLOCK
    d = 2 * d2
    ff = w_gate.shape[2]
    grid_spec = pltpu.PrefetchScalarGridSpec(
        num_scalar_prefetch=2,
        grid=(nb,),
        in_specs=[pl.BlockSpec((MOE_BLOCK, d2), lambda i, be, nu: (i, 0)),
                  pl.BlockSpec(memory_space=pl.ANY),
                  pl.BlockSpec(memory_space=pl.ANY),
                  pl.BlockSpec(memory_space=pl.ANY)],
        out_specs=pl.BlockSpec((MOE_BLOCK, d2), lambda i, be, nu: (i, 0)),
        scratch_shapes=[pltpu.VMEM((2, d, ff), BF16), pltpu.VMEM((2, d, ff), BF16),
                        pltpu.VMEM((2, ff, d), BF16),
                        pltpu.VMEM((PIECE_RING, d // MAT_PIECES, ff), F32),
                        pltpu.VMEM((PIECE_RING, ff // MAT_PIECES, d), F32),
                        pltpu.SemaphoreType.DMA((PIECE_RING,)),
                        pltpu.SMEM((4,), I32)],
    )
    return pl.pallas_call(
        _moe_ffn_kernel,
        grid_spec=grid_spec,
        out_shape=jax.ShapeDtypeStruct((rows, d2), I32),
        compiler_params=_cparams(("arbitrary",)),
        name="moe_ffn",
    )(block_expert, n_used, xs, w_gate, w_up, w_down)


def _moe_combine_kernel(p1_ref, p2_ref, q1_ref, q2_ref, x_ref, g1_ref, g2_ref, g_ref, b_ref, ys_hbm,
                        o_ref, yb, sem):
    i = pl.program_id(0)
    n = pl.num_programs(0)
    slot = lax.rem(i, 2)
    blk = x_ref.shape[0]

    def row_copy(s, k, r, pos):
        return pltpu.make_async_copy(ys_hbm.at[pl.ds(pos, 1)], yb.at[s, k, pl.ds(r, 1)], sem.at[s])

    def fetch(s, a_ref, b_ref2):
        def body(r, c):
            row_copy(s, 0, r, a_ref[0, 0, r]).start()
            row_copy(s, 1, r, b_ref2[0, 0, r]).start()
            return c
        lax.fori_loop(0, blk, body, 0, unroll=8)

    @pl.when(i == 0)
    def _():
        fetch(0, p1_ref, p2_ref)

    @pl.when(i + 1 < n)
    def _():
        fetch(1 - slot, q1_ref, q2_ref)

    for k in range(2):
        pltpu.make_async_copy(ys_hbm.at[pl.ds(0, blk)], yb.at[slot, k], sem.at[slot]).wait()
    y1 = jnp.concatenate(_unpack_halves(yb[slot, 0]), axis=1)
    y2 = jnp.concatenate(_unpack_halves(yb[slot, 1]), axis=1)
    y = y1 * g1_ref[...] + y2 * g2_ref[...]
    o_ref[...] = _ln_rows(ALPHA * x_ref[...] + y, g_ref[...], b_ref[...])


def _moe_combine(x2, ys, p1, p2, g1, g2, g, b):
    t, d = x2.shape
    nt = t // MOE_BLOCK
    p1 = p1.reshape(nt, 1, MOE_BLOCK)
    p2 = p2.reshape(nt, 1, MOE_BLOCK)
    pos = pl.BlockSpec((1, 1, MOE_BLOCK), lambda i: (i, 0, 0), memory_space=pltpu.SMEM)
    nxt = pl.BlockSpec((1, 1, MOE_BLOCK), lambda i: (jnp.minimum(i + 1, nt - 1), 0, 0),
                       memory_space=pltpu.SMEM)
    col = pl.BlockSpec((MOE_BLOCK, 1), lambda i: (i, 0))
    vec = pl.BlockSpec((1, d), lambda i: (0, 0))
    return pl.pallas_call(
        _moe_combine_kernel,
        grid=(nt,),
        in_specs=[pos, pos, nxt, nxt, pl.BlockSpec((MOE_BLOCK, d), lambda i: (i, 0)), col, col, vec, vec,
                  pl.BlockSpec(memory_space=pl.ANY)],
        out_specs=pl.BlockSpec((MOE_BLOCK, d), lambda i: (i, 0)),
        out_shape=jax.ShapeDtypeStruct((t, d), F32),
        scratch_shapes=[pltpu.VMEM((2, 2, MOE_BLOCK, d // 2), I32),
                        pltpu.SemaphoreType.DMA((2,))],
        compiler_params=_cparams(("arbitrary",)),
        name="moe_combine",
    )(p1, p2, p1, p2, x2, g1, g2, g.reshape(1, d), b.reshape(1, d), ys)


def _tile(n, pref):
    return pref if n % pref == 0 else n


def _layer(x, mem, w_in, conv_w, conv_b, conv_ln_g, conv_ln_b, kv_norm_g, w_uk, w_uv, rel_bias,
           conv_out_g, attn_out_g, w_out, ln1_g, ln1_b, w_mq, w_mk, w_mv, w_mo, ln2_g, ln2_b,
           w_router_grp, w_router_exp, w_gate, w_up, w_down, ln3_g, ln3_b):
    bsz, seq, d = x.shape
    t = bsz * seq
    d_conv = conv_w.shape[1]
    d_attn = N_HEADS * HEAD_DIM
    c_glu = 2 * d_conv
    c_qi = H_IDX * D_IDX
    o_q, o_kv = c_glu, c_glu + d_attn
    o_qi = o_kv + KV_RANK
    o_ki = o_qi + c_qi
    topk = min(TOPK_MAX, seq // 4)

    xf = x.reshape(t, d)
    w_inb = w_in.astype(BF16)
    tail_w = jnp.concatenate([w_inb[:, o_kv:o_qi], w_inb[:, o_ki:]], axis=1)
    tail_w = jnp.pad(tail_w, ((0, 0), (0, KV_RANK + LANES - tail_w.shape[1])))
    tm = _tile(t, 1024)

    u, xb = _matmul_cast(xf, w_inb, c_glu, BF16, _tile(t, 512), 512, "proj_glu")
    q_hm = _matmul_heads(xb, w_inb, o_q, d_attn, BF16, tm, 512, "proj_q")
    qi_hm = _matmul_heads(xb, w_inb, o_qi, c_qi, BF16, tm, 512, "proj_qidx")
    tail = _matmul(xb, tail_w, F32, tm, KV_RANK + LANES, "proj_tail")

    conv_n = _conformer_conv(u, bsz, seq, conv_w, conv_b, conv_ln_g, conv_ln_b, conv_out_g)
    tk = min(ATT_BLOCK, seq)
    ckv_n, ckv_t, kia, kib, kw = _prep_latent(tail, kv_norm_g, tm, tk)
    ckv_t = ckv_t.reshape(bsz, seq // tk, KV_RANK, tk)
    mask = _indexer_mask(qi_hm, kw, kia, kib, bsz, seq, topk)
    w_uk_t = w_uk.transpose(0, 2, 1).reshape(d_attn, KV_RANK).astype(BF16)
    w_uv_all = w_uv.transpose(1, 0, 2).reshape(KV_RANK, d_attn).astype(BF16)
    k_t = _key_heads(w_uk_t, ckv_t)
    v_hm = _matmul_value_heads(ckv_n, w_uv_all, tm, 512, "value_heads")
    attn = _head_attention(q_hm, k_t, v_hm, mask, rel_bias, bsz, seq)

    w_outb = w_out.astype(BF16)
    pre1 = _matmul2_residual(conv_n, attn, attn_out_g, w_outb[:d_conv], w_outb[d_conv:], xf, tm, 512,
                             "out_proj")

    mem_len = mem.shape[1]
    memb = mem.reshape(bsz * mem_len, d).astype(BF16)
    w_kv = jnp.concatenate([w_mk, w_mv], axis=1).astype(BF16)
    kv = _matmul(memb, w_kv, BF16, _tile(bsz * mem_len, 512), 512, "mem_kv")
    w_router = jnp.concatenate([w_router_grp, w_router_exp], axis=1)
    w_router = jnp.pad(w_router, ((0, 0), (0, LANES - w_router.shape[1]))).astype(BF16)
    x2, logits = _memory_attention(pre1, ln1_g, ln1_b, kv, w_mq.astype(BF16), w_mo.astype(BF16),
                                   ln2_g, ln2_b, w_router, bsz, seq, 256)

    e1, e2, g1, g2 = _router(logits, _tile(t, 1024))
    nb = (2 * t + N_EXPERTS * (MOE_BLOCK - 1) + MOE_BLOCK - 1) // MOE_BLOCK
    r1, r2, cnt = _moe_rank(e1, e2, 512)
    p1, p2, block_expert, n_used = _moe_place(e1, e2, r1, r2, cnt, nb, 512)
    xs = _moe_dispatch(x2, p1, p2, n_used, nb)
    ys = _moe_ffn(xs, w_gate, w_up, w_down, block_expert.reshape(nb), n_used[0, 0:1])
    x3 = _moe_combine(x2, ys, p1, p2, g1, g2, ln3_g, ln3_b)
    return x3.reshape(bsz, seq, d)


def kernel(x, mem, w_in, conv_w, conv_b, conv_ln_g, conv_ln_b, kv_norm_g, w_uk, w_uv, rel_bias, conv_out_g, attn_out_g, w_out, ln1_g, ln1_b, w_mq, w_mk, w_mv, w_mo, ln2_g, ln2_b, w_router_grp, w_router_exp, w_gate, w_up, w_down, ln3_g, ln3_b):
    for l in range(w_in.shape[0]):
        x = _layer(x, mem, w_in[l], conv_w[l], conv_b[l], conv_ln_g[l], conv_ln_b[l], kv_norm_g[l],
                   w_uk[l], w_uv[l], rel_bias, conv_out_g[l], attn_out_g[l], w_out[l], ln1_g[l], ln1_b[l],
                   w_mq[l], w_mk[l], w_mv[l], w_mo[l], ln2_g[l], ln2_b[l], w_router_grp[l],
                   w_router_exp[l], w_gate[l], w_up[l], w_down[l], ln3_g[l], ln3_b[l])
    return x
```

```python
import functools
import math

import jax
import jax.numpy as jnp
from jax import lax
from jax.experimental import pallas as pl
from jax.experimental.pallas import tpu as pltpu

F32 = jnp.float32
BF16 = jnp.bfloat16
I32 = jnp.int32
I16 = jnp.int16

DEPTH = 1
CONV_WIDTH = 31
N_HEADS = 16
HEAD_DIM = 128
KV_RANK = 512
H_IDX = 32
D_IDX = 64
TOPK_MAX = 256
N_BUCKETS = 32
MAX_DIST = 128
MEM_HEADS = 4
MEM_HEAD_DIM = 128
N_GROUPS = 8
EXP_PER_GROUP = 8
N_EXPERTS = N_GROUPS * EXP_PER_GROUP
MOE_BLOCK = 128
ALPHA = (2.0 * DEPTH) ** 0.25
LN_EPS = 1e-5

LANES = 128
SUBLANES = 8
V7X_VMEM_BYTES = 64 * 1024 * 1024
VMEM_LIMIT = 56 * 1024 * 1024
NEG = -1e30
INT_MIN = -(2 ** 31)
HALF16 = 2 ** 15

LOG2E = 1.4426950408889634

ATT_BLOCK = 256
ATT_HEAD_GROUP = 8
CONV_TS = 256
CONV_HALO = 32
CONV_CC = 256
CONV_RC = 32


def _cparams(sem):
    return pltpu.CompilerParams(dimension_semantics=sem, vmem_limit_bytes=VMEM_LIMIT)


def _dot(a, b):
    return jnp.dot(a, b, preferred_element_type=F32)


def _dot_nt(a, b):
    return lax.dot_general(a, b, (((1,), (1,)), ((), ())), preferred_element_type=F32)


def _mm_kernel(a_ref, b_ref, o_ref):
    o_ref[...] = _dot(a_ref[...], b_ref[...]).astype(o_ref.dtype)


def _matmul(a, b, out_dtype, tm, tn, name):
    m, k = a.shape
    n = b.shape[1]
    return pl.pallas_call(
        _mm_kernel,
        grid=(m // tm, n // tn),
        in_specs=[pl.BlockSpec((tm, k), lambda i, j: (i, 0)),
                  pl.BlockSpec((k, tn), lambda i, j: (0, j))],
        out_specs=pl.BlockSpec((tm, tn), lambda i, j: (i, j)),
        out_shape=jax.ShapeDtypeStruct((m, n), out_dtype),
        compiler_params=_cparams(("parallel", "parallel")),
        name=name,
    )(a, b)


def _matmul_cols(a, b, n, out_dtype, tm, tn, name):
    m, k = a.shape
    return pl.pallas_call(
        _mm_kernel,
        grid=(m // tm, n // tn),
        in_specs=[pl.BlockSpec((tm, k), lambda i, j: (i, 0)),
                  pl.BlockSpec((k, tn), lambda i, j: (0, j))],
        out_specs=pl.BlockSpec((tm, tn), lambda i, j: (i, j)),
        out_shape=jax.ShapeDtypeStruct((m, n), out_dtype),
        compiler_params=_cparams(("parallel", "parallel")),
        name=name,
    )(a, b)


def _mm_heads_kernel(a_ref, b_ref, o_ref):
    r = _dot(a_ref[...], b_ref[...])
    for p in range(o_ref.shape[0]):
        o_ref[p] = r[:, p * LANES:(p + 1) * LANES].astype(o_ref.dtype)


def _matmul_heads(a, b, col0, n, out_dtype, tm, tn, name):
    m, k = a.shape
    assert col0 % tn == 0
    return pl.pallas_call(
        _mm_heads_kernel,
        grid=(m // tm, n // tn),
        in_specs=[pl.BlockSpec((tm, k), lambda i, j: (i, 0)),
                  pl.BlockSpec((k, tn), lambda i, j: (0, col0 // tn + j))],
        out_specs=pl.BlockSpec((tn // LANES, tm, LANES), lambda i, j: (j, i, 0)),
        out_shape=jax.ShapeDtypeStruct((n // LANES, m, LANES), out_dtype),
        compiler_params=_cparams(("parallel", "parallel")),
        name=name,
    )(a, b)


def _mm_value_heads_kernel(a_ref, b_ref, o_ref):
    r = _dot(a_ref[...], b_ref[...])
    tm = r.shape[0]
    for p in range(o_ref.shape[0]):
        o_ref[p, :, 0:LANES] = r[:, p * LANES:(p + 1) * LANES].astype(o_ref.dtype)
        o_ref[p, :, LANES:2 * LANES] = jnp.ones((tm, LANES), o_ref.dtype)


def _matmul_value_heads(a, b, tm, tn, name):
    m, k = a.shape
    n = b.shape[1]
    return pl.pallas_call(
        _mm_value_heads_kernel,
        grid=(m // tm, n // tn),
        in_specs=[pl.BlockSpec((tm, k), lambda i, j: (i, 0)),
                  pl.BlockSpec((k, tn), lambda i, j: (0, j))],
        out_specs=pl.BlockSpec((tn // LANES, tm, 2 * LANES), lambda i, j: (j, i, 0)),
        out_shape=jax.ShapeDtypeStruct((n // LANES, m, 2 * LANES), BF16),
        compiler_params=_cparams(("parallel", "parallel")),
        name=name,
    )(a, b)


def _mm2_res_kernel(a1_ref, a2_ref, g2_ref, w1_ref, w2_ref, r_ref, o_ref, a2n):
    @pl.when(pl.program_id(1) == 0)
    def _():
        x = a2_ref[...].astype(F32)
        ms = jnp.mean(x * x, axis=-1, keepdims=True)
        a2n[...] = (x * lax.rsqrt(ms + LN_EPS) * g2_ref[...]).astype(a2n.dtype)

    o_ref[...] = (ALPHA * r_ref[...] + _dot(a1_ref[...], w1_ref[...])
                  + _dot(a2n[...], w2_ref[...]))


def _matmul2_residual(a1, a2, g2, w1, w2, res, tm, tn, name):
    m, k1 = a1.shape
    k2 = a2.shape[1]
    n = w1.shape[1]
    return pl.pallas_call(
        _mm2_res_kernel,
        grid=(m // tm, n // tn),
        in_specs=[pl.BlockSpec((tm, k1), lambda i, j: (i, 0)),
                  pl.BlockSpec((tm, k2), lambda i, j: (i, 0)),
                  pl.BlockSpec((1, k2), lambda i, j: (0, 0)),
                  pl.BlockSpec((k1, tn), lambda i, j: (0, j)),
                  pl.BlockSpec((k2, tn), lambda i, j: (0, j)),
                  pl.BlockSpec((tm, tn), lambda i, j: (i, j))],
        out_specs=pl.BlockSpec((tm, tn), lambda i, j: (i, j)),
        out_shape=jax.ShapeDtypeStruct((m, n), F32),
        scratch_shapes=[pltpu.VMEM((tm, k2), BF16)],
        compiler_params=_cparams(("parallel", "arbitrary")),
        name=name,
    )(a1, a2, g2.reshape(1, k2), w1, w2, res)


def _ln_rows(x, g, b):
    mu = jnp.mean(x, axis=-1, keepdims=True)
    xc = x - mu
    var = jnp.mean(xc * xc, axis=-1, keepdims=True)
    return xc * lax.rsqrt(var + LN_EPS) * g + b


def _conv_kernel(a_ref, g_ref, cw_ref, cb_ref, lg_ref, lb_ref, og_ref, o_ref, hbuf, ybuf, hs):
    ts = a_ref.shape[0]
    nch = hbuf.shape[0]
    cc = hbuf.shape[2]
    d_conv = nch * cc

    nrow = CONV_HALO + ts

    @pl.when(pl.program_id(1) == 0)
    def _():
        hbuf[:, 0:CONV_HALO, :] = jnp.zeros((nch, CONV_HALO, cc), F32)
        hbuf[:, nrow:nrow + SUBLANES, :] = jnp.zeros((nch, SUBLANES, cc), F32)

    for c in range(nch):
        a = a_ref[:, c * cc:(c + 1) * cc].astype(F32)
        g = g_ref[:, c * cc:(c + 1) * cc].astype(F32)
        hbuf[c, CONV_HALO:nrow, :] = a * jax.nn.sigmoid(g)

    first = CONV_HALO - (CONV_WIDTH - 1)

    def chunk_body(c, carry):
        for o in range(1, SUBLANES):
            hs[o - 1] = hbuf[c, o:o + nrow, :]
        for r0 in range(0, ts, CONV_RC):
            acc = jnp.zeros((CONV_RC, cc), F32)
            for j in range(CONV_WIDTH):
                o = (first + j) % SUBLANES
                base = r0 + first + j - o
                rows = hbuf[c, base:base + CONV_RC, :] if o == 0 else hs[o - 1, base:base + CONV_RC, :]
                acc = acc + cw_ref[c, j:j + 1, :] * rows
            ybuf[c, r0:r0 + CONV_RC, :] = acc + cb_ref[c]
        hbuf[c, 0:CONV_HALO, :] = hbuf[c, ts:nrow, :]
        return carry

    lax.fori_loop(0, nch, chunk_body, 0)

    s1 = jnp.zeros((ts, 1), F32)
    for c in range(nch):
        s1 = s1 + jnp.sum(ybuf[c], axis=1, keepdims=True)
    mu = s1 * (1.0 / d_conv)
    s2 = jnp.zeros((ts, 1), F32)
    for c in range(nch):
        yc = ybuf[c] - mu
        s2 = s2 + jnp.sum(yc * yc, axis=1, keepdims=True)
    rstd = lax.rsqrt(s2 * (1.0 / d_conv) + LN_EPS)
    s3 = jnp.zeros((ts, 1), F32)
    for c in range(nch):
        z = (ybuf[c] - mu) * rstd * lg_ref[:, c * cc:(c + 1) * cc] + lb_ref[:, c * cc:(c + 1) * cc]
        z = z * jax.nn.sigmoid(z)
        ybuf[c] = z
        s3 = s3 + jnp.sum(z * z, axis=1, keepdims=True)
    rr = lax.rsqrt(s3 * (1.0 / d_conv) + LN_EPS)
    for c in range(nch):
        o_ref[:, c * cc:(c + 1) * cc] = (ybuf[c] * rr * og_ref[:, c * cc:(c + 1) * cc]).astype(o_ref.dtype)


def _conformer_conv(u, bsz, seq, conv_w, conv_b, ln_g, ln_b, out_g):
    d_conv = u.shape[1] // 2
    ts = min(CONV_TS, seq)
    nch = d_conv // CONV_CC
    nt = seq // ts
    cw = conv_w.reshape(CONV_WIDTH, nch, CONV_CC).transpose(1, 0, 2)
    cb = conv_b.reshape(nch, 1, CONV_CC)
    vec = pl.BlockSpec((1, d_conv), lambda b, i: (0, 0))
    return pl.pallas_call(
        _conv_kernel,
        grid=(bsz, nt),
        in_specs=[pl.BlockSpec((ts, d_conv), lambda b, i: (b * nt + i, 0)),
                  pl.BlockSpec((ts, d_conv), lambda b, i: (b * nt + i, 1)),
                  pl.BlockSpec((nch, CONV_WIDTH, CONV_CC), lambda b, i: (0, 0, 0)),
                  pl.BlockSpec((nch, 1, CONV_CC), lambda b, i: (0, 0, 0)),
                  vec, vec, vec],
        out_specs=pl.BlockSpec((ts, d_conv), lambda b, i: (b * nt + i, 0)),
        out_shape=jax.ShapeDtypeStruct((bsz * seq, d_conv), BF16),
        scratch_shapes=[pltpu.VMEM((nch, CONV_HALO + ts + SUBLANES, CONV_CC), F32),
                        pltpu.VMEM((nch, ts, CONV_CC), F32),
                        pltpu.VMEM((SUBLANES - 1, CONV_HALO + ts, CONV_CC), F32)],
        compiler_params=_cparams(("arbitrary", "arbitrary")),
        name="conformer_conv",
    )(u, u, cw, cb, ln_g.reshape(1, d_conv), ln_b.reshape(1, d_conv), out_g.reshape(1, d_conv))


def _prep_kernel(t_ref, g_ref, ckv_ref, ckvt_ref, kia_ref, kib_ref, kw_ref):
    ckv = t_ref[:, 0:KV_RANK]
    ms = jnp.mean(ckv * ckv, axis=-1, keepdims=True)
    ckv_n = ckv * lax.rsqrt(ms + LN_EPS) * g_ref[...]
    ckv_ref[...] = ckv_n.astype(ckv_ref.dtype)
    tk = ckvt_ref.shape[3]
    for c in range(ckvt_ref.shape[1]):
        ckvt_ref[0, c] = ckv_n[c * tk:(c + 1) * tk, :].T.astype(ckvt_ref.dtype)
    kw = t_ref[:, KV_RANK:KV_RANK + LANES]
    kw_ref[...] = kw
    lane = lax.broadcasted_iota(I32, kw.shape, 1)
    kia_ref[...] = jnp.where(lane < D_IDX, kw, 0.0).astype(kia_ref.dtype)
    kib_ref[...] = jnp.where(lane >= D_IDX, pltpu.roll(kw, D_IDX, 1), 0.0).astype(kib_ref.dtype)


def _prep_latent(tail, kv_norm_g, tm, tk):
    m, w = tail.shape
    return pl.pallas_call(
        _prep_kernel,
        grid=(m // tm,),
        in_specs=[pl.BlockSpec((tm, w), lambda i: (i, 0)),
                  pl.BlockSpec((1, KV_RANK), lambda i: (0, 0))],
        out_specs=[pl.BlockSpec((tm, KV_RANK), lambda i: (i, 0)),
                   pl.BlockSpec((1, tm // tk, KV_RANK, tk), lambda i: (i, 0, 0, 0)),
                   pl.BlockSpec((tm, LANES), lambda i: (i, 0)),
                   pl.BlockSpec((tm, LANES), lambda i: (i, 0)),
                   pl.BlockSpec((tm, LANES), lambda i: (i, 0))],
        out_shape=[jax.ShapeDtypeStruct((m, KV_RANK), BF16),
                   jax.ShapeDtypeStruct((m // tm, tm // tk, KV_RANK, tk), BF16),
                   jax.ShapeDtypeStruct((m, LANES), BF16),
                   jax.ShapeDtypeStruct((m, LANES), BF16),
                   jax.ShapeDtypeStruct((m, LANES), F32)],
        compiler_params=_cparams(("parallel",)),
        name="prep_latent",
    )(tail, kv_norm_g.reshape(1, KV_RANK))


def _indexer_kernel(qi_ref, kw_ref, kia_ref, kib_ref, o_ref, keybuf, hibuf, lobuf, wb, *, topk):
    i = pl.program_id(1)
    npairs, tq, _ = qi_ref.shape
    nk = o_ref.shape[1]
    tk = o_ref.shape[3]
    half = tk // 2
    kf = float(topk)
    group = 4

    for h in range(H_IDX):
        wb[h] = jnp.broadcast_to(kw_ref[:, D_IDX + h:D_IDX + h + 1], (tq, LANES))

    row = lax.broadcasted_iota(I32, (tq, tk), 0) + i * tq
    col0 = lax.broadcasted_iota(I32, (tq, tk), 1)

    def score_chunk(j, carry):
        k0 = pl.multiple_of(j * tk, tk)
        kd = jnp.concatenate([kia_ref[pl.ds(k0, tk), :], kib_ref[pl.ds(k0, tk), :]], axis=0)
        acc = jnp.zeros((tq, tk), F32)
        for p0 in range(0, npairs, group):
            lhs = qi_ref[p0:p0 + group].reshape(group * tq, LANES)
            zz = _dot_nt(lhs, kd)
            for p in range(group):
                h = 2 * (p0 + p)
                z = zz[p * tq:(p + 1) * tq]
                we = jnp.concatenate([wb[h]] * (tk // LANES), axis=1)
                wo = jnp.concatenate([wb[h + 1]] * (tk // LANES), axis=1)
                acc = acc + we * jnp.maximum(z[:, 0:tk], 0.0) + wo * jnp.maximum(z[:, tk:2 * tk], 0.0)
        bits = pltpu.bitcast(acc, I32)
        key = jnp.where(bits >= 0, bits, bits ^ jnp.int32(0x7FFFFFFF))
        key = jnp.where(col0 + j * tk <= row, key, INT_MIN)
        keybuf[j] = key
        hibuf[j] = lax.shift_right_arithmetic(key, 16).astype(I16)
        lobuf[j] = ((key & 0xFFFF) - HALF16).astype(I16)
        return carry

    lax.fori_loop(0, i + 1, score_chunk, 0)

    def count16(buf, cand):
        c16 = cand.astype(I16)

        def body(j, c):
            for s in range(tk // LANES):
                c = c + jnp.where(buf[j, :, s * LANES:(s + 1) * LANES] >= c16, jnp.int16(1), jnp.int16(0))
            return c

        c = lax.fori_loop(0, i + 1, body, jnp.zeros((tq, LANES), I16))
        return jnp.broadcast_to(jnp.sum(c.astype(F32), axis=1, keepdims=True), (tq, LANES))

    def search16(buf):
        t = jnp.where(count16(buf, jnp.zeros((tq, LANES), I32)) >= kf, 0, -HALF16).astype(I32)

        def bit_body(it, t):
            cand = t | jnp.left_shift(jnp.int32(1), 14 - it)
            return jnp.where(count16(buf, cand) >= kf, cand, t)

        return lax.fori_loop(0, 15, bit_body, t)

    tau_hi = search16(hibuf)
    th16 = jnp.concatenate([tau_hi.astype(I16)] * (tk // LANES), axis=1)

    def fold_chunk(j, carry):
        h = hibuf[j]
        lobuf[j] = jnp.where(h > th16, jnp.int16(HALF16 - 1), jnp.where(h < th16, jnp.int16(-HALF16), lobuf[j]))
        return carry

    lax.fori_loop(0, i + 1, fold_chunk, 0)
    tau_lo = search16(lobuf)
    tau = lax.shift_left(tau_hi, 16) | (tau_lo + HALF16)

    rows_per_pass = 128
    lane = lax.broadcasted_iota(I32, (rows_per_pass, LANES), 1)

    def count(pred):
        parts = []
        for r0 in range(0, tq, rows_per_pass):
            rows = slice(r0, r0 + rows_per_pass)

            def body(j, c):
                for s in range(tk // LANES):
                    kk = keybuf[j, rows, s * LANES:(s + 1) * LANES]
                    c = c + jnp.where(pred(kk, lane + (j * tk + s * LANES), rows), 1.0, 0.0)
                return c

            parts.append(lax.fori_loop(0, i + 1, body, jnp.zeros((rows_per_pass, LANES), F32)))
        return jnp.concatenate(
            [jnp.broadcast_to(jnp.sum(c, axis=1, keepdims=True), (rows_per_pass, LANES)) for c in parts], axis=0)

    def count_ge(cand):
        return count(lambda kk, col, rows: kk >= cand[rows])

    n_ge = count_ge(tau)

    def tie_cut():
        need = kf - count(lambda kk, col, rows: kk > tau[rows])

        def cut_body(it, cut):
            cand = cut + jnp.left_shift(jnp.int32(1), 30 - it)
            below = count(lambda kk, col, rows: (kk == tau[rows]) & (col < cand[rows]))
            return jnp.where(below < need, cand, cut)

        return lax.fori_loop(0, 31, cut_body, jnp.zeros((tq, LANES), I32))

    has_ties = jnp.max(n_ge) > kf
    cut = lax.cond(has_ties, tie_cut, lambda: jnp.full((tq, LANES), 2 ** 30, I32))
    taub = jnp.concatenate([tau] * (tk // LANES), axis=1)
    cutb = jnp.concatenate([cut] * (tk // LANES), axis=1)

    def write_chunk(j, carry):
        kk = keybuf[j]
        sel = ((kk > taub) | ((kk == taub) & (col0 + j * tk <= cutb))) & (kk != INT_MIN)
        o_ref[0, j] = jnp.where(sel, 0.0, NEG).astype(o_ref.dtype)
        return carry

    lax.fori_loop(0, i + 1, write_chunk, 0)

    def write_rest(j, carry):
        o_ref[0, j] = jnp.full((tq, tk), NEG, o_ref.dtype)
        return carry

    lax.fori_loop(i + 1, nk, write_rest, 0)


def _indexer_mask(qi_hm, kw, kia, kib, bsz, seq, topk):
    tq = min(ATT_BLOCK, seq)
    nq = seq // tq
    npairs = qi_hm.shape[0]
    return pl.pallas_call(
        functools.partial(_indexer_kernel, topk=topk),
        grid=(bsz, nq),
        in_specs=[pl.BlockSpec((npairs, tq, LANES), lambda b, i: (0, b * nq + i, 0)),
                  pl.BlockSpec((tq, LANES), lambda b, i: (b * nq + i, 0)),
                  pl.BlockSpec((seq, LANES), lambda b, i: (b, 0)),
                  pl.BlockSpec((seq, LANES), lambda b, i: (b, 0))],
        out_specs=pl.BlockSpec((1, nq, tq, tq), lambda b, i: (b * nq + i, 0, 0, 0)),
        out_shape=jax.ShapeDtypeStruct((bsz * nq, nq, tq, tq), BF16),
        scratch_shapes=[pltpu.VMEM((nq, tq, tq), I32),
                        pltpu.VMEM((nq, tq, tq), I16),
                        pltpu.VMEM((nq, tq, tq), I16),
                        pltpu.VMEM((H_IDX, tq, LANES), F32)],
        compiler_params=_cparams(("parallel", "parallel")),
        name="indexer_mask",
    )(qi_hm, kw, kia, kib)


def _t5_bucket(dist):
    n = jnp.maximum(dist, 0)
    max_exact = N_BUCKETS // 2
    nf = jnp.maximum(n, 1).astype(F32)
    large = max_exact + (jnp.log(nf / max_exact) / math.log(MAX_DIST / max_exact)
                         * (N_BUCKETS - max_exact)).astype(I32)
    large = jnp.minimum(large, N_BUCKETS - 1)
    return jnp.where(n < max_exact, n, large)


def _key_heads_kernel(wt_ref, ct_ref, o_ref):
    r = _dot(wt_ref[...], ct_ref[0, 0]) * (HEAD_DIM ** -0.5 * LOG2E)
    for h in range(o_ref.shape[1]):
        o_ref[0, h, 0] = r[h * HEAD_DIM:(h + 1) * HEAD_DIM].astype(o_ref.dtype)


def _key_heads(w_uk_t, ckv_t):
    bsz, nk, r_lat, tk = ckv_t.shape
    nh = w_uk_t.shape[0] // HEAD_DIM
    return pl.pallas_call(
        _key_heads_kernel,
        grid=(bsz, nk),
        in_specs=[pl.BlockSpec((nh * HEAD_DIM, r_lat), lambda b, j: (0, 0)),
                  pl.BlockSpec((1, 1, r_lat, tk), lambda b, j: (b, j, 0, 0))],
        out_specs=pl.BlockSpec((1, nh, 1, HEAD_DIM, tk), lambda b, j: (b, 0, j, 0, 0)),
        out_shape=jax.ShapeDtypeStruct((bsz, nh, nk, HEAD_DIM, tk), BF16),
        compiler_params=_cparams(("parallel", "parallel")),
        name="key_heads",
    )(w_uk_t, ckv_t)


def _attn_kernel(relb_ref, q_ref, kt_ref, v_ref, mask_ref, o_ref, acc, m_s, btab):
    b = pl.program_id(0)
    g = pl.program_id(1)
    i = pl.program_id(2)
    hg, tq, _ = q_ref.shape
    nh = btab.shape[1]
    tk = mask_ref.shape[3]

    @pl.when((b == 0) & (g == 0) & (i == 0))
    def _():
        r = lax.broadcasted_iota(I32, (tq, tk), 0)
        c = lax.broadcasted_iota(I32, (tq, tk), 1)
        for t in range(2):
            bk = _t5_bucket(r - c + t * tk)

            def fill(h, carry):
                far = relb_ref[N_BUCKETS - 1, h]
                v = jnp.zeros((tq, tk), F32)
                for k in range(N_BUCKETS):
                    v = jnp.where(bk == k, (relb_ref[k, h] - far) * LOG2E, v)
                btab[t, h] = v
                return carry

            lax.fori_loop(0, nh, fill, 0)

    acc[...] = jnp.zeros(acc.shape, F32)
    m_s[...] = jnp.full(m_s.shape, NEG, F32)

    def lane_tile(v, n):
        return jnp.concatenate([v] * n, axis=1)

    def chunk(j, near):
        rows = pl.ds(pl.multiple_of(j * tk, tk), tk)
        mk = mask_ref[0, j].astype(F32)
        for h in range(hg):
            s = _dot(q_ref[h], kt_ref[0, h, j]) + mk
            if near:
                s = s + btab[i - j, g * hg + h]
            m_old = m_s[h]
            m_new = jnp.maximum(m_old, jnp.max(s, axis=1, keepdims=True))
            a = jnp.exp2(m_old - m_new)
            p = jnp.exp2(s - lane_tile(m_new, tk // LANES))
            acc[h] = lane_tile(a, 2) * acc[h] + _dot(p.astype(BF16), v_ref[h, rows, :])
            m_s[h] = m_new

    def far_chunk(j, carry):
        chunk(j, False)
        return carry

    def near_chunk(j, carry):
        chunk(j, True)
        return carry

    lax.fori_loop(0, jnp.maximum(i - 1, 0), far_chunk, 0)
    lax.fori_loop(jnp.maximum(i - 1, 0), i + 1, near_chunk, 0)

    for h in range(hg):
        o_ref[:, h * HEAD_DIM:(h + 1) * HEAD_DIM] = (
            acc[h, :, :HEAD_DIM] / acc[h, :, HEAD_DIM:]).astype(o_ref.dtype)


def _head_attention(q_hm, k_t, v_hm, mask, rel_bias, bsz, seq):
    nh = q_hm.shape[0]
    hg = ATT_HEAD_GROUP
    tq = mask.shape[2]
    nq = seq // tq
    assert tq == mask.shape[3] and tq >= MAX_DIST
    once = pl.Buffered(1)
    return pl.pallas_call(
        _attn_kernel,
        grid=(bsz, nh // hg, nq),
        in_specs=[pl.BlockSpec(memory_space=pltpu.SMEM),
                  pl.BlockSpec((hg, tq, HEAD_DIM), lambda b, g, i: (g, b * nq + i, 0)),
                  pl.BlockSpec((1, hg, nq, HEAD_DIM, tq), lambda b, g, i: (b, g, 0, 0, 0), pipeline_mode=once),
                  pl.BlockSpec((hg, seq, 2 * HEAD_DIM), lambda b, g, i: (g, b, 0), pipeline_mode=once),
                  pl.BlockSpec((1, nq, tq, tq), lambda b, g, i: (b * nq + i, 0, 0, 0))],
        out_specs=pl.BlockSpec((tq, hg * HEAD_DIM), lambda b, g, i: (b * nq + i, g)),
        out_shape=jax.ShapeDtypeStruct((bsz * seq, nh * HEAD_DIM), BF16),
        scratch_shapes=[pltpu.VMEM((hg, tq, 2 * HEAD_DIM), F32),
                        pltpu.VMEM((hg, tq, LANES), F32),
                        pltpu.VMEM((2, nh, tq, tq), F32)],
        compiler_params=_cparams(("arbitrary", "arbitrary", "arbitrary")),
        name="head_attention",
    )(rel_bias, q_hm, k_t, v_hm, mask)


def _memattn_kernel(x_ref, g1_ref, b1_ref, kv_ref, wq_ref, wo_ref, g_ref, b_ref, wr_ref, o_ref, lg_ref):
    x = _ln_rows(x_ref[...], g1_ref[...], b1_ref[...])
    d_mem = MEM_HEADS * MEM_HEAD_DIM
    q = (_dot(x.astype(BF16), wq_ref[...]) * (MEM_HEAD_DIM ** -0.5)).astype(BF16)
    outs = []
    for h in range(MEM_HEADS):
        lo = h * MEM_HEAD_DIM
        k = kv_ref[:, lo:lo + MEM_HEAD_DIM]
        v = kv_ref[:, d_mem + lo:d_mem + lo + MEM_HEAD_DIM]
        s = _dot_nt(q[:, lo:lo + MEM_HEAD_DIM], k)
        p = jnp.exp(s - jnp.max(s, axis=1, keepdims=True))
        p = p / jnp.sum(p, axis=1, keepdims=True)
        outs.append(_dot(p.astype(BF16), v).astype(BF16))
    o = jnp.concatenate(outs, axis=1)
    x2 = _ln_rows(ALPHA * x + _dot(o, wo_ref[...]), g_ref[...], b_ref[...])
    o_ref[...] = x2
    lg_ref[...] = _dot(x2.astype(BF16), wr_ref[...])


def _memory_attention(pre1, g1, b1, kv, w_mq, w_mo, g, b, w_router, bsz, seq, tm):
    t, d = pre1.shape
    nt = seq // tm
    mem_len = kv.shape[0] // bsz
    d_mem = w_mq.shape[1]
    return pl.pallas_call(
        _memattn_kernel,
        grid=(bsz, nt),
        in_specs=[pl.BlockSpec((tm, d), lambda bi, i: (bi * nt + i, 0)),
                  pl.BlockSpec((1, d), lambda bi, i: (0, 0)),
                  pl.BlockSpec((1, d), lambda bi, i: (0, 0)),
                  pl.BlockSpec((mem_len, 2 * d_mem), lambda bi, i: (bi, 0)),
                  pl.BlockSpec((d, d_mem), lambda bi, i: (0, 0)),
                  pl.BlockSpec((d_mem, d), lambda bi, i: (0, 0)),
                  pl.BlockSpec((1, d), lambda bi, i: (0, 0)),
                  pl.BlockSpec((1, d), lambda bi, i: (0, 0)),
                  pl.BlockSpec((d, LANES), lambda bi, i: (0, 0))],
        out_specs=[pl.BlockSpec((tm, d), lambda bi, i: (bi * nt + i, 0)),
                   pl.BlockSpec((tm, LANES), lambda bi, i: (bi * nt + i, 0))],
        out_shape=[jax.ShapeDtypeStruct((t, d), F32),
                   jax.ShapeDtypeStruct((t, LANES), F32)],
        compiler_params=_cparams(("parallel", "parallel")),
        name="memory_attention",
    )(pre1, g1.reshape(1, d), b1.reshape(1, d), kv, w_mq, w_mo, g.reshape(1, d), b.reshape(1, d), w_router)


def _router_kernel(lg_ref, e1_ref, e2_ref, g1_ref, g2_ref):
    x = lg_ref[...]
    lane = lax.broadcasted_iota(I32, x.shape, 1)
    lane_f = lane.astype(F32)

    def argmax(mask):
        v = jnp.where(mask, x, -jnp.inf)
        mx = jnp.max(v, axis=1, keepdims=True)
        idx = jnp.min(jnp.where(mask & (v == mx), lane_f, float(LANES)), axis=1, keepdims=True)
        return mx, idx.astype(I32)

    gmask = lane < N_GROUPS
    gmax, gsel = argmax(gmask)
    gsum = jnp.sum(jnp.where(gmask, jnp.exp(x - gmax), 0.0), axis=1, keepdims=True)
    g_p = 1.0 / gsum
    lo = N_GROUPS + gsel * EXP_PER_GROUP
    emask = (lane >= lo) & (lane < lo + EXP_PER_GROUP)
    m1, i1 = argmax(emask)
    m2, i2 = argmax(emask & (lane != i1))
    esum = jnp.sum(jnp.where(emask, jnp.exp(x - m1), 0.0), axis=1, keepdims=True)
    p1 = 1.0 / esum
    p2 = jnp.exp(m2 - m1) / esum
    e1_ref[...] = i1 - N_GROUPS
    e2_ref[...] = i2 - N_GROUPS
    g1_ref[...] = g_p * (p1 / (p1 + p2))
    g2_ref[...] = g_p * (p2 / (p1 + p2))


def _router(logits, tm):
    t = logits.shape[0]
    col = pl.BlockSpec((tm, 1), lambda i: (i, 0))
    return pl.pallas_call(
        _router_kernel,
        grid=(t // tm,),
        in_specs=[pl.BlockSpec((tm, LANES), lambda i: (i, 0))],
        out_specs=[col, col, col, col],
        out_shape=[jax.ShapeDtypeStruct((t, 1), I32), jax.ShapeDtypeStruct((t, 1), I32),
                   jax.ShapeDtypeStruct((t, 1), F32), jax.ShapeDtypeStruct((t, 1), F32)],
        compiler_params=_cparams(("parallel",)),
        name="moe_router",
    )(logits)


HI16 = -65536


def _pack_halves(x):
    d = x.shape[1] // 2
    lo = pltpu.bitcast(x[:, :d].astype(BF16).astype(F32), I32)
    hi = pltpu.bitcast(x[:, d:].astype(BF16).astype(F32), I32)
    return lax.shift_right_logical(lo, 16) | (hi & HI16)


def _unpack_halves(u):
    return pltpu.bitcast(lax.shift_left(u, 16), F32), pltpu.bitcast(u & HI16, F32)


def _moe_rank_kernel(e1_ref, e2_ref, r1_ref, r2_ref, cnt_ref, base):
    i = pl.program_id(0)
    tm = e1_ref.shape[0]

    @pl.when(i == 0)
    def _():
        base[...] = jnp.zeros(base.shape, F32)

    lane = lax.broadcasted_iota(I32, (tm, LANES), 1)
    rr = lax.broadcasted_iota(I32, (tm, tm), 0)
    cc = lax.broadcasted_iota(I32, (tm, tm), 1)
    earlier = jnp.where(cc < rr, 1.0, 0.0).astype(BF16)
    for slot, (e_ref, r_ref) in enumerate(((e1_ref, r1_ref), (e2_ref, r2_ref))):
        oh = jnp.where(lane == e_ref[...], 1.0, 0.0)
        before = _dot(earlier, oh.astype(BF16)) + base[slot:slot + 1, :]
        r_ref[...] = jnp.sum(oh * before, axis=1, keepdims=True).astype(I32)
        base[slot:slot + 1, :] = base[slot:slot + 1, :] + jnp.sum(oh, axis=0, keepdims=True)
    cnt_ref[...] = base[...]


def _moe_rank(e1, e2, tm):
    t = e1.shape[0]
    col = pl.BlockSpec((tm, 1), lambda i: (i, 0))
    return pl.pallas_call(
        _moe_rank_kernel,
        grid=(t // tm,),
        in_specs=[col, col],
        out_specs=[col, col, pl.BlockSpec((8, LANES), lambda i: (0, 0))],
        out_shape=[jax.ShapeDtypeStruct((t, 1), I32), jax.ShapeDtypeStruct((t, 1), I32),
                   jax.ShapeDtypeStruct((8, LANES), F32)],
        scratch_shapes=[pltpu.VMEM((8, LANES), F32)],
        compiler_params=_cparams(("arbitrary",)),
        name="moe_rank",
    )(e1, e2)


def _moe_place_kernel(e1_ref, e2_ref, r1_ref, r2_ref, cnt_ref, p1_ref, p2_ref, be_ref, nu_ref):
    tm = e1_ref.shape[0]
    nbp = be_ref.shape[0]
    lane8 = lax.broadcasted_iota(I32, (8, LANES), 1)
    cnt = cnt_ref[...].astype(I32)
    c0 = jnp.broadcast_to(cnt[0:1], (8, LANES))
    c1 = jnp.broadcast_to(cnt[1:2], (8, LANES))
    blk_shift = MOE_BLOCK.bit_length() - 1
    padded = lax.shift_left(lax.shift_right_logical(c0 + c1 + (MOE_BLOCK - 1), blk_shift), blk_shift)
    pad_end = padded
    s = 1
    while s < LANES:
        pad_end = pad_end + jnp.where(lane8 >= s, pltpu.roll(pad_end, s, 1), 0)
        s *= 2
    start0 = (pad_end - padded).astype(F32)
    start1 = (pad_end - padded + c0).astype(F32)

    lane = lax.broadcasted_iota(I32, (tm, LANES), 1)
    for e_ref, r_ref, p_ref, start in ((e1_ref, r1_ref, p1_ref, start0), (e2_ref, r2_ref, p2_ref, start1)):
        seg = jnp.sum(jnp.where(lane == e_ref[...], start[0:1], 0.0), axis=1, keepdims=True)
        p_ref[...] = seg.astype(I32) + r_ref[...]

    block_row = lax.broadcasted_iota(I32, (nbp, LANES), 0) * MOE_BLOCK
    lane_b = lax.broadcasted_iota(I32, (nbp, LANES), 1)
    ended = (pad_end[0:1] <= block_row) & (lane_b < N_EXPERTS)
    be = jnp.sum(jnp.where(ended, 1.0, 0.0), axis=1, keepdims=True)
    be_ref[...] = jnp.minimum(be, N_EXPERTS - 1.0).astype(I32)
    total = jnp.max(pad_end, axis=1, keepdims=True)
    sub = lax.broadcasted_iota(I32, (8, LANES), 0)
    n_used = jnp.broadcast_to(lax.shift_right_logical(total, blk_shift), (8, LANES))
    nu_ref[...] = jnp.where(sub == 0, n_used, jnp.where(sub == 1, pad_end, jnp.where(sub == 2, padded, 0)))


def _moe_place(e1, e2, r1, r2, cnt, nb, tm):
    t = e1.shape[0]
    col = pl.BlockSpec((tm, 1), lambda i: (i, 0))
    return pl.pallas_call(
        _moe_place_kernel,
        grid=(t // tm,),
        in_specs=[col, col, col, col, pl.BlockSpec((8, LANES), lambda i: (0, 0))],
        out_specs=[col, col, pl.BlockSpec((nb, 1), lambda i: (0, 0)),
                   pl.BlockSpec((8, LANES), lambda i: (0, 0))],
        out_shape=[jax.ShapeDtypeStruct((t, 1), I32), jax.ShapeDtypeStruct((t, 1), I32),
                   jax.ShapeDtypeStruct((nb, 1), I32), jax.ShapeDtypeStruct((8, LANES), I32)],
        compiler_params=_cparams(("arbitrary",)),
        name="moe_place",
    )(e1, e2, r1, r2, cnt)


def _moe_dispatch_kernel(seg_ref, p1_ref, p2_ref, x_ref, xs_hbm, pk, zbuf, sem, zsem):
    i = pl.program_id(0)
    n = pl.num_programs(0)
    slot = lax.rem(i, 2)
    blk = x_ref.shape[0]
    nb = xs_hbm.shape[0] // blk

    def row_copy(s, r, pos):
        return pltpu.make_async_copy(pk.at[s, pl.ds(r, 1)], xs_hbm.at[pl.ds(pos, 1)], sem.at[s])

    def drain(s):
        for _ in range(2):
            pltpu.make_async_copy(pk.at[s], xs_hbm.at[pl.ds(0, blk)], sem.at[s]).wait()

    def zero_copy(row0):
        return pltpu.make_async_copy(zbuf, xs_hbm.at[pl.ds(pl.multiple_of(row0, blk), blk)], zsem)

    @pl.when(i == 0)
    def _():
        zbuf[...] = jnp.zeros(zbuf.shape, zbuf.dtype)
        n_used = seg_ref[0, 0]

        def seg_start(e, c):
            @pl.when(seg_ref[2, e] > 0)
            def _():
                zero_copy(seg_ref[1, e] - blk).start()
            return c

        def tail_start(b, c):
            zero_copy(b * blk).start()
            return c

        def seg_wait(e, c):
            @pl.when(seg_ref[2, e] > 0)
            def _():
                zero_copy(0).wait()
            return c

        def tail_wait(b, c):
            zero_copy(0).wait()
            return c

        lax.fori_loop(0, N_EXPERTS, seg_start, 0)
        lax.fori_loop(n_used, nb, tail_start, 0)
        lax.fori_loop(0, N_EXPERTS, seg_wait, 0)
        lax.fori_loop(n_used, nb, tail_wait, 0)

    @pl.when(i >= 2)
    def _():
        drain(slot)

    pk[slot] = _pack_halves(x_ref[...])

    def issue(r, c):
        row_copy(slot, r, p1_ref[0, 0, r]).start()
        row_copy(slot, r, p2_ref[0, 0, r]).start()
        return c

    lax.fori_loop(0, blk, issue, 0, unroll=8)

    @pl.when(i == n - 1)
    def _():
        drain(slot)

    @pl.when((i == n - 1) & (i >= 1))
    def _():
        drain(1 - slot)


def _moe_dispatch(x2, p1, p2, seg, nb):
    t, d = x2.shape
    nt = t // MOE_BLOCK
    rows = nb * MOE_BLOCK
    pos = pl.BlockSpec((1, 1, MOE_BLOCK), lambda i: (i, 0, 0), memory_space=pltpu.SMEM)
    return pl.pallas_call(
        _moe_dispatch_kernel,
        grid=(nt,),
        in_specs=[pl.BlockSpec(memory_space=pltpu.SMEM), pos, pos,
                  pl.BlockSpec((MOE_BLOCK, d), lambda i: (i, 0))],
        out_specs=pl.BlockSpec(memory_space=pl.ANY),
        out_shape=jax.ShapeDtypeStruct((rows, d // 2), I32),
        scratch_shapes=[pltpu.VMEM((2, MOE_BLOCK, d // 2), I32),
                        pltpu.VMEM((MOE_BLOCK, d // 2), I32),
                        pltpu.SemaphoreType.DMA((2,)),
                        pltpu.SemaphoreType.DMA(())],
        compiler_params=_cparams(("arbitrary",)),
        name="moe_dispatch",
    )(seg, p1.reshape(nt, 1, MOE_BLOCK), p2.reshape(nt, 1, MOE_BLOCK), x2)


MAT_PIECES = 4
N_PIECES = 3 * MAT_PIECES
PIECES_PER_BLOCK = 3
PIECE_RING = 4


def _moe_ffn_kernel(be_ref, nu_ref, xs_ref, wg_hbm, wu_hbm, wd_hbm, ys_ref,
                    wgb, wub, wdb, sa, sb, sem, st):
    i = pl.program_id(0)
    nb = be_ref.shape[0]
    n_used = nu_ref[0]
    ra = wgb.shape[1] // MAT_PIECES
    rb = wdb.shape[1] // MAT_PIECES

    def piece_copy(p, e):
        k = p % PIECE_RING
        m, r = divmod(p, MAT_PIECES)
        if m == 0:
            return pltpu.make_async_copy(wg_hbm.at[e, pl.ds(r * ra, ra)], sa.at[k], sem.at[k])
        if m == 1:
            return pltpu.make_async_copy(wu_hbm.at[e, pl.ds(r * ra, ra)], sa.at[k], sem.at[k])
        return pltpu.make_async_copy(wd_hbm.at[e, pl.ds(r * rb, rb)], sb.at[k], sem.at[k])

    def piece_round(p, slot):
        k = p % PIECE_RING
        m, r = divmod(p, MAT_PIECES)
        if m == 0:
            wgb[slot, pl.ds(r * ra, ra), :] = sa[k].astype(BF16)
        elif m == 1:
            wub[slot, pl.ds(r * ra, ra), :] = sa[k].astype(BF16)
        else:
            wdb[slot, pl.ds(r * rb, rb), :] = sb[k].astype(BF16)

    def start_one():
        e, started, finished = st[1], st[2], st[3]
        can = (started < N_PIECES) & (started - finished < PIECE_RING)
        for p in range(N_PIECES):
            @pl.when(can & (started == p))
            def _():
                piece_copy(p, e).start()
        st[2] = started + can.astype(I32)

    def finish_one(slot):
        e, finished = st[1], st[3]

        @pl.when((e >= 0) & (finished < N_PIECES))
        def _():
            for p in range(N_PIECES):
                @pl.when(finished == p)
                def _():
                    piece_copy(p, e).wait()
                    piece_round(p, slot)
            st[3] = finished + 1
            start_one()

    def prepare(e):
        st[1] = e
        st[2] = 0
        st[3] = 0

        @pl.when(e >= 0)
        def _():
            for _ in range(PIECE_RING):
                start_one()

    @pl.when(i < n_used)
    def _():
        e = be_ref[i]

        @pl.when(i == 0)
        def _():
            st[0] = 1
            prepare(e)

        @pl.when((i == 0) | (e != be_ref[jnp.maximum(i - 1, 0)]))
        def _():
            slot = 1 - st[0]

            def fin(_, c):
                finish_one(slot)
                return c

            lax.fori_loop(0, N_PIECES, fin, 0)
            st[0] = slot
            k = lax.while_loop(lambda k: (k < n_used) & (be_ref[jnp.minimum(k, nb - 1)] == e),
                               lambda k: k + 1, i + 1)
            prepare(jnp.where(k < n_used, be_ref[jnp.minimum(k, nb - 1)], -1))

        def ahead(_, c):
            finish_one(1 - st[0])
            return c

        lax.fori_loop(0, PIECES_PER_BLOCK, ahead, 0)

        slot = st[0]
        lo, hi = _unpack_halves(xs_ref[...])
        lo = lo.astype(BF16)
        hi = hi.astype(BF16)
        d2 = lo.shape[1]
        g = _dot(lo, wgb[slot, :d2]) + _dot(hi, wgb[slot, d2:])
        u = _dot(lo, wub[slot, :d2]) + _dot(hi, wub[slot, d2:])
        hmid = (g * jax.nn.sigmoid(g) * u).astype(BF16)
        ys_ref[...] = _pack_halves(_dot(hmid, wdb[slot]))

    @pl.when(i >= n_used)
    def _():
        ys_ref[...] = jnp.zeros(ys_ref.shape, ys_ref.dtype)


def _moe_ffn(xs, w_gate, w_up, w_down, block_expert, n_used):
    rows, d2 = xs.shape
    nb = rows // MOE_BLOCK
    d = 2 * d2
    ff = w_gate.shape[2]
    grid_spec = pltpu.PrefetchScalarGridSpec(
        num_scalar_prefetch=2,
        grid=(nb,),
        in_specs=[pl.BlockSpec((MOE_BLOCK, d2), lambda i, be, nu: (i, 0)),
                  pl.BlockSpec(memory_space=pl.ANY),
                  pl.BlockSpec(memory_space=pl.ANY),
                  pl.BlockSpec(memory_space=pl.ANY)],
        out_specs=pl.BlockSpec((MOE_BLOCK, d2), lambda i, be, nu: (i, 0)),
        scratch_shapes=[pltpu.VMEM((2, d, ff), BF16), pltpu.VMEM((2, d, ff), BF16),
                        pltpu.VMEM((2, ff, d), BF16),
                        pltpu.VMEM((PIECE_RING, d // MAT_PIECES, ff), F32),
                        pltpu.VMEM((PIECE_RING, ff // MAT_PIECES, d), F32),
                        pltpu.SemaphoreType.DMA((PIECE_RING,)),
                        pltpu.SMEM((4,), I32)],
    )
    return pl.pallas_call(
        _moe_ffn_kernel,
        grid_spec=grid_spec,
        out_shape=jax.ShapeDtypeStruct((rows, d2), I32),
        compiler_params=_cparams(("arbitrary",)),
        name="moe_ffn",
    )(block_expert, n_used, xs, w_gate, w_up, w_down)


def _moe_combine_kernel(p1_ref, p2_ref, q1_ref, q2_ref, x_ref, g1_ref, g2_ref, g_ref, b_ref, ys_hbm,
                        o_ref, yb, sem):
    i = pl.program_id(0)
    n = pl.num_programs(0)
    slot = lax.rem(i, 2)
    blk = x_ref.shape[0]

    def row_copy(s, k, r, pos):
        return pltpu.make_async_copy(ys_hbm.at[pl.ds(pos, 1)], yb.at[s, k, pl.ds(r, 1)], sem.at[s])

    def fetch(s, a_ref, b_ref2):
        def body(r, c):
            row_copy(s, 0, r, a_ref[0, 0, r]).start()
            row_copy(s, 1, r, b_ref2[0, 0, r]).start()
            return c
        lax.fori_loop(0, blk, body, 0, unroll=8)

    @pl.when(i == 0)
    def _():
        fetch(0, p1_ref, p2_ref)

    @pl.when(i + 1 < n)
    def _():
        fetch(1 - slot, q1_ref, q2_ref)

    for k in range(2):
        pltpu.make_async_copy(ys_hbm.at[pl.ds(0, blk)], yb.at[slot, k], sem.at[slot]).wait()
    y1 = jnp.concatenate(_unpack_halves(yb[slot, 0]), axis=1)
    y2 = jnp.concatenate(_unpack_halves(yb[slot, 1]), axis=1)
    y = y1 * g1_ref[...] + y2 * g2_ref[...]
    o_ref[...] = _ln_rows(ALPHA * x_ref[...] + y, g_ref[...], b_ref[...])


def _moe_combine(x2, ys, p1, p2, g1, g2, g, b):
    t, d = x2.shape
    nt = t // MOE_BLOCK
    p1 = p1.reshape(nt, 1, MOE_BLOCK)
    p2 = p2.reshape(nt, 1, MOE_BLOCK)
    pos = pl.BlockSpec((1, 1, MOE_BLOCK), lambda i: (i, 0, 0), memory_space=pltpu.SMEM)
    nxt = pl.BlockSpec((1, 1, MOE_BLOCK), lambda i: (jnp.minimum(i + 1, nt - 1), 0, 0),
                       memory_space=pltpu.SMEM)
    col = pl.BlockSpec((MOE_BLOCK, 1), lambda i: (i, 0))
    vec = pl.BlockSpec((1, d), lambda i: (0, 0))
    return pl.pallas_call(
        _moe_combine_kernel,
        grid=(nt,),
        in_specs=[pos, pos, nxt, nxt, pl.BlockSpec((MOE_BLOCK, d), lambda i: (i, 0)), col, col, vec, vec,
                  pl.BlockSpec(memory_space=pl.ANY)],
        out_specs=pl.BlockSpec((MOE_BLOCK, d), lambda i: (i, 0)),
        out_shape=jax.ShapeDtypeStruct((t, d), F32),
        scratch_shapes=[pltpu.VMEM((2, 2, MOE_BLOCK, d // 2), I32),
                        pltpu.SemaphoreType.DMA((2,))],
        compiler_params=_cparams(("arbitrary",)),
        name="moe_combine",
    )(p1, p2, p1, p2, x2, g1, g2, g.reshape(1, d), b.reshape(1, d), ys)


def _tile(n, pref):
    return pref if n % pref == 0 else n


def _layer(x, mem, w_in, conv_w, conv_b, conv_ln_g, conv_ln_b, kv_norm_g, w_uk, w_uv, rel_bias,
           conv_out_g, attn_out_g, w_out, ln1_g, ln1_b, w_mq, w_mk, w_mv, w_mo, ln2_g, ln2_b,
           w_router_grp, w_router_exp, w_gate, w_up, w_down, ln3_g, ln3_b):
    bsz, seq, d = x.shape
    t = bsz * seq
    d_conv = conv_w.shape[1]
    d_attn = N_HEADS * HEAD_DIM
    c_glu = 2 * d_conv
    c_qi = H_IDX * D_IDX
    o_q, o_kv = c_glu, c_glu + d_attn
    o_qi = o_kv + KV_RANK
    o_ki = o_qi + c_qi
    topk = min(TOPK_MAX, seq // 4)

    xf = x.reshape(t, d)
    xb = xf.astype(BF16)
    w_inb = w_in.astype(BF16)
    tail_w = jnp.concatenate([w_inb[:, o_kv:o_qi], w_inb[:, o_ki:]], axis=1)
    tail_w = jnp.pad(tail_w, ((0, 0), (0, KV_RANK + LANES - tail_w.shape[1])))
    tm = _tile(t, 1024)

    u = _matmul_cols(xb, w_inb, c_glu, BF16, tm, 512, "proj_glu")
    q_hm = _matmul_heads(xb, w_inb, o_q, d_attn, BF16, tm, 512, "proj_q")
    qi_hm = _matmul_heads(xb, w_inb, o_qi, c_qi, BF16, tm, 512, "proj_qidx")
    tail = _matmul(xb, tail_w, F32, tm, KV_RANK + LANES, "proj_tail")

    conv_n = _conformer_conv(u, bsz, seq, conv_w, conv_b, conv_ln_g, conv_ln_b, conv_out_g)
    tk = min(ATT_BLOCK, seq)
    ckv_n, ckv_t, kia, kib, kw = _prep_latent(tail, kv_norm_g, tm, tk)
    ckv_t = ckv_t.reshape(bsz, seq // tk, KV_RANK, tk)
    mask = _indexer_mask(qi_hm, kw, kia, kib, bsz, seq, topk)
    w_uk_t = w_uk.transpose(0, 2, 1).reshape(d_attn, KV_RANK).astype(BF16)
    w_uv_all = w_uv.transpose(1, 0, 2).reshape(KV_RANK, d_attn).astype(BF16)
    k_t = _key_heads(w_uk_t, ckv_t)
    v_hm = _matmul_value_heads(ckv_n, w_uv_all, tm, 512, "value_heads")
    attn = _head_attention(q_hm, k_t, v_hm, mask, rel_bias, bsz, seq)

    w_outb = w_out.astype(BF16)
    pre1 = _matmul2_residual(conv_n, attn, attn_out_g, w_outb[:d_conv], w_outb[d_conv:], xf, tm, 512,
                             "out_proj")

    mem_len = mem.shape[1]
    memb = mem.reshape(bsz * mem_len, d).astype(BF16)
    w_kv = jnp.concatenate([w_mk, w_mv], axis=1).astype(BF16)
    kv = _matmul(memb, w_kv, BF16, _tile(bsz * mem_len, 512), 512, "mem_kv")
    w_router = jnp.concatenate([w_router_grp, w_router_exp], axis=1)
    w_router = jnp.pad(w_router, ((0, 0), (0, LANES - w_router.shape[1]))).astype(BF16)
    x2, logits = _memory_attention(pre1, ln1_g, ln1_b, kv, w_mq.astype(BF16), w_mo.astype(BF16),
                                   ln2_g, ln2_b, w_router, bsz, seq, 256)

    e1, e2, g1, g2 = _router(logits, _tile(t, 1024))
    nb = (2 * t + N_EXPERTS * (MOE_BLOCK - 1) + MOE_BLOCK - 1) // MOE_BLOCK
    r1, r2, cnt = _moe_rank(e1, e2, 512)
    p1, p2, block_expert, n_used = _moe_place(e1, e2, r1, r2, cnt, nb, 512)
    xs = _moe_dispatch(x2, p1, p2, n_used, nb)
    ys = _moe_ffn(xs, w_gate, w_up, w_down, block_expert.reshape(nb), n_used[0, 0:1])
    x3 = _moe_combine(x2, ys, p1, p2, g1, g2, ln3_g, ln3_b)
    return x3.reshape(bsz, seq, d)


def kernel(x, mem, w_in, conv_w, conv_b, conv_ln_g, conv_ln_b, kv_norm_g, w_uk, w_uv, rel_bias, conv_out_g, attn_out_g, w_out, ln1_g, ln1_b, w_mq, w_mk, w_mv, w_mo, ln2_g, ln2_b, w_router_grp, w_router_exp, w_gate, w_up, w_down, ln3_g, ln3_b):
    for l in range(w_in.shape[0]):
        x = _layer(x, mem, w_in[l], conv_w[l], conv_b[l], conv_ln_g[l], conv_ln_b[l], kv_norm_g[l],
                   w_uk[l], w_uv[l], rel_bias, conv_out_g[l], attn_out_g[l], w_out[l], ln1_g[l], ln1_b[l],
                   w_mq[l], w_mk[l], w_mv[l], w_mo[l], ln2_g[l], ln2_b[l], w_router_grp[l],
                   w_router_exp[l], w_gate[l], w_up[l], w_down[l], ln3_g[l], ln3_b[l])
    return x
```

```python
import functools
import math

import jax
import jax.numpy as jnp
from jax import lax
from jax.experimental import pallas as pl
from jax.experimental.pallas import tpu as pltpu

F32 = jnp.float32
BF16 = jnp.bfloat16
I32 = jnp.int32

DEPTH = 1
CONV_WIDTH = 31
N_HEADS = 16
HEAD_DIM = 128
KV_RANK = 512
H_IDX = 32
D_IDX = 64
TOPK_MAX = 256
N_BUCKETS = 32
MAX_DIST = 128
MEM_HEADS = 4
MEM_HEAD_DIM = 128
N_GROUPS = 8
EXP_PER_GROUP = 8
N_EXPERTS = N_GROUPS * EXP_PER_GROUP
MOE_BLOCK = 128
ALPHA = (2.0 * DEPTH) ** 0.25
LN_EPS = 1e-5

LANES = 128
SUBLANES = 8
V7X_VMEM_BYTES = 64 * 1024 * 1024
VMEM_LIMIT = 56 * 1024 * 1024
NEG = -1e30
INT_MIN = -(2 ** 31)

LOG2E = 1.4426950408889634

ATT_BLOCK = 256
ATT_HEAD_GROUP = 8
CONV_TS = 256
CONV_HALO = 32
CONV_CC = 256
CONV_RC = 32


def _cparams(sem):
    return pltpu.CompilerParams(dimension_semantics=sem, vmem_limit_bytes=VMEM_LIMIT)


def _dot(a, b):
    return jnp.dot(a, b, preferred_element_type=F32)


def _dot_nt(a, b):
    return lax.dot_general(a, b, (((1,), (1,)), ((), ())), preferred_element_type=F32)


def _mm_kernel(a_ref, b_ref, o_ref):
    o_ref[...] = _dot(a_ref[...], b_ref[...]).astype(o_ref.dtype)


def _matmul(a, b, out_dtype, tm, tn, name):
    m, k = a.shape
    n = b.shape[1]
    return pl.pallas_call(
        _mm_kernel,
        grid=(m // tm, n // tn),
        in_specs=[pl.BlockSpec((tm, k), lambda i, j: (i, 0)),
                  pl.BlockSpec((k, tn), lambda i, j: (0, j))],
        out_specs=pl.BlockSpec((tm, tn), lambda i, j: (i, j)),
        out_shape=jax.ShapeDtypeStruct((m, n), out_dtype),
        compiler_params=_cparams(("parallel", "parallel")),
        name=name,
    )(a, b)


def _matmul_cols(a, b, n, out_dtype, tm, tn, name):
    m, k = a.shape
    return pl.pallas_call(
        _mm_kernel,
        grid=(m // tm, n // tn),
        in_specs=[pl.BlockSpec((tm, k), lambda i, j: (i, 0)),
                  pl.BlockSpec((k, tn), lambda i, j: (0, j))],
        out_specs=pl.BlockSpec((tm, tn), lambda i, j: (i, j)),
        out_shape=jax.ShapeDtypeStruct((m, n), out_dtype),
        compiler_params=_cparams(("parallel", "parallel")),
        name=name,
    )(a, b)


def _mm_heads_kernel(a_ref, b_ref, o_ref):
    r = _dot(a_ref[...], b_ref[...])
    for p in range(o_ref.shape[0]):
        o_ref[p] = r[:, p * LANES:(p + 1) * LANES].astype(o_ref.dtype)


def _matmul_heads(a, b, col0, n, out_dtype, tm, tn, name):
    m, k = a.shape
    assert col0 % tn == 0
    return pl.pallas_call(
        _mm_heads_kernel,
        grid=(m // tm, n // tn),
        in_specs=[pl.BlockSpec((tm, k), lambda i, j: (i, 0)),
                  pl.BlockSpec((k, tn), lambda i, j: (0, col0 // tn + j))],
        out_specs=pl.BlockSpec((tn // LANES, tm, LANES), lambda i, j: (j, i, 0)),
        out_shape=jax.ShapeDtypeStruct((n // LANES, m, LANES), out_dtype),
        compiler_params=_cparams(("parallel", "parallel")),
        name=name,
    )(a, b)


def _mm_value_heads_kernel(a_ref, b_ref, o_ref):
    r = _dot(a_ref[...], b_ref[...])
    tm = r.shape[0]
    for p in range(o_ref.shape[0]):
        o_ref[p, :, 0:LANES] = r[:, p * LANES:(p + 1) * LANES].astype(o_ref.dtype)
        o_ref[p, :, LANES:2 * LANES] = jnp.ones((tm, LANES), o_ref.dtype)


def _matmul_value_heads(a, b, tm, tn, name):
    m, k = a.shape
    n = b.shape[1]
    return pl.pallas_call(
        _mm_value_heads_kernel,
        grid=(m // tm, n // tn),
        in_specs=[pl.BlockSpec((tm, k), lambda i, j: (i, 0)),
                  pl.BlockSpec((k, tn), lambda i, j: (0, j))],
        out_specs=pl.BlockSpec((tn // LANES, tm, 2 * LANES), lambda i, j: (j, i, 0)),
        out_shape=jax.ShapeDtypeStruct((n // LANES, m, 2 * LANES), BF16),
        compiler_params=_cparams(("parallel", "parallel")),
        name=name,
    )(a, b)


def _mm2_res_kernel(a1_ref, a2_ref, g2_ref, w1_ref, w2_ref, r_ref, o_ref, a2n):
    @pl.when(pl.program_id(1) == 0)
    def _():
        x = a2_ref[...].astype(F32)
        ms = jnp.mean(x * x, axis=-1, keepdims=True)
        a2n[...] = (x * lax.rsqrt(ms + LN_EPS) * g2_ref[...]).astype(a2n.dtype)

    o_ref[...] = (ALPHA * r_ref[...] + _dot(a1_ref[...], w1_ref[...])
                  + _dot(a2n[...], w2_ref[...]))


def _matmul2_residual(a1, a2, g2, w1, w2, res, tm, tn, name):
    m, k1 = a1.shape
    k2 = a2.shape[1]
    n = w1.shape[1]
    return pl.pallas_call(
        _mm2_res_kernel,
        grid=(m // tm, n // tn),
        in_specs=[pl.BlockSpec((tm, k1), lambda i, j: (i, 0)),
                  pl.BlockSpec((tm, k2), lambda i, j: (i, 0)),
                  pl.BlockSpec((1, k2), lambda i, j: (0, 0)),
                  pl.BlockSpec((k1, tn), lambda i, j: (0, j)),
                  pl.BlockSpec((k2, tn), lambda i, j: (0, j)),
                  pl.BlockSpec((tm, tn), lambda i, j: (i, j))],
        out_specs=pl.BlockSpec((tm, tn), lambda i, j: (i, j)),
        out_shape=jax.ShapeDtypeStruct((m, n), F32),
        scratch_shapes=[pltpu.VMEM((tm, k2), BF16)],
        compiler_params=_cparams(("parallel", "arbitrary")),
        name=name,
    )(a1, a2, g2.reshape(1, k2), w1, w2, res)


def _ln_rows(x, g, b):
    mu = jnp.mean(x, axis=-1, keepdims=True)
    xc = x - mu
    var = jnp.mean(xc * xc, axis=-1, keepdims=True)
    return xc * lax.rsqrt(var + LN_EPS) * g + b


def _conv_kernel(a_ref, g_ref, cw_ref, cb_ref, lg_ref, lb_ref, og_ref, o_ref, hbuf, ybuf, hs):
    ts = a_ref.shape[0]
    nch = hbuf.shape[0]
    cc = hbuf.shape[2]
    d_conv = nch * cc

    nrow = CONV_HALO + ts

    @pl.when(pl.program_id(1) == 0)
    def _():
        hbuf[:, 0:CONV_HALO, :] = jnp.zeros((nch, CONV_HALO, cc), F32)
        hbuf[:, nrow:nrow + SUBLANES, :] = jnp.zeros((nch, SUBLANES, cc), F32)

    for c in range(nch):
        a = a_ref[:, c * cc:(c + 1) * cc].astype(F32)
        g = g_ref[:, c * cc:(c + 1) * cc].astype(F32)
        hbuf[c, CONV_HALO:nrow, :] = a * jax.nn.sigmoid(g)

    first = CONV_HALO - (CONV_WIDTH - 1)

    def chunk_body(c, carry):
        for o in range(1, SUBLANES):
            hs[o - 1] = hbuf[c, o:o + nrow, :]
        for r0 in range(0, ts, CONV_RC):
            acc = jnp.zeros((CONV_RC, cc), F32)
            for j in range(CONV_WIDTH):
                o = (first + j) % SUBLANES
                base = r0 + first + j - o
                rows = hbuf[c, base:base + CONV_RC, :] if o == 0 else hs[o - 1, base:base + CONV_RC, :]
                acc = acc + cw_ref[c, j:j + 1, :] * rows
            ybuf[c, r0:r0 + CONV_RC, :] = acc + cb_ref[c]
        hbuf[c, 0:CONV_HALO, :] = hbuf[c, ts:nrow, :]
        return carry

    lax.fori_loop(0, nch, chunk_body, 0)

    s1 = jnp.zeros((ts, 1), F32)
    for c in range(nch):
        s1 = s1 + jnp.sum(ybuf[c], axis=1, keepdims=True)
    mu = s1 * (1.0 / d_conv)
    s2 = jnp.zeros((ts, 1), F32)
    for c in range(nch):
        yc = ybuf[c] - mu
        s2 = s2 + jnp.sum(yc * yc, axis=1, keepdims=True)
    rstd = lax.rsqrt(s2 * (1.0 / d_conv) + LN_EPS)
    s3 = jnp.zeros((ts, 1), F32)
    for c in range(nch):
        z = (ybuf[c] - mu) * rstd * lg_ref[:, c * cc:(c + 1) * cc] + lb_ref[:, c * cc:(c + 1) * cc]
        z = z * jax.nn.sigmoid(z)
        ybuf[c] = z
        s3 = s3 + jnp.sum(z * z, axis=1, keepdims=True)
    rr = lax.rsqrt(s3 * (1.0 / d_conv) + LN_EPS)
    for c in range(nch):
        o_ref[:, c * cc:(c + 1) * cc] = (ybuf[c] * rr * og_ref[:, c * cc:(c + 1) * cc]).astype(o_ref.dtype)


def _conformer_conv(u, bsz, seq, conv_w, conv_b, ln_g, ln_b, out_g):
    d_conv = u.shape[1] // 2
    ts = min(CONV_TS, seq)
    nch = d_conv // CONV_CC
    nt = seq // ts
    cw = conv_w.reshape(CONV_WIDTH, nch, CONV_CC).transpose(1, 0, 2)
    cb = conv_b.reshape(nch, 1, CONV_CC)
    vec = pl.BlockSpec((1, d_conv), lambda b, i: (0, 0))
    return pl.pallas_call(
        _conv_kernel,
        grid=(bsz, nt),
        in_specs=[pl.BlockSpec((ts, d_conv), lambda b, i: (b * nt + i, 0)),
                  pl.BlockSpec((ts, d_conv), lambda b, i: (b * nt + i, 1)),
                  pl.BlockSpec((nch, CONV_WIDTH, CONV_CC), lambda b, i: (0, 0, 0)),
                  pl.BlockSpec((nch, 1, CONV_CC), lambda b, i: (0, 0, 0)),
                  vec, vec, vec],
        out_specs=pl.BlockSpec((ts, d_conv), lambda b, i: (b * nt + i, 0)),
        out_shape=jax.ShapeDtypeStruct((bsz * seq, d_conv), BF16),
        scratch_shapes=[pltpu.VMEM((nch, CONV_HALO + ts + SUBLANES, CONV_CC), F32),
                        pltpu.VMEM((nch, ts, CONV_CC), F32),
                        pltpu.VMEM((SUBLANES - 1, CONV_HALO + ts, CONV_CC), F32)],
        compiler_params=_cparams(("arbitrary", "arbitrary")),
        name="conformer_conv",
    )(u, u, cw, cb, ln_g.reshape(1, d_conv), ln_b.reshape(1, d_conv), out_g.reshape(1, d_conv))


def _prep_kernel(t_ref, g_ref, ckv_ref, ckvt_ref, kia_ref, kib_ref, kw_ref):
    ckv = t_ref[:, 0:KV_RANK]
    ms = jnp.mean(ckv * ckv, axis=-1, keepdims=True)
    ckv_n = ckv * lax.rsqrt(ms + LN_EPS) * g_ref[...]
    ckv_ref[...] = ckv_n.astype(ckv_ref.dtype)
    tk = ckvt_ref.shape[3]
    for c in range(ckvt_ref.shape[1]):
        ckvt_ref[0, c] = ckv_n[c * tk:(c + 1) * tk, :].T.astype(ckvt_ref.dtype)
    kw = t_ref[:, KV_RANK:KV_RANK + LANES]
    kw_ref[...] = kw
    lane = lax.broadcasted_iota(I32, kw.shape, 1)
    kia_ref[...] = jnp.where(lane < D_IDX, kw, 0.0).astype(kia_ref.dtype)
    kib_ref[...] = jnp.where(lane >= D_IDX, pltpu.roll(kw, D_IDX, 1), 0.0).astype(kib_ref.dtype)


def _prep_latent(tail, kv_norm_g, tm, tk):
    m, w = tail.shape
    return pl.pallas_call(
        _prep_kernel,
        grid=(m // tm,),
        in_specs=[pl.BlockSpec((tm, w), lambda i: (i, 0)),
                  pl.BlockSpec((1, KV_RANK), lambda i: (0, 0))],
        out_specs=[pl.BlockSpec((tm, KV_RANK), lambda i: (i, 0)),
                   pl.BlockSpec((1, tm // tk, KV_RANK, tk), lambda i: (i, 0, 0, 0)),
                   pl.BlockSpec((tm, LANES), lambda i: (i, 0)),
                   pl.BlockSpec((tm, LANES), lambda i: (i, 0)),
                   pl.BlockSpec((tm, LANES), lambda i: (i, 0))],
        out_shape=[jax.ShapeDtypeStruct((m, KV_RANK), BF16),
                   jax.ShapeDtypeStruct((m // tm, tm // tk, KV_RANK, tk), BF16),
                   jax.ShapeDtypeStruct((m, LANES), BF16),
                   jax.ShapeDtypeStruct((m, LANES), BF16),
                   jax.ShapeDtypeStruct((m, LANES), F32)],
        compiler_params=_cparams(("parallel",)),
        name="prep_latent",
    )(tail, kv_norm_g.reshape(1, KV_RANK))


def _indexer_kernel(qi_ref, kw_ref, kia_ref, kib_ref, o_ref, keybuf, wb, *, topk):
    i = pl.program_id(1)
    npairs, tq, _ = qi_ref.shape
    nk = o_ref.shape[1]
    tk = o_ref.shape[3]
    half = tk // 2
    kf = float(topk)
    group = 4

    for h in range(H_IDX):
        wb[h] = jnp.broadcast_to(kw_ref[:, D_IDX + h:D_IDX + h + 1], (tq, LANES))

    row = lax.broadcasted_iota(I32, (tq, tk), 0) + i * tq
    col0 = lax.broadcasted_iota(I32, (tq, tk), 1)

    def score_chunk(j, carry):
        k0 = pl.multiple_of(j * tk, tk)
        kd = jnp.concatenate([kia_ref[pl.ds(k0, tk), :], kib_ref[pl.ds(k0, tk), :]], axis=0)
        acc = jnp.zeros((tq, tk), F32)
        for p0 in range(0, npairs, group):
            lhs = qi_ref[p0:p0 + group].reshape(group * tq, LANES)
            zz = _dot_nt(lhs, kd)
            for p in range(group):
                h = 2 * (p0 + p)
                z = zz[p * tq:(p + 1) * tq]
                we = jnp.concatenate([wb[h]] * (tk // LANES), axis=1)
                wo = jnp.concatenate([wb[h + 1]] * (tk // LANES), axis=1)
                acc = acc + we * jnp.maximum(z[:, 0:tk], 0.0) + wo * jnp.maximum(z[:, tk:2 * tk], 0.0)
        bits = pltpu.bitcast(acc, I32)
        key = jnp.where(bits >= 0, bits, bits ^ jnp.int32(0x7FFFFFFF))
        keybuf[j] = jnp.where(col0 + j * tk <= row, key, INT_MIN)
        return carry

    lax.fori_loop(0, i + 1, score_chunk, 0)

    rows_per_pass = 128
    lane = lax.broadcasted_iota(I32, (rows_per_pass, LANES), 1)

    def count(pred):
        parts = []
        for r0 in range(0, tq, rows_per_pass):
            rows = slice(r0, r0 + rows_per_pass)

            def body(j, c):
                for s in range(tk // LANES):
                    kk = keybuf[j, rows, s * LANES:(s + 1) * LANES]
                    c = c + jnp.where(pred(kk, lane + (j * tk + s * LANES), rows), 1.0, 0.0)
                return c

            parts.append(lax.fori_loop(0, i + 1, body, jnp.zeros((rows_per_pass, LANES), F32)))
        return jnp.concatenate(
            [jnp.broadcast_to(jnp.sum(c, axis=1, keepdims=True), (rows_per_pass, LANES)) for c in parts], axis=0)

    def count_ge(cand):
        return count(lambda kk, col, rows: kk >= cand[rows])

    tau = jnp.where(count_ge(jnp.zeros((tq, LANES), I32)) >= kf, 0, INT_MIN).astype(I32)

    def bit_body(it, tau):
        cand = tau | jnp.left_shift(jnp.int32(1), 30 - it)
        return jnp.where(count_ge(cand) >= kf, cand, tau)

    tau = lax.fori_loop(0, 31, bit_body, tau)
    n_ge = count_ge(tau)

    def tie_cut():
        need = kf - count(lambda kk, col, rows: kk > tau[rows])

        def cut_body(it, cut):
            cand = cut + jnp.left_shift(jnp.int32(1), 30 - it)
            below = count(lambda kk, col, rows: (kk == tau[rows]) & (col < cand[rows]))
            return jnp.where(below < need, cand, cut)

        return lax.fori_loop(0, 31, cut_body, jnp.zeros((tq, LANES), I32))

    has_ties = jnp.max(n_ge) > kf
    cut = lax.cond(has_ties, tie_cut, lambda: jnp.full((tq, LANES), 2 ** 30, I32))
    taub = jnp.concatenate([tau] * (tk // LANES), axis=1)
    cutb = jnp.concatenate([cut] * (tk // LANES), axis=1)

    def write_chunk(j, carry):
        kk = keybuf[j]
        sel = ((kk > taub) | ((kk == taub) & (col0 + j * tk <= cutb))) & (kk != INT_MIN)
        o_ref[0, j] = jnp.where(sel, 0.0, NEG).astype(o_ref.dtype)
        return carry

    lax.fori_loop(0, i + 1, write_chunk, 0)

    def write_rest(j, carry):
        o_ref[0, j] = jnp.full((tq, tk), NEG, o_ref.dtype)
        return carry

    lax.fori_loop(i + 1, nk, write_rest, 0)


def _indexer_mask(qi_hm, kw, kia, kib, bsz, seq, topk):
    tq = min(ATT_BLOCK, seq)
    nq = seq // tq
    npairs = qi_hm.shape[0]
    return pl.pallas_call(
        functools.partial(_indexer_kernel, topk=topk),
        grid=(bsz, nq),
        in_specs=[pl.BlockSpec((npairs, tq, LANES), lambda b, i: (0, b * nq + i, 0)),
                  pl.BlockSpec((tq, LANES), lambda b, i: (b * nq + i, 0)),
                  pl.BlockSpec((seq, LANES), lambda b, i: (b, 0)),
                  pl.BlockSpec((seq, LANES), lambda b, i: (b, 0))],
        out_specs=pl.BlockSpec((1, nq, tq, tq), lambda b, i: (b * nq + i, 0, 0, 0)),
        out_shape=jax.ShapeDtypeStruct((bsz * nq, nq, tq, tq), BF16),
        scratch_shapes=[pltpu.VMEM((nq, tq, tq), I32),
                        pltpu.VMEM((H_IDX, tq, LANES), F32)],
        compiler_params=_cparams(("parallel", "parallel")),
        name="indexer_mask",
    )(qi_hm, kw, kia, kib)


def _t5_bucket(dist):
    n = jnp.maximum(dist, 0)
    max_exact = N_BUCKETS // 2
    nf = jnp.maximum(n, 1).astype(F32)
    large = max_exact + (jnp.log(nf / max_exact) / math.log(MAX_DIST / max_exact)
                         * (N_BUCKETS - max_exact)).astype(I32)
    large = jnp.minimum(large, N_BUCKETS - 1)
    return jnp.where(n < max_exact, n, large)


def _key_heads_kernel(wt_ref, ct_ref, o_ref):
    r = _dot(wt_ref[...], ct_ref[0, 0]) * (HEAD_DIM ** -0.5 * LOG2E)
    for h in range(o_ref.shape[1]):
        o_ref[0, h, 0] = r[h * HEAD_DIM:(h + 1) * HEAD_DIM].astype(o_ref.dtype)


def _key_heads(w_uk_t, ckv_t):
    bsz, nk, r_lat, tk = ckv_t.shape
    nh = w_uk_t.shape[0] // HEAD_DIM
    return pl.pallas_call(
        _key_heads_kernel,
        grid=(bsz, nk),
        in_specs=[pl.BlockSpec((nh * HEAD_DIM, r_lat), lambda b, j: (0, 0)),
                  pl.BlockSpec((1, 1, r_lat, tk), lambda b, j: (b, j, 0, 0))],
        out_specs=pl.BlockSpec((1, nh, 1, HEAD_DIM, tk), lambda b, j: (b, 0, j, 0, 0)),
        out_shape=jax.ShapeDtypeStruct((bsz, nh, nk, HEAD_DIM, tk), BF16),
        compiler_params=_cparams(("parallel", "parallel")),
        name="key_heads",
    )(w_uk_t, ckv_t)


def _attn_kernel(relb_ref, q_ref, kt_ref, v_ref, mask_ref, o_ref, acc, m_s, btab):
    b = pl.program_id(0)
    g = pl.program_id(1)
    i = pl.program_id(2)
    hg, tq, _ = q_ref.shape
    nh = btab.shape[1]
    tk = mask_ref.shape[3]

    @pl.when((b == 0) & (g == 0) & (i == 0))
    def _():
        r = lax.broadcasted_iota(I32, (tq, tk), 0)
        c = lax.broadcasted_iota(I32, (tq, tk), 1)
        for t in range(2):
            bk = _t5_bucket(r - c + t * tk)

            def fill(h, carry):
                far = relb_ref[N_BUCKETS - 1, h]
                v = jnp.zeros((tq, tk), F32)
                for k in range(N_BUCKETS):
                    v = jnp.where(bk == k, (relb_ref[k, h] - far) * LOG2E, v)
                btab[t, h] = v
                return carry

            lax.fori_loop(0, nh, fill, 0)

    acc[...] = jnp.zeros(acc.shape, F32)
    m_s[...] = jnp.full(m_s.shape, NEG, F32)

    def lane_tile(v, n):
        return jnp.concatenate([v] * n, axis=1)

    def chunks(js, near):
        mks = [mask_ref[0, j].astype(F32) for j in js]
        for h in range(hg):
            for j, mk in zip(js, mks):
                s = _dot(q_ref[h], kt_ref[0, h, j]) + mk
                if near:
                    s = s + btab[i - j, g * hg + h]
                m_old = m_s[h]
                m_new = jnp.maximum(m_old, jnp.max(s, axis=1, keepdims=True))
                a = jnp.exp2(m_old - m_new)
                p = jnp.exp2(s - lane_tile(m_new, tk // LANES))
                rows = pl.ds(pl.multiple_of(j * tk, tk), tk)
                acc[h] = lane_tile(a, 2) * acc[h] + _dot(p.astype(BF16), v_ref[h, rows, :])
                m_s[h] = m_new

    n_far = jnp.maximum(i - 1, 0)

    def far_quad(p, carry):
        chunks([4 * p, 4 * p + 1, 4 * p + 2, 4 * p + 3], False)
        return carry

    lax.fori_loop(0, lax.shift_right_logical(n_far, 2), far_quad, 0)
    done = n_far & ~3

    @pl.when((n_far & 2) == 2)
    def _():
        chunks([done, done + 1], False)

    @pl.when((n_far & 1) == 1)
    def _():
        chunks([n_far - 1], False)

    @pl.when(i >= 1)
    def _():
        chunks([i - 1, i], True)

    @pl.when(i == 0)
    def _():
        chunks([i], True)

    for h in range(hg):
        o_ref[:, h * HEAD_DIM:(h + 1) * HEAD_DIM] = (
            acc[h, :, :HEAD_DIM] / acc[h, :, HEAD_DIM:]).astype(o_ref.dtype)


def _head_attention(q_hm, k_t, v_hm, mask, rel_bias, bsz, seq):
    nh = q_hm.shape[0]
    hg = ATT_HEAD_GROUP
    tq = mask.shape[2]
    nq = seq // tq
    assert tq == mask.shape[3] and tq >= MAX_DIST
    once = pl.Buffered(1)
    return pl.pallas_call(
        _attn_kernel,
        grid=(bsz, nh // hg, nq),
        in_specs=[pl.BlockSpec(memory_space=pltpu.SMEM),
                  pl.BlockSpec((hg, tq, HEAD_DIM), lambda b, g, i: (g, b * nq + i, 0)),
                  pl.BlockSpec((1, hg, nq, HEAD_DIM, tq), lambda b, g, i: (b, g, 0, 0, 0), pipeline_mode=once),
                  pl.BlockSpec((hg, seq, 2 * HEAD_DIM), lambda b, g, i: (g, b, 0), pipeline_mode=once),
                  pl.BlockSpec((1, nq, tq, tq), lambda b, g, i: (b * nq + i, 0, 0, 0))],
        out_specs=pl.BlockSpec((tq, hg * HEAD_DIM), lambda b, g, i: (b * nq + i, g)),
        out_shape=jax.ShapeDtypeStruct((bsz * seq, nh * HEAD_DIM), BF16),
        scratch_shapes=[pltpu.VMEM((hg, tq, 2 * HEAD_DIM), F32),
                        pltpu.VMEM((hg, tq, LANES), F32),
                        pltpu.VMEM((2, nh, tq, tq), F32)],
        compiler_params=_cparams(("arbitrary", "arbitrary", "arbitrary")),
        name="head_attention",
    )(rel_bias, q_hm, k_t, v_hm, mask)


def _memattn_kernel(x_ref, g1_ref, b1_ref, kv_ref, wq_ref, wo_ref, g_ref, b_ref, wr_ref, o_ref, lg_ref):
    x = _ln_rows(x_ref[...], g1_ref[...], b1_ref[...])
    d_mem = MEM_HEADS * MEM_HEAD_DIM
    q = (_dot(x.astype(BF16), wq_ref[...]) * (MEM_HEAD_DIM ** -0.5)).astype(BF16)
    outs = []
    for h in range(MEM_HEADS):
        lo = h * MEM_HEAD_DIM
        k = kv_ref[:, lo:lo + MEM_HEAD_DIM]
        v = kv_ref[:, d_mem + lo:d_mem + lo + MEM_HEAD_DIM]
        s = _dot_nt(q[:, lo:lo + MEM_HEAD_DIM], k)
        p = jnp.exp(s - jnp.max(s, axis=1, keepdims=True))
        p = p / jnp.sum(p, axis=1, keepdims=True)
        outs.append(_dot(p.astype(BF16), v).astype(BF16))
    o = jnp.concatenate(outs, axis=1)
    x2 = _ln_rows(ALPHA * x + _dot(o, wo_ref[...]), g_ref[...], b_ref[...])
    o_ref[...] = x2
    lg_ref[...] = _dot(x2.astype(BF16), wr_ref[...])


def _memory_attention(pre1, g1, b1, kv, w_mq, w_mo, g, b, w_router, bsz, seq, tm):
    t, d = pre1.shape
    nt = seq // tm
    mem_len = kv.shape[0] // bsz
    d_mem = w_mq.shape[1]
    return pl.pallas_call(
        _memattn_kernel,
        grid=(bsz, nt),
        in_specs=[pl.BlockSpec((tm, d), lambda bi, i: (bi * nt + i, 0)),
                  pl.BlockSpec((1, d), lambda bi, i: (0, 0)),
                  pl.BlockSpec((1, d), lambda bi, i: (0, 0)),
                  pl.BlockSpec((mem_len, 2 * d_mem), lambda bi, i: (bi, 0)),
                  pl.BlockSpec((d, d_mem), lambda bi, i: (0, 0)),
                  pl.BlockSpec((d_mem, d), lambda bi, i: (0, 0)),
                  pl.BlockSpec((1, d), lambda bi, i: (0, 0)),
                  pl.BlockSpec((1, d), lambda bi, i: (0, 0)),
                  pl.BlockSpec((d, LANES), lambda bi, i: (0, 0))],
        out_specs=[pl.BlockSpec((tm, d), lambda bi, i: (bi * nt + i, 0)),
                   pl.BlockSpec((tm, LANES), lambda bi, i: (bi * nt + i, 0))],
        out_shape=[jax.ShapeDtypeStruct((t, d), F32),
                   jax.ShapeDtypeStruct((t, LANES), F32)],
        compiler_params=_cparams(("parallel", "parallel")),
        name="memory_attention",
    )(pre1, g1.reshape(1, d), b1.reshape(1, d), kv, w_mq, w_mo, g.reshape(1, d), b.reshape(1, d), w_router)


def _router_kernel(lg_ref, e1_ref, e2_ref, g1_ref, g2_ref):
    x = lg_ref[...]
    lane = lax.broadcasted_iota(I32, x.shape, 1)
    lane_f = lane.astype(F32)

    def argmax(mask):
        v = jnp.where(mask, x, -jnp.inf)
        mx = jnp.max(v, axis=1, keepdims=True)
        idx = jnp.min(jnp.where(mask & (v == mx), lane_f, float(LANES)), axis=1, keepdims=True)
        return mx, idx.astype(I32)

    gmask = lane < N_GROUPS
    gmax, gsel = argmax(gmask)
    gsum = jnp.sum(jnp.where(gmask, jnp.exp(x - gmax), 0.0), axis=1, keepdims=True)
    g_p = 1.0 / gsum
    lo = N_GROUPS + gsel * EXP_PER_GROUP
    emask = (lane >= lo) & (lane < lo + EXP_PER_GROUP)
    m1, i1 = argmax(emask)
    m2, i2 = argmax(emask & (lane != i1))
    esum = jnp.sum(jnp.where(emask, jnp.exp(x - m1), 0.0), axis=1, keepdims=True)
    p1 = 1.0 / esum
    p2 = jnp.exp(m2 - m1) / esum
    e1_ref[...] = i1 - N_GROUPS
    e2_ref[...] = i2 - N_GROUPS
    g1_ref[...] = g_p * (p1 / (p1 + p2))
    g2_ref[...] = g_p * (p2 / (p1 + p2))


def _router(logits, tm):
    t = logits.shape[0]
    col = pl.BlockSpec((tm, 1), lambda i: (i, 0))
    return pl.pallas_call(
        _router_kernel,
        grid=(t // tm,),
        in_specs=[pl.BlockSpec((tm, LANES), lambda i: (i, 0))],
        out_specs=[col, col, col, col],
        out_shape=[jax.ShapeDtypeStruct((t, 1), I32), jax.ShapeDtypeStruct((t, 1), I32),
                   jax.ShapeDtypeStruct((t, 1), F32), jax.ShapeDtypeStruct((t, 1), F32)],
        compiler_params=_cparams(("parallel",)),
        name="moe_router",
    )(logits)


HI16 = -65536


def _pack_halves(x):
    d = x.shape[1] // 2
    lo = pltpu.bitcast(x[:, :d].astype(BF16).astype(F32), I32)
    hi = pltpu.bitcast(x[:, d:].astype(BF16).astype(F32), I32)
    return lax.shift_right_logical(lo, 16) | (hi & HI16)


def _unpack_halves(u):
    return pltpu.bitcast(lax.shift_left(u, 16), F32), pltpu.bitcast(u & HI16, F32)


def _moe_rank_kernel(e1_ref, e2_ref, r1_ref, r2_ref, cnt_ref, base):
    i = pl.program_id(0)
    tm = e1_ref.shape[0]

    @pl.when(i == 0)
    def _():
        base[...] = jnp.zeros(base.shape, F32)

    lane = lax.broadcasted_iota(I32, (tm, LANES), 1)
    rr = lax.broadcasted_iota(I32, (tm, tm), 0)
    cc = lax.broadcasted_iota(I32, (tm, tm), 1)
    earlier = jnp.where(cc < rr, 1.0, 0.0).astype(BF16)
    for slot, (e_ref, r_ref) in enumerate(((e1_ref, r1_ref), (e2_ref, r2_ref))):
        oh = jnp.where(lane == e_ref[...], 1.0, 0.0)
        before = _dot(earlier, oh.astype(BF16)) + base[slot:slot + 1, :]
        r_ref[...] = jnp.sum(oh * before, axis=1, keepdims=True).astype(I32)
        base[slot:slot + 1, :] = base[slot:slot + 1, :] + jnp.sum(oh, axis=0, keepdims=True)
    cnt_ref[...] = base[...]


def _moe_rank(e1, e2, tm):
    t = e1.shape[0]
    col = pl.BlockSpec((tm, 1), lambda i: (i, 0))
    return pl.pallas_call(
        _moe_rank_kernel,
        grid=(t // tm,),
        in_specs=[col, col],
        out_specs=[col, col, pl.BlockSpec((8, LANES), lambda i: (0, 0))],
        out_shape=[jax.ShapeDtypeStruct((t, 1), I32), jax.ShapeDtypeStruct((t, 1), I32),
                   jax.ShapeDtypeStruct((8, LANES), F32)],
        scratch_shapes=[pltpu.VMEM((8, LANES), F32)],
        compiler_params=_cparams(("arbitrary",)),
        name="moe_rank",
    )(e1, e2)


def _moe_place_kernel(e1_ref, e2_ref, r1_ref, r2_ref, cnt_ref, p1_ref, p2_ref, be_ref, nu_ref):
    tm = e1_ref.shape[0]
    nbp = be_ref.shape[0]
    lane8 = lax.broadcasted_iota(I32, (8, LANES), 1)
    cnt = cnt_ref[...].astype(I32)
    c0 = jnp.broadcast_to(cnt[0:1], (8, LANES))
    c1 = jnp.broadcast_to(cnt[1:2], (8, LANES))
    blk_shift = MOE_BLOCK.bit_length() - 1
    padded = lax.shift_left(lax.shift_right_logical(c0 + c1 + (MOE_BLOCK - 1), blk_shift), blk_shift)
    pad_end = padded
    s = 1
    while s < LANES:
        pad_end = pad_end + jnp.where(lane8 >= s, pltpu.roll(pad_end, s, 1), 0)
        s *= 2
    start0 = (pad_end - padded).astype(F32)
    start1 = (pad_end - padded + c0).astype(F32)

    lane = lax.broadcasted_iota(I32, (tm, LANES), 1)
    for e_ref, r_ref, p_ref, start in ((e1_ref, r1_ref, p1_ref, start0), (e2_ref, r2_ref, p2_ref, start1)):
        seg = jnp.sum(jnp.where(lane == e_ref[...], start[0:1], 0.0), axis=1, keepdims=True)
        p_ref[...] = seg.astype(I32) + r_ref[...]

    block_row = lax.broadcasted_iota(I32, (nbp, LANES), 0) * MOE_BLOCK
    lane_b = lax.broadcasted_iota(I32, (nbp, LANES), 1)
    ended = (pad_end[0:1] <= block_row) & (lane_b < N_EXPERTS)
    be = jnp.sum(jnp.where(ended, 1.0, 0.0), axis=1, keepdims=True)
    be_ref[...] = jnp.minimum(be, N_EXPERTS - 1.0).astype(I32)
    total = jnp.max(pad_end, axis=1, keepdims=True)
    sub = lax.broadcasted_iota(I32, (8, LANES), 0)
    n_used = jnp.broadcast_to(lax.shift_right_logical(total, blk_shift), (8, LANES))
    nu_ref[...] = jnp.where(sub == 0, n_used, jnp.where(sub == 1, pad_end, jnp.where(sub == 2, padded, 0)))


def _moe_place(e1, e2, r1, r2, cnt, nb, tm):
    t = e1.shape[0]
    col = pl.BlockSpec((tm, 1), lambda i: (i, 0))
    return pl.pallas_call(
        _moe_place_kernel,
        grid=(t // tm,),
        in_specs=[col, col, col, col, pl.BlockSpec((8, LANES), lambda i: (0, 0))],
        out_specs=[col, col, pl.BlockSpec((nb, 1), lambda i: (0, 0)),
                   pl.BlockSpec((8, LANES), lambda i: (0, 0))],
        out_shape=[jax.ShapeDtypeStruct((t, 1), I32), jax.ShapeDtypeStruct((t, 1), I32),
                   jax.ShapeDtypeStruct((nb, 1), I32), jax.ShapeDtypeStruct((8, LANES), I32)],
        compiler_params=_cparams(("arbitrary",)),
        name="moe_place",
    )(e1, e2, r1, r2, cnt)


def _moe_dispatch_kernel(seg_ref, p1_ref, p2_ref, x_ref, xs_hbm, pk, zbuf, sem, zsem):
    i = pl.program_id(0)
    n = pl.num_programs(0)
    slot = lax.rem(i, 2)
    blk = x_ref.shape[0]
    nb = xs_hbm.shape[0] // blk

    def row_copy(s, r, pos):
        return pltpu.make_async_copy(pk.at[s, pl.ds(r, 1)], xs_hbm.at[pl.ds(pos, 1)], sem.at[s])

    def drain(s):
        for _ in range(2):
            pltpu.make_async_copy(pk.at[s], xs_hbm.at[pl.ds(0, blk)], sem.at[s]).wait()

    def zero_copy(row0):
        return pltpu.make_async_copy(zbuf, xs_hbm.at[pl.ds(pl.multiple_of(row0, blk), blk)], zsem)

    @pl.when(i == 0)
    def _():
        zbuf[...] = jnp.zeros(zbuf.shape, zbuf.dtype)
        n_used = seg_ref[0, 0]

        def seg_start(e, c):
            @pl.when(seg_ref[2, e] > 0)
            def _():
                zero_copy(seg_ref[1, e] - blk).start()
            return c

        def tail_start(b, c):
            zero_copy(b * blk).start()
            return c

        def seg_wait(e, c):
            @pl.when(seg_ref[2, e] > 0)
            def _():
                zero_copy(0).wait()
            return c

        def tail_wait(b, c):
            zero_copy(0).wait()
            return c

        lax.fori_loop(0, N_EXPERTS, seg_start, 0)
        lax.fori_loop(n_used, nb, tail_start, 0)
        lax.fori_loop(0, N_EXPERTS, seg_wait, 0)
        lax.fori_loop(n_used, nb, tail_wait, 0)

    @pl.when(i >= 2)
    def _():
        drain(slot)

    pk[slot] = _pack_halves(x_ref[...])

    def issue(r, c):
        row_copy(slot, r, p1_ref[0, 0, r]).start()
        row_copy(slot, r, p2_ref[0, 0, r]).start()
        return c

    lax.fori_loop(0, blk, issue, 0, unroll=8)

    @pl.when(i == n - 1)
    def _():
        drain(slot)

    @pl.when((i == n - 1) & (i >= 1))
    def _():
        drain(1 - slot)


def _moe_dispatch(x2, p1, p2, seg, nb):
    t, d = x2.shape
    nt = t // MOE_BLOCK
    rows = nb * MOE_BLOCK
    pos = pl.BlockSpec((1, 1, MOE_BLOCK), lambda i: (i, 0, 0), memory_space=pltpu.SMEM)
    return pl.pallas_call(
        _moe_dispatch_kernel,
        grid=(nt,),
        in_specs=[pl.BlockSpec(memory_space=pltpu.SMEM), pos, pos,
                  pl.BlockSpec((MOE_BLOCK, d), lambda i: (i, 0))],
        out_specs=pl.BlockSpec(memory_space=pl.ANY),
        out_shape=jax.ShapeDtypeStruct((rows, d // 2), I32),
        scratch_shapes=[pltpu.VMEM((2, MOE_BLOCK, d // 2), I32),
                        pltpu.VMEM((MOE_BLOCK, d // 2), I32),
                        pltpu.SemaphoreType.DMA((2,)),
                        pltpu.SemaphoreType.DMA(())],
        compiler_params=_cparams(("arbitrary",)),
        name="moe_dispatch",
    )(seg, p1.reshape(nt, 1, MOE_BLOCK), p2.reshape(nt, 1, MOE_BLOCK), x2)


MAT_PIECES = 4
N_PIECES = 3 * MAT_PIECES
PIECES_PER_BLOCK = 3
PIECE_RING = 4


def _moe_ffn_kernel(be_ref, nu_ref, xs_ref, wg_hbm, wu_hbm, wd_hbm, ys_ref,
                    wgb, wub, wdb, sa, sb, sem, st):
    i = pl.program_id(0)
    nb = be_ref.shape[0]
    n_used = nu_ref[0]
    ra = wgb.shape[1] // MAT_PIECES
    rb = wdb.shape[1] // MAT_PIECES

    def piece_copy(p, e):
        k = p % PIECE_RING
        m, r = divmod(p, MAT_PIECES)
        if m == 0:
            return pltpu.make_async_copy(wg_hbm.at[e, pl.ds(r * ra, ra)], sa.at[k], sem.at[k])
        if m == 1:
            return pltpu.make_async_copy(wu_hbm.at[e, pl.ds(r * ra, ra)], sa.at[k], sem.at[k])
        return pltpu.make_async_copy(wd_hbm.at[e, pl.ds(r * rb, rb)], sb.at[k], sem.at[k])

    def piece_round(p, slot):
        k = p % PIECE_RING
        m, r = divmod(p, MAT_PIECES)
        if m == 0:
            wgb[slot, pl.ds(r * ra, ra), :] = sa[k].astype(BF16)
        elif m == 1:
            wub[slot, pl.ds(r * ra, ra), :] = sa[k].astype(BF16)
        else:
            wdb[slot, pl.ds(r * rb, rb), :] = sb[k].astype(BF16)

    def start_one():
        e, started, finished = st[1], st[2], st[3]
        can = (started < N_PIECES) & (started - finished < PIECE_RING)
        for p in range(N_PIECES):
            @pl.when(can & (started == p))
            def _():
                piece_copy(p, e).start()
        st[2] = started + can.astype(I32)

    def finish_one(slot):
        e, finished = st[1], st[3]

        @pl.when((e >= 0) & (finished < N_PIECES))
        def _():
            for p in range(N_PIECES):
                @pl.when(finished == p)
                def _():
                    piece_copy(p, e).wait()
                    piece_round(p, slot)
            st[3] = finished + 1
            start_one()

    def prepare(e):
        st[1] = e
        st[2] = 0
        st[3] = 0

        @pl.when(e >= 0)
        def _():
            for _ in range(PIECE_RING):
                start_one()

    @pl.when(i < n_used)
    def _():
        e = be_ref[i]

        @pl.when(i == 0)
        def _():
            st[0] = 1
            prepare(e)

        @pl.when((i == 0) | (e != be_ref[jnp.maximum(i - 1, 0)]))
        def _():
            slot = 1 - st[0]

            def fin(_, c):
                finish_one(slot)
                return c

            lax.fori_loop(0, N_PIECES, fin, 0)
            st[0] = slot
            k = lax.while_loop(lambda k: (k < n_used) & (be_ref[jnp.minimum(k, nb - 1)] == e),
                               lambda k: k + 1, i + 1)
            prepare(jnp.where(k < n_used, be_ref[jnp.minimum(k, nb - 1)], -1))

        def ahead(_, c):
            finish_one(1 - st[0])
            return c

        lax.fori_loop(0, PIECES_PER_BLOCK, ahead, 0)

        slot = st[0]
        lo, hi = _unpack_halves(xs_ref[...])
        lo = lo.astype(BF16)
        hi = hi.astype(BF16)
        d2 = lo.shape[1]
        g = _dot(lo, wgb[slot, :d2]) + _dot(hi, wgb[slot, d2:])
        u = _dot(lo, wub[slot, :d2]) + _dot(hi, wub[slot, d2:])
        hmid = (g * jax.nn.sigmoid(g) * u).astype(BF16)
        ys_ref[...] = _pack_halves(_dot(hmid, wdb[slot]))

    @pl.when(i >= n_used)
    def _():
        ys_ref[...] = jnp.zeros(ys_ref.shape, ys_ref.dtype)


def _moe_ffn(xs, w_gate, w_up, w_down, block_expert, n_used):
    rows, d2 = xs.shape
    nb = rows // MOE_BLOCK
    d = 2 * d2
    ff = w_gate.shape[2]
    grid_spec = pltpu.PrefetchScalarGridSpec(
        num_scalar_prefetch=2,
        grid=(nb,),
        in_specs=[pl.BlockSpec((MOE_BLOCK, d2), lambda i, be, nu: (i, 0)),
                  pl.BlockSpec(memory_space=pl.ANY),
                  pl.BlockSpec(memory_space=pl.ANY),
                  pl.BlockSpec(memory_space=pl.ANY)],
        out_specs=pl.BlockSpec((MOE_BLOCK, d2), lambda i, be, nu: (i, 0)),
        scratch_shapes=[pltpu.VMEM((2, d, ff), BF16), pltpu.VMEM((2, d, ff), BF16),
                        pltpu.VMEM((2, ff, d), BF16),
                        pltpu.VMEM((PIECE_RING, d // MAT_PIECES, ff), F32),
                        pltpu.VMEM((PIECE_RING, ff // MAT_PIECES, d), F32),
                        pltpu.SemaphoreType.DMA((PIECE_RING,)),
                        pltpu.SMEM((4,), I32)],
    )
    return pl.pallas_call(
        _moe_ffn_kernel,
        grid_spec=grid_spec,
        out_shape=jax.ShapeDtypeStruct((rows, d2), I32),
        compiler_params=_cparams(("arbitrary",)),
        name="moe_ffn",
    )(block_expert, n_used, xs, w_gate, w_up, w_down)


def _moe_combine_kernel(p1_ref, p2_ref, q1_ref, q2_ref, x_ref, g1_ref, g2_ref, g_ref, b_ref, ys_hbm,
                        o_ref, yb, sem):
    i = pl.program_id(0)
    n = pl.num_programs(0)
    slot = lax.rem(i, 2)
    blk = x_ref.shape[0]

    def row_copy(s, k, r, pos):
        return pltpu.make_async_copy(ys_hbm.at[pl.ds(pos, 1)], yb.at[s, k, pl.ds(r, 1)], sem.at[s])

    def fetch(s, a_ref, b_ref2):
        def body(r, c):
            row_copy(s, 0, r, a_ref[0, 0, r]).start()
            row_copy(s, 1, r, b_ref2[0, 0, r]).start()
            return c
        lax.fori_loop(0, blk, body, 0, unroll=8)

    @pl.when(i == 0)
    def _():
        fetch(0, p1_ref, p2_ref)

    @pl.when(i + 1 < n)
    def _():
        fetch(1 - slot, q1_ref, q2_ref)

    for k in range(2):
        pltpu.make_async_copy(ys_hbm.at[pl.ds(0, blk)], yb.at[slot, k], sem.at[slot]).wait()
    y1 = jnp.concatenate(_unpack_halves(yb[slot, 0]), axis=1)
    y2 = jnp.concatenate(_unpack_halves(yb[slot, 1]), axis=1)
    y = y1 * g1_ref[...] + y2 * g2_ref[...]
    o_ref[...] = _ln_rows(ALPHA * x_ref[...] + y, g_ref[...], b_ref[...])


def _moe_combine(x2, ys, p1, p2, g1, g2, g, b):
    t, d = x2.shape
    nt = t // MOE_BLOCK
    p1 = p1.reshape(nt, 1, MOE_BLOCK)
    p2 = p2.reshape(nt, 1, MOE_BLOCK)
    pos = pl.BlockSpec((1, 1, MOE_BLOCK), lambda i: (i, 0, 0), memory_space=pltpu.SMEM)
    nxt = pl.BlockSpec((1, 1, MOE_BLOCK), lambda i: (jnp.minimum(i + 1, nt - 1), 0, 0),
                       memory_space=pltpu.SMEM)
    col = pl.BlockSpec((MOE_BLOCK, 1), lambda i: (i, 0))
    vec = pl.BlockSpec((1, d), lambda i: (0, 0))
    return pl.pallas_call(
        _moe_combine_kernel,
        grid=(nt,),
        in_specs=[pos, pos, nxt, nxt, pl.BlockSpec((MOE_BLOCK, d), lambda i: (i, 0)), col, col, vec, vec,
                  pl.BlockSpec(memory_space=pl.ANY)],
        out_specs=pl.BlockSpec((MOE_BLOCK, d), lambda i: (i, 0)),
        out_shape=jax.ShapeDtypeStruct((t, d), F32),
        scratch_shapes=[pltpu.VMEM((2, 2, MOE_BLOCK, d // 2), I32),
                        pltpu.SemaphoreType.DMA((2,))],
        compiler_params=_cparams(("arbitrary",)),
        name="moe_combine",
    )(p1, p2, p1, p2, x2, g1, g2, g.reshape(1, d), b.reshape(1, d), ys)


def _tile(n, pref):
    return pref if n % pref == 0 else n


def _layer(x, mem, w_in, conv_w, conv_b, conv_ln_g, conv_ln_b, kv_norm_g, w_uk, w_uv, rel_bias,
           conv_out_g, attn_out_g, w_out, ln1_g, ln1_b, w_mq, w_mk, w_mv, w_mo, ln2_g, ln2_b,
           w_router_grp, w_router_exp, w_gate, w_up, w_down, ln3_g, ln3_b):
    bsz, seq, d = x.shape
    t = bsz * seq
    d_conv = conv_w.shape[1]
    d_attn = N_HEADS * HEAD_DIM
    c_glu = 2 * d_conv
    c_qi = H_IDX * D_IDX
    o_q, o_kv = c_glu, c_glu + d_attn
    o_qi = o_kv + KV_RANK
    o_ki = o_qi + c_qi
    topk = min(TOPK_MAX, seq // 4)

    xf = x.reshape(t, d)
    xb = xf.astype(BF16)
    w_inb = w_in.astype(BF16)
    tail_w = jnp.concatenate([w_inb[:, o_kv:o_qi], w_inb[:, o_ki:]], axis=1)
    tail_w = jnp.pad(tail_w, ((0, 0), (0, KV_RANK + LANES - tail_w.shape[1])))
    tm = _tile(t, 1024)

    u = _matmul_cols(xb, w_inb, c_glu, BF16, tm, 512, "proj_glu")
    q_hm = _matmul_heads(xb, w_inb, o_q, d_attn, BF16, tm, 512, "proj_q")
    qi_hm = _matmul_heads(xb, w_inb, o_qi, c_qi, BF16, tm, 512, "proj_qidx")
    tail = _matmul(xb, tail_w, F32, tm, KV_RANK + LANES, "proj_tail")

    conv_n = _conformer_conv(u, bsz, seq, conv_w, conv_b, conv_ln_g, conv_ln_b, conv_out_g)
    tk = min(ATT_BLOCK, seq)
    ckv_n, ckv_t, kia, kib, kw = _prep_latent(tail, kv_norm_g, tm, tk)
    ckv_t = ckv_t.reshape(bsz, seq // tk, KV_RANK, tk)
    mask = _indexer_mask(qi_hm, kw, kia, kib, bsz, seq, topk)
    w_uk_t = w_uk.transpose(0, 2, 1).reshape(d_attn, KV_RANK).astype(BF16)
    w_uv_all = w_uv.transpose(1, 0, 2).reshape(KV_RANK, d_attn).astype(BF16)
    k_t = _key_heads(w_uk_t, ckv_t)
    v_hm = _matmul_value_heads(ckv_n, w_uv_all, tm, 512, "value_heads")
    attn = _head_attention(q_hm, k_t, v_hm, mask, rel_bias, bsz, seq)

    w_outb = w_out.astype(BF16)
    pre1 = _matmul2_residual(conv_n, attn, attn_out_g, w_outb[:d_conv], w_outb[d_conv:], xf, tm, 512,
                             "out_proj")

    mem_len = mem.shape[1]
    memb = mem.reshape(bsz * mem_len, d).astype(BF16)
    w_kv = jnp.concatenate([w_mk, w_mv], axis=1).astype(BF16)
    kv = _matmul(memb, w_kv, BF16, _tile(bsz * mem_len, 512), 512, "mem_kv")
    w_router = jnp.concatenate([w_router_grp, w_router_exp], axis=1)
    w_router = jnp.pad(w_router, ((0, 0), (0, LANES - w_router.shape[1]))).astype(BF16)
    x2, logits = _memory_attention(pre1, ln1_g, ln1_b, kv, w_mq.astype(BF16), w_mo.astype(BF16),
                                   ln2_g, ln2_b, w_router, bsz, seq, 256)

    e1, e2, g1, g2 = _router(logits, _tile(t, 1024))
    nb = (2 * t + N_EXPERTS * (MOE_BLOCK - 1) + MOE_BLOCK - 1) // MOE_BLOCK
    r1, r2, cnt = _moe_rank(e1, e2, 512)
    p1, p2, block_expert, n_used = _moe_place(e1, e2, r1, r2, cnt, nb, 512)
    xs = _moe_dispatch(x2, p1, p2, n_used, nb)
    ys = _moe_ffn(xs, w_gate, w_up, w_down, block_expert.reshape(nb), n_used[0, 0:1])
    x3 = _moe_combine(x2, ys, p1, p2, g1, g2, ln3_g, ln3_b)
    return x3.reshape(bsz, seq, d)


def kernel(x, mem, w_in, conv_w, conv_b, conv_ln_g, conv_ln_b, kv_norm_g, w_uk, w_uv, rel_bias, conv_out_g, attn_out_g, w_out, ln1_g, ln1_b, w_mq, w_mk, w_mv, w_mo, ln2_g, ln2_b, w_router_grp, w_router_exp, w_gate, w_up, w_down, ln3_g, ln3_b):
    for l in range(w_in.shape[0]):
        x = _layer(x, mem, w_in[l], conv_w[l], conv_b[l], conv_ln_g[l], conv_ln_b[l], kv_norm_g[l],
                   w_uk[l], w_uv[l], rel_bias, conv_out_g[l], attn_out_g[l], w_out[l], ln1_g[l], ln1_b[l],
                   w_mq[l], w_mk[l], w_mv[l], w_mo[l], ln2_g[l], ln2_b[l], w_router_grp[l],
                   w_router_exp[l], w_gate[l], w_up[l], w_down[l], ln3_g[l], ln3_b[l])
    return x
```

```python
import functools
import math

import jax
import jax.numpy as jnp
from jax import lax
from jax.experimental import pallas as pl
from jax.experimental.pallas import tpu as pltpu

F32 = jnp.float32
BF16 = jnp.bfloat16
I32 = jnp.int32

DEPTH = 1
CONV_WIDTH = 31
N_HEADS = 16
HEAD_DIM = 128
KV_RANK = 512
H_IDX = 32
D_IDX = 64
TOPK_MAX = 256
N_BUCKETS = 32
MAX_DIST = 128
MEM_HEADS = 4
MEM_HEAD_DIM = 128
N_GROUPS = 8
EXP_PER_GROUP = 8
N_EXPERTS = N_GROUPS * EXP_PER_GROUP
MOE_BLOCK = 128
ALPHA = (2.0 * DEPTH) ** 0.25
LN_EPS = 1e-5

LANES = 128
SUBLANES = 8
V7X_VMEM_BYTES = 64 * 1024 * 1024
VMEM_LIMIT = 56 * 1024 * 1024
NEG = -1e30
INT_MIN = -(2 ** 31)

LOG2E = 1.4426950408889634

ATT_BLOCK = 256
ATT_HEAD_GROUP = 8
CONV_TS = 256
CONV_HALO = 32
CONV_CC = 256
CONV_RC = 32


def _cparams(sem):
    return pltpu.CompilerParams(dimension_semantics=sem, vmem_limit_bytes=VMEM_LIMIT)


def _dot(a, b):
    return jnp.dot(a, b, preferred_element_type=F32)


def _dot_nt(a, b):
    return lax.dot_general(a, b, (((1,), (1,)), ((), ())), preferred_element_type=F32)


def _mm_kernel(a_ref, b_ref, o_ref):
    o_ref[...] = _dot(a_ref[...], b_ref[...].astype(a_ref.dtype)).astype(o_ref.dtype)


def _mm_cast_kernel(a_ref, b_ref, o_ref, ab_ref):
    ab = a_ref[...].astype(ab_ref.dtype)
    ab_ref[...] = ab
    o_ref[...] = _dot(ab, b_ref[...]).astype(o_ref.dtype)


def _matmul_cast(a, b, out_dtype, tm, name):
    m, k = a.shape
    n = b.shape[1]
    return pl.pallas_call(
        _mm_cast_kernel,
        grid=(m // tm,),
        in_specs=[pl.BlockSpec((tm, k), lambda i: (i, 0)),
                  pl.BlockSpec((k, n), lambda i: (0, 0))],
        out_specs=[pl.BlockSpec((tm, n), lambda i: (i, 0)),
                   pl.BlockSpec((tm, k), lambda i: (i, 0))],
        out_shape=[jax.ShapeDtypeStruct((m, n), out_dtype), jax.ShapeDtypeStruct((m, k), b.dtype)],
        compiler_params=_cparams(("parallel",)),
        name=name,
    )(a, b)


def _matmul(a, b, out_dtype, tm, tn, name):
    m, k = a.shape
    n = b.shape[1]
    return pl.pallas_call(
        _mm_kernel,
        grid=(m // tm, n // tn),
        in_specs=[pl.BlockSpec((tm, k), lambda i, j: (i, 0)),
                  pl.BlockSpec((k, tn), lambda i, j: (0, j))],
        out_specs=pl.BlockSpec((tm, tn), lambda i, j: (i, j)),
        out_shape=jax.ShapeDtypeStruct((m, n), out_dtype),
        compiler_params=_cparams(("parallel", "parallel")),
        name=name,
    )(a, b)


def _matmul_cols(a, b, n, out_dtype, tm, tn, name):
    m, k = a.shape
    return pl.pallas_call(
        _mm_kernel,
        grid=(m // tm, n // tn),
        in_specs=[pl.BlockSpec((tm, k), lambda i, j: (i, 0)),
                  pl.BlockSpec((k, tn), lambda i, j: (0, j))],
        out_specs=pl.BlockSpec((tm, tn), lambda i, j: (i, j)),
        out_shape=jax.ShapeDtypeStruct((m, n), out_dtype),
        compiler_params=_cparams(("parallel", "parallel")),
        name=name,
    )(a, b)


def _mm_heads_kernel(a_ref, b_ref, o_ref):
    r = _dot(a_ref[...], b_ref[...].astype(a_ref.dtype))
    for p in range(o_ref.shape[0]):
        o_ref[p] = r[:, p * LANES:(p + 1) * LANES].astype(o_ref.dtype)


def _matmul_heads(a, b, col0, n, out_dtype, tm, tn, name):
    m, k = a.shape
    assert col0 % tn == 0
    return pl.pallas_call(
        _mm_heads_kernel,
        grid=(m // tm, n // tn),
        in_specs=[pl.BlockSpec((tm, k), lambda i, j: (i, 0)),
                  pl.BlockSpec((k, tn), lambda i, j: (0, col0 // tn + j))],
        out_specs=pl.BlockSpec((tn // LANES, tm, LANES), lambda i, j: (j, i, 0)),
        out_shape=jax.ShapeDtypeStruct((n // LANES, m, LANES), out_dtype),
        compiler_params=_cparams(("parallel", "parallel")),
        name=name,
    )(a, b)


def _mm_value_heads_kernel(a_ref, b_ref, o_ref):
    r = _dot(a_ref[...], b_ref[...])
    tm = r.shape[0]
    for p in range(o_ref.shape[0]):
        o_ref[p, :, 0:LANES] = r[:, p * LANES:(p + 1) * LANES].astype(o_ref.dtype)
        o_ref[p, :, LANES:2 * LANES] = jnp.ones((tm, LANES), o_ref.dtype)


def _matmul_value_heads(a, b, tm, tn, name):
    m, k = a.shape
    n = b.shape[1]
    return pl.pallas_call(
        _mm_value_heads_kernel,
        grid=(m // tm, n // tn),
        in_specs=[pl.BlockSpec((tm, k), lambda i, j: (i, 0)),
                  pl.BlockSpec((k, tn), lambda i, j: (0, j))],
        out_specs=pl.BlockSpec((tn // LANES, tm, 2 * LANES), lambda i, j: (j, i, 0)),
        out_shape=jax.ShapeDtypeStruct((n // LANES, m, 2 * LANES), BF16),
        compiler_params=_cparams(("parallel", "parallel")),
        name=name,
    )(a, b)


def _mm2_res_kernel(a1_ref, a2_ref, g2_ref, w1_ref, w2_ref, r_ref, o_ref, a2n):
    @pl.when(pl.program_id(1) == 0)
    def _():
        x = a2_ref[...].astype(F32)
        ms = jnp.mean(x * x, axis=-1, keepdims=True)
        a2n[...] = (x * lax.rsqrt(ms + LN_EPS) * g2_ref[...]).astype(a2n.dtype)

    o_ref[...] = (ALPHA * r_ref[...] + _dot(a1_ref[...], w1_ref[...].astype(a1_ref.dtype))
                  + _dot(a2n[...], w2_ref[...].astype(a2n.dtype)))


def _matmul2_residual(a1, a2, g2, w, res, tm, tn, name):
    m, k = a1.shape
    assert a2.shape == a1.shape and w.shape[0] == 2 * k
    n = w.shape[1]
    return pl.pallas_call(
        _mm2_res_kernel,
        grid=(m // tm, n // tn),
        in_specs=[pl.BlockSpec((tm, k), lambda i, j: (i, 0)),
                  pl.BlockSpec((tm, k), lambda i, j: (i, 0)),
                  pl.BlockSpec((1, k), lambda i, j: (0, 0)),
                  pl.BlockSpec((k, tn), lambda i, j: (0, j)),
                  pl.BlockSpec((k, tn), lambda i, j: (1, j)),
                  pl.BlockSpec((tm, tn), lambda i, j: (i, j))],
        out_specs=pl.BlockSpec((tm, tn), lambda i, j: (i, j)),
        out_shape=jax.ShapeDtypeStruct((m, n), F32),
        scratch_shapes=[pltpu.VMEM((tm, k), BF16)],
        compiler_params=_cparams(("parallel", "arbitrary")),
        name=name,
    )(a1, a2, g2.reshape(1, k), w, w, res)


def _ln_rows(x, g, b):
    mu = jnp.mean(x, axis=-1, keepdims=True)
    xc = x - mu
    var = jnp.mean(xc * xc, axis=-1, keepdims=True)
    return xc * lax.rsqrt(var + LN_EPS) * g + b


def _conv_kernel(a_ref, g_ref, cw_ref, cb_ref, lg_ref, lb_ref, og_ref, o_ref, hbuf, ybuf, hs):
    ts = a_ref.shape[0]
    nch = hbuf.shape[0]
    cc = hbuf.shape[2]
    d_conv = nch * cc

    nrow = CONV_HALO + ts

    @pl.when(pl.program_id(1) == 0)
    def _():
        hbuf[:, 0:CONV_HALO, :] = jnp.zeros((nch, CONV_HALO, cc), F32)
        hbuf[:, nrow:nrow + SUBLANES, :] = jnp.zeros((nch, SUBLANES, cc), F32)

    for c in range(nch):
        a = a_ref[:, c * cc:(c + 1) * cc].astype(F32)
        g = g_ref[:, c * cc:(c + 1) * cc].astype(F32)
        hbuf[c, CONV_HALO:nrow, :] = a * jax.nn.sigmoid(g)

    first = CONV_HALO - (CONV_WIDTH - 1)

    def chunk_body(c, carry):
        for o in range(1, SUBLANES):
            hs[o - 1] = hbuf[c, o:o + nrow, :]
        for r0 in range(0, ts, CONV_RC):
            acc = jnp.zeros((CONV_RC, cc), F32)
            for j in range(CONV_WIDTH):
                o = (first + j) % SUBLANES
                base = r0 + first + j - o
                rows = hbuf[c, base:base + CONV_RC, :] if o == 0 else hs[o - 1, base:base + CONV_RC, :]
                acc = acc + cw_ref[c, j:j + 1, :] * rows
            ybuf[c, r0:r0 + CONV_RC, :] = acc + cb_ref[c]
        hbuf[c, 0:CONV_HALO, :] = hbuf[c, ts:nrow, :]
        return carry

    lax.fori_loop(0, nch, chunk_body, 0)

    s1 = jnp.zeros((ts, 1), F32)
    for c in range(nch):
        s1 = s1 + jnp.sum(ybuf[c], axis=1, keepdims=True)
    mu = s1 * (1.0 / d_conv)
    s2 = jnp.zeros((ts, 1), F32)
    for c in range(nch):
        yc = ybuf[c] - mu
        s2 = s2 + jnp.sum(yc * yc, axis=1, keepdims=True)
    rstd = lax.rsqrt(s2 * (1.0 / d_conv) + LN_EPS)
    s3 = jnp.zeros((ts, 1), F32)
    for c in range(nch):
        z = (ybuf[c] - mu) * rstd * lg_ref[:, c * cc:(c + 1) * cc] + lb_ref[:, c * cc:(c + 1) * cc]
        z = z * jax.nn.sigmoid(z)
        ybuf[c] = z
        s3 = s3 + jnp.sum(z * z, axis=1, keepdims=True)
    rr = lax.rsqrt(s3 * (1.0 / d_conv) + LN_EPS)
    for c in range(nch):
        o_ref[:, c * cc:(c + 1) * cc] = (ybuf[c] * rr * og_ref[:, c * cc:(c + 1) * cc]).astype(o_ref.dtype)


def _conformer_conv(u, bsz, seq, conv_w, conv_b, ln_g, ln_b, out_g):
    d_conv = u.shape[1] // 2
    ts = min(CONV_TS, seq)
    nch = d_conv // CONV_CC
    nt = seq // ts
    cw = conv_w.reshape(CONV_WIDTH, nch, CONV_CC).transpose(1, 0, 2)
    cb = conv_b.reshape(nch, 1, CONV_CC)
    vec = pl.BlockSpec((1, d_conv), lambda b, i: (0, 0))
    return pl.pallas_call(
        _conv_kernel,
        grid=(bsz, nt),
        in_specs=[pl.BlockSpec((ts, d_conv), lambda b, i: (b * nt + i, 0)),
                  pl.BlockSpec((ts, d_conv), lambda b, i: (b * nt + i, 1)),
                  pl.BlockSpec((nch, CONV_WIDTH, CONV_CC), lambda b, i: (0, 0, 0)),
                  pl.BlockSpec((nch, 1, CONV_CC), lambda b, i: (0, 0, 0)),
                  vec, vec, vec],
        out_specs=pl.BlockSpec((ts, d_conv), lambda b, i: (b * nt + i, 0)),
        out_shape=jax.ShapeDtypeStruct((bsz * seq, d_conv), BF16),
        scratch_shapes=[pltpu.VMEM((nch, CONV_HALO + ts + SUBLANES, CONV_CC), F32),
                        pltpu.VMEM((nch, ts, CONV_CC), F32),
                        pltpu.VMEM((SUBLANES - 1, CONV_HALO + ts, CONV_CC), F32)],
        compiler_params=_cparams(("arbitrary", "arbitrary")),
        name="conformer_conv",
    )(u, u, cw, cb, ln_g.reshape(1, d_conv), ln_b.reshape(1, d_conv), out_g.reshape(1, d_conv))


def _prep_kernel(t_ref, g_ref, ckv_ref, ckvt_ref, kia_ref, kib_ref, kw_ref):
    ckv = t_ref[:, 0:KV_RANK]
    ms = jnp.mean(ckv * ckv, axis=-1, keepdims=True)
    ckv_n = ckv * lax.rsqrt(ms + LN_EPS) * g_ref[...]
    ckv_ref[...] = ckv_n.astype(ckv_ref.dtype)
    tk = ckvt_ref.shape[3]
    for c in range(ckvt_ref.shape[1]):
        ckvt_ref[0, c] = ckv_n[c * tk:(c + 1) * tk, :].T.astype(ckvt_ref.dtype)
    kw = t_ref[:, KV_RANK:KV_RANK + LANES]
    kw_ref[...] = kw
    lane = lax.broadcasted_iota(I32, kw.shape, 1)
    kia_ref[...] = jnp.where(lane < D_IDX, kw, 0.0).astype(kia_ref.dtype)
    kib_ref[...] = jnp.where(lane >= D_IDX, pltpu.roll(kw, D_IDX, 1), 0.0).astype(kib_ref.dtype)


def _prep_latent(tail, kv_norm_g, tm, tk):
    m, w = tail.shape
    return pl.pallas_call(
        _prep_kernel,
        grid=(m // tm,),
        in_specs=[pl.BlockSpec((tm, w), lambda i: (i, 0)),
                  pl.BlockSpec((1, KV_RANK), lambda i: (0, 0))],
        out_specs=[pl.BlockSpec((tm, KV_RANK), lambda i: (i, 0)),
                   pl.BlockSpec((1, tm // tk, KV_RANK, tk), lambda i: (i, 0, 0, 0)),
                   pl.BlockSpec((tm, LANES), lambda i: (i, 0)),
                   pl.BlockSpec((tm, LANES), lambda i: (i, 0)),
                   pl.BlockSpec((tm, LANES), lambda i: (i, 0))],
        out_shape=[jax.ShapeDtypeStruct((m, KV_RANK), BF16),
                   jax.ShapeDtypeStruct((m // tm, tm // tk, KV_RANK, tk), BF16),
                   jax.ShapeDtypeStruct((m, LANES), BF16),
                   jax.ShapeDtypeStruct((m, LANES), BF16),
                   jax.ShapeDtypeStruct((m, LANES), F32)],
        compiler_params=_cparams(("parallel",)),
        name="prep_latent",
    )(tail, kv_norm_g.reshape(1, KV_RANK))


def _indexer_kernel(qi_ref, kw_ref, kia_ref, kib_ref, o_ref, keybuf, wb, *, topk):
    i = pl.program_id(1)
    npairs, tq, _ = qi_ref.shape
    nk = o_ref.shape[1]
    tk = o_ref.shape[3]
    half = tk // 2
    kf = float(topk)
    group = 4

    for h in range(H_IDX):
        wb[h] = jnp.broadcast_to(kw_ref[:, D_IDX + h:D_IDX + h + 1], (tq, LANES))

    row = lax.broadcasted_iota(I32, (tq, tk), 0) + i * tq
    col0 = lax.broadcasted_iota(I32, (tq, tk), 1)

    def score_chunk(j, carry):
        k0 = pl.multiple_of(j * tk, tk)
        kd = jnp.concatenate([kia_ref[pl.ds(k0, tk), :], kib_ref[pl.ds(k0, tk), :]], axis=0)
        acc = jnp.zeros((tq, tk), F32)
        for p0 in range(0, npairs, group):
            lhs = qi_ref[p0:p0 + group].reshape(group * tq, LANES)
            zz = _dot_nt(lhs, kd)
            for p in range(group):
                h = 2 * (p0 + p)
                z = zz[p * tq:(p + 1) * tq]
                we = jnp.concatenate([wb[h]] * (tk // LANES), axis=1)
                wo = jnp.concatenate([wb[h + 1]] * (tk // LANES), axis=1)
                acc = acc + we * jnp.maximum(z[:, 0:tk], 0.0) + wo * jnp.maximum(z[:, tk:2 * tk], 0.0)
        bits = pltpu.bitcast(acc, I32)
        key = jnp.where(bits >= 0, bits, bits ^ jnp.int32(0x7FFFFFFF))
        keybuf[j] = jnp.where(col0 + j * tk <= row, key, INT_MIN)
        return carry

    lax.fori_loop(0, i + 1, score_chunk, 0)

    rows_per_pass = 128
    lane = lax.broadcasted_iota(I32, (rows_per_pass, LANES), 1)

    def count(pred):
        parts = []
        for r0 in range(0, tq, rows_per_pass):
            rows = slice(r0, r0 + rows_per_pass)

            def body(j, c):
                for s in range(tk // LANES):
                    kk = keybuf[j, rows, s * LANES:(s + 1) * LANES]
                    c = c + jnp.where(pred(kk, lane + (j * tk + s * LANES), rows), 1.0, 0.0)
                return c

            parts.append(lax.fori_loop(0, i + 1, body, jnp.zeros((rows_per_pass, LANES), F32)))
        return jnp.concatenate(
            [jnp.broadcast_to(jnp.sum(c, axis=1, keepdims=True), (rows_per_pass, LANES)) for c in parts], axis=0)

    def count_ge(cand):
        return count(lambda kk, col, rows: kk >= cand[rows])

    tau = jnp.where(count_ge(jnp.zeros((tq, LANES), I32)) >= kf, 0, INT_MIN).astype(I32)

    def bit_body(it, tau):
        cand = tau | jnp.left_shift(jnp.int32(1), 30 - it)
        return jnp.where(count_ge(cand) >= kf, cand, tau)

    tau = lax.fori_loop(0, 31, bit_body, tau)
    n_ge = count_ge(tau)

    def tie_cut():
        need = kf - count(lambda kk, col, rows: kk > tau[rows])

        def cut_body(it, cut):
            cand = cut + jnp.left_shift(jnp.int32(1), 30 - it)
            below = count(lambda kk, col, rows: (kk == tau[rows]) & (col < cand[rows]))
            return jnp.where(below < need, cand, cut)

        return lax.fori_loop(0, 31, cut_body, jnp.zeros((tq, LANES), I32))

    has_ties = jnp.max(n_ge) > kf
    cut = lax.cond(has_ties, tie_cut, lambda: jnp.full((tq, LANES), 2 ** 30, I32))
    taub = jnp.concatenate([tau] * (tk // LANES), axis=1)
    cutb = jnp.concatenate([cut] * (tk // LANES), axis=1)

    def write_chunk(j, carry):
        kk = keybuf[j]
        sel = ((kk > taub) | ((kk == taub) & (col0 + j * tk <= cutb))) & (kk != INT_MIN)
        o_ref[0, j] = jnp.where(sel, 0.0, NEG).astype(o_ref.dtype)
        return carry

    lax.fori_loop(0, i + 1, write_chunk, 0)

    def write_rest(j, carry):
        o_ref[0, j] = jnp.full((tq, tk), NEG, o_ref.dtype)
        return carry

    lax.fori_loop(i + 1, nk, write_rest, 0)


def _indexer_mask(qi_hm, kw, kia, kib, bsz, seq, topk):
    tq = min(ATT_BLOCK, seq)
    nq = seq // tq
    npairs = qi_hm.shape[0]
    return pl.pallas_call(
        functools.partial(_indexer_kernel, topk=topk),
        grid=(bsz, nq),
        in_specs=[pl.BlockSpec((npairs, tq, LANES), lambda b, i: (0, b * nq + i, 0)),
                  pl.BlockSpec((tq, LANES), lambda b, i: (b * nq + i, 0)),
                  pl.BlockSpec((seq, LANES), lambda b, i: (b, 0)),
                  pl.BlockSpec((seq, LANES), lambda b, i: (b, 0))],
        out_specs=pl.BlockSpec((1, nq, tq, tq), lambda b, i: (b * nq + i, 0, 0, 0)),
        out_shape=jax.ShapeDtypeStruct((bsz * nq, nq, tq, tq), BF16),
        scratch_shapes=[pltpu.VMEM((nq, tq, tq), I32),
                        pltpu.VMEM((H_IDX, tq, LANES), F32)],
        compiler_params=_cparams(("parallel", "parallel")),
        name="indexer_mask",
    )(qi_hm, kw, kia, kib)


def _t5_bucket(dist):
    n = jnp.maximum(dist, 0)
    max_exact = N_BUCKETS // 2
    nf = jnp.maximum(n, 1).astype(F32)
    large = max_exact + (jnp.log(nf / max_exact) / math.log(MAX_DIST / max_exact)
                         * (N_BUCKETS - max_exact)).astype(I32)
    large = jnp.minimum(large, N_BUCKETS - 1)
    return jnp.where(n < max_exact, n, large)


def _key_heads_kernel(wt_ref, ct_ref, o_ref):
    r = _dot(wt_ref[...], ct_ref[0, 0]) * (HEAD_DIM ** -0.5 * LOG2E)
    for h in range(o_ref.shape[1]):
        o_ref[0, h, 0] = r[h * HEAD_DIM:(h + 1) * HEAD_DIM].astype(o_ref.dtype)


def _key_heads(w_uk_t, ckv_t):
    bsz, nk, r_lat, tk = ckv_t.shape
    nh = w_uk_t.shape[0] // HEAD_DIM
    return pl.pallas_call(
        _key_heads_kernel,
        grid=(bsz, nk),
        in_specs=[pl.BlockSpec((nh * HEAD_DIM, r_lat), lambda b, j: (0, 0)),
                  pl.BlockSpec((1, 1, r_lat, tk), lambda b, j: (b, j, 0, 0))],
        out_specs=pl.BlockSpec((1, nh, 1, HEAD_DIM, tk), lambda b, j: (b, 0, j, 0, 0)),
        out_shape=jax.ShapeDtypeStruct((bsz, nh, nk, HEAD_DIM, tk), BF16),
        compiler_params=_cparams(("parallel", "parallel")),
        name="key_heads",
    )(w_uk_t, ckv_t)


def _attn_kernel(relb_ref, q_ref, kt_ref, v_ref, mask_ref, o_ref, acc, m_s, btab):
    b = pl.program_id(0)
    g = pl.program_id(1)
    i = pl.program_id(2)
    hg, tq, _ = q_ref.shape
    nh = btab.shape[1]
    tk = mask_ref.shape[3]

    @pl.when((b == 0) & (g == 0) & (i == 0))
    def _():
        r = lax.broadcasted_iota(I32, (tq, tk), 0)
        c = lax.broadcasted_iota(I32, (tq, tk), 1)
        for t in range(2):
            bk = _t5_bucket(r - c + t * tk)

            def fill(h, carry):
                far = relb_ref[N_BUCKETS - 1, h]
                v = jnp.zeros((tq, tk), F32)
                for k in range(N_BUCKETS):
                    v = jnp.where(bk == k, (relb_ref[k, h] - far) * LOG2E, v)
                btab[t, h] = v
                return carry

            lax.fori_loop(0, nh, fill, 0)

    acc[...] = jnp.zeros(acc.shape, F32)
    m_s[...] = jnp.full(m_s.shape, NEG, F32)

    def lane_tile(v, n):
        return jnp.concatenate([v] * n, axis=1)

    def chunks(js, near):
        mks = [mask_ref[0, j].astype(F32) for j in js]
        for h in range(hg):
            for j, mk in zip(js, mks):
                s = _dot(q_ref[h], kt_ref[0, h, j]) + mk
                if near:
                    s = s + btab[i - j, g * hg + h]
                m_old = m_s[h]
                m_new = jnp.maximum(m_old, jnp.max(s, axis=1, keepdims=True))
                a = jnp.exp2(m_old - m_new)
                p = jnp.exp2(s - lane_tile(m_new, tk // LANES))
                rows = pl.ds(pl.multiple_of(j * tk, tk), tk)
                acc[h] = lane_tile(a, 2) * acc[h] + _dot(p.astype(BF16), v_ref[h, rows, :])
                m_s[h] = m_new

    n_far = jnp.maximum(i - 1, 0)

    def far_quad(p, carry):
        chunks([4 * p, 4 * p + 1, 4 * p + 2, 4 * p + 3], False)
        return carry

    lax.fori_loop(0, lax.shift_right_logical(n_far, 2), far_quad, 0)
    done = n_far & ~3

    @pl.when((n_far & 2) == 2)
    def _():
        chunks([done, done + 1], False)

    @pl.when((n_far & 1) == 1)
    def _():
        chunks([n_far - 1], False)

    @pl.when(i >= 1)
    def _():
        chunks([i - 1, i], True)

    @pl.when(i == 0)
    def _():
        chunks([i], True)

    for h in range(hg):
        o_ref[:, h * HEAD_DIM:(h + 1) * HEAD_DIM] = (
            acc[h, :, :HEAD_DIM] / acc[h, :, HEAD_DIM:]).astype(o_ref.dtype)


def _head_attention(q_hm, k_t, v_hm, mask, rel_bias, bsz, seq):
    nh = q_hm.shape[0]
    hg = ATT_HEAD_GROUP
    tq = mask.shape[2]
    nq = seq // tq
    assert tq == mask.shape[3] and tq >= MAX_DIST
    once = pl.Buffered(1)
    return pl.pallas_call(
        _attn_kernel,
        grid=(bsz, nh // hg, nq),
        in_specs=[pl.BlockSpec(memory_space=pltpu.SMEM),
                  pl.BlockSpec((hg, tq, HEAD_DIM), lambda b, g, i: (g, b * nq + i, 0)),
                  pl.BlockSpec((1, hg, nq, HEAD_DIM, tq), lambda b, g, i: (b, g, 0, 0, 0), pipeline_mode=once),
                  pl.BlockSpec((hg, seq, 2 * HEAD_DIM), lambda b, g, i: (g, b, 0), pipeline_mode=once),
                  pl.BlockSpec((1, nq, tq, tq), lambda b, g, i: (b * nq + i, 0, 0, 0))],
        out_specs=pl.BlockSpec((tq, hg * HEAD_DIM), lambda b, g, i: (b * nq + i, g)),
        out_shape=jax.ShapeDtypeStruct((bsz * seq, nh * HEAD_DIM), BF16),
        scratch_shapes=[pltpu.VMEM((hg, tq, 2 * HEAD_DIM), F32),
                        pltpu.VMEM((hg, tq, LANES), F32),
                        pltpu.VMEM((2, nh, tq, tq), F32)],
        compiler_params=_cparams(("arbitrary", "arbitrary", "arbitrary")),
        name="head_attention",
    )(rel_bias, q_hm, k_t, v_hm, mask)


def _memattn_kernel(x_ref, g1_ref, b1_ref, kv_ref, wq_ref, wo_ref, g_ref, b_ref, wr_ref, o_ref, lg_ref):
    x = _ln_rows(x_ref[...], g1_ref[...], b1_ref[...])
    d_mem = MEM_HEADS * MEM_HEAD_DIM
    q = (_dot(x.astype(BF16), wq_ref[...]) * (MEM_HEAD_DIM ** -0.5)).astype(BF16)
    outs = []
    for h in range(MEM_HEADS):
        lo = h * MEM_HEAD_DIM
        k = kv_ref[:, lo:lo + MEM_HEAD_DIM]
        v = kv_ref[:, d_mem + lo:d_mem + lo + MEM_HEAD_DIM]
        s = _dot_nt(q[:, lo:lo + MEM_HEAD_DIM], k)
        p = jnp.exp(s - jnp.max(s, axis=1, keepdims=True))
        p = p / jnp.sum(p, axis=1, keepdims=True)
        outs.append(_dot(p.astype(BF16), v).astype(BF16))
    o = jnp.concatenate(outs, axis=1)
    x2 = _ln_rows(ALPHA * x + _dot(o, wo_ref[...]), g_ref[...], b_ref[...])
    o_ref[...] = x2
    lg_ref[...] = _dot(x2.astype(BF16), wr_ref[...])


def _memory_attention(pre1, g1, b1, kv, w_mq, w_mo, g, b, w_router, bsz, seq, tm):
    t, d = pre1.shape
    nt = seq // tm
    mem_len = kv.shape[0] // bsz
    d_mem = w_mq.shape[1]
    return pl.pallas_call(
        _memattn_kernel,
        grid=(bsz, nt),
        in_specs=[pl.BlockSpec((tm, d), lambda bi, i: (bi * nt + i, 0)),
                  pl.BlockSpec((1, d), lambda bi, i: (0, 0)),
                  pl.BlockSpec((1, d), lambda bi, i: (0, 0)),
                  pl.BlockSpec((mem_len, 2 * d_mem), lambda bi, i: (bi, 0)),
                  pl.BlockSpec((d, d_mem), lambda bi, i: (0, 0), pipeline_mode=pl.Buffered(1)),
                  pl.BlockSpec((d_mem, d), lambda bi, i: (0, 0), pipeline_mode=pl.Buffered(1)),
                  pl.BlockSpec((1, d), lambda bi, i: (0, 0)),
                  pl.BlockSpec((1, d), lambda bi, i: (0, 0)),
                  pl.BlockSpec((d, LANES), lambda bi, i: (0, 0), pipeline_mode=pl.Buffered(1))],
        out_specs=[pl.BlockSpec((tm, d), lambda bi, i: (bi * nt + i, 0)),
                   pl.BlockSpec((tm, LANES), lambda bi, i: (bi * nt + i, 0))],
        out_shape=[jax.ShapeDtypeStruct((t, d), F32),
                   jax.ShapeDtypeStruct((t, LANES), F32)],
        compiler_params=_cparams(("parallel", "parallel")),
        name="memory_attention",
    )(pre1, g1.reshape(1, d), b1.reshape(1, d), kv, w_mq, w_mo, g.reshape(1, d), b.reshape(1, d), w_router)


def _router_kernel(lg_ref, e1_ref, e2_ref, g1_ref, g2_ref):
    x = lg_ref[...]
    lane = lax.broadcasted_iota(I32, x.shape, 1)
    lane_f = lane.astype(F32)

    def argmax(mask):
        v = jnp.where(mask, x, -jnp.inf)
        mx = jnp.max(v, axis=1, keepdims=True)
        idx = jnp.min(jnp.where(mask & (v == mx), lane_f, float(LANES)), axis=1, keepdims=True)
        return mx, idx.astype(I32)

    gmask = lane < N_GROUPS
    gmax, gsel = argmax(gmask)
    gsum = jnp.sum(jnp.where(gmask, jnp.exp(x - gmax), 0.0), axis=1, keepdims=True)
    g_p = 1.0 / gsum
    lo = N_GROUPS + gsel * EXP_PER_GROUP
    emask = (lane >= lo) & (lane < lo + EXP_PER_GROUP)
    m1, i1 = argmax(emask)
    m2, i2 = argmax(emask & (lane != i1))
    esum = jnp.sum(jnp.where(emask, jnp.exp(x - m1), 0.0), axis=1, keepdims=True)
    p1 = 1.0 / esum
    p2 = jnp.exp(m2 - m1) / esum
    e1_ref[...] = i1 - N_GROUPS
    e2_ref[...] = i2 - N_GROUPS
    g1_ref[...] = g_p * (p1 / (p1 + p2))
    g2_ref[...] = g_p * (p2 / (p1 + p2))


def _router(logits, tm):
    t = logits.shape[0]
    col = pl.BlockSpec((tm, 1), lambda i: (i, 0))
    return pl.pallas_call(
        _router_kernel,
        grid=(t // tm,),
        in_specs=[pl.BlockSpec((tm, LANES), lambda i: (i, 0))],
        out_specs=[col, col, col, col],
        out_shape=[jax.ShapeDtypeStruct((t, 1), I32), jax.ShapeDtypeStruct((t, 1), I32),
                   jax.ShapeDtypeStruct((t, 1), F32), jax.ShapeDtypeStruct((t, 1), F32)],
        compiler_params=_cparams(("parallel",)),
        name="moe_router",
    )(logits)


HI16 = -65536


def _pack_halves(x):
    d = x.shape[1] // 2
    lo = pltpu.bitcast(x[:, :d].astype(BF16).astype(F32), I32)
    hi = pltpu.bitcast(x[:, d:].astype(BF16).astype(F32), I32)
    return lax.shift_right_logical(lo, 16) | (hi & HI16)


def _unpack_halves(u):
    return pltpu.bitcast(lax.shift_left(u, 16), F32), pltpu.bitcast(u & HI16, F32)


def _moe_rank_kernel(e1_ref, e2_ref, r1_ref, r2_ref, cnt_ref, base):
    i = pl.program_id(0)
    tm = e1_ref.shape[0]

    @pl.when(i == 0)
    def _():
        base[...] = jnp.zeros(base.shape, F32)

    lane = lax.broadcasted_iota(I32, (tm, LANES), 1)
    rr = lax.broadcasted_iota(I32, (tm, tm), 0)
    cc = lax.broadcasted_iota(I32, (tm, tm), 1)
    earlier = jnp.where(cc < rr, 1.0, 0.0).astype(BF16)
    for slot, (e_ref, r_ref) in enumerate(((e1_ref, r1_ref), (e2_ref, r2_ref))):
        oh = jnp.where(lane == e_ref[...], 1.0, 0.0)
        before = _dot(earlier, oh.astype(BF16)) + base[slot:slot + 1, :]
        r_ref[...] = jnp.sum(oh * before, axis=1, keepdims=True).astype(I32)
        base[slot:slot + 1, :] = base[slot:slot + 1, :] + jnp.sum(oh, axis=0, keepdims=True)
    cnt_ref[...] = base[...]


def _moe_rank(e1, e2, tm):
    t = e1.shape[0]
    col = pl.BlockSpec((tm, 1), lambda i: (i, 0))
    return pl.pallas_call(
        _moe_rank_kernel,
        grid=(t // tm,),
        in_specs=[col, col],
        out_specs=[col, col, pl.BlockSpec((8, LANES), lambda i: (0, 0))],
        out_shape=[jax.ShapeDtypeStruct((t, 1), I32), jax.ShapeDtypeStruct((t, 1), I32),
                   jax.ShapeDtypeStruct((8, LANES), F32)],
        scratch_shapes=[pltpu.VMEM((8, LANES), F32)],
        compiler_params=_cparams(("arbitrary",)),
        name="moe_rank",
    )(e1, e2)


def _moe_place_kernel(e1_ref, e2_ref, r1_ref, r2_ref, cnt_ref, p1_ref, p2_ref, be_ref, nu_ref):
    tm = e1_ref.shape[0]
    nbp = be_ref.shape[0]
    lane8 = lax.broadcasted_iota(I32, (8, LANES), 1)
    cnt = cnt_ref[...].astype(I32)
    c0 = jnp.broadcast_to(cnt[0:1], (8, LANES))
    c1 = jnp.broadcast_to(cnt[1:2], (8, LANES))
    blk_shift = MOE_BLOCK.bit_length() - 1
    padded = lax.shift_left(lax.shift_right_logical(c0 + c1 + (MOE_BLOCK - 1), blk_shift), blk_shift)
    pad_end = padded
    s = 1
    while s < LANES:
        pad_end = pad_end + jnp.where(lane8 >= s, pltpu.roll(pad_end, s, 1), 0)
        s *= 2
    start0 = (pad_end - padded).astype(F32)
    start1 = (pad_end - padded + c0).astype(F32)

    lane = lax.broadcasted_iota(I32, (tm, LANES), 1)
    for e_ref, r_ref, p_ref, start in ((e1_ref, r1_ref, p1_ref, start0), (e2_ref, r2_ref, p2_ref, start1)):
        seg = jnp.sum(jnp.where(lane == e_ref[...], start[0:1], 0.0), axis=1, keepdims=True)
        p_ref[...] = seg.astype(I32) + r_ref[...]

    block_row = lax.broadcasted_iota(I32, (nbp, LANES), 0) * MOE_BLOCK
    lane_b = lax.broadcasted_iota(I32, (nbp, LANES), 1)
    ended = (pad_end[0:1] <= block_row) & (lane_b < N_EXPERTS)
    be = jnp.sum(jnp.where(ended, 1.0, 0.0), axis=1, keepdims=True)
    be_ref[...] = jnp.minimum(be, N_EXPERTS - 1.0).astype(I32)
    total = jnp.max(pad_end, axis=1, keepdims=True)
    sub = lax.broadcasted_iota(I32, (8, LANES), 0)
    n_used = jnp.broadcast_to(lax.shift_right_logical(total, blk_shift), (8, LANES))
    nu_ref[...] = jnp.where(sub == 0, n_used, jnp.where(sub == 1, pad_end, jnp.where(sub == 2, padded, 0)))


def _moe_place(e1, e2, r1, r2, cnt, nb, tm):
    t = e1.shape[0]
    col = pl.BlockSpec((tm, 1), lambda i: (i, 0))
    return pl.pallas_call(
        _moe_place_kernel,
        grid=(t // tm,),
        in_specs=[col, col, col, col, pl.BlockSpec((8, LANES), lambda i: (0, 0))],
        out_specs=[col, col, pl.BlockSpec((nb, 1), lambda i: (0, 0)),
                   pl.BlockSpec((8, LANES), lambda i: (0, 0))],
        out_shape=[jax.ShapeDtypeStruct((t, 1), I32), jax.ShapeDtypeStruct((t, 1), I32),
                   jax.ShapeDtypeStruct((nb, 1), I32), jax.ShapeDtypeStruct((8, LANES), I32)],
        compiler_params=_cparams(("arbitrary",)),
        name="moe_place",
    )(e1, e2, r1, r2, cnt)


def _moe_dispatch_kernel(seg_ref, p1_ref, p2_ref, x_ref, xs_hbm, pk, zbuf, sem, zsem):
    i = pl.program_id(0)
    n = pl.num_programs(0)
    slot = lax.rem(i, 2)
    blk = x_ref.shape[0]
    nb = xs_hbm.shape[0] // blk

    def row_copy(s, r, pos):
        return pltpu.make_async_copy(pk.at[s, pl.ds(r, 1)], xs_hbm.at[pl.ds(pos, 1)], sem.at[s])

    def drain(s):
        for _ in range(2):
            pltpu.make_async_copy(pk.at[s], xs_hbm.at[pl.ds(0, blk)], sem.at[s]).wait()

    def zero_copy(row0):
        return pltpu.make_async_copy(zbuf, xs_hbm.at[pl.ds(pl.multiple_of(row0, blk), blk)], zsem)

    @pl.when(i == 0)
    def _():
        zbuf[...] = jnp.zeros(zbuf.shape, zbuf.dtype)
        n_used = seg_ref[0, 0]

        def seg_start(e, c):
            @pl.when(seg_ref[2, e] > 0)
            def _():
                zero_copy(seg_ref[1, e] - blk).start()
            return c

        def tail_start(b, c):
            zero_copy(b * blk).start()
            return c

        def seg_wait(e, c):
            @pl.when(seg_ref[2, e] > 0)
            def _():
                zero_copy(0).wait()
            return c

        def tail_wait(b, c):
            zero_copy(0).wait()
            return c

        lax.fori_loop(0, N_EXPERTS, seg_start, 0)
        lax.fori_loop(n_used, nb, tail_start, 0)
        lax.fori_loop(0, N_EXPERTS, seg_wait, 0)
        lax.fori_loop(n_used, nb, tail_wait, 0)

    @pl.when(i >= 2)
    def _():
        drain(slot)

    pk[slot] = _pack_halves(x_ref[...])

    def issue(r, c):
        row_copy(slot, r, p1_ref[0, 0, r]).start()
        row_copy(slot, r, p2_ref[0, 0, r]).start()
        return c

    lax.fori_loop(0, blk, issue, 0, unroll=8)

    @pl.when(i == n - 1)
    def _():
        drain(slot)

    @pl.when((i == n - 1) & (i >= 1))
    def _():
        drain(1 - slot)


def _moe_dispatch(x2, p1, p2, seg, nb):
    t, d = x2.shape
    nt = t // MOE_BLOCK
    rows = nb * MOE_BLOCK
    pos = pl.BlockSpec((1, 1, MOE_BLOCK), lambda i: (i, 0, 0), memory_space=pltpu.SMEM)
    return pl.pallas_call(
        _moe_dispatch_kernel,
        grid=(nt,),
        in_specs=[pl.BlockSpec(memory_space=pltpu.SMEM), pos, pos,
                  pl.BlockSpec((MOE_BLOCK, d), lambda i: (i, 0))],
        out_specs=pl.BlockSpec(memory_space=pl.ANY),
        out_shape=jax.ShapeDtypeStruct((rows, d // 2), I32),
        scratch_shapes=[pltpu.VMEM((2, MOE_BLOCK, d // 2), I32),
                        pltpu.VMEM((MOE_BLOCK, d // 2), I32),
                        pltpu.SemaphoreType.DMA((2,)),
                        pltpu.SemaphoreType.DMA(())],
        compiler_params=_cparams(("arbitrary",)),
        name="moe_dispatch",
    )(seg, p1.reshape(nt, 1, MOE_BLOCK), p2.reshape(nt, 1, MOE_BLOCK), x2)


MAT_PIECES = 4
N_PIECES = 3 * MAT_PIECES
PIECES_PER_BLOCK = 3
PIECE_RING = 4


def _moe_ffn_kernel(be_ref, nu_ref, xs_ref, wg_hbm, wu_hbm, wd_hbm, ys_ref,
                    wgb, wub, wdb, sa, sb, sem, st):
    i = pl.program_id(0)
    nb = be_ref.shape[0]
    n_used = nu_ref[0]
    ra = wgb.shape[1] // MAT_PIECES
    rb = wdb.shape[1] // MAT_PIECES

    def piece_copy(p, e):
        k = p % PIECE_RING
        m, r = divmod(p, MAT_PIECES)
        if m == 0:
            return pltpu.make_async_copy(wg_hbm.at[e, pl.ds(r * ra, ra)], sa.at[k], sem.at[k])
        if m == 1:
            return pltpu.make_async_copy(wu_hbm.at[e, pl.ds(r * ra, ra)], sa.at[k], sem.at[k])
        return pltpu.make_async_copy(wd_hbm.at[e, pl.ds(r * rb, rb)], sb.at[k], sem.at[k])

    def piece_round(p, slot):
        k = p % PIECE_RING
        m, r = divmod(p, MAT_PIECES)
        if m == 0:
            wgb[slot, pl.ds(r * ra, ra), :] = sa[k].astype(BF16)
        elif m == 1:
            wub[slot, pl.ds(r * ra, ra), :] = sa[k].astype(BF16)
        else:
            wdb[slot, pl.ds(r * rb, rb), :] = sb[k].astype(BF16)

    def start_one():
        e, started, finished = st[1], st[2], st[3]
        can = (started < N_PIECES) & (started - finished < PIECE_RING)
        for p in range(N_PIECES):
            @pl.when(can & (started == p))
            def _():
                piece_copy(p, e).start(priority=1)
        st[2] = started + can.astype(I32)

    def finish_one(slot):
        e, finished = st[1], st[3]

        @pl.when((e >= 0) & (finished < N_PIECES))
        def _():
            for p in range(N_PIECES):
                @pl.when(finished == p)
                def _():
                    piece_copy(p, e).wait()
                    piece_round(p, slot)
            st[3] = finished + 1
            start_one()

    def prepare(e):
        st[1] = e
        st[2] = 0
        st[3] = 0

        @pl.when(e >= 0)
        def _():
            for _ in range(PIECE_RING):
                start_one()

    @pl.when(i < n_used)
    def _():
        e = be_ref[i]

        @pl.when(i == 0)
        def _():
            st[0] = 1
            prepare(e)

        @pl.when((i == 0) | (e != be_ref[jnp.maximum(i - 1, 0)]))
        def _():
            slot = 1 - st[0]

            def fin(_, c):
                finish_one(slot)
                return c

            lax.fori_loop(0, N_PIECES, fin, 0)
            st[0] = slot
            k = lax.while_loop(lambda k: (k < n_used) & (be_ref[jnp.minimum(k, nb - 1)] == e),
                               lambda k: k + 1, i + 1)
            prepare(jnp.where(k < n_used, be_ref[jnp.minimum(k, nb - 1)], -1))

        def ahead(_, c):
            finish_one(1 - st[0])
            return c

        lax.fori_loop(0, PIECES_PER_BLOCK, ahead, 0)

        slot = st[0]
        lo, hi = _unpack_halves(xs_ref[...])
        lo = lo.astype(BF16)
        hi = hi.astype(BF16)
        d2 = lo.shape[1]
        g = _dot(lo, wgb[slot, :d2]) + _dot(hi, wgb[slot, d2:])
        u = _dot(lo, wub[slot, :d2]) + _dot(hi, wub[slot, d2:])
        hmid = (g * jax.nn.sigmoid(g) * u).astype(BF16)
        ys_ref[...] = _pack_halves(_dot(hmid, wdb[slot]))

    @pl.when(i >= n_used)
    def _():
        ys_ref[...] = jnp.zeros(ys_ref.shape, ys_ref.dtype)


def _moe_ffn(xs, w_gate, w_up, w_down, block_expert, n_used):
    rows, d2 = xs.shape
    nb = rows // MOE_BLOCK
    d = 2 * d2
    ff = w_gate.shape[2]
    grid_spec = pltpu.PrefetchScalarGridSpec(
        num_scalar_prefetch=2,
        grid=(nb,),
        in_specs=[pl.BlockSpec((MOE_BLOCK, d2), lambda i, be, nu: (i, 0)),
                  pl.BlockSpec(memory_space=pl.ANY),
                  pl.BlockSpec(memory_space=pl.ANY),
                  pl.BlockSpec(memory_space=pl.ANY)],
        out_specs=pl.BlockSpec((MOE_BLOCK, d2), lambda i, be, nu: (i, 0)),
        scratch_shapes=[pltpu.VMEM((2, d, ff), BF16), pltpu.VMEM((2, d, ff), BF16),
                        pltpu.VMEM((2, ff, d), BF16),
                        pltpu.VMEM((PIECE_RING, d // MAT_PIECES, ff), F32),
                        pltpu.VMEM((PIECE_RING, ff // MAT_PIECES, d), F32),
                        pltpu.SemaphoreType.DMA((PIECE_RING,)),
                        pltpu.SMEM((4,), I32)],
    )
    return pl.pallas_call(
        _moe_ffn_kernel,
        grid_spec=grid_spec,
        out_shape=jax.ShapeDtypeStruct((rows, d2), I32),
        compiler_params=_cparams(("arbitrary",)),
        name="moe_ffn",
    )(block_expert, n_used, xs, w_gate, w_up, w_down)


def _moe_combine_kernel(p1_ref, p2_ref, q1_ref, q2_ref, x_ref, g1_ref, g2_ref, g_ref, b_ref, ys_hbm,
                        o_ref, yb, sem):
    i = pl.program_id(0)
    n = pl.num_programs(0)
    slot = lax.rem(i, 2)
    blk = x_ref.shape[0]

    def row_copy(s, k, r, pos):
        return pltpu.make_async_copy(ys_hbm.at[pl.ds(pos, 1)], yb.at[s, k, pl.ds(r, 1)], sem.at[s])

    def fetch(s, a_ref, b_ref2):
        def body(r, c):
            row_copy(s, 0, r, a_ref[0, 0, r]).start()
            row_copy(s, 1, r, b_ref2[0, 0, r]).start()
            return c
        lax.fori_loop(0, blk, body, 0, unroll=8)

    @pl.when(i == 0)
    def _():
        fetch(0, p1_ref, p2_ref)

    @pl.when(i + 1 < n)
    def _():
        fetch(1 - slot, q1_ref, q2_ref)

    for k in range(2):
        pltpu.make_async_copy(ys_hbm.at[pl.ds(0, blk)], yb.at[slot, k], sem.at[slot]).wait()
    y1 = jnp.concatenate(_unpack_halves(yb[slot, 0]), axis=1)
    y2 = jnp.concatenate(_unpack_halves(yb[slot, 1]), axis=1)
    y = y1 * g1_ref[...] + y2 * g2_ref[...]
    o_ref[...] = _ln_rows(ALPHA * x_ref[...] + y, g_ref[...], b_ref[...])


def _moe_combine(x2, ys, p1, p2, g1, g2, g, b):
    t, d = x2.shape
    nt = t // MOE_BLOCK
    p1 = p1.reshape(nt, 1, MOE_BLOCK)
    p2 = p2.reshape(nt, 1, MOE_BLOCK)
    pos = pl.BlockSpec((1, 1, MOE_BLOCK), lambda i: (i, 0, 0), memory_space=pltpu.SMEM)
    nxt = pl.BlockSpec((1, 1, MOE_BLOCK), lambda i: (jnp.minimum(i + 1, nt - 1), 0, 0),
                       memory_space=pltpu.SMEM)
    col = pl.BlockSpec((MOE_BLOCK, 1), lambda i: (i, 0))
    vec = pl.BlockSpec((1, d), lambda i: (0, 0))
    return pl.pallas_call(
        _moe_combine_kernel,
        grid=(nt,),
        in_specs=[pos, pos, nxt, nxt, pl.BlockSpec((MOE_BLOCK, d), lambda i: (i, 0)), col, col, vec, vec,
                  pl.BlockSpec(memory_space=pl.ANY)],
        out_specs=pl.BlockSpec((MOE_BLOCK, d), lambda i: (i, 0)),
        out_shape=jax.ShapeDtypeStruct((t, d), F32),
        scratch_shapes=[pltpu.VMEM((2, 2, MOE_BLOCK, d // 2), I32),
                        pltpu.SemaphoreType.DMA((2,))],
        compiler_params=_cparams(("arbitrary",)),
        name="moe_combine",
    )(p1, p2, p1, p2, x2, g1, g2, g.reshape(1, d), b.reshape(1, d), ys)


def _tile(n, pref):
    return pref if n % pref == 0 else n


def _layer(x, mem, w_in, conv_w, conv_b, conv_ln_g, conv_ln_b, kv_norm_g, w_uk, w_uv, rel_bias,
           conv_out_g, attn_out_g, w_out, ln1_g, ln1_b, w_mq, w_mk, w_mv, w_mo, ln2_g, ln2_b,
           w_router_grp, w_router_exp, w_gate, w_up, w_down, ln3_g, ln3_b):
    bsz, seq, d = x.shape
    t = bsz * seq
    d_conv = conv_w.shape[1]
    d_attn = N_HEADS * HEAD_DIM
    c_glu = 2 * d_conv
    c_qi = H_IDX * D_IDX
    o_q, o_kv = c_glu, c_glu + d_attn
    o_qi = o_kv + KV_RANK
    o_ki = o_qi + c_qi
    topk = min(TOPK_MAX, seq // 4)

    xf = x.reshape(t, d)
    tail_w = jnp.concatenate([w_in[:, o_kv:o_qi], w_in[:, o_ki:]], axis=1)
    tail_w = jnp.pad(tail_w, ((0, 0), (0, KV_RANK + LANES - tail_w.shape[1]))).astype(BF16)
    tm = _tile(t, 1024)

    tail, xb = _matmul_cast(xf, tail_w, F32, _tile(t, 512), "proj_tail")
    u = _matmul_cols(xb, w_in, c_glu, BF16, tm, 512, "proj_glu")
    q_hm = _matmul_heads(xb, w_in, o_q, d_attn, BF16, tm, 512, "proj_q")
    qi_hm = _matmul_heads(xb, w_in, o_qi, c_qi, BF16, tm, 512, "proj_qidx")

    conv_n = _conformer_conv(u, bsz, seq, conv_w, conv_b, conv_ln_g, conv_ln_b, conv_out_g)
    tk = min(ATT_BLOCK, seq)
    ckv_n, ckv_t, kia, kib, kw = _prep_latent(tail, kv_norm_g, tm, tk)
    ckv_t = ckv_t.reshape(bsz, seq // tk, KV_RANK, tk)
    mask = _indexer_mask(qi_hm, kw, kia, kib, bsz, seq, topk)
    w_uk_t = w_uk.transpose(0, 2, 1).reshape(d_attn, KV_RANK).astype(BF16)
    w_uv_all = w_uv.transpose(1, 0, 2).reshape(KV_RANK, d_attn).astype(BF16)
    k_t = _key_heads(w_uk_t, ckv_t)
    v_hm = _matmul_value_heads(ckv_n, w_uv_all, tm, 512, "value_heads")
    attn = _head_attention(q_hm, k_t, v_hm, mask, rel_bias, bsz, seq)

    pre1 = _matmul2_residual(conv_n, attn, attn_out_g, w_out, xf, tm, 512, "out_proj")

    mem_len = mem.shape[1]
    memb = mem.reshape(bsz * mem_len, d).astype(BF16)
    w_kv = jnp.concatenate([w_mk, w_mv], axis=1).astype(BF16)
    kv = _matmul(memb, w_kv, BF16, _tile(bsz * mem_len, 512), 512, "mem_kv")
    w_router = jnp.concatenate([w_router_grp, w_router_exp], axis=1)
    w_router = jnp.pad(w_router, ((0, 0), (0, LANES - w_router.shape[1]))).astype(BF16)
    x2, logits = _memory_attention(pre1, ln1_g, ln1_b, kv, w_mq.astype(BF16), w_mo.astype(BF16),
                                   ln2_g, ln2_b, w_router, bsz, seq, 256)

    e1, e2, g1, g2 = _router(logits, _tile(t, 1024))
    nb = (2 * t + N_EXPERTS * (MOE_BLOCK - 1) + MOE_BLOCK - 1) // MOE_BLOCK
    r1, r2, cnt = _moe_rank(e1, e2, 512)
    p1, p2, block_expert, n_used = _moe_place(e1, e2, r1, r2, cnt, nb, 512)
    xs = _moe_dispatch(x2, p1, p2, n_used, nb)
    ys = _moe_ffn(xs, w_gate, w_up, w_down, block_expert.reshape(nb), n_used[0, 0:1])
    x3 = _moe_combine(x2, ys, p1, p2, g1, g2, ln3_g, ln3_b)
    return x3.reshape(bsz, seq, d)


def kernel(x, mem, w_in, conv_w, conv_b, conv_ln_g, conv_ln_b, kv_norm_g, w_uk, w_uv, rel_bias, conv_out_g, attn_out_g, w_out, ln1_g, ln1_b, w_mq, w_mk, w_mv, w_mo, ln2_g, ln2_b, w_router_grp, w_router_exp, w_gate, w_up, w_down, ln3_g, ln3_b):
    for l in range(w_in.shape[0]):
        x = _layer(x, mem, w_in[l], conv_w[l], conv_b[l], conv_ln_g[l], conv_ln_b[l], kv_norm_g[l],
                   w_uk[l], w_uv[l], rel_bias, conv_out_g[l], attn_out_g[l], w_out[l], ln1_g[l], ln1_b[l],
                   w_mq[l], w_mk[l], w_mv[l], w_mo[l], ln2_g[l], ln2_b[l], w_router_grp[l],
                   w_router_exp[l], w_gate[l], w_up[l], w_down[l], ln3_g[l], ln3_b[l])
    return x
```

```python
import functools
import math

import jax
import jax.numpy as jnp
from jax import lax
from jax.experimental import pallas as pl
from jax.experimental.pallas import tpu as pltpu

F32 = jnp.float32
BF16 = jnp.bfloat16
I32 = jnp.int32

DEPTH = 1
CONV_WIDTH = 31
N_HEADS = 16
HEAD_DIM = 128
KV_RANK = 512
H_IDX = 32
D_IDX = 64
TOPK_MAX = 256
N_BUCKETS = 32
MAX_DIST = 128
MEM_HEADS = 4
MEM_HEAD_DIM = 128
N_GROUPS = 8
EXP_PER_GROUP = 8
N_EXPERTS = N_GROUPS * EXP_PER_GROUP
MOE_BLOCK = 128
ALPHA = (2.0 * DEPTH) ** 0.25
LN_EPS = 1e-5

LANES = 128
SUBLANES = 8
V7X_VMEM_BYTES = 64 * 1024 * 1024
VMEM_LIMIT = 56 * 1024 * 1024
NEG = -1e30
INT_MIN = -(2 ** 31)

LOG2E = 1.4426950408889634

ATT_BLOCK = 256
ATT_HEAD_GROUP = 8
CONV_TS = 256
CONV_HALO = 32
CONV_CC = 256
CONV_RC = 32


def _cparams(sem):
    return pltpu.CompilerParams(dimension_semantics=sem, vmem_limit_bytes=VMEM_LIMIT)


def _dot(a, b):
    return jnp.dot(a, b, preferred_element_type=F32)


def _dot_nt(a, b):
    return lax.dot_general(a, b, (((1,), (1,)), ((), ())), preferred_element_type=F32)


def _mm_kernel(a_ref, b_ref, o_ref):
    o_ref[...] = _dot(a_ref[...], b_ref[...]).astype(o_ref.dtype)


def _matmul(a, b, out_dtype, tm, tn, name):
    m, k = a.shape
    n = b.shape[1]
    return pl.pallas_call(
        _mm_kernel,
        grid=(m // tm, n // tn),
        in_specs=[pl.BlockSpec((tm, k), lambda i, j: (i, 0)),
                  pl.BlockSpec((k, tn), lambda i, j: (0, j))],
        out_specs=pl.BlockSpec((tm, tn), lambda i, j: (i, j)),
        out_shape=jax.ShapeDtypeStruct((m, n), out_dtype),
        compiler_params=_cparams(("parallel", "parallel")),
        name=name,
    )(a, b)


def _matmul_cols(a, b, n, out_dtype, tm, tn, name):
    m, k = a.shape
    return pl.pallas_call(
        _mm_kernel,
        grid=(m // tm, n // tn),
        in_specs=[pl.BlockSpec((tm, k), lambda i, j: (i, 0)),
                  pl.BlockSpec((k, tn), lambda i, j: (0, j))],
        out_specs=pl.BlockSpec((tm, tn), lambda i, j: (i, j)),
        out_shape=jax.ShapeDtypeStruct((m, n), out_dtype),
        compiler_params=_cparams(("parallel", "parallel")),
        name=name,
    )(a, b)


def _mm_heads_kernel(a_ref, b_ref, o_ref):
    r = _dot(a_ref[...], b_ref[...])
    for p in range(o_ref.shape[0]):
        o_ref[p] = r[:, p * LANES:(p + 1) * LANES].astype(o_ref.dtype)


def _matmul_heads(a, b, col0, n, out_dtype, tm, tn, name):
    m, k = a.shape
    assert col0 % tn == 0
    return pl.pallas_call(
        _mm_heads_kernel,
        grid=(m // tm, n // tn),
        in_specs=[pl.BlockSpec((tm, k), lambda i, j: (i, 0)),
                  pl.BlockSpec((k, tn), lambda i, j: (0, col0 // tn + j))],
        out_specs=pl.BlockSpec((tn // LANES, tm, LANES), lambda i, j: (j, i, 0)),
        out_shape=jax.ShapeDtypeStruct((n // LANES, m, LANES), out_dtype),
        compiler_params=_cparams(("parallel", "parallel")),
        name=name,
    )(a, b)


def _mm_value_heads_kernel(a_ref, b_ref, o_ref):
    r = _dot(a_ref[...], b_ref[...])
    tm = r.shape[0]
    for p in range(o_ref.shape[0]):
        o_ref[p, :, 0:LANES] = r[:, p * LANES:(p + 1) * LANES].astype(o_ref.dtype)
        o_ref[p, :, LANES:2 * LANES] = jnp.ones((tm, LANES), o_ref.dtype)


def _matmul_value_heads(a, b, tm, tn, name):
    m, k = a.shape
    n = b.shape[1]
    return pl.pallas_call(
        _mm_value_heads_kernel,
        grid=(m // tm, n // tn),
        in_specs=[pl.BlockSpec((tm, k), lambda i, j: (i, 0)),
                  pl.BlockSpec((k, tn), lambda i, j: (0, j))],
        out_specs=pl.BlockSpec((tn // LANES, tm, 2 * LANES), lambda i, j: (j, i, 0)),
        out_shape=jax.ShapeDtypeStruct((n // LANES, m, 2 * LANES), BF16),
        compiler_params=_cparams(("parallel", "parallel")),
        name=name,
    )(a, b)


def _mm2_res_kernel(a1_ref, a2_ref, g2_ref, w1_ref, w2_ref, r_ref, o_ref, a2n):
    @pl.when(pl.program_id(1) == 0)
    def _():
        x = a2_ref[...].astype(F32)
        ms = jnp.mean(x * x, axis=-1, keepdims=True)
        a2n[...] = (x * lax.rsqrt(ms + LN_EPS) * g2_ref[...]).astype(a2n.dtype)

    o_ref[...] = (ALPHA * r_ref[...] + _dot(a1_ref[...], w1_ref[...])
                  + _dot(a2n[...], w2_ref[...]))


def _matmul2_residual(a1, a2, g2, w, res, tm, tn, name):
    m, k = a1.shape
    assert a2.shape == a1.shape and w.shape[0] == 2 * k
    n = w.shape[1]
    return pl.pallas_call(
        _mm2_res_kernel,
        grid=(m // tm, n // tn),
        in_specs=[pl.BlockSpec((tm, k), lambda i, j: (i, 0)),
                  pl.BlockSpec((tm, k), lambda i, j: (i, 0)),
                  pl.BlockSpec((1, k), lambda i, j: (0, 0)),
                  pl.BlockSpec((k, tn), lambda i, j: (0, j)),
                  pl.BlockSpec((k, tn), lambda i, j: (1, j)),
                  pl.BlockSpec((tm, tn), lambda i, j: (i, j))],
        out_specs=pl.BlockSpec((tm, tn), lambda i, j: (i, j)),
        out_shape=jax.ShapeDtypeStruct((m, n), F32),
        scratch_shapes=[pltpu.VMEM((tm, k), BF16)],
        compiler_params=_cparams(("parallel", "arbitrary")),
        name=name,
    )(a1, a2, g2.reshape(1, k), w, w, res)


def _ln_rows(x, g, b):
    mu = jnp.mean(x, axis=-1, keepdims=True)
    xc = x - mu
    var = jnp.mean(xc * xc, axis=-1, keepdims=True)
    return xc * lax.rsqrt(var + LN_EPS) * g + b


def _conv_kernel(a_ref, g_ref, cw_ref, cb_ref, lg_ref, lb_ref, og_ref, o_ref, hbuf, ybuf, hs):
    ts = a_ref.shape[0]
    nch = hbuf.shape[0]
    cc = hbuf.shape[2]
    d_conv = nch * cc

    nrow = CONV_HALO + ts

    @pl.when(pl.program_id(1) == 0)
    def _():
        hbuf[:, 0:CONV_HALO, :] = jnp.zeros((nch, CONV_HALO, cc), F32)
        hbuf[:, nrow:nrow + SUBLANES, :] = jnp.zeros((nch, SUBLANES, cc), F32)

    for c in range(nch):
        a = a_ref[:, c * cc:(c + 1) * cc].astype(F32)
        g = g_ref[:, c * cc:(c + 1) * cc].astype(F32)
        hbuf[c, CONV_HALO:nrow, :] = a * jax.nn.sigmoid(g)

    first = CONV_HALO - (CONV_WIDTH - 1)

    def chunk_body(c, carry):
        for o in range(1, SUBLANES):
            hs[o - 1] = hbuf[c, o:o + nrow, :]
        for r0 in range(0, ts, CONV_RC):
            acc = jnp.zeros((CONV_RC, cc), F32)
            for j in range(CONV_WIDTH):
                o = (first + j) % SUBLANES
                base = r0 + first + j - o
                rows = hbuf[c, base:base + CONV_RC, :] if o == 0 else hs[o - 1, base:base + CONV_RC, :]
                acc = acc + cw_ref[c, j:j + 1, :] * rows
            ybuf[c, r0:r0 + CONV_RC, :] = acc + cb_ref[c]
        hbuf[c, 0:CONV_HALO, :] = hbuf[c, ts:nrow, :]
        return carry

    lax.fori_loop(0, nch, chunk_body, 0)

    s1 = jnp.zeros((ts, 1), F32)
    for c in range(nch):
        s1 = s1 + jnp.sum(ybuf[c], axis=1, keepdims=True)
    mu = s1 * (1.0 / d_conv)
    s2 = jnp.zeros((ts, 1), F32)
    for c in range(nch):
        yc = ybuf[c] - mu
        s2 = s2 + jnp.sum(yc * yc, axis=1, keepdims=True)
    rstd = lax.rsqrt(s2 * (1.0 / d_conv) + LN_EPS)
    s3 = jnp.zeros((ts, 1), F32)
    for c in range(nch):
        z = (ybuf[c] - mu) * rstd * lg_ref[:, c * cc:(c + 1) * cc] + lb_ref[:, c * cc:(c + 1) * cc]
        z = z * jax.nn.sigmoid(z)
        ybuf[c] = z
        s3 = s3 + jnp.sum(z * z, axis=1, keepdims=True)
    rr = lax.rsqrt(s3 * (1.0 / d_conv) + LN_EPS)
    for c in range(nch):
        o_ref[:, c * cc:(c + 1) * cc] = (ybuf[c] * rr * og_ref[:, c * cc:(c + 1) * cc]).astype(o_ref.dtype)


def _conformer_conv(u, bsz, seq, conv_w, conv_b, ln_g, ln_b, out_g):
    d_conv = u.shape[1] // 2
    ts = min(CONV_TS, seq)
    nch = d_conv // CONV_CC
    nt = seq // ts
    cw = conv_w.reshape(CONV_WIDTH, nch, CONV_CC).transpose(1, 0, 2)
    cb = conv_b.reshape(nch, 1, CONV_CC)
    vec = pl.BlockSpec((1, d_conv), lambda b, i: (0, 0))
    return pl.pallas_call(
        _conv_kernel,
        grid=(bsz, nt),
        in_specs=[pl.BlockSpec((ts, d_conv), lambda b, i: (b * nt + i, 0)),
                  pl.BlockSpec((ts, d_conv), lambda b, i: (b * nt + i, 1)),
                  pl.BlockSpec((nch, CONV_WIDTH, CONV_CC), lambda b, i: (0, 0, 0)),
                  pl.BlockSpec((nch, 1, CONV_CC), lambda b, i: (0, 0, 0)),
                  vec, vec, vec],
        out_specs=pl.BlockSpec((ts, d_conv), lambda b, i: (b * nt + i, 0)),
        out_shape=jax.ShapeDtypeStruct((bsz * seq, d_conv), BF16),
        scratch_shapes=[pltpu.VMEM((nch, CONV_HALO + ts + SUBLANES, CONV_CC), F32),
                        pltpu.VMEM((nch, ts, CONV_CC), F32),
                        pltpu.VMEM((SUBLANES - 1, CONV_HALO + ts, CONV_CC), F32)],
        compiler_params=_cparams(("arbitrary", "arbitrary")),
        name="conformer_conv",
    )(u, u, cw, cb, ln_g.reshape(1, d_conv), ln_b.reshape(1, d_conv), out_g.reshape(1, d_conv))


def _prep_kernel(t_ref, g_ref, ckv_ref, ckvt_ref, kia_ref, kib_ref, kw_ref):
    ckv = t_ref[:, 0:KV_RANK]
    ms = jnp.mean(ckv * ckv, axis=-1, keepdims=True)
    ckv_n = ckv * lax.rsqrt(ms + LN_EPS) * g_ref[...]
    ckv_ref[...] = ckv_n.astype(ckv_ref.dtype)
    tk = ckvt_ref.shape[3]
    for c in range(ckvt_ref.shape[1]):
        ckvt_ref[0, c] = ckv_n[c * tk:(c + 1) * tk, :].T.astype(ckvt_ref.dtype)
    kw = t_ref[:, KV_RANK:KV_RANK + LANES]
    kw_ref[...] = kw
    lane = lax.broadcasted_iota(I32, kw.shape, 1)
    kia_ref[...] = jnp.where(lane < D_IDX, kw, 0.0).astype(kia_ref.dtype)
    kib_ref[...] = jnp.where(lane >= D_IDX, pltpu.roll(kw, D_IDX, 1), 0.0).astype(kib_ref.dtype)


def _prep_latent(tail, kv_norm_g, tm, tk):
    m, w = tail.shape
    return pl.pallas_call(
        _prep_kernel,
        grid=(m // tm,),
        in_specs=[pl.BlockSpec((tm, w), lambda i: (i, 0)),
                  pl.BlockSpec((1, KV_RANK), lambda i: (0, 0))],
        out_specs=[pl.BlockSpec((tm, KV_RANK), lambda i: (i, 0)),
                   pl.BlockSpec((1, tm // tk, KV_RANK, tk), lambda i: (i, 0, 0, 0)),
                   pl.BlockSpec((tm, LANES), lambda i: (i, 0)),
                   pl.BlockSpec((tm, LANES), lambda i: (i, 0)),
                   pl.BlockSpec((tm, LANES), lambda i: (i, 0))],
        out_shape=[jax.ShapeDtypeStruct((m, KV_RANK), BF16),
                   jax.ShapeDtypeStruct((m // tm, tm // tk, KV_RANK, tk), BF16),
                   jax.ShapeDtypeStruct((m, LANES), BF16),
                   jax.ShapeDtypeStruct((m, LANES), BF16),
                   jax.ShapeDtypeStruct((m, LANES), F32)],
        compiler_params=_cparams(("parallel",)),
        name="prep_latent",
    )(tail, kv_norm_g.reshape(1, KV_RANK))


def _indexer_t_kernel(qi_ref, kw_ref, kia_ref, kib_ref, o_ref, keybuf, *, topk):
    i = pl.program_id(1)
    npairs, tq, _ = qi_ref.shape
    nk = o_ref.shape[1]
    tk = o_ref.shape[3]
    kf = float(topk)
    group = 4
    slabs = tk // SUBLANES

    w_t = kw_ref[...].T
    key_idx = lax.broadcasted_iota(I32, (tk, tq), 0)
    qry_idx = lax.broadcasted_iota(I32, (tk, tq), 1) + i * tq

    def score_chunk(j, carry):
        k0 = pl.multiple_of(j * tk, tk)
        kd = jnp.concatenate([kia_ref[pl.ds(k0, tk), :], kib_ref[pl.ds(k0, tk), :]], axis=0)
        acc = jnp.zeros((tk, tq), F32)
        for p0 in range(0, npairs, group):
            rhs = qi_ref[p0:p0 + group].reshape(group * tq, LANES)
            zz = _dot_nt(kd, rhs)
            for p in range(group):
                h = 2 * (p0 + p)
                z = zz[:, p * tq:(p + 1) * tq]
                acc = (acc + w_t[D_IDX + h:D_IDX + h + 1, :] * jnp.maximum(z[0:tk], 0.0)
                       + w_t[D_IDX + h + 1:D_IDX + h + 2, :] * jnp.maximum(z[tk:2 * tk], 0.0))
        bits = pltpu.bitcast(acc, I32)
        key = jnp.where(bits >= 0, bits, bits ^ jnp.int32(0x7FFFFFFF))
        keybuf[j] = jnp.where(key_idx + j * tk <= qry_idx, key, INT_MIN)
        return carry

    lax.fori_loop(0, i + 1, score_chunk, 0)

    def count(pred):
        def body(j, c):
            hit = jnp.where(pred(keybuf[j], key_idx + j * tk), 1.0, 0.0)
            return c + jnp.sum(hit.reshape(slabs, SUBLANES, tq), axis=0)

        c = lax.fori_loop(0, i + 1, body, jnp.zeros((SUBLANES, tq), F32))
        return jnp.broadcast_to(jnp.sum(c, axis=0, keepdims=True), (SUBLANES, tq))

    def tile_rows(v):
        return jnp.concatenate([v] * slabs, axis=0)

    def count_ge(cand):
        cb = tile_rows(cand)
        return count(lambda kk, idx: kk >= cb)

    tau = jnp.where(count_ge(jnp.zeros((SUBLANES, tq), I32)) >= kf, 0, INT_MIN).astype(I32)

    def bit_body(it, tau):
        cand = tau | jnp.left_shift(jnp.int32(1), 30 - it)
        return jnp.where(count_ge(cand) >= kf, cand, tau)

    tau = lax.fori_loop(0, 31, bit_body, tau)
    n_ge = count_ge(tau)
    taub = tile_rows(tau)

    def tie_cut():
        need = kf - count(lambda kk, idx: kk > taub)

        def cut_body(it, cut):
            cand = cut + jnp.left_shift(jnp.int32(1), 30 - it)
            cb = tile_rows(cand)
            below = count(lambda kk, idx: (kk == taub) & (idx < cb))
            return jnp.where(below < need, cand, cut)

        return lax.fori_loop(0, 31, cut_body, jnp.zeros((SUBLANES, tq), I32))

    has_ties = jnp.max(n_ge) > kf
    cut = lax.cond(has_ties, tie_cut, lambda: jnp.full((SUBLANES, tq), 2 ** 30, I32))
    cutb = tile_rows(cut)
    eye = jnp.where(lax.broadcasted_iota(I32, (tq, tq), 0) == lax.broadcasted_iota(I32, (tq, tq), 1),
                    1.0, 0.0).astype(BF16)

    def write_chunk(j, carry):
        kk = keybuf[j]
        sel = ((kk > taub) | ((kk == taub) & (key_idx + j * tk <= cutb))) & (kk != INT_MIN)
        sel_qk = _dot_nt(eye, jnp.where(sel, 1.0, 0.0).astype(BF16))
        o_ref[0, j] = jnp.where(sel_qk > 0.5, 0.0, NEG).astype(o_ref.dtype)
        return carry

    lax.fori_loop(0, i + 1, write_chunk, 0)

    def write_rest(j, carry):
        o_ref[0, j] = jnp.full((tq, tk), NEG, o_ref.dtype)
        return carry

    lax.fori_loop(i + 1, nk, write_rest, 0)


def _indexer_mask(qi_hm, kw, kia, kib, bsz, seq, topk):
    tq = min(ATT_BLOCK, seq)
    nq = seq // tq
    npairs = qi_hm.shape[0]
    return pl.pallas_call(
        functools.partial(_indexer_t_kernel, topk=topk),
        grid=(bsz, nq),
        in_specs=[pl.BlockSpec((npairs, tq, LANES), lambda b, i: (0, b * nq + i, 0)),
                  pl.BlockSpec((tq, LANES), lambda b, i: (b * nq + i, 0)),
                  pl.BlockSpec((seq, LANES), lambda b, i: (b, 0)),
                  pl.BlockSpec((seq, LANES), lambda b, i: (b, 0))],
        out_specs=pl.BlockSpec((1, nq, tq, tq), lambda b, i: (b * nq + i, 0, 0, 0)),
        out_shape=jax.ShapeDtypeStruct((bsz * nq, nq, tq, tq), BF16),
        scratch_shapes=[pltpu.VMEM((nq, tq, tq), I32)],
        compiler_params=_cparams(("parallel", "parallel")),
        name="indexer_mask",
    )(qi_hm, kw, kia, kib)


def _t5_bucket(dist):
    n = jnp.maximum(dist, 0)
    max_exact = N_BUCKETS // 2
    nf = jnp.maximum(n, 1).astype(F32)
    large = max_exact + (jnp.log(nf / max_exact) / math.log(MAX_DIST / max_exact)
                         * (N_BUCKETS - max_exact)).astype(I32)
    large = jnp.minimum(large, N_BUCKETS - 1)
    return jnp.where(n < max_exact, n, large)


def _key_heads_kernel(wt_ref, ct_ref, o_ref):
    r = _dot(wt_ref[...], ct_ref[0, 0]) * (HEAD_DIM ** -0.5 * LOG2E)
    for h in range(o_ref.shape[1]):
        o_ref[0, h, 0] = r[h * HEAD_DIM:(h + 1) * HEAD_DIM].astype(o_ref.dtype)


def _key_heads(w_uk_t, ckv_t):
    bsz, nk, r_lat, tk = ckv_t.shape
    nh = w_uk_t.shape[0] // HEAD_DIM
    return pl.pallas_call(
        _key_heads_kernel,
        grid=(bsz, nk),
        in_specs=[pl.BlockSpec((nh * HEAD_DIM, r_lat), lambda b, j: (0, 0)),
                  pl.BlockSpec((1, 1, r_lat, tk), lambda b, j: (b, j, 0, 0))],
        out_specs=pl.BlockSpec((1, nh, 1, HEAD_DIM, tk), lambda b, j: (b, 0, j, 0, 0)),
        out_shape=jax.ShapeDtypeStruct((bsz, nh, nk, HEAD_DIM, tk), BF16),
        compiler_params=_cparams(("parallel", "parallel")),
        name="key_heads",
    )(w_uk_t, ckv_t)


def _attn_kernel(relb_ref, q_ref, kt_ref, v_ref, mask_ref, o_ref, acc, m_s, btab):
    b = pl.program_id(0)
    g = pl.program_id(1)
    i = pl.program_id(2)
    hg, tq, _ = q_ref.shape
    nh = btab.shape[1]
    tk = mask_ref.shape[3]

    @pl.when((b == 0) & (g == 0) & (i == 0))
    def _():
        r = lax.broadcasted_iota(I32, (tq, tk), 0)
        c = lax.broadcasted_iota(I32, (tq, tk), 1)
        for t in range(2):
            bk = _t5_bucket(r - c + t * tk)

            def fill(h, carry):
                far = relb_ref[N_BUCKETS - 1, h]
                v = jnp.zeros((tq, tk), F32)
                for k in range(N_BUCKETS):
                    v = jnp.where(bk == k, (relb_ref[k, h] - far) * LOG2E, v)
                btab[t, h] = v
                return carry

            lax.fori_loop(0, nh, fill, 0)

    acc[...] = jnp.zeros(acc.shape, F32)
    m_s[...] = jnp.full(m_s.shape, NEG, F32)

    def lane_tile(v, n):
        return jnp.concatenate([v] * n, axis=1)

    def chunks(js, near):
        mks = [mask_ref[0, j].astype(F32) for j in js]
        for h in range(hg):
            for j, mk in zip(js, mks):
                s = _dot(q_ref[h], kt_ref[0, h, j]) + mk
                if near:
                    s = s + btab[i - j, g * hg + h]
                m_old = m_s[h]
                m_new = jnp.maximum(m_old, jnp.max(s, axis=1, keepdims=True))
                a = jnp.exp2(m_old - m_new)
                p = jnp.exp2(s - lane_tile(m_new, tk // LANES))
                rows = pl.ds(pl.multiple_of(j * tk, tk), tk)
                acc[h] = lane_tile(a, 2) * acc[h] + _dot(p.astype(BF16), v_ref[h, rows, :])
                m_s[h] = m_new

    n_far = jnp.maximum(i - 1, 0)

    def far_quad(p, carry):
        chunks([4 * p, 4 * p + 1, 4 * p + 2, 4 * p + 3], False)
        return carry

    lax.fori_loop(0, lax.shift_right_logical(n_far, 2), far_quad, 0)
    done = n_far & ~3

    @pl.when((n_far & 2) == 2)
    def _():
        chunks([done, done + 1], False)

    @pl.when((n_far & 1) == 1)
    def _():
        chunks([n_far - 1], False)

    @pl.when(i >= 1)
    def _():
        chunks([i - 1, i], True)

    @pl.when(i == 0)
    def _():
        chunks([i], True)

    for h in range(hg):
        o_ref[:, h * HEAD_DIM:(h + 1) * HEAD_DIM] = (
            acc[h, :, :HEAD_DIM] / acc[h, :, HEAD_DIM:]).astype(o_ref.dtype)


def _head_attention(q_hm, k_t, v_hm, mask, rel_bias, bsz, seq):
    nh = q_hm.shape[0]
    hg = ATT_HEAD_GROUP
    tq = mask.shape[2]
    nq = seq // tq
    assert tq == mask.shape[3] and tq >= MAX_DIST
    once = pl.Buffered(1)
    return pl.pallas_call(
        _attn_kernel,
        grid=(bsz, nh // hg, nq),
        in_specs=[pl.BlockSpec(memory_space=pltpu.SMEM),
                  pl.BlockSpec((hg, tq, HEAD_DIM), lambda b, g, i: (g, b * nq + i, 0)),
                  pl.BlockSpec((1, hg, nq, HEAD_DIM, tq), lambda b, g, i: (b, g, 0, 0, 0), pipeline_mode=once),
                  pl.BlockSpec((hg, seq, 2 * HEAD_DIM), lambda b, g, i: (g, b, 0), pipeline_mode=once),
                  pl.BlockSpec((1, nq, tq, tq), lambda b, g, i: (b * nq + i, 0, 0, 0))],
        out_specs=pl.BlockSpec((tq, hg * HEAD_DIM), lambda b, g, i: (b * nq + i, g)),
        out_shape=jax.ShapeDtypeStruct((bsz * seq, nh * HEAD_DIM), BF16),
        scratch_shapes=[pltpu.VMEM((hg, tq, 2 * HEAD_DIM), F32),
                        pltpu.VMEM((hg, tq, LANES), F32),
                        pltpu.VMEM((2, nh, tq, tq), F32)],
        compiler_params=_cparams(("arbitrary", "arbitrary", "arbitrary")),
        name="head_attention",
    )(rel_bias, q_hm, k_t, v_hm, mask)


def _memattn_kernel(x_ref, g1_ref, b1_ref, kv_ref, wq_ref, wo_ref, g_ref, b_ref, wr_ref, o_ref, lg_ref):
    x = _ln_rows(x_ref[...], g1_ref[...], b1_ref[...])
    d_mem = MEM_HEADS * MEM_HEAD_DIM
    q = (_dot(x.astype(BF16), wq_ref[...]) * (MEM_HEAD_DIM ** -0.5)).astype(BF16)
    outs = []
    for h in range(MEM_HEADS):
        lo = h * MEM_HEAD_DIM
        k = kv_ref[:, lo:lo + MEM_HEAD_DIM]
        v = kv_ref[:, d_mem + lo:d_mem + lo + MEM_HEAD_DIM]
        s = _dot_nt(q[:, lo:lo + MEM_HEAD_DIM], k)
        p = jnp.exp(s - jnp.max(s, axis=1, keepdims=True))
        p = p / jnp.sum(p, axis=1, keepdims=True)
        outs.append(_dot(p.astype(BF16), v).astype(BF16))
    o = jnp.concatenate(outs, axis=1)
    x2 = _ln_rows(ALPHA * x + _dot(o, wo_ref[...]), g_ref[...], b_ref[...])
    o_ref[...] = x2
    lg_ref[...] = _dot(x2.astype(BF16), wr_ref[...])


def _memory_attention(pre1, g1, b1, kv, w_mq, w_mo, g, b, w_router, bsz, seq, tm):
    t, d = pre1.shape
    nt = seq // tm
    mem_len = kv.shape[0] // bsz
    d_mem = w_mq.shape[1]
    return pl.pallas_call(
        _memattn_kernel,
        grid=(bsz, nt),
        in_specs=[pl.BlockSpec((tm, d), lambda bi, i: (bi * nt + i, 0)),
                  pl.BlockSpec((1, d), lambda bi, i: (0, 0)),
                  pl.BlockSpec((1, d), lambda bi, i: (0, 0)),
                  pl.BlockSpec((mem_len, 2 * d_mem), lambda bi, i: (bi, 0)),
                  pl.BlockSpec((d, d_mem), lambda bi, i: (0, 0), pipeline_mode=pl.Buffered(1)),
                  pl.BlockSpec((d_mem, d), lambda bi, i: (0, 0), pipeline_mode=pl.Buffered(1)),
                  pl.BlockSpec((1, d), lambda bi, i: (0, 0)),
                  pl.BlockSpec((1, d), lambda bi, i: (0, 0)),
                  pl.BlockSpec((d, LANES), lambda bi, i: (0, 0), pipeline_mode=pl.Buffered(1))],
        out_specs=[pl.BlockSpec((tm, d), lambda bi, i: (bi * nt + i, 0)),
                   pl.BlockSpec((tm, LANES), lambda bi, i: (bi * nt + i, 0))],
        out_shape=[jax.ShapeDtypeStruct((t, d), F32),
                   jax.ShapeDtypeStruct((t, LANES), F32)],
        compiler_params=_cparams(("parallel", "parallel")),
        name="memory_attention",
    )(pre1, g1.reshape(1, d), b1.reshape(1, d), kv, w_mq, w_mo, g.reshape(1, d), b.reshape(1, d), w_router)


def _router_kernel(lg_ref, e1_ref, e2_ref, g1_ref, g2_ref):
    x = lg_ref[...]
    lane = lax.broadcasted_iota(I32, x.shape, 1)
    lane_f = lane.astype(F32)

    def argmax(mask):
        v = jnp.where(mask, x, -jnp.inf)
        mx = jnp.max(v, axis=1, keepdims=True)
        idx = jnp.min(jnp.where(mask & (v == mx), lane_f, float(LANES)), axis=1, keepdims=True)
        return mx, idx.astype(I32)

    gmask = lane < N_GROUPS
    gmax, gsel = argmax(gmask)
    gsum = jnp.sum(jnp.where(gmask, jnp.exp(x - gmax), 0.0), axis=1, keepdims=True)
    g_p = 1.0 / gsum
    lo = N_GROUPS + gsel * EXP_PER_GROUP
    emask = (lane >= lo) & (lane < lo + EXP_PER_GROUP)
    m1, i1 = argmax(emask)
    m2, i2 = argmax(emask & (lane != i1))
    esum = jnp.sum(jnp.where(emask, jnp.exp(x - m1), 0.0), axis=1, keepdims=True)
    p1 = 1.0 / esum
    p2 = jnp.exp(m2 - m1) / esum
    e1_ref[...] = i1 - N_GROUPS
    e2_ref[...] = i2 - N_GROUPS
    g1_ref[...] = g_p * (p1 / (p1 + p2))
    g2_ref[...] = g_p * (p2 / (p1 + p2))


def _router(logits, tm):
    t = logits.shape[0]
    col = pl.BlockSpec((tm, 1), lambda i: (i, 0))
    return pl.pallas_call(
        _router_kernel,
        grid=(t // tm,),
        in_specs=[pl.BlockSpec((tm, LANES), lambda i: (i, 0))],
        out_specs=[col, col, col, col],
        out_shape=[jax.ShapeDtypeStruct((t, 1), I32), jax.ShapeDtypeStruct((t, 1), I32),
                   jax.ShapeDtypeStruct((t, 1), F32), jax.ShapeDtypeStruct((t, 1), F32)],
        compiler_params=_cparams(("parallel",)),
        name="moe_router",
    )(logits)


HI16 = -65536


def _pack_halves(x):
    d = x.shape[1] // 2
    lo = pltpu.bitcast(x[:, :d].astype(BF16).astype(F32), I32)
    hi = pltpu.bitcast(x[:, d:].astype(BF16).astype(F32), I32)
    return lax.shift_right_logical(lo, 16) | (hi & HI16)


def _unpack_halves(u):
    return pltpu.bitcast(lax.shift_left(u, 16), F32), pltpu.bitcast(u & HI16, F32)


def _moe_rank_kernel(e1_ref, e2_ref, r1_ref, r2_ref, cnt_ref, base):
    i = pl.program_id(0)
    tm = e1_ref.shape[0]

    @pl.when(i == 0)
    def _():
        base[...] = jnp.zeros(base.shape, F32)

    lane = lax.broadcasted_iota(I32, (tm, LANES), 1)
    rr = lax.broadcasted_iota(I32, (tm, tm), 0)
    cc = lax.broadcasted_iota(I32, (tm, tm), 1)
    earlier = jnp.where(cc < rr, 1.0, 0.0).astype(BF16)
    for slot, (e_ref, r_ref) in enumerate(((e1_ref, r1_ref), (e2_ref, r2_ref))):
        oh = jnp.where(lane == e_ref[...], 1.0, 0.0)
        before = _dot(earlier, oh.astype(BF16)) + base[slot:slot + 1, :]
        r_ref[...] = jnp.sum(oh * before, axis=1, keepdims=True).astype(I32)
        base[slot:slot + 1, :] = base[slot:slot + 1, :] + jnp.sum(oh, axis=0, keepdims=True)
    cnt_ref[...] = base[...]


def _moe_rank(e1, e2, tm):
    t = e1.shape[0]
    col = pl.BlockSpec((tm, 1), lambda i: (i, 0))
    return pl.pallas_call(
        _moe_rank_kernel,
        grid=(t // tm,),
        in_specs=[col, col],
        out_specs=[col, col, pl.BlockSpec((8, LANES), lambda i: (0, 0))],
        out_shape=[jax.ShapeDtypeStruct((t, 1), I32), jax.ShapeDtypeStruct((t, 1), I32),
                   jax.ShapeDtypeStruct((8, LANES), F32)],
        scratch_shapes=[pltpu.VMEM((8, LANES), F32)],
        compiler_params=_cparams(("arbitrary",)),
        name="moe_rank",
    )(e1, e2)


def _moe_place_kernel(e1_ref, e2_ref, r1_ref, r2_ref, cnt_ref, p1_ref, p2_ref, be_ref, nu_ref):
    tm = e1_ref.shape[0]
    nbp = be_ref.shape[0]
    lane8 = lax.broadcasted_iota(I32, (8, LANES), 1)
    cnt = cnt_ref[...].astype(I32)
    c0 = jnp.broadcast_to(cnt[0:1], (8, LANES))
    c1 = jnp.broadcast_to(cnt[1:2], (8, LANES))
    blk_shift = MOE_BLOCK.bit_length() - 1
    padded = lax.shift_left(lax.shift_right_logical(c0 + c1 + (MOE_BLOCK - 1), blk_shift), blk_shift)
    pad_end = padded
    s = 1
    while s < LANES:
        pad_end = pad_end + jnp.where(lane8 >= s, pltpu.roll(pad_end, s, 1), 0)
        s *= 2
    start0 = (pad_end - padded).astype(F32)
    start1 = (pad_end - padded + c0).astype(F32)

    lane = lax.broadcasted_iota(I32, (tm, LANES), 1)
    for e_ref, r_ref, p_ref, start in ((e1_ref, r1_ref, p1_ref, start0), (e2_ref, r2_ref, p2_ref, start1)):
        seg = jnp.sum(jnp.where(lane == e_ref[...], start[0:1], 0.0), axis=1, keepdims=True)
        p_ref[...] = seg.astype(I32) + r_ref[...]

    block_row = lax.broadcasted_iota(I32, (nbp, LANES), 0) * MOE_BLOCK
    lane_b = lax.broadcasted_iota(I32, (nbp, LANES), 1)
    ended = (pad_end[0:1] <= block_row) & (lane_b < N_EXPERTS)
    be = jnp.sum(jnp.where(ended, 1.0, 0.0), axis=1, keepdims=True)
    be_ref[...] = jnp.minimum(be, N_EXPERTS - 1.0).astype(I32)
    total = jnp.max(pad_end, axis=1, keepdims=True)
    sub = lax.broadcasted_iota(I32, (8, LANES), 0)
    n_used = jnp.broadcast_to(lax.shift_right_logical(total, blk_shift), (8, LANES))
    nu_ref[...] = jnp.where(sub == 0, n_used, jnp.where(sub == 1, pad_end, jnp.where(sub == 2, padded, 0)))


def _moe_place(e1, e2, r1, r2, cnt, nb, tm):
    t = e1.shape[0]
    col = pl.BlockSpec((tm, 1), lambda i: (i, 0))
    return pl.pallas_call(
        _moe_place_kernel,
        grid=(t // tm,),
        in_specs=[col, col, col, col, pl.BlockSpec((8, LANES), lambda i: (0, 0))],
        out_specs=[col, col, pl.BlockSpec((nb, 1), lambda i: (0, 0)),
                   pl.BlockSpec((8, LANES), lambda i: (0, 0))],
        out_shape=[jax.ShapeDtypeStruct((t, 1), I32), jax.ShapeDtypeStruct((t, 1), I32),
                   jax.ShapeDtypeStruct((nb, 1), I32), jax.ShapeDtypeStruct((8, LANES), I32)],
        compiler_params=_cparams(("arbitrary",)),
        name="moe_place",
    )(e1, e2, r1, r2, cnt)


def _moe_dispatch_kernel(seg_ref, p1_ref, p2_ref, x_ref, xs_hbm, pk, zbuf, sem, zsem):
    i = pl.program_id(0)
    n = pl.num_programs(0)
    slot = lax.rem(i, 2)
    blk = x_ref.shape[0]
    nb = xs_hbm.shape[0] // blk

    def row_copy(s, r, pos):
        return pltpu.make_async_copy(pk.at[s, pl.ds(r, 1)], xs_hbm.at[pl.ds(pos, 1)], sem.at[s])

    def drain(s):
        for _ in range(2):
            pltpu.make_async_copy(pk.at[s], xs_hbm.at[pl.ds(0, blk)], sem.at[s]).wait()

    def zero_copy(row0):
        return pltpu.make_async_copy(zbuf, xs_hbm.at[pl.ds(pl.multiple_of(row0, blk), blk)], zsem)

    @pl.when(i == 0)
    def _():
        zbuf[...] = jnp.zeros(zbuf.shape, zbuf.dtype)
        n_used = seg_ref[0, 0]

        def seg_start(e, c):
            @pl.when(seg_ref[2, e] > 0)
            def _():
                zero_copy(seg_ref[1, e] - blk).start()
            return c

        def tail_start(b, c):
            zero_copy(b * blk).start()
            return c

        def seg_wait(e, c):
            @pl.when(seg_ref[2, e] > 0)
            def _():
                zero_copy(0).wait()
            return c

        def tail_wait(b, c):
            zero_copy(0).wait()
            return c

        lax.fori_loop(0, N_EXPERTS, seg_start, 0)
        lax.fori_loop(n_used, nb, tail_start, 0)
        lax.fori_loop(0, N_EXPERTS, seg_wait, 0)
        lax.fori_loop(n_used, nb, tail_wait, 0)

    @pl.when(i >= 2)
    def _():
        drain(slot)

    pk[slot] = _pack_halves(x_ref[...])

    def issue(r, c):
        row_copy(slot, r, p1_ref[0, 0, r]).start()
        row_copy(slot, r, p2_ref[0, 0, r]).start()
        return c

    lax.fori_loop(0, blk, issue, 0, unroll=8)

    @pl.when(i == n - 1)
    def _():
        drain(slot)

    @pl.when((i == n - 1) & (i >= 1))
    def _():
        drain(1 - slot)


def _moe_dispatch(x2, p1, p2, seg, nb):
    t, d = x2.shape
    nt = t // MOE_BLOCK
    rows = nb * MOE_BLOCK
    pos = pl.BlockSpec((1, 1, MOE_BLOCK), lambda i: (i, 0, 0), memory_space=pltpu.SMEM)
    return pl.pallas_call(
        _moe_dispatch_kernel,
        grid=(nt,),
        in_specs=[pl.BlockSpec(memory_space=pltpu.SMEM), pos, pos,
                  pl.BlockSpec((MOE_BLOCK, d), lambda i: (i, 0))],
        out_specs=pl.BlockSpec(memory_space=pl.ANY),
        out_shape=jax.ShapeDtypeStruct((rows, d // 2), I32),
        scratch_shapes=[pltpu.VMEM((2, MOE_BLOCK, d // 2), I32),
                        pltpu.VMEM((MOE_BLOCK, d // 2), I32),
                        pltpu.SemaphoreType.DMA((2,)),
                        pltpu.SemaphoreType.DMA(())],
        compiler_params=_cparams(("arbitrary",)),
        name="moe_dispatch",
    )(seg, p1.reshape(nt, 1, MOE_BLOCK), p2.reshape(nt, 1, MOE_BLOCK), x2)


MAT_PIECES = 4
N_PIECES = 3 * MAT_PIECES
PIECES_PER_BLOCK = 3
PIECE_RING = 4


def _moe_ffn_kernel(be_ref, nu_ref, xs_ref, wg_hbm, wu_hbm, wd_hbm, ys_ref,
                    wgb, wub, wdb, sa, sb, sem, st):
    i = pl.program_id(0)
    nb = be_ref.shape[0]
    n_used = nu_ref[0]
    ra = wgb.shape[1] // MAT_PIECES
    rb = wdb.shape[1] // MAT_PIECES

    def piece_copy(p, e):
        k = p % PIECE_RING
        m, r = divmod(p, MAT_PIECES)
        if m == 0:
            return pltpu.make_async_copy(wg_hbm.at[e, pl.ds(r * ra, ra)], sa.at[k], sem.at[k])
        if m == 1:
            return pltpu.make_async_copy(wu_hbm.at[e, pl.ds(r * ra, ra)], sa.at[k], sem.at[k])
        return pltpu.make_async_copy(wd_hbm.at[e, pl.ds(r * rb, rb)], sb.at[k], sem.at[k])

    def piece_round(p, slot):
        k = p % PIECE_RING
        m, r = divmod(p, MAT_PIECES)
        if m == 0:
            wgb[slot, pl.ds(r * ra, ra), :] = sa[k].astype(BF16)
        elif m == 1:
            wub[slot, pl.ds(r * ra, ra), :] = sa[k].astype(BF16)
        else:
            wdb[slot, pl.ds(r * rb, rb), :] = sb[k].astype(BF16)

    def start_one():
        e, started, finished = st[1], st[2], st[3]
        can = (started < N_PIECES) & (started - finished < PIECE_RING)
        for p in range(N_PIECES):
            @pl.when(can & (started == p))
            def _():
                piece_copy(p, e).start(priority=p % 2)
        st[2] = started + can.astype(I32)

    def finish_one(slot):
        e, finished = st[1], st[3]

        @pl.when((e >= 0) & (finished < N_PIECES))
        def _():
            for p in range(N_PIECES):
                @pl.when(finished == p)
                def _():
                    piece_copy(p, e).wait()
                    piece_round(p, slot)
            st[3] = finished + 1
            start_one()

    def prepare(e):
        st[1] = e
        st[2] = 0
        st[3] = 0

        @pl.when(e >= 0)
        def _():
            for _ in range(PIECE_RING):
                start_one()

    @pl.when(i < n_used)
    def _():
        e = be_ref[i]

        @pl.when(i == 0)
        def _():
            st[0] = 1
            prepare(e)

        @pl.when((i == 0) | (e != be_ref[jnp.maximum(i - 1, 0)]))
        def _():
            slot = 1 - st[0]

            def fin(_, c):
                finish_one(slot)
                return c

            lax.fori_loop(0, N_PIECES, fin, 0)
            st[0] = slot
            k = lax.while_loop(lambda k: (k < n_used) & (be_ref[jnp.minimum(k, nb - 1)] == e),
                               lambda k: k + 1, i + 1)
            prepare(jnp.where(k < n_used, be_ref[jnp.minimum(k, nb - 1)], -1))

        def ahead(_, c):
            finish_one(1 - st[0])
            return c

        lax.fori_loop(0, PIECES_PER_BLOCK, ahead, 0)

        slot = st[0]
        lo, hi = _unpack_halves(xs_ref[...])
        lo = lo.astype(BF16)
        hi = hi.astype(BF16)
        d2 = lo.shape[1]
        g = _dot(lo, wgb[slot, :d2]) + _dot(hi, wgb[slot, d2:])
        u = _dot(lo, wub[slot, :d2]) + _dot(hi, wub[slot, d2:])
        hmid = (g * jax.nn.sigmoid(g) * u).astype(BF16)
        ys_ref[...] = _pack_halves(_dot(hmid, wdb[slot]))

    @pl.when(i >= n_used)
    def _():
        ys_ref[...] = jnp.zeros(ys_ref.shape, ys_ref.dtype)


def _moe_ffn(xs, w_gate, w_up, w_down, block_expert, n_used):
    rows, d2 = xs.shape
    nb = rows // MOE_BLOCK
    d = 2 * d2
    ff = w_gate.shape[2]
    grid_spec = pltpu.PrefetchScalarGridSpec(
        num_scalar_prefetch=2,
        grid=(nb,),
        in_specs=[pl.BlockSpec((MOE_BLOCK, d2), lambda i, be, nu: (i, 0)),
                  pl.BlockSpec(memory_space=pl.ANY),
                  pl.BlockSpec(memory_space=pl.ANY),
                  pl.BlockSpec(memory_space=pl.ANY)],
        out_specs=pl.BlockSpec((MOE_BLOCK, d2), lambda i, be, nu: (i, 0)),
        scratch_shapes=[pltpu.VMEM((2, d, ff), BF16), pltpu.VMEM((2, d, ff), BF16),
                        pltpu.VMEM((2, ff, d), BF16),
                        pltpu.VMEM((PIECE_RING, d // MAT_PIECES, ff), F32),
                        pltpu.VMEM((PIECE_RING, ff // MAT_PIECES, d), F32),
                        pltpu.SemaphoreType.DMA((PIECE_RING,)),
                        pltpu.SMEM((4,), I32)],
    )
    return pl.pallas_call(
        _moe_ffn_kernel,
        grid_spec=grid_spec,
        out_shape=jax.ShapeDtypeStruct((rows, d2), I32),
        compiler_params=_cparams(("arbitrary",)),
        name="moe_ffn",
    )(block_expert, n_used, xs, w_gate, w_up, w_down)


def _moe_combine_kernel(p1_ref, p2_ref, q1_ref, q2_ref, x_ref, g1_ref, g2_ref, g_ref, b_ref, ys_hbm,
                        o_ref, yb, sem):
    i = pl.program_id(0)
    n = pl.num_programs(0)
    slot = lax.rem(i, 2)
    blk = x_ref.shape[0]

    def row_copy(s, k, r, pos):
        return pltpu.make_async_copy(ys_hbm.at[pl.ds(pos, 1)], yb.at[s, k, pl.ds(r, 1)], sem.at[s])

    def fetch(s, a_ref, b_ref2):
        def body(r, c):
            row_copy(s, 0, r, a_ref[0, 0, r]).start()
            row_copy(s, 1, r, b_ref2[0, 0, r]).start()
            return c
        lax.fori_loop(0, blk, body, 0, unroll=8)

    @pl.when(i == 0)
    def _():
        fetch(0, p1_ref, p2_ref)

    @pl.when(i + 1 < n)
    def _():
        fetch(1 - slot, q1_ref, q2_ref)

    for k in range(2):
        pltpu.make_async_copy(ys_hbm.at[pl.ds(0, blk)], yb.at[slot, k], sem.at[slot]).wait()
    y1 = jnp.concatenate(_unpack_halves(yb[slot, 0]), axis=1)
    y2 = jnp.concatenate(_unpack_halves(yb[slot, 1]), axis=1)
    y = y1 * g1_ref[...] + y2 * g2_ref[...]
    o_ref[...] = _ln_rows(ALPHA * x_ref[...] + y, g_ref[...], b_ref[...])


def _moe_combine(x2, ys, p1, p2, g1, g2, g, b):
    t, d = x2.shape
    nt = t // MOE_BLOCK
    p1 = p1.reshape(nt, 1, MOE_BLOCK)
    p2 = p2.reshape(nt, 1, MOE_BLOCK)
    pos = pl.BlockSpec((1, 1, MOE_BLOCK), lambda i: (i, 0, 0), memory_space=pltpu.SMEM)
    nxt = pl.BlockSpec((1, 1, MOE_BLOCK), lambda i: (jnp.minimum(i + 1, nt - 1), 0, 0),
                       memory_space=pltpu.SMEM)
    col = pl.BlockSpec((MOE_BLOCK, 1), lambda i: (i, 0))
    vec = pl.BlockSpec((1, d), lambda i: (0, 0))
    return pl.pallas_call(
        _moe_combine_kernel,
        grid=(nt,),
        in_specs=[pos, pos, nxt, nxt, pl.BlockSpec((MOE_BLOCK, d), lambda i: (i, 0)), col, col, vec, vec,
                  pl.BlockSpec(memory_space=pl.ANY)],
        out_specs=pl.BlockSpec((MOE_BLOCK, d), lambda i: (i, 0)),
        out_shape=jax.ShapeDtypeStruct((t, d), F32),
        scratch_shapes=[pltpu.VMEM((2, 2, MOE_BLOCK, d // 2), I32),
                        pltpu.SemaphoreType.DMA((2,))],
        compiler_params=_cparams(("arbitrary",)),
        name="moe_combine",
    )(p1, p2, p1, p2, x2, g1, g2, g.reshape(1, d), b.reshape(1, d), ys)


def _tile(n, pref):
    return pref if n % pref == 0 else n


def _layer(x, mem, w_in, conv_w, conv_b, conv_ln_g, conv_ln_b, kv_norm_g, w_uk, w_uv, rel_bias,
           conv_out_g, attn_out_g, w_out, ln1_g, ln1_b, w_mq, w_mk, w_mv, w_mo, ln2_g, ln2_b,
           w_router_grp, w_router_exp, w_gate, w_up, w_down, ln3_g, ln3_b):
    bsz, seq, d = x.shape
    t = bsz * seq
    d_conv = conv_w.shape[1]
    d_attn = N_HEADS * HEAD_DIM
    c_glu = 2 * d_conv
    c_qi = H_IDX * D_IDX
    o_q, o_kv = c_glu, c_glu + d_attn
    o_qi = o_kv + KV_RANK
    o_ki = o_qi + c_qi
    topk = min(TOPK_MAX, seq // 4)

    xf = x.reshape(t, d)
    xb = xf.astype(BF16)
    w_inb = w_in.astype(BF16)
    tail_w = jnp.concatenate([w_inb[:, o_kv:o_qi], w_inb[:, o_ki:]], axis=1)
    tail_w = jnp.pad(tail_w, ((0, 0), (0, KV_RANK + LANES - tail_w.shape[1])))
    tm = _tile(t, 1024)

    u = _matmul_cols(xb, w_inb, c_glu, BF16, tm, 512, "proj_glu")
    q_hm = _matmul_heads(xb, w_inb, o_q, d_attn, BF16, tm, 512, "proj_q")
    qi_hm = _matmul_heads(xb, w_inb, o_qi, c_qi, BF16, tm, 512, "proj_qidx")
    tail = _matmul(xb, tail_w, F32, tm, KV_RANK + LANES, "proj_tail")

    conv_n = _conformer_conv(u, bsz, seq, conv_w, conv_b, conv_ln_g, conv_ln_b, conv_out_g)
    tk = min(ATT_BLOCK, seq)
    ckv_n, ckv_t, kia, kib, kw = _prep_latent(tail, kv_norm_g, tm, tk)
    ckv_t = ckv_t.reshape(bsz, seq // tk, KV_RANK, tk)
    mask = _indexer_mask(qi_hm, kw, kia, kib, bsz, seq, topk)
    w_uk_t = w_uk.transpose(0, 2, 1).reshape(d_attn, KV_RANK).astype(BF16)
    w_uv_all = w_uv.transpose(1, 0, 2).reshape(KV_RANK, d_attn).astype(BF16)
    k_t = _key_heads(w_uk_t, ckv_t)
    v_hm = _matmul_value_heads(ckv_n, w_uv_all, tm, 512, "value_heads")
    attn = _head_attention(q_hm, k_t, v_hm, mask, rel_bias, bsz, seq)

    pre1 = _matmul2_residual(conv_n, attn, attn_out_g, w_out.astype(BF16), xf, tm, 512, "out_proj")

    mem_len = mem.shape[1]
    memb = mem.reshape(bsz * mem_len, d).astype(BF16)
    w_kv = jnp.concatenate([w_mk, w_mv], axis=1).astype(BF16)
    kv = _matmul(memb, w_kv, BF16, _tile(bsz * mem_len, 512), 512, "mem_kv")
    w_router = jnp.concatenate([w_router_grp, w_router_exp], axis=1)
    w_router = jnp.pad(w_router, ((0, 0), (0, LANES - w_router.shape[1]))).astype(BF16)
    x2, logits = _memory_attention(pre1, ln1_g, ln1_b, kv, w_mq.astype(BF16), w_mo.astype(BF16),
                                   ln2_g, ln2_b, w_router, bsz, seq, 256)

    e1, e2, g1, g2 = _router(logits, _tile(t, 1024))
    nb = (2 * t + N_EXPERTS * (MOE_BLOCK - 1) + MOE_BLOCK - 1) // MOE_BLOCK
    r1, r2, cnt = _moe_rank(e1, e2, 512)
    p1, p2, block_expert, n_used = _moe_place(e1, e2, r1, r2, cnt, nb, 512)
    xs = _moe_dispatch(x2, p1, p2, n_used, nb)
    ys = _moe_ffn(xs, w_gate, w_up, w_down, block_expert.reshape(nb), n_used[0, 0:1])
    x3 = _moe_combine(x2, ys, p1, p2, g1, g2, ln3_g, ln3_b)
    return x3.reshape(bsz, seq, d)


def kernel(x, mem, w_in, conv_w, conv_b, conv_ln_g, conv_ln_b, kv_norm_g, w_uk, w_uv, rel_bias, conv_out_g, attn_out_g, w_out, ln1_g, ln1_b, w_mq, w_mk, w_mv, w_mo, ln2_g, ln2_b, w_router_grp, w_router_exp, w_gate, w_up, w_down, ln3_g, ln3_b):
    for l in range(w_in.shape[0]):
        x = _layer(x, mem, w_in[l], conv_w[l], conv_b[l], conv_ln_g[l], conv_ln_b[l], kv_norm_g[l],
                   w_uk[l], w_uv[l], rel_bias, conv_out_g[l], attn_out_g[l], w_out[l], ln1_g[l], ln1_b[l],
                   w_mq[l], w_mk[l], w_mv[l], w_mo[l], ln2_g[l], ln2_b[l], w_router_grp[l],
                   w_router_exp[l], w_gate[l], w_up[l], w_down[l], ln3_g[l], ln3_b[l])
    return x
```

```python
import functools
import math

import jax
import jax.numpy as jnp
from jax import lax
from jax.experimental import pallas as pl
from jax.experimental.pallas import tpu as pltpu

F32 = jnp.float32
BF16 = jnp.bfloat16
I32 = jnp.int32

DEPTH = 1
CONV_WIDTH = 31
N_HEADS = 16
HEAD_DIM = 128
KV_RANK = 512
H_IDX = 32
D_IDX = 64
TOPK_MAX = 256
N_BUCKETS = 32
MAX_DIST = 128
MEM_HEADS = 4
MEM_HEAD_DIM = 128
N_GROUPS = 8
EXP_PER_GROUP = 8
N_EXPERTS = N_GROUPS * EXP_PER_GROUP
MOE_BLOCK = 128
ALPHA = (2.0 * DEPTH) ** 0.25
LN_EPS = 1e-5

LANES = 128
SUBLANES = 8
V7X_VMEM_BYTES = 64 * 1024 * 1024
VMEM_LIMIT = 56 * 1024 * 1024
NEG = -1e30
INT_MIN = -(2 ** 31)

LOG2E = 1.4426950408889634

ATT_BLOCK = 256
ATT_HEAD_GROUP = 8
CONV_TS = 256
CONV_HALO = 32
CONV_CC = 256
CONV_RC = 32


def _cparams(sem):
    return pltpu.CompilerParams(dimension_semantics=sem, vmem_limit_bytes=VMEM_LIMIT)


def _dot(a, b):
    return jnp.dot(a, b, preferred_element_type=F32)


def _dot_nt(a, b):
    return lax.dot_general(a, b, (((1,), (1,)), ((), ())), preferred_element_type=F32)


def _mm_kernel(a_ref, b_ref, o_ref):
    o_ref[...] = _dot(a_ref[...], b_ref[...]).astype(o_ref.dtype)


def _matmul(a, b, out_dtype, tm, tn, name):
    m, k = a.shape
    n = b.shape[1]
    return pl.pallas_call(
        _mm_kernel,
        grid=(m // tm, n // tn),
        in_specs=[pl.BlockSpec((tm, k), lambda i, j: (i, 0)),
                  pl.BlockSpec((k, tn), lambda i, j: (0, j))],
        out_specs=pl.BlockSpec((tm, tn), lambda i, j: (i, j)),
        out_shape=jax.ShapeDtypeStruct((m, n), out_dtype),
        compiler_params=_cparams(("parallel", "parallel")),
        name=name,
    )(a, b)


def _matmul_cols(a, b, n, out_dtype, tm, tn, name):
    m, k = a.shape
    return pl.pallas_call(
        _mm_kernel,
        grid=(m // tm, n // tn),
        in_specs=[pl.BlockSpec((tm, k), lambda i, j: (i, 0)),
                  pl.BlockSpec((k, tn), lambda i, j: (0, j))],
        out_specs=pl.BlockSpec((tm, tn), lambda i, j: (i, j)),
        out_shape=jax.ShapeDtypeStruct((m, n), out_dtype),
        compiler_params=_cparams(("parallel", "parallel")),
        name=name,
    )(a, b)


def _mm_heads_kernel(a_ref, b_ref, o_ref):
    r = _dot(a_ref[...], b_ref[...])
    for p in range(o_ref.shape[0]):
        o_ref[p] = r[:, p * LANES:(p + 1) * LANES].astype(o_ref.dtype)


def _matmul_heads(a, b, col0, n, out_dtype, tm, tn, name):
    m, k = a.shape
    assert col0 % tn == 0
    return pl.pallas_call(
        _mm_heads_kernel,
        grid=(m // tm, n // tn),
        in_specs=[pl.BlockSpec((tm, k), lambda i, j: (i, 0)),
                  pl.BlockSpec((k, tn), lambda i, j: (0, col0 // tn + j))],
        out_specs=pl.BlockSpec((tn // LANES, tm, LANES), lambda i, j: (j, i, 0)),
        out_shape=jax.ShapeDtypeStruct((n // LANES, m, LANES), out_dtype),
        compiler_params=_cparams(("parallel", "parallel")),
        name=name,
    )(a, b)


def _mm_value_heads_kernel(a_ref, b_ref, o_ref):
    r = _dot(a_ref[...], b_ref[...])
    tm = r.shape[0]
    for p in range(o_ref.shape[0]):
        o_ref[p, :, 0:LANES] = r[:, p * LANES:(p + 1) * LANES].astype(o_ref.dtype)
        o_ref[p, :, LANES:2 * LANES] = jnp.ones((tm, LANES), o_ref.dtype)


def _matmul_value_heads(a, b, tm, tn, name):
    m, k = a.shape
    n = b.shape[1]
    return pl.pallas_call(
        _mm_value_heads_kernel,
        grid=(m // tm, n // tn),
        in_specs=[pl.BlockSpec((tm, k), lambda i, j: (i, 0)),
                  pl.BlockSpec((k, tn), lambda i, j: (0, j))],
        out_specs=pl.BlockSpec((tn // LANES, tm, 2 * LANES), lambda i, j: (j, i, 0)),
        out_shape=jax.ShapeDtypeStruct((n // LANES, m, 2 * LANES), BF16),
        compiler_params=_cparams(("parallel", "parallel")),
        name=name,
    )(a, b)


def _mm2_res_kernel(a1_ref, a2_ref, g2_ref, w1_ref, w2_ref, r_ref, o_ref, a2n):
    @pl.when(pl.program_id(1) == 0)
    def _():
        x = a2_ref[...].astype(F32)
        ms = jnp.mean(x * x, axis=-1, keepdims=True)
        a2n[...] = (x * lax.rsqrt(ms + LN_EPS) * g2_ref[...]).astype(a2n.dtype)

    o_ref[...] = (ALPHA * r_ref[...] + _dot(a1_ref[...], w1_ref[...])
                  + _dot(a2n[...], w2_ref[...]))


def _matmul2_residual(a1, a2, g2, w, res, tm, tn, name):
    m, k = a1.shape
    assert a2.shape == a1.shape and w.shape[0] == 2 * k
    n = w.shape[1]
    return pl.pallas_call(
        _mm2_res_kernel,
        grid=(m // tm, n // tn),
        in_specs=[pl.BlockSpec((tm, k), lambda i, j: (i, 0)),
                  pl.BlockSpec((tm, k), lambda i, j: (i, 0)),
                  pl.BlockSpec((1, k), lambda i, j: (0, 0)),
                  pl.BlockSpec((k, tn), lambda i, j: (0, j)),
                  pl.BlockSpec((k, tn), lambda i, j: (1, j)),
                  pl.BlockSpec((tm, tn), lambda i, j: (i, j))],
        out_specs=pl.BlockSpec((tm, tn), lambda i, j: (i, j)),
        out_shape=jax.ShapeDtypeStruct((m, n), F32),
        scratch_shapes=[pltpu.VMEM((tm, k), BF16)],
        compiler_params=_cparams(("parallel", "arbitrary")),
        name=name,
    )(a1, a2, g2.reshape(1, k), w, w, res)


def _ln_rows(x, g, b):
    mu = jnp.mean(x, axis=-1, keepdims=True)
    xc = x - mu
    var = jnp.mean(xc * xc, axis=-1, keepdims=True)
    return xc * lax.rsqrt(var + LN_EPS) * g + b


def _conv_kernel(a_ref, g_ref, cw_ref, cb_ref, lg_ref, lb_ref, og_ref, o_ref, hbuf, ybuf, hs):
    ts = a_ref.shape[0]
    nch = hbuf.shape[0]
    cc = hbuf.shape[2]
    d_conv = nch * cc

    nrow = CONV_HALO + ts

    @pl.when(pl.program_id(1) == 0)
    def _():
        hbuf[:, 0:CONV_HALO, :] = jnp.zeros((nch, CONV_HALO, cc), F32)
        hbuf[:, nrow:nrow + SUBLANES, :] = jnp.zeros((nch, SUBLANES, cc), F32)

    for c in range(nch):
        a = a_ref[:, c * cc:(c + 1) * cc].astype(F32)
        g = g_ref[:, c * cc:(c + 1) * cc].astype(F32)
        hbuf[c, CONV_HALO:nrow, :] = a * jax.nn.sigmoid(g)

    first = CONV_HALO - (CONV_WIDTH - 1)

    def chunk_body(c, carry):
        for o in range(1, SUBLANES):
            hs[o - 1] = hbuf[c, o:o + nrow, :]
        for r0 in range(0, ts, CONV_RC):
            acc = jnp.zeros((CONV_RC, cc), F32)
            for j in range(CONV_WIDTH):
                o = (first + j) % SUBLANES
                base = r0 + first + j - o
                rows = hbuf[c, base:base + CONV_RC, :] if o == 0 else hs[o - 1, base:base + CONV_RC, :]
                acc = acc + cw_ref[c, j:j + 1, :] * rows
            ybuf[c, r0:r0 + CONV_RC, :] = acc + cb_ref[c]
        hbuf[c, 0:CONV_HALO, :] = hbuf[c, ts:nrow, :]
        return carry

    lax.fori_loop(0, nch, chunk_body, 0)

    s1 = jnp.zeros((ts, 1), F32)
    for c in range(nch):
        s1 = s1 + jnp.sum(ybuf[c], axis=1, keepdims=True)
    mu = s1 * (1.0 / d_conv)
    s2 = jnp.zeros((ts, 1), F32)
    for c in range(nch):
        yc = ybuf[c] - mu
        s2 = s2 + jnp.sum(yc * yc, axis=1, keepdims=True)
    rstd = lax.rsqrt(s2 * (1.0 / d_conv) + LN_EPS)
    s3 = jnp.zeros((ts, 1), F32)
    for c in range(nch):
        z = (ybuf[c] - mu) * rstd * lg_ref[:, c * cc:(c + 1) * cc] + lb_ref[:, c * cc:(c + 1) * cc]
        z = z * jax.nn.sigmoid(z)
        ybuf[c] = z
        s3 = s3 + jnp.sum(z * z, axis=1, keepdims=True)
    rr = lax.rsqrt(s3 * (1.0 / d_conv) + LN_EPS)
    for c in range(nch):
        o_ref[:, c * cc:(c + 1) * cc] = (ybuf[c] * rr * og_ref[:, c * cc:(c + 1) * cc]).astype(o_ref.dtype)


def _conformer_conv(u, bsz, seq, conv_w, conv_b, ln_g, ln_b, out_g):
    d_conv = u.shape[1] // 2
    ts = min(CONV_TS, seq)
    nch = d_conv // CONV_CC
    nt = seq // ts
    cw = conv_w.reshape(CONV_WIDTH, nch, CONV_CC).transpose(1, 0, 2)
    cb = conv_b.reshape(nch, 1, CONV_CC)
    vec = pl.BlockSpec((1, d_conv), lambda b, i: (0, 0))
    return pl.pallas_call(
        _conv_kernel,
        grid=(bsz, nt),
        in_specs=[pl.BlockSpec((ts, d_conv), lambda b, i: (b * nt + i, 0)),
                  pl.BlockSpec((ts, d_conv), lambda b, i: (b * nt + i, 1)),
                  pl.BlockSpec((nch, CONV_WIDTH, CONV_CC), lambda b, i: (0, 0, 0)),
                  pl.BlockSpec((nch, 1, CONV_CC), lambda b, i: (0, 0, 0)),
                  vec, vec, vec],
        out_specs=pl.BlockSpec((ts, d_conv), lambda b, i: (b * nt + i, 0)),
        out_shape=jax.ShapeDtypeStruct((bsz * seq, d_conv), BF16),
        scratch_shapes=[pltpu.VMEM((nch, CONV_HALO + ts + SUBLANES, CONV_CC), F32),
                        pltpu.VMEM((nch, ts, CONV_CC), F32),
                        pltpu.VMEM((SUBLANES - 1, CONV_HALO + ts, CONV_CC), F32)],
        compiler_params=_cparams(("arbitrary", "arbitrary")),
        name="conformer_conv",
    )(u, u, cw, cb, ln_g.reshape(1, d_conv), ln_b.reshape(1, d_conv), out_g.reshape(1, d_conv))


def _prep_kernel(t_ref, g_ref, ckv_ref, ckvt_ref, kia_ref, kib_ref, kw_ref):
    ckv = t_ref[:, 0:KV_RANK]
    ms = jnp.mean(ckv * ckv, axis=-1, keepdims=True)
    ckv_n = ckv * lax.rsqrt(ms + LN_EPS) * g_ref[...]
    ckv_ref[...] = ckv_n.astype(ckv_ref.dtype)
    tk = ckvt_ref.shape[3]
    for c in range(ckvt_ref.shape[1]):
        ckvt_ref[0, c] = ckv_n[c * tk:(c + 1) * tk, :].T.astype(ckvt_ref.dtype)
    kw = t_ref[:, KV_RANK:KV_RANK + LANES]
    kw_ref[...] = kw
    lane = lax.broadcasted_iota(I32, kw.shape, 1)
    kia_ref[...] = jnp.where(lane < D_IDX, kw, 0.0).astype(kia_ref.dtype)
    kib_ref[...] = jnp.where(lane >= D_IDX, pltpu.roll(kw, D_IDX, 1), 0.0).astype(kib_ref.dtype)


def _prep_latent(tail, kv_norm_g, tm, tk):
    m, w = tail.shape
    return pl.pallas_call(
        _prep_kernel,
        grid=(m // tm,),
        in_specs=[pl.BlockSpec((tm, w), lambda i: (i, 0)),
                  pl.BlockSpec((1, KV_RANK), lambda i: (0, 0))],
        out_specs=[pl.BlockSpec((tm, KV_RANK), lambda i: (i, 0)),
                   pl.BlockSpec((1, tm // tk, KV_RANK, tk), lambda i: (i, 0, 0, 0)),
                   pl.BlockSpec((tm, LANES), lambda i: (i, 0)),
                   pl.BlockSpec((tm, LANES), lambda i: (i, 0)),
                   pl.BlockSpec((tm, LANES), lambda i: (i, 0))],
        out_shape=[jax.ShapeDtypeStruct((m, KV_RANK), BF16),
                   jax.ShapeDtypeStruct((m // tm, tm // tk, KV_RANK, tk), BF16),
                   jax.ShapeDtypeStruct((m, LANES), BF16),
                   jax.ShapeDtypeStruct((m, LANES), BF16),
                   jax.ShapeDtypeStruct((m, LANES), F32)],
        compiler_params=_cparams(("parallel",)),
        name="prep_latent",
    )(tail, kv_norm_g.reshape(1, KV_RANK))


def _indexer_t_kernel(qi_ref, kw_ref, kia_ref, kib_ref, o_ref, keybuf, *, topk):
    i = pl.program_id(1)
    npairs, tq, _ = qi_ref.shape
    nk = o_ref.shape[1]
    tk = o_ref.shape[3]
    kf = float(topk)
    group = 4
    slabs = tk // SUBLANES

    w_t = kw_ref[...].T
    key_idx = lax.broadcasted_iota(I32, (tk, tq), 0)
    qry_idx = lax.broadcasted_iota(I32, (tk, tq), 1) + i * tq

    def score_chunk(j, carry):
        k0 = pl.multiple_of(j * tk, tk)
        kd = jnp.concatenate([kia_ref[pl.ds(k0, tk), :], kib_ref[pl.ds(k0, tk), :]], axis=0)
        acc = jnp.zeros((tk, tq), F32)
        for p0 in range(0, npairs, group):
            rhs = qi_ref[p0:p0 + group].reshape(group * tq, LANES)
            zz = _dot_nt(kd, rhs)
            for p in range(group):
                h = 2 * (p0 + p)
                z = zz[:, p * tq:(p + 1) * tq]
                acc = (acc + w_t[D_IDX + h:D_IDX + h + 1, :] * jnp.maximum(z[0:tk], 0.0)
                       + w_t[D_IDX + h + 1:D_IDX + h + 2, :] * jnp.maximum(z[tk:2 * tk], 0.0))
        bits = pltpu.bitcast(acc, I32)
        key = jnp.where(bits >= 0, bits, bits ^ jnp.int32(0x7FFFFFFF))
        keybuf[j] = jnp.where(key_idx + j * tk <= qry_idx, key, INT_MIN)
        return carry

    lax.fori_loop(0, i + 1, score_chunk, 0)

    def count(pred):
        def one(j, c):
            hit = jnp.where(pred(keybuf[j], key_idx + j * tk), 1.0, 0.0)
            return c + jnp.sum(hit.reshape(slabs, SUBLANES, tq), axis=0)

        def pair(p, c):
            return one(2 * p + 1, one(2 * p, c))

        c = lax.fori_loop(0, lax.shift_right_logical(i + 1, 1), pair, jnp.zeros((SUBLANES, tq), F32))
        c = lax.cond((i & 1) == 0, lambda c: one(i, c), lambda c: c, c)
        return jnp.broadcast_to(jnp.sum(c, axis=0, keepdims=True), (SUBLANES, tq))

    def tile_rows(v):
        return jnp.concatenate([v] * slabs, axis=0)

    def count_ge(cand):
        cb = tile_rows(cand)
        return count(lambda kk, idx: kk >= cb)

    tau = jnp.where(count_ge(jnp.zeros((SUBLANES, tq), I32)) >= kf, 0, INT_MIN).astype(I32)

    def bit_body(it, tau):
        cand = tau | jnp.left_shift(jnp.int32(1), 30 - it)
        return jnp.where(count_ge(cand) >= kf, cand, tau)

    tau = lax.fori_loop(0, 31, bit_body, tau)
    n_ge = count_ge(tau)
    taub = tile_rows(tau)

    def tie_cut():
        need = kf - count(lambda kk, idx: kk > taub)

        def cut_body(it, cut):
            cand = cut + jnp.left_shift(jnp.int32(1), 30 - it)
            cb = tile_rows(cand)
            below = count(lambda kk, idx: (kk == taub) & (idx < cb))
            return jnp.where(below < need, cand, cut)

        return lax.fori_loop(0, 31, cut_body, jnp.zeros((SUBLANES, tq), I32))

    has_ties = jnp.max(n_ge) > kf
    cut = lax.cond(has_ties, tie_cut, lambda: jnp.full((SUBLANES, tq), 2 ** 30, I32))
    cutb = tile_rows(cut)
    eye = jnp.where(lax.broadcasted_iota(I32, (tq, tq), 0) == lax.broadcasted_iota(I32, (tq, tq), 1),
                    1.0, 0.0).astype(BF16)

    def write_chunk(j, carry):
        kk = keybuf[j]
        sel = ((kk > taub) | ((kk == taub) & (key_idx + j * tk <= cutb))) & (kk != INT_MIN)
        sel_qk = _dot_nt(eye, jnp.where(sel, 1.0, 0.0).astype(BF16))
        o_ref[0, j] = jnp.where(sel_qk > 0.5, 0.0, NEG).astype(o_ref.dtype)
        return carry

    lax.fori_loop(0, i + 1, write_chunk, 0)

    def write_rest(j, carry):
        o_ref[0, j] = jnp.full((tq, tk), NEG, o_ref.dtype)
        return carry

    lax.fori_loop(i + 1, nk, write_rest, 0)


def _indexer_mask(qi_hm, kw, kia, kib, bsz, seq, topk):
    tq = min(ATT_BLOCK, seq)
    nq = seq // tq
    npairs = qi_hm.shape[0]
    return pl.pallas_call(
        functools.partial(_indexer_t_kernel, topk=topk),
        grid=(bsz, nq),
        in_specs=[pl.BlockSpec((npairs, tq, LANES), lambda b, i: (0, b * nq + i, 0)),
                  pl.BlockSpec((tq, LANES), lambda b, i: (b * nq + i, 0)),
                  pl.BlockSpec((seq, LANES), lambda b, i: (b, 0)),
                  pl.BlockSpec((seq, LANES), lambda b, i: (b, 0))],
        out_specs=pl.BlockSpec((1, nq, tq, tq), lambda b, i: (b * nq + i, 0, 0, 0)),
        out_shape=jax.ShapeDtypeStruct((bsz * nq, nq, tq, tq), BF16),
        scratch_shapes=[pltpu.VMEM((nq, tq, tq), I32)],
        compiler_params=_cparams(("parallel", "parallel")),
        name="indexer_mask",
    )(qi_hm, kw, kia, kib)


def _t5_bucket(dist):
    n = jnp.maximum(dist, 0)
    max_exact = N_BUCKETS // 2
    nf = jnp.maximum(n, 1).astype(F32)
    large = max_exact + (jnp.log(nf / max_exact) / math.log(MAX_DIST / max_exact)
                         * (N_BUCKETS - max_exact)).astype(I32)
    large = jnp.minimum(large, N_BUCKETS - 1)
    return jnp.where(n < max_exact, n, large)


def _key_heads_kernel(wt_ref, ct_ref, o_ref):
    r = _dot(wt_ref[...], ct_ref[0, 0]) * (HEAD_DIM ** -0.5 * LOG2E)
    for h in range(o_ref.shape[1]):
        o_ref[0, h, 0] = r[h * HEAD_DIM:(h + 1) * HEAD_DIM].astype(o_ref.dtype)


def _key_heads(w_uk_t, ckv_t):
    bsz, nk, r_lat, tk = ckv_t.shape
    nh = w_uk_t.shape[0] // HEAD_DIM
    return pl.pallas_call(
        _key_heads_kernel,
        grid=(bsz, nk),
        in_specs=[pl.BlockSpec((nh * HEAD_DIM, r_lat), lambda b, j: (0, 0)),
                  pl.BlockSpec((1, 1, r_lat, tk), lambda b, j: (b, j, 0, 0))],
        out_specs=pl.BlockSpec((1, nh, 1, HEAD_DIM, tk), lambda b, j: (b, 0, j, 0, 0)),
        out_shape=jax.ShapeDtypeStruct((bsz, nh, nk, HEAD_DIM, tk), BF16),
        compiler_params=_cparams(("parallel", "parallel")),
        name="key_heads",
    )(w_uk_t, ckv_t)


def _attn_kernel(relb_ref, q_ref, kt_ref, v_ref, mask_ref, o_ref, acc, m_s, btab):
    b = pl.program_id(0)
    g = pl.program_id(1)
    i = pl.program_id(2)
    hg, tq, _ = q_ref.shape
    nh = btab.shape[1]
    tk = mask_ref.shape[3]

    @pl.when((b == 0) & (g == 0) & (i == 0))
    def _():
        r = lax.broadcasted_iota(I32, (tq, tk), 0)
        c = lax.broadcasted_iota(I32, (tq, tk), 1)
        for t in range(2):
            bk = _t5_bucket(r - c + t * tk)

            def fill(h, carry):
                far = relb_ref[N_BUCKETS - 1, h]
                v = jnp.zeros((tq, tk), F32)
                for k in range(N_BUCKETS):
                    v = jnp.where(bk == k, (relb_ref[k, h] - far) * LOG2E, v)
                btab[t, h] = v
                return carry

            lax.fori_loop(0, nh, fill, 0)

    acc[...] = jnp.zeros(acc.shape, F32)
    m_s[...] = jnp.full(m_s.shape, NEG, F32)

    def lane_tile(v, n):
        return jnp.concatenate([v] * n, axis=1)

    def chunks(js, near):
        mks = [mask_ref[0, j].astype(F32) for j in js]
        for h in range(hg):
            for j, mk in zip(js, mks):
                s = _dot(q_ref[h], kt_ref[0, h, j]) + mk
                if near:
                    s = s + btab[i - j, g * hg + h]
                m_old = m_s[h]
                m_new = jnp.maximum(m_old, jnp.max(s, axis=1, keepdims=True))
                a = jnp.exp2(m_old - m_new)
                p = jnp.exp2(s - lane_tile(m_new, tk // LANES))
                rows = pl.ds(pl.multiple_of(j * tk, tk), tk)
                acc[h] = lane_tile(a, 2) * acc[h] + _dot(p.astype(BF16), v_ref[h, rows, :])
                m_s[h] = m_new

    n_far = jnp.maximum(i - 1, 0)

    def far_quad(p, carry):
        chunks([4 * p, 4 * p + 1, 4 * p + 2, 4 * p + 3], False)
        return carry

    lax.fori_loop(0, lax.shift_right_logical(n_far, 2), far_quad, 0)
    done = n_far & ~3

    @pl.when((n_far & 2) == 2)
    def _():
        chunks([done, done + 1], False)

    @pl.when((n_far & 1) == 1)
    def _():
        chunks([n_far - 1], False)

    @pl.when(i >= 1)
    def _():
        chunks([i - 1, i], True)

    @pl.when(i == 0)
    def _():
        chunks([i], True)

    for h in range(hg):
        o_ref[:, h * HEAD_DIM:(h + 1) * HEAD_DIM] = (
            acc[h, :, :HEAD_DIM] / acc[h, :, HEAD_DIM:]).astype(o_ref.dtype)


def _head_attention(q_hm, k_t, v_hm, mask, rel_bias, bsz, seq):
    nh = q_hm.shape[0]
    hg = ATT_HEAD_GROUP
    tq = mask.shape[2]
    nq = seq // tq
    assert tq == mask.shape[3] and tq >= MAX_DIST
    once = pl.Buffered(1)
    return pl.pallas_call(
        _attn_kernel,
        grid=(bsz, nh // hg, nq),
        in_specs=[pl.BlockSpec(memory_space=pltpu.SMEM),
                  pl.BlockSpec((hg, tq, HEAD_DIM), lambda b, g, i: (g, b * nq + i, 0)),
                  pl.BlockSpec((1, hg, nq, HEAD_DIM, tq), lambda b, g, i: (b, g, 0, 0, 0), pipeline_mode=once),
                  pl.BlockSpec((hg, seq, 2 * HEAD_DIM), lambda b, g, i: (g, b, 0), pipeline_mode=once),
                  pl.BlockSpec((1, nq, tq, tq), lambda b, g, i: (b * nq + i, 0, 0, 0))],
        out_specs=pl.BlockSpec((tq, hg * HEAD_DIM), lambda b, g, i: (b * nq + i, g)),
        out_shape=jax.ShapeDtypeStruct((bsz * seq, nh * HEAD_DIM), BF16),
        scratch_shapes=[pltpu.VMEM((hg, tq, 2 * HEAD_DIM), F32),
                        pltpu.VMEM((hg, tq, LANES), F32),
                        pltpu.VMEM((2, nh, tq, tq), F32)],
        compiler_params=_cparams(("arbitrary", "arbitrary", "arbitrary")),
        name="head_attention",
    )(rel_bias, q_hm, k_t, v_hm, mask)


def _memattn_kernel(x_ref, g1_ref, b1_ref, kv_ref, wq_ref, wo_ref, g_ref, b_ref, wr_ref, o_ref, lg_ref):
    x = _ln_rows(x_ref[...], g1_ref[...], b1_ref[...])
    d_mem = MEM_HEADS * MEM_HEAD_DIM
    q = (_dot(x.astype(BF16), wq_ref[...]) * (MEM_HEAD_DIM ** -0.5)).astype(BF16)
    outs = []
    for h in range(MEM_HEADS):
        lo = h * MEM_HEAD_DIM
        k = kv_ref[:, lo:lo + MEM_HEAD_DIM]
        v = kv_ref[:, d_mem + lo:d_mem + lo + MEM_HEAD_DIM]
        s = _dot_nt(q[:, lo:lo + MEM_HEAD_DIM], k)
        p = jnp.exp(s - jnp.max(s, axis=1, keepdims=True))
        p = p / jnp.sum(p, axis=1, keepdims=True)
        outs.append(_dot(p.astype(BF16), v).astype(BF16))
    o = jnp.concatenate(outs, axis=1)
    x2 = _ln_rows(ALPHA * x + _dot(o, wo_ref[...]), g_ref[...], b_ref[...])
    o_ref[...] = x2
    lg_ref[...] = _dot(x2.astype(BF16), wr_ref[...])


def _memory_attention(pre1, g1, b1, kv, w_mq, w_mo, g, b, w_router, bsz, seq, tm):
    t, d = pre1.shape
    nt = seq // tm
    mem_len = kv.shape[0] // bsz
    d_mem = w_mq.shape[1]
    return pl.pallas_call(
        _memattn_kernel,
        grid=(bsz, nt),
        in_specs=[pl.BlockSpec((tm, d), lambda bi, i: (bi * nt + i, 0)),
                  pl.BlockSpec((1, d), lambda bi, i: (0, 0)),
                  pl.BlockSpec((1, d), lambda bi, i: (0, 0)),
                  pl.BlockSpec((mem_len, 2 * d_mem), lambda bi, i: (bi, 0)),
                  pl.BlockSpec((d, d_mem), lambda bi, i: (0, 0), pipeline_mode=pl.Buffered(1)),
                  pl.BlockSpec((d_mem, d), lambda bi, i: (0, 0), pipeline_mode=pl.Buffered(1)),
                  pl.BlockSpec((1, d), lambda bi, i: (0, 0)),
                  pl.BlockSpec((1, d), lambda bi, i: (0, 0)),
                  pl.BlockSpec((d, LANES), lambda bi, i: (0, 0), pipeline_mode=pl.Buffered(1))],
        out_specs=[pl.BlockSpec((tm, d), lambda bi, i: (bi * nt + i, 0)),
                   pl.BlockSpec((tm, LANES), lambda bi, i: (bi * nt + i, 0))],
        out_shape=[jax.ShapeDtypeStruct((t, d), F32),
                   jax.ShapeDtypeStruct((t, LANES), F32)],
        compiler_params=_cparams(("parallel", "parallel")),
        name="memory_attention",
    )(pre1, g1.reshape(1, d), b1.reshape(1, d), kv, w_mq, w_mo, g.reshape(1, d), b.reshape(1, d), w_router)


def _router_kernel(lg_ref, e1_ref, e2_ref, g1_ref, g2_ref):
    x = lg_ref[...]
    lane = lax.broadcasted_iota(I32, x.shape, 1)
    lane_f = lane.astype(F32)

    def argmax(mask):
        v = jnp.where(mask, x, -jnp.inf)
        mx = jnp.max(v, axis=1, keepdims=True)
        idx = jnp.min(jnp.where(mask & (v == mx), lane_f, float(LANES)), axis=1, keepdims=True)
        return mx, idx.astype(I32)

    gmask = lane < N_GROUPS
    gmax, gsel = argmax(gmask)
    gsum = jnp.sum(jnp.where(gmask, jnp.exp(x - gmax), 0.0), axis=1, keepdims=True)
    g_p = 1.0 / gsum
    lo = N_GROUPS + gsel * EXP_PER_GROUP
    emask = (lane >= lo) & (lane < lo + EXP_PER_GROUP)
    m1, i1 = argmax(emask)
    m2, i2 = argmax(emask & (lane != i1))
    esum = jnp.sum(jnp.where(emask, jnp.exp(x - m1), 0.0), axis=1, keepdims=True)
    p1 = 1.0 / esum
    p2 = jnp.exp(m2 - m1) / esum
    e1_ref[...] = i1 - N_GROUPS
    e2_ref[...] = i2 - N_GROUPS
    g1_ref[...] = g_p * (p1 / (p1 + p2))
    g2_ref[...] = g_p * (p2 / (p1 + p2))


def _router(logits, tm):
    t = logits.shape[0]
    col = pl.BlockSpec((tm, 1), lambda i: (i, 0))
    return pl.pallas_call(
        _router_kernel,
        grid=(t // tm,),
        in_specs=[pl.BlockSpec((tm, LANES), lambda i: (i, 0))],
        out_specs=[col, col, col, col],
        out_shape=[jax.ShapeDtypeStruct((t, 1), I32), jax.ShapeDtypeStruct((t, 1), I32),
                   jax.ShapeDtypeStruct((t, 1), F32), jax.ShapeDtypeStruct((t, 1), F32)],
        compiler_params=_cparams(("parallel",)),
        name="moe_router",
    )(logits)


HI16 = -65536


def _pack_halves(x):
    d = x.shape[1] // 2
    lo = pltpu.bitcast(x[:, :d].astype(BF16).astype(F32), I32)
    hi = pltpu.bitcast(x[:, d:].astype(BF16).astype(F32), I32)
    return lax.shift_right_logical(lo, 16) | (hi & HI16)


def _unpack_halves(u):
    return pltpu.bitcast(lax.shift_left(u, 16), F32), pltpu.bitcast(u & HI16, F32)


def _moe_rank_kernel(e1_ref, e2_ref, r1_ref, r2_ref, cnt_ref, base):
    i = pl.program_id(0)
    tm = e1_ref.shape[0]

    @pl.when(i == 0)
    def _():
        base[...] = jnp.zeros(base.shape, F32)

    lane = lax.broadcasted_iota(I32, (tm, LANES), 1)
    rr = lax.broadcasted_iota(I32, (tm, tm), 0)
    cc = lax.broadcasted_iota(I32, (tm, tm), 1)
    earlier = jnp.where(cc < rr, 1.0, 0.0).astype(BF16)
    for slot, (e_ref, r_ref) in enumerate(((e1_ref, r1_ref), (e2_ref, r2_ref))):
        oh = jnp.where(lane == e_ref[...], 1.0, 0.0)
        before = _dot(earlier, oh.astype(BF16)) + base[slot:slot + 1, :]
        r_ref[...] = jnp.sum(oh * before, axis=1, keepdims=True).astype(I32)
        base[slot:slot + 1, :] = base[slot:slot + 1, :] + jnp.sum(oh, axis=0, keepdims=True)
    cnt_ref[...] = base[...]


def _moe_rank(e1, e2, tm):
    t = e1.shape[0]
    col = pl.BlockSpec((tm, 1), lambda i: (i, 0))
    return pl.pallas_call(
        _moe_rank_kernel,
        grid=(t // tm,),
        in_specs=[col, col],
        out_specs=[col, col, pl.BlockSpec((8, LANES), lambda i: (0, 0))],
        out_shape=[jax.ShapeDtypeStruct((t, 1), I32), jax.ShapeDtypeStruct((t, 1), I32),
                   jax.ShapeDtypeStruct((8, LANES), F32)],
        scratch_shapes=[pltpu.VMEM((8, LANES), F32)],
        compiler_params=_cparams(("arbitrary",)),
        name="moe_rank",
    )(e1, e2)


def _moe_place_kernel(e1_ref, e2_ref, r1_ref, r2_ref, cnt_ref, p1_ref, p2_ref, be_ref, nu_ref):
    tm = e1_ref.shape[0]
    nbp = be_ref.shape[0]
    lane8 = lax.broadcasted_iota(I32, (8, LANES), 1)
    cnt = cnt_ref[...].astype(I32)
    c0 = jnp.broadcast_to(cnt[0:1], (8, LANES))
    c1 = jnp.broadcast_to(cnt[1:2], (8, LANES))
    blk_shift = MOE_BLOCK.bit_length() - 1
    padded = lax.shift_left(lax.shift_right_logical(c0 + c1 + (MOE_BLOCK - 1), blk_shift), blk_shift)
    pad_end = padded
    s = 1
    while s < LANES:
        pad_end = pad_end + jnp.where(lane8 >= s, pltpu.roll(pad_end, s, 1), 0)
        s *= 2
    start0 = (pad_end - padded).astype(F32)
    start1 = (pad_end - padded + c0).astype(F32)

    lane = lax.broadcasted_iota(I32, (tm, LANES), 1)
    for e_ref, r_ref, p_ref, start in ((e1_ref, r1_ref, p1_ref, start0), (e2_ref, r2_ref, p2_ref, start1)):
        seg = jnp.sum(jnp.where(lane == e_ref[...], start[0:1], 0.0), axis=1, keepdims=True)
        p_ref[...] = seg.astype(I32) + r_ref[...]

    block_row = lax.broadcasted_iota(I32, (nbp, LANES), 0) * MOE_BLOCK
    lane_b = lax.broadcasted_iota(I32, (nbp, LANES), 1)
    ended = (pad_end[0:1] <= block_row) & (lane_b < N_EXPERTS)
    be = jnp.sum(jnp.where(ended, 1.0, 0.0), axis=1, keepdims=True)
    be_ref[...] = jnp.minimum(be, N_EXPERTS - 1.0).astype(I32)
    total = jnp.max(pad_end, axis=1, keepdims=True)
    sub = lax.broadcasted_iota(I32, (8, LANES), 0)
    n_used = jnp.broadcast_to(lax.shift_right_logical(total, blk_shift), (8, LANES))
    nu_ref[...] = jnp.where(sub == 0, n_used, jnp.where(sub == 1, pad_end, jnp.where(sub == 2, padded, 0)))


def _moe_place(e1, e2, r1, r2, cnt, nb, tm):
    t = e1.shape[0]
    col = pl.BlockSpec((tm, 1), lambda i: (i, 0))
    return pl.pallas_call(
        _moe_place_kernel,
        grid=(t // tm,),
        in_specs=[col, col, col, col, pl.BlockSpec((8, LANES), lambda i: (0, 0))],
        out_specs=[col, col, pl.BlockSpec((nb, 1), lambda i: (0, 0)),
                   pl.BlockSpec((8, LANES), lambda i: (0, 0))],
        out_shape=[jax.ShapeDtypeStruct((t, 1), I32), jax.ShapeDtypeStruct((t, 1), I32),
                   jax.ShapeDtypeStruct((nb, 1), I32), jax.ShapeDtypeStruct((8, LANES), I32)],
        compiler_params=_cparams(("arbitrary",)),
        name="moe_place",
    )(e1, e2, r1, r2, cnt)


def _moe_dispatch_kernel(seg_ref, p1_ref, p2_ref, x_ref, xs_hbm, pk, zbuf, sem, zsem):
    i = pl.program_id(0)
    n = pl.num_programs(0)
    slot = lax.rem(i, 2)
    blk = x_ref.shape[0]
    nb = xs_hbm.shape[0] // blk

    def row_copy(s, r, pos):
        return pltpu.make_async_copy(pk.at[s, pl.ds(r, 1)], xs_hbm.at[pl.ds(pos, 1)], sem.at[s])

    def drain(s):
        for _ in range(2):
            pltpu.make_async_copy(pk.at[s], xs_hbm.at[pl.ds(0, blk)], sem.at[s]).wait()

    def zero_copy(row0):
        return pltpu.make_async_copy(zbuf, xs_hbm.at[pl.ds(pl.multiple_of(row0, blk), blk)], zsem)

    @pl.when(i == 0)
    def _():
        zbuf[...] = jnp.zeros(zbuf.shape, zbuf.dtype)
        n_used = seg_ref[0, 0]

        def seg_start(e, c):
            @pl.when(seg_ref[2, e] > 0)
            def _():
                zero_copy(seg_ref[1, e] - blk).start()
            return c

        def tail_start(b, c):
            zero_copy(b * blk).start()
            return c

        def seg_wait(e, c):
            @pl.when(seg_ref[2, e] > 0)
            def _():
                zero_copy(0).wait()
            return c

        def tail_wait(b, c):
            zero_copy(0).wait()
            return c

        lax.fori_loop(0, N_EXPERTS, seg_start, 0)
        lax.fori_loop(n_used, nb, tail_start, 0)
        lax.fori_loop(0, N_EXPERTS, seg_wait, 0)
        lax.fori_loop(n_used, nb, tail_wait, 0)

    @pl.when(i >= 2)
    def _():
        drain(slot)

    pk[slot] = _pack_halves(x_ref[...])

    def issue(r, c):
        row_copy(slot, r, p1_ref[0, 0, r]).start()
        row_copy(slot, r, p2_ref[0, 0, r]).start()
        return c

    lax.fori_loop(0, blk, issue, 0, unroll=8)

    @pl.when(i == n - 1)
    def _():
        drain(slot)

    @pl.when((i == n - 1) & (i >= 1))
    def _():
        drain(1 - slot)


def _moe_dispatch(x2, p1, p2, seg, nb):
    t, d = x2.shape
    nt = t // MOE_BLOCK
    rows = nb * MOE_BLOCK
    pos = pl.BlockSpec((1, 1, MOE_BLOCK), lambda i: (i, 0, 0), memory_space=pltpu.SMEM)
    return pl.pallas_call(
        _moe_dispatch_kernel,
        grid=(nt,),
        in_specs=[pl.BlockSpec(memory_space=pltpu.SMEM), pos, pos,
                  pl.BlockSpec((MOE_BLOCK, d), lambda i: (i, 0))],
        out_specs=pl.BlockSpec(memory_space=pl.ANY),
        out_shape=jax.ShapeDtypeStruct((rows, d // 2), I32),
        scratch_shapes=[pltpu.VMEM((2, MOE_BLOCK, d // 2), I32),
                        pltpu.VMEM((MOE_BLOCK, d // 2), I32),
                        pltpu.SemaphoreType.DMA((2,)),
                        pltpu.SemaphoreType.DMA(())],
        compiler_params=_cparams(("arbitrary",)),
        name="moe_dispatch",
    )(seg, p1.reshape(nt, 1, MOE_BLOCK), p2.reshape(nt, 1, MOE_BLOCK), x2)


MAT_PIECES = 4
N_PIECES = 3 * MAT_PIECES
PIECES_PER_BLOCK = 3
PIECE_RING = 4


def _moe_ffn_kernel(be_ref, nu_ref, xs_ref, wg_hbm, wu_hbm, wd_hbm, ys_ref,
                    wgb, wub, wdb, sa, sb, sem, st):
    i = pl.program_id(0)
    nb = be_ref.shape[0]
    n_used = nu_ref[0]
    ra = wgb.shape[1] // MAT_PIECES
    rb = wdb.shape[1] // MAT_PIECES

    def piece_copy(p, e):
        k = p % PIECE_RING
        m, r = divmod(p, MAT_PIECES)
        if m == 0:
            return pltpu.make_async_copy(wg_hbm.at[e, pl.ds(r * ra, ra)], sa.at[k], sem.at[k])
        if m == 1:
            return pltpu.make_async_copy(wu_hbm.at[e, pl.ds(r * ra, ra)], sa.at[k], sem.at[k])
        return pltpu.make_async_copy(wd_hbm.at[e, pl.ds(r * rb, rb)], sb.at[k], sem.at[k])

    def piece_round(p, slot):
        k = p % PIECE_RING
        m, r = divmod(p, MAT_PIECES)
        if m == 0:
            wgb[slot, pl.ds(r * ra, ra), :] = sa[k].astype(BF16)
        elif m == 1:
            wub[slot, pl.ds(r * ra, ra), :] = sa[k].astype(BF16)
        else:
            wdb[slot, pl.ds(r * rb, rb), :] = sb[k].astype(BF16)

    def start_one():
        e, started, finished = st[1], st[2], st[3]
        can = (started < N_PIECES) & (started - finished < PIECE_RING)
        for p in range(N_PIECES):
            @pl.when(can & (started == p))
            def _():
                piece_copy(p, e).start(priority=p % 2)
        st[2] = started + can.astype(I32)

    def finish_one(slot):
        e, finished = st[1], st[3]

        @pl.when((e >= 0) & (finished < N_PIECES))
        def _():
            for p in range(N_PIECES):
                @pl.when(finished == p)
                def _():
                    piece_copy(p, e).wait()
                    piece_round(p, slot)
            st[3] = finished + 1
            start_one()

    def prepare(e):
        st[1] = e
        st[2] = 0
        st[3] = 0

        @pl.when(e >= 0)
        def _():
            for _ in range(PIECE_RING):
                start_one()

    @pl.when(i < n_used)
    def _():
        e = be_ref[i]

        @pl.when(i == 0)
        def _():
            st[0] = 1
            prepare(e)

        @pl.when((i == 0) | (e != be_ref[jnp.maximum(i - 1, 0)]))
        def _():
            slot = 1 - st[0]

            def fin(_, c):
                finish_one(slot)
                return c

            lax.fori_loop(0, N_PIECES, fin, 0)
            st[0] = slot
            k = lax.while_loop(lambda k: (k < n_used) & (be_ref[jnp.minimum(k, nb - 1)] == e),
                               lambda k: k + 1, i + 1)
            prepare(jnp.where(k < n_used, be_ref[jnp.minimum(k, nb - 1)], -1))

        def ahead(_, c):
            finish_one(1 - st[0])
            return c

        lax.fori_loop(0, PIECES_PER_BLOCK, ahead, 0)

        slot = st[0]
        lo, hi = _unpack_halves(xs_ref[...])
        lo = lo.astype(BF16)
        hi = hi.astype(BF16)
        d2 = lo.shape[1]
        g = _dot(lo, wgb[slot, :d2]) + _dot(hi, wgb[slot, d2:])
        u = _dot(lo, wub[slot, :d2]) + _dot(hi, wub[slot, d2:])
        hmid = (g * jax.nn.sigmoid(g) * u).astype(BF16)
        ys_ref[...] = _pack_halves(_dot(hmid, wdb[slot]))

    @pl.when(i >= n_used)
    def _():
        ys_ref[...] = jnp.zeros(ys_ref.shape, ys_ref.dtype)


def _moe_ffn(xs, w_gate, w_up, w_down, block_expert, n_used):
    rows, d2 = xs.shape
    nb = rows // MOE_BLOCK
    d = 2 * d2
    ff = w_gate.shape[2]
    grid_spec = pltpu.PrefetchScalarGridSpec(
        num_scalar_prefetch=2,
        grid=(nb,),
        in_specs=[pl.BlockSpec((MOE_BLOCK, d2), lambda i, be, nu: (i, 0)),
                  pl.BlockSpec(memory_space=pl.ANY),
                  pl.BlockSpec(memory_space=pl.ANY),
                  pl.BlockSpec(memory_space=pl.ANY)],
        out_specs=pl.BlockSpec((MOE_BLOCK, d2), lambda i, be, nu: (i, 0)),
        scratch_shapes=[pltpu.VMEM((2, d, ff), BF16), pltpu.VMEM((2, d, ff), BF16),
                        pltpu.VMEM((2, ff, d), BF16),
                        pltpu.VMEM((PIECE_RING, d // MAT_PIECES, ff), F32),
                        pltpu.VMEM((PIECE_RING, ff // MAT_PIECES, d), F32),
                        pltpu.SemaphoreType.DMA((PIECE_RING,)),
                        pltpu.SMEM((4,), I32)],
    )
    return pl.pallas_call(
        _moe_ffn_kernel,
        grid_spec=grid_spec,
        out_shape=jax.ShapeDtypeStruct((rows, d2), I32),
        compiler_params=_cparams(("arbitrary",)),
        name="moe_ffn",
    )(block_expert, n_used, xs, w_gate, w_up, w_down)


def _moe_combine_kernel(p1_ref, p2_ref, q1_ref, q2_ref, x_ref, g1_ref, g2_ref, g_ref, b_ref, ys_hbm,
                        o_ref, yb, sem):
    i = pl.program_id(0)
    n = pl.num_programs(0)
    slot = lax.rem(i, 2)
    blk = x_ref.shape[0]

    def row_copy(s, k, r, pos):
        return pltpu.make_async_copy(ys_hbm.at[pl.ds(pos, 1)], yb.at[s, k, pl.ds(r, 1)], sem.at[s])

    def fetch(s, a_ref, b_ref2):
        def body(r, c):
            row_copy(s, 0, r, a_ref[0, 0, r]).start()
            row_copy(s, 1, r, b_ref2[0, 0, r]).start()
            return c
        lax.fori_loop(0, blk, body, 0, unroll=8)

    @pl.when(i == 0)
    def _():
        fetch(0, p1_ref, p2_ref)

    @pl.when(i + 1 < n)
    def _():
        fetch(1 - slot, q1_ref, q2_ref)

    for k in range(2):
        pltpu.make_async_copy(ys_hbm.at[pl.ds(0, blk)], yb.at[slot, k], sem.at[slot]).wait()
    y1 = jnp.concatenate(_unpack_halves(yb[slot, 0]), axis=1)
    y2 = jnp.concatenate(_unpack_halves(yb[slot, 1]), axis=1)
    y = y1 * g1_ref[...] + y2 * g2_ref[...]
    o_ref[...] = _ln_rows(ALPHA * x_ref[...] + y, g_ref[...], b_ref[...])


def _moe_combine(x2, ys, p1, p2, g1, g2, g, b):
    t, d = x2.shape
    nt = t // MOE_BLOCK
    p1 = p1.reshape(nt, 1, MOE_BLOCK)
    p2 = p2.reshape(nt, 1, MOE_BLOCK)
    pos = pl.BlockSpec((1, 1, MOE_BLOCK), lambda i: (i, 0, 0), memory_space=pltpu.SMEM)
    nxt = pl.BlockSpec((1, 1, MOE_BLOCK), lambda i: (jnp.minimum(i + 1, nt - 1), 0, 0),
                       memory_space=pltpu.SMEM)
    col = pl.BlockSpec((MOE_BLOCK, 1), lambda i: (i, 0))
    vec = pl.BlockSpec((1, d), lambda i: (0, 0))
    return pl.pallas_call(
        _moe_combine_kernel,
        grid=(nt,),
        in_specs=[pos, pos, nxt, nxt, pl.BlockSpec((MOE_BLOCK, d), lambda i: (i, 0)), col, col, vec, vec,
                  pl.BlockSpec(memory_space=pl.ANY)],
        out_specs=pl.BlockSpec((MOE_BLOCK, d), lambda i: (i, 0)),
        out_shape=jax.ShapeDtypeStruct((t, d), F32),
        scratch_shapes=[pltpu.VMEM((2, 2, MOE_BLOCK, d // 2), I32),
                        pltpu.SemaphoreType.DMA((2,))],
        compiler_params=_cparams(("arbitrary",)),
        name="moe_combine",
    )(p1, p2, p1, p2, x2, g1, g2, g.reshape(1, d), b.reshape(1, d), ys)


def _tile(n, pref):
    return pref if n % pref == 0 else n


def _layer(x, mem, w_in, conv_w, conv_b, conv_ln_g, conv_ln_b, kv_norm_g, w_uk, w_uv, rel_bias,
           conv_out_g, attn_out_g, w_out, ln1_g, ln1_b, w_mq, w_mk, w_mv, w_mo, ln2_g, ln2_b,
           w_router_grp, w_router_exp, w_gate, w_up, w_down, ln3_g, ln3_b):
    bsz, seq, d = x.shape
    t = bsz * seq
    d_conv = conv_w.shape[1]
    d_attn = N_HEADS * HEAD_DIM
    c_glu = 2 * d_conv
    c_qi = H_IDX * D_IDX
    o_q, o_kv = c_glu, c_glu + d_attn
    o_qi = o_kv + KV_RANK
    o_ki = o_qi + c_qi
    topk = min(TOPK_MAX, seq // 4)

    xf = x.reshape(t, d)
    xb = xf.astype(BF16)
    w_inb = w_in.astype(BF16)
    tail_w = jnp.concatenate([w_inb[:, o_kv:o_qi], w_inb[:, o_ki:]], axis=1)
    tail_w = jnp.pad(tail_w, ((0, 0), (0, KV_RANK + LANES - tail_w.shape[1])))
    tm = _tile(t, 1024)

    u = _matmul_cols(xb, w_inb, c_glu, BF16, tm, 512, "proj_glu")
    q_hm = _matmul_heads(xb, w_inb, o_q, d_attn, BF16, tm, 512, "proj_q")
    qi_hm = _matmul_heads(xb, w_inb, o_qi, c_qi, BF16, tm, 512, "proj_qidx")
    tail = _matmul(xb, tail_w, F32, tm, KV_RANK + LANES, "proj_tail")

    conv_n = _conformer_conv(u, bsz, seq, conv_w, conv_b, conv_ln_g, conv_ln_b, conv_out_g)
    tk = min(ATT_BLOCK, seq)
    ckv_n, ckv_t, kia, kib, kw = _prep_latent(tail, kv_norm_g, tm, tk)
    ckv_t = ckv_t.reshape(bsz, seq // tk, KV_RANK, tk)
    mask = _indexer_mask(qi_hm, kw, kia, kib, bsz, seq, topk)
    w_uk_t = w_uk.transpose(0, 2, 1).reshape(d_attn, KV_RANK).astype(BF16)
    w_uv_all = w_uv.transpose(1, 0, 2).reshape(KV_RANK, d_attn).astype(BF16)
    k_t = _key_heads(w_uk_t, ckv_t)
    v_hm = _matmul_value_heads(ckv_n, w_uv_all, tm, 512, "value_heads")
    attn = _head_attention(q_hm, k_t, v_hm, mask, rel_bias, bsz, seq)

    pre1 = _matmul2_residual(conv_n, attn, attn_out_g, w_out.astype(BF16), xf, tm, 512, "out_proj")

    mem_len = mem.shape[1]
    memb = mem.reshape(bsz * mem_len, d).astype(BF16)
    w_kv = jnp.concatenate([w_mk, w_mv], axis=1).astype(BF16)
    kv = _matmul(memb, w_kv, BF16, _tile(bsz * mem_len, 512), 512, "mem_kv")
    w_router = jnp.concatenate([w_router_grp, w_router_exp], axis=1)
    w_router = jnp.pad(w_router, ((0, 0), (0, LANES - w_router.shape[1]))).astype(BF16)
    x2, logits = _memory_attention(pre1, ln1_g, ln1_b, kv, w_mq.astype(BF16), w_mo.astype(BF16),
                                   ln2_g, ln2_b, w_router, bsz, seq, 256)

    e1, e2, g1, g2 = _router(logits, _tile(t, 1024))
    nb = (2 * t + N_EXPERTS * (MOE_BLOCK - 1) + MOE_BLOCK - 1) // MOE_BLOCK
    r1, r2, cnt = _moe_rank(e1, e2, 512)
    p1, p2, block_expert, n_used = _moe_place(e1, e2, r1, r2, cnt, nb, 512)
    xs = _moe_dispatch(x2, p1, p2, n_used, nb)
    ys = _moe_ffn(xs, w_gate, w_up, w_down, block_expert.reshape(nb), n_used[0, 0:1])
    x3 = _moe_combine(x2, ys, p1, p2, g1, g2, ln3_g, ln3_b)
    return x3.reshape(bsz, seq, d)


def kernel(x, mem, w_in, conv_w, conv_b, conv_ln_g, conv_ln_b, kv_norm_g, w_uk, w_uv, rel_bias, conv_out_g, attn_out_g, w_out, ln1_g, ln1_b, w_mq, w_mk, w_mv, w_mo, ln2_g, ln2_b, w_router_grp, w_router_exp, w_gate, w_up, w_down, ln3_g, ln3_b):
    for l in range(w_in.shape[0]):
        x = _layer(x, mem, w_in[l], conv_w[l], conv_b[l], conv_ln_g[l], conv_ln_b[l], kv_norm_g[l],
                   w_uk[l], w_uv[l], rel_bias, conv_out_g[l], attn_out_g[l], w_out[l], ln1_g[l], ln1_b[l],
                   w_mq[l], w_mk[l], w_mv[l], w_mo[l], ln2_g[l], ln2_b[l], w_router_grp[l],
                   w_router_exp[l], w_gate[l], w_up[l], w_down[l], ln3_g[l], ln3_b[l])
    return x
```

```python
import functools
import math

import jax
import jax.numpy as jnp
from jax import lax
from jax.experimental import pallas as pl
from jax.experimental.pallas import tpu as pltpu

F32 = jnp.float32
BF16 = jnp.bfloat16
I32 = jnp.int32

DEPTH = 1
CONV_WIDTH = 31
N_HEADS = 16
HEAD_DIM = 128
KV_RANK = 512
H_IDX = 32
D_IDX = 64
TOPK_MAX = 256
N_BUCKETS = 32
MAX_DIST = 128
MEM_HEADS = 4
MEM_HEAD_DIM = 128
N_GROUPS = 8
EXP_PER_GROUP = 8
N_EXPERTS = N_GROUPS * EXP_PER_GROUP
MOE_BLOCK = 128
ALPHA = (2.0 * DEPTH) ** 0.25
LN_EPS = 1e-5

LANES = 128
SUBLANES = 8
V7X_VMEM_BYTES = 64 * 1024 * 1024
VMEM_LIMIT = 56 * 1024 * 1024
NEG = -1e30
INT_MIN = -(2 ** 31)

LOG2E = 1.4426950408889634

ATT_BLOCK = 256
ATT_HEAD_GROUP = 8
CONV_TS = 256
CONV_HALO = 32
CONV_CC = 256
CONV_RC = 32


def _cparams(sem):
    return pltpu.CompilerParams(dimension_semantics=sem, vmem_limit_bytes=VMEM_LIMIT)


def _dot(a, b):
    return jnp.dot(a, b, preferred_element_type=F32)


def _dot_nt(a, b):
    return lax.dot_general(a, b, (((1,), (1,)), ((), ())), preferred_element_type=F32)


def _mm_kernel(a_ref, b_ref, o_ref):
    o_ref[...] = _dot(a_ref[...], b_ref[...]).astype(o_ref.dtype)


def _matmul(a, b, out_dtype, tm, tn, name):
    m, k = a.shape
    n = b.shape[1]
    return pl.pallas_call(
        _mm_kernel,
        grid=(m // tm, n // tn),
        in_specs=[pl.BlockSpec((tm, k), lambda i, j: (i, 0)),
                  pl.BlockSpec((k, tn), lambda i, j: (0, j))],
        out_specs=pl.BlockSpec((tm, tn), lambda i, j: (i, j)),
        out_shape=jax.ShapeDtypeStruct((m, n), out_dtype),
        compiler_params=_cparams(("parallel", "parallel")),
        name=name,
    )(a, b)


def _matmul_cols(a, b, n, out_dtype, tm, tn, name):
    m, k = a.shape
    return pl.pallas_call(
        _mm_kernel,
        grid=(m // tm, n // tn),
        in_specs=[pl.BlockSpec((tm, k), lambda i, j: (i, 0)),
                  pl.BlockSpec((k, tn), lambda i, j: (0, j))],
        out_specs=pl.BlockSpec((tm, tn), lambda i, j: (i, j)),
        out_shape=jax.ShapeDtypeStruct((m, n), out_dtype),
        compiler_params=_cparams(("parallel", "parallel")),
        name=name,
    )(a, b)


def _mm_heads_kernel(a_ref, b_ref, o_ref):
    r = _dot(a_ref[...], b_ref[...])
    for p in range(o_ref.shape[0]):
        o_ref[p] = r[:, p * LANES:(p + 1) * LANES].astype(o_ref.dtype)


def _matmul_heads(a, b, col0, n, out_dtype, tm, tn, name):
    m, k = a.shape
    assert col0 % tn == 0
    return pl.pallas_call(
        _mm_heads_kernel,
        grid=(m // tm, n // tn),
        in_specs=[pl.BlockSpec((tm, k), lambda i, j: (i, 0)),
                  pl.BlockSpec((k, tn), lambda i, j: (0, col0 // tn + j))],
        out_specs=pl.BlockSpec((tn // LANES, tm, LANES), lambda i, j: (j, i, 0)),
        out_shape=jax.ShapeDtypeStruct((n // LANES, m, LANES), out_dtype),
        compiler_params=_cparams(("parallel", "parallel")),
        name=name,
    )(a, b)


def _mm_value_heads_kernel(a_ref, b_ref, o_ref):
    r = _dot(a_ref[...], b_ref[...])
    tm = r.shape[0]
    for p in range(o_ref.shape[0]):
        o_ref[p, :, 0:LANES] = r[:, p * LANES:(p + 1) * LANES].astype(o_ref.dtype)
        o_ref[p, :, LANES:2 * LANES] = jnp.ones((tm, LANES), o_ref.dtype)


def _matmul_value_heads(a, b, tm, tn, name):
    m, k = a.shape
    n = b.shape[1]
    return pl.pallas_call(
        _mm_value_heads_kernel,
        grid=(m // tm, n // tn),
        in_specs=[pl.BlockSpec((tm, k), lambda i, j: (i, 0)),
                  pl.BlockSpec((k, tn), lambda i, j: (0, j))],
        out_specs=pl.BlockSpec((tn // LANES, tm, 2 * LANES), lambda i, j: (j, i, 0)),
        out_shape=jax.ShapeDtypeStruct((n // LANES, m, 2 * LANES), BF16),
        compiler_params=_cparams(("parallel", "parallel")),
        name=name,
    )(a, b)


def _mm2_res_kernel(a1_ref, a2_ref, g2_ref, w1_ref, w2_ref, r_ref, o_ref, a2n):
    @pl.when(pl.program_id(1) == 0)
    def _():
        x = a2_ref[...].astype(F32)
        ms = jnp.mean(x * x, axis=-1, keepdims=True)
        a2n[...] = (x * lax.rsqrt(ms + LN_EPS) * g2_ref[...]).astype(a2n.dtype)

    o_ref[...] = (ALPHA * r_ref[...] + _dot(a1_ref[...], w1_ref[...])
                  + _dot(a2n[...], w2_ref[...]))


def _matmul2_residual(a1, a2, g2, w, res, tm, tn, name):
    m, k = a1.shape
    assert a2.shape == a1.shape and w.shape[0] == 2 * k
    n = w.shape[1]
    return pl.pallas_call(
        _mm2_res_kernel,
        grid=(m // tm, n // tn),
        in_specs=[pl.BlockSpec((tm, k), lambda i, j: (i, 0)),
                  pl.BlockSpec((tm, k), lambda i, j: (i, 0)),
                  pl.BlockSpec((1, k), lambda i, j: (0, 0)),
                  pl.BlockSpec((k, tn), lambda i, j: (0, j)),
                  pl.BlockSpec((k, tn), lambda i, j: (1, j)),
                  pl.BlockSpec((tm, tn), lambda i, j: (i, j))],
        out_specs=pl.BlockSpec((tm, tn), lambda i, j: (i, j)),
        out_shape=jax.ShapeDtypeStruct((m, n), F32),
        scratch_shapes=[pltpu.VMEM((tm, k), BF16)],
        compiler_params=_cparams(("parallel", "arbitrary")),
        name=name,
    )(a1, a2, g2.reshape(1, k), w, w, res)


def _ln_rows(x, g, b):
    mu = jnp.mean(x, axis=-1, keepdims=True)
    xc = x - mu
    var = jnp.mean(xc * xc, axis=-1, keepdims=True)
    return xc * lax.rsqrt(var + LN_EPS) * g + b


def _conv_kernel(a_ref, g_ref, cw_ref, cb_ref, lg_ref, lb_ref, og_ref, o_ref, hbuf, ybuf, hs):
    ts = a_ref.shape[0]
    nch = hbuf.shape[0]
    cc = hbuf.shape[2]
    d_conv = nch * cc

    nrow = CONV_HALO + ts

    @pl.when(pl.program_id(1) == 0)
    def _():
        hbuf[:, 0:CONV_HALO, :] = jnp.zeros((nch, CONV_HALO, cc), F32)
        hbuf[:, nrow:nrow + SUBLANES, :] = jnp.zeros((nch, SUBLANES, cc), F32)

    for c in range(nch):
        a = a_ref[:, c * cc:(c + 1) * cc].astype(F32)
        g = g_ref[:, c * cc:(c + 1) * cc].astype(F32)
        hbuf[c, CONV_HALO:nrow, :] = a * jax.nn.sigmoid(g)

    first = CONV_HALO - (CONV_WIDTH - 1)

    def chunk_body(c, carry):
        for o in range(1, SUBLANES):
            hs[o - 1] = hbuf[c, o:o + nrow, :]
        for r0 in range(0, ts, CONV_RC):
            acc = jnp.zeros((CONV_RC, cc), F32)
            for j in range(CONV_WIDTH):
                o = (first + j) % SUBLANES
                base = r0 + first + j - o
                rows = hbuf[c, base:base + CONV_RC, :] if o == 0 else hs[o - 1, base:base + CONV_RC, :]
                acc = acc + cw_ref[c, j:j + 1, :] * rows
            ybuf[c, r0:r0 + CONV_RC, :] = acc + cb_ref[c]
        hbuf[c, 0:CONV_HALO, :] = hbuf[c, ts:nrow, :]
        return carry

    lax.fori_loop(0, nch, chunk_body, 0)

    s1 = jnp.zeros((ts, 1), F32)
    for c in range(nch):
        s1 = s1 + jnp.sum(ybuf[c], axis=1, keepdims=True)
    mu = s1 * (1.0 / d_conv)
    s2 = jnp.zeros((ts, 1), F32)
    for c in range(nch):
        yc = ybuf[c] - mu
        s2 = s2 + jnp.sum(yc * yc, axis=1, keepdims=True)
    rstd = lax.rsqrt(s2 * (1.0 / d_conv) + LN_EPS)
    s3 = jnp.zeros((ts, 1), F32)
    for c in range(nch):
        z = (ybuf[c] - mu) * rstd * lg_ref[:, c * cc:(c + 1) * cc] + lb_ref[:, c * cc:(c + 1) * cc]
        z = z * jax.nn.sigmoid(z)
        ybuf[c] = z
        s3 = s3 + jnp.sum(z * z, axis=1, keepdims=True)
    rr = lax.rsqrt(s3 * (1.0 / d_conv) + LN_EPS)
    for c in range(nch):
        o_ref[:, c * cc:(c + 1) * cc] = (ybuf[c] * rr * og_ref[:, c * cc:(c + 1) * cc]).astype(o_ref.dtype)


def _conformer_conv(u, bsz, seq, conv_w, conv_b, ln_g, ln_b, out_g):
    d_conv = u.shape[1] // 2
    ts = min(CONV_TS, seq)
    nch = d_conv // CONV_CC
    nt = seq // ts
    cw = conv_w.reshape(CONV_WIDTH, nch, CONV_CC).transpose(1, 0, 2)
    cb = conv_b.reshape(nch, 1, CONV_CC)
    vec = pl.BlockSpec((1, d_conv), lambda b, i: (0, 0))
    return pl.pallas_call(
        _conv_kernel,
        grid=(bsz, nt),
        in_specs=[pl.BlockSpec((ts, d_conv), lambda b, i: (b * nt + i, 0)),
                  pl.BlockSpec((ts, d_conv), lambda b, i: (b * nt + i, 1)),
                  pl.BlockSpec((nch, CONV_WIDTH, CONV_CC), lambda b, i: (0, 0, 0)),
                  pl.BlockSpec((nch, 1, CONV_CC), lambda b, i: (0, 0, 0)),
                  vec, vec, vec],
        out_specs=pl.BlockSpec((ts, d_conv), lambda b, i: (b * nt + i, 0)),
        out_shape=jax.ShapeDtypeStruct((bsz * seq, d_conv), BF16),
        scratch_shapes=[pltpu.VMEM((nch, CONV_HALO + ts + SUBLANES, CONV_CC), F32),
                        pltpu.VMEM((nch, ts, CONV_CC), F32),
                        pltpu.VMEM((SUBLANES - 1, CONV_HALO + ts, CONV_CC), F32)],
        compiler_params=_cparams(("arbitrary", "arbitrary")),
        name="conformer_conv",
    )(u, u, cw, cb, ln_g.reshape(1, d_conv), ln_b.reshape(1, d_conv), out_g.reshape(1, d_conv))


def _prep_kernel(t_ref, g_ref, ckv_ref, ckvt_ref, kia_ref, kib_ref, kw_ref):
    ckv = t_ref[:, 0:KV_RANK]
    ms = jnp.mean(ckv * ckv, axis=-1, keepdims=True)
    ckv_n = ckv * lax.rsqrt(ms + LN_EPS) * g_ref[...]
    ckv_ref[...] = ckv_n.astype(ckv_ref.dtype)
    tk = ckvt_ref.shape[3]
    for c in range(ckvt_ref.shape[1]):
        ckvt_ref[0, c] = ckv_n[c * tk:(c + 1) * tk, :].T.astype(ckvt_ref.dtype)
    kw = t_ref[:, KV_RANK:KV_RANK + LANES]
    kw_ref[...] = kw
    lane = lax.broadcasted_iota(I32, kw.shape, 1)
    kia_ref[...] = jnp.where(lane < D_IDX, kw, 0.0).astype(kia_ref.dtype)
    kib_ref[...] = jnp.where(lane >= D_IDX, pltpu.roll(kw, D_IDX, 1), 0.0).astype(kib_ref.dtype)


def _prep_latent(tail, kv_norm_g, tm, tk):
    m, w = tail.shape
    return pl.pallas_call(
        _prep_kernel,
        grid=(m // tm,),
        in_specs=[pl.BlockSpec((tm, w), lambda i: (i, 0)),
                  pl.BlockSpec((1, KV_RANK), lambda i: (0, 0))],
        out_specs=[pl.BlockSpec((tm, KV_RANK), lambda i: (i, 0)),
                   pl.BlockSpec((1, tm // tk, KV_RANK, tk), lambda i: (i, 0, 0, 0)),
                   pl.BlockSpec((tm, LANES), lambda i: (i, 0)),
                   pl.BlockSpec((tm, LANES), lambda i: (i, 0)),
                   pl.BlockSpec((tm, LANES), lambda i: (i, 0))],
        out_shape=[jax.ShapeDtypeStruct((m, KV_RANK), BF16),
                   jax.ShapeDtypeStruct((m // tm, tm // tk, KV_RANK, tk), BF16),
                   jax.ShapeDtypeStruct((m, LANES), BF16),
                   jax.ShapeDtypeStruct((m, LANES), BF16),
                   jax.ShapeDtypeStruct((m, LANES), F32)],
        compiler_params=_cparams(("parallel",)),
        name="prep_latent",
    )(tail, kv_norm_g.reshape(1, KV_RANK))


def _indexer_t_kernel(qi_ref, kw_ref, kia_ref, kib_ref, o_ref, keybuf, *, topk):
    i = pl.program_id(1)
    npairs, tq, _ = qi_ref.shape
    nk = o_ref.shape[1]
    tk = o_ref.shape[3]
    kf = float(topk)
    group = 4
    slabs = tk // SUBLANES

    w_t = kw_ref[...].T
    key_idx = lax.broadcasted_iota(I32, (tk, tq), 0)
    qry_idx = lax.broadcasted_iota(I32, (tk, tq), 1) + i * tq

    def score_chunk(j, carry):
        k0 = pl.multiple_of(j * tk, tk)
        kd = jnp.concatenate([kia_ref[pl.ds(k0, tk), :], kib_ref[pl.ds(k0, tk), :]], axis=0)
        acc = jnp.zeros((tk, tq), F32)
        for p0 in range(0, npairs, group):
            rhs = qi_ref[p0:p0 + group].reshape(group * tq, LANES)
            zz = _dot_nt(kd, rhs)
            for p in range(group):
                h = 2 * (p0 + p)
                z = zz[:, p * tq:(p + 1) * tq]
                acc = (acc + w_t[D_IDX + h:D_IDX + h + 1, :] * jnp.maximum(z[0:tk], 0.0)
                       + w_t[D_IDX + h + 1:D_IDX + h + 2, :] * jnp.maximum(z[tk:2 * tk], 0.0))
        bits = pltpu.bitcast(acc, I32)
        key = jnp.where(bits >= 0, bits, bits ^ jnp.int32(0x7FFFFFFF))
        keybuf[j] = jnp.where(key_idx + j * tk <= qry_idx, key, INT_MIN)
        return carry

    def chunk_pairs(fn):
        def pair(p, carry):
            fn(2 * p, carry)
            fn(2 * p + 1, carry)
            return carry

        lax.fori_loop(0, lax.shift_right_logical(i + 1, 1), pair, 0)

        @pl.when((i & 1) == 0)
        def _():
            fn(i, 0)

    chunk_pairs(score_chunk)

    def count(pred):
        def one(j, c):
            hit = jnp.where(pred(keybuf[j], key_idx + j * tk), 1.0, 0.0)
            return c + jnp.sum(hit.reshape(slabs, SUBLANES, tq), axis=0)

        def pair(p, c):
            return one(2 * p + 1, one(2 * p, c))

        c = lax.fori_loop(0, lax.shift_right_logical(i + 1, 1), pair, jnp.zeros((SUBLANES, tq), F32))
        c = lax.cond((i & 1) == 0, lambda c: one(i, c), lambda c: c, c)
        return jnp.broadcast_to(jnp.sum(c, axis=0, keepdims=True), (SUBLANES, tq))

    def tile_rows(v):
        return jnp.concatenate([v] * slabs, axis=0)

    def count_ge(cand):
        cb = tile_rows(cand)
        return count(lambda kk, idx: kk >= cb)

    tau = jnp.where(count_ge(jnp.zeros((SUBLANES, tq), I32)) >= kf, 0, INT_MIN).astype(I32)

    def bit_body(it, tau):
        cand = tau | jnp.left_shift(jnp.int32(1), 30 - it)
        return jnp.where(count_ge(cand) >= kf, cand, tau)

    tau = lax.fori_loop(0, 31, bit_body, tau)
    n_ge = count_ge(tau)
    taub = tile_rows(tau)

    def tie_cut():
        need = kf - count(lambda kk, idx: kk > taub)

        def cut_body(it, cut):
            cand = cut + jnp.left_shift(jnp.int32(1), 30 - it)
            cb = tile_rows(cand)
            below = count(lambda kk, idx: (kk == taub) & (idx < cb))
            return jnp.where(below < need, cand, cut)

        return lax.fori_loop(0, 31, cut_body, jnp.zeros((SUBLANES, tq), I32))

    has_ties = jnp.max(n_ge) > kf
    cut = lax.cond(has_ties, tie_cut, lambda: jnp.full((SUBLANES, tq), 2 ** 30, I32))
    cutb = tile_rows(cut)
    eye = jnp.where(lax.broadcasted_iota(I32, (tq, tq), 0) == lax.broadcasted_iota(I32, (tq, tq), 1),
                    1.0, 0.0).astype(BF16)

    def write_chunk(j, carry):
        kk = keybuf[j]
        sel = ((kk > taub) | ((kk == taub) & (key_idx + j * tk <= cutb))) & (kk != INT_MIN)
        sel_qk = _dot_nt(eye, jnp.where(sel, 1.0, 0.0).astype(BF16))
        o_ref[0, j] = jnp.where(sel_qk > 0.5, 0.0, NEG).astype(o_ref.dtype)
        return carry

    chunk_pairs(write_chunk)

    def write_rest(j, carry):
        o_ref[0, j] = jnp.full((tq, tk), NEG, o_ref.dtype)
        return carry

    lax.fori_loop(i + 1, nk, write_rest, 0)


def _indexer_mask(qi_hm, kw, kia, kib, bsz, seq, topk):
    tq = min(ATT_BLOCK, seq)
    nq = seq // tq
    npairs = qi_hm.shape[0]
    return pl.pallas_call(
        functools.partial(_indexer_t_kernel, topk=topk),
        grid=(bsz, nq),
        in_specs=[pl.BlockSpec((npairs, tq, LANES), lambda b, i: (0, b * nq + i, 0)),
                  pl.BlockSpec((tq, LANES), lambda b, i: (b * nq + i, 0)),
                  pl.BlockSpec((seq, LANES), lambda b, i: (b, 0)),
                  pl.BlockSpec((seq, LANES), lambda b, i: (b, 0))],
        out_specs=pl.BlockSpec((1, nq, tq, tq), lambda b, i: (b * nq + i, 0, 0, 0)),
        out_shape=jax.ShapeDtypeStruct((bsz * nq, nq, tq, tq), BF16),
        scratch_shapes=[pltpu.VMEM((nq, tq, tq), I32)],
        compiler_params=_cparams(("parallel", "parallel")),
        name="indexer_mask",
    )(qi_hm, kw, kia, kib)


def _t5_bucket(dist):
    n = jnp.maximum(dist, 0)
    max_exact = N_BUCKETS // 2
    nf = jnp.maximum(n, 1).astype(F32)
    large = max_exact + (jnp.log(nf / max_exact) / math.log(MAX_DIST / max_exact)
                         * (N_BUCKETS - max_exact)).astype(I32)
    large = jnp.minimum(large, N_BUCKETS - 1)
    return jnp.where(n < max_exact, n, large)


def _key_heads_kernel(wt_ref, ct_ref, o_ref):
    r = _dot(wt_ref[...], ct_ref[0, 0]) * (HEAD_DIM ** -0.5 * LOG2E)
    for h in range(o_ref.shape[1]):
        o_ref[0, h, 0] = r[h * HEAD_DIM:(h + 1) * HEAD_DIM].astype(o_ref.dtype)


def _key_heads(w_uk_t, ckv_t):
    bsz, nk, r_lat, tk = ckv_t.shape
    nh = w_uk_t.shape[0] // HEAD_DIM
    return pl.pallas_call(
        _key_heads_kernel,
        grid=(bsz, nk),
        in_specs=[pl.BlockSpec((nh * HEAD_DIM, r_lat), lambda b, j: (0, 0)),
                  pl.BlockSpec((1, 1, r_lat, tk), lambda b, j: (b, j, 0, 0))],
        out_specs=pl.BlockSpec((1, nh, 1, HEAD_DIM, tk), lambda b, j: (b, 0, j, 0, 0)),
        out_shape=jax.ShapeDtypeStruct((bsz, nh, nk, HEAD_DIM, tk), BF16),
        compiler_params=_cparams(("parallel", "parallel")),
        name="key_heads",
    )(w_uk_t, ckv_t)


def _attn_kernel(relb_ref, q_ref, kt_ref, v_ref, mask_ref, o_ref, acc, m_s, btab):
    b = pl.program_id(0)
    g = pl.program_id(1)
    i = pl.program_id(2)
    hg, tq, _ = q_ref.shape
    nh = btab.shape[1]
    tk = mask_ref.shape[3]

    @pl.when((b == 0) & (g == 0) & (i == 0))
    def _():
        r = lax.broadcasted_iota(I32, (tq, tk), 0)
        c = lax.broadcasted_iota(I32, (tq, tk), 1)
        for t in range(2):
            bk = _t5_bucket(r - c + t * tk)

            def fill(h, carry):
                far = relb_ref[N_BUCKETS - 1, h]
                v = jnp.zeros((tq, tk), F32)
                for k in range(N_BUCKETS):
                    v = jnp.where(bk == k, (relb_ref[k, h] - far) * LOG2E, v)
                btab[t, h] = v
                return carry

            lax.fori_loop(0, nh, fill, 0)

    acc[...] = jnp.zeros(acc.shape, F32)
    m_s[...] = jnp.full(m_s.shape, NEG, F32)

    def lane_tile(v, n):
        return jnp.concatenate([v] * n, axis=1)

    def chunks(js, near):
        mks = [mask_ref[0, j].astype(F32) for j in js]
        for h in range(hg):
            for j, mk in zip(js, mks):
                s = _dot(q_ref[h], kt_ref[0, h, j]) + mk
                if near:
                    s = s + btab[i - j, g * hg + h]
                m_old = m_s[h]
                m_new = jnp.maximum(m_old, jnp.max(s, axis=1, keepdims=True))
                a = jnp.exp2(m_old - m_new)
                p = jnp.exp2(s - lane_tile(m_new, tk // LANES))
                rows = pl.ds(pl.multiple_of(j * tk, tk), tk)
                acc[h] = lane_tile(a, 2) * acc[h] + _dot(p.astype(BF16), v_ref[h, rows, :])
                m_s[h] = m_new

    n_far = jnp.maximum(i - 1, 0)

    def far_quad(p, carry):
        chunks([4 * p, 4 * p + 1, 4 * p + 2, 4 * p + 3], False)
        return carry

    lax.fori_loop(0, lax.shift_right_logical(n_far, 2), far_quad, 0)
    done = n_far & ~3

    @pl.when((n_far & 2) == 2)
    def _():
        chunks([done, done + 1], False)

    @pl.when((n_far & 1) == 1)
    def _():
        chunks([n_far - 1], False)

    @pl.when(i >= 1)
    def _():
        chunks([i - 1, i], True)

    @pl.when(i == 0)
    def _():
        chunks([i], True)

    for h in range(hg):
        o_ref[:, h * HEAD_DIM:(h + 1) * HEAD_DIM] = (
            acc[h, :, :HEAD_DIM] / acc[h, :, HEAD_DIM:]).astype(o_ref.dtype)


def _head_attention(q_hm, k_t, v_hm, mask, rel_bias, bsz, seq):
    nh = q_hm.shape[0]
    hg = ATT_HEAD_GROUP
    tq = mask.shape[2]
    nq = seq // tq
    assert tq == mask.shape[3] and tq >= MAX_DIST
    once = pl.Buffered(1)
    return pl.pallas_call(
        _attn_kernel,
        grid=(bsz, nh // hg, nq),
        in_specs=[pl.BlockSpec(memory_space=pltpu.SMEM),
                  pl.BlockSpec((hg, tq, HEAD_DIM), lambda b, g, i: (g, b * nq + i, 0)),
                  pl.BlockSpec((1, hg, nq, HEAD_DIM, tq), lambda b, g, i: (b, g, 0, 0, 0), pipeline_mode=once),
                  pl.BlockSpec((hg, seq, 2 * HEAD_DIM), lambda b, g, i: (g, b, 0), pipeline_mode=once),
                  pl.BlockSpec((1, nq, tq, tq), lambda b, g, i: (b * nq + i, 0, 0, 0))],
        out_specs=pl.BlockSpec((tq, hg * HEAD_DIM), lambda b, g, i: (b * nq + i, g)),
        out_shape=jax.ShapeDtypeStruct((bsz * seq, nh * HEAD_DIM), BF16),
        scratch_shapes=[pltpu.VMEM((hg, tq, 2 * HEAD_DIM), F32),
                        pltpu.VMEM((hg, tq, LANES), F32),
                        pltpu.VMEM((2, nh, tq, tq), F32)],
        compiler_params=_cparams(("arbitrary", "arbitrary", "arbitrary")),
        name="head_attention",
    )(rel_bias, q_hm, k_t, v_hm, mask)


def _memattn_kernel(x_ref, g1_ref, b1_ref, kv_ref, wq_ref, wo_ref, g_ref, b_ref, wr_ref, o_ref, lg_ref):
    x = _ln_rows(x_ref[...], g1_ref[...], b1_ref[...])
    d_mem = MEM_HEADS * MEM_HEAD_DIM
    q = (_dot(x.astype(BF16), wq_ref[...]) * (MEM_HEAD_DIM ** -0.5)).astype(BF16)
    outs = []
    for h in range(MEM_HEADS):
        lo = h * MEM_HEAD_DIM
        k = kv_ref[:, lo:lo + MEM_HEAD_DIM]
        v = kv_ref[:, d_mem + lo:d_mem + lo + MEM_HEAD_DIM]
        s = _dot_nt(q[:, lo:lo + MEM_HEAD_DIM], k)
        p = jnp.exp(s - jnp.max(s, axis=1, keepdims=True))
        p = p / jnp.sum(p, axis=1, keepdims=True)
        outs.append(_dot(p.astype(BF16), v).astype(BF16))
    o = jnp.concatenate(outs, axis=1)
    x2 = _ln_rows(ALPHA * x + _dot(o, wo_ref[...]), g_ref[...], b_ref[...])
    o_ref[...] = x2
    lg_ref[...] = _dot(x2.astype(BF16), wr_ref[...])


def _memory_attention(pre1, g1, b1, kv, w_mq, w_mo, g, b, w_router, bsz, seq, tm):
    t, d = pre1.shape
    nt = seq // tm
    mem_len = kv.shape[0] // bsz
    d_mem = w_mq.shape[1]
    return pl.pallas_call(
        _memattn_kernel,
        grid=(bsz, nt),
        in_specs=[pl.BlockSpec((tm, d), lambda bi, i: (bi * nt + i, 0)),
                  pl.BlockSpec((1, d), lambda bi, i: (0, 0)),
                  pl.BlockSpec((1, d), lambda bi, i: (0, 0)),
                  pl.BlockSpec((mem_len, 2 * d_mem), lambda bi, i: (bi, 0)),
                  pl.BlockSpec((d, d_mem), lambda bi, i: (0, 0), pipeline_mode=pl.Buffered(1)),
                  pl.BlockSpec((d_mem, d), lambda bi, i: (0, 0), pipeline_mode=pl.Buffered(1)),
                  pl.BlockSpec((1, d), lambda bi, i: (0, 0)),
                  pl.BlockSpec((1, d), lambda bi, i: (0, 0)),
                  pl.BlockSpec((d, LANES), lambda bi, i: (0, 0), pipeline_mode=pl.Buffered(1))],
        out_specs=[pl.BlockSpec((tm, d), lambda bi, i: (bi * nt + i, 0)),
                   pl.BlockSpec((tm, LANES), lambda bi, i: (bi * nt + i, 0))],
        out_shape=[jax.ShapeDtypeStruct((t, d), F32),
                   jax.ShapeDtypeStruct((t, LANES), F32)],
        compiler_params=_cparams(("parallel", "parallel")),
        name="memory_attention",
    )(pre1, g1.reshape(1, d), b1.reshape(1, d), kv, w_mq, w_mo, g.reshape(1, d), b.reshape(1, d), w_router)


def _router_kernel(lg_ref, e1_ref, e2_ref, g1_ref, g2_ref):
    x = lg_ref[...]
    lane = lax.broadcasted_iota(I32, x.shape, 1)
    lane_f = lane.astype(F32)

    def argmax(mask):
        v = jnp.where(mask, x, -jnp.inf)
        mx = jnp.max(v, axis=1, keepdims=True)
        idx = jnp.min(jnp.where(mask & (v == mx), lane_f, float(LANES)), axis=1, keepdims=True)
        return mx, idx.astype(I32)

    gmask = lane < N_GROUPS
    gmax, gsel = argmax(gmask)
    gsum = jnp.sum(jnp.where(gmask, jnp.exp(x - gmax), 0.0), axis=1, keepdims=True)
    g_p = 1.0 / gsum
    lo = N_GROUPS + gsel * EXP_PER_GROUP
    emask = (lane >= lo) & (lane < lo + EXP_PER_GROUP)
    m1, i1 = argmax(emask)
    m2, i2 = argmax(emask & (lane != i1))
    esum = jnp.sum(jnp.where(emask, jnp.exp(x - m1), 0.0), axis=1, keepdims=True)
    p1 = 1.0 / esum
    p2 = jnp.exp(m2 - m1) / esum
    e1_ref[...] = i1 - N_GROUPS
    e2_ref[...] = i2 - N_GROUPS
    g1_ref[...] = g_p * (p1 / (p1 + p2))
    g2_ref[...] = g_p * (p2 / (p1 + p2))


def _router(logits, tm):
    t = logits.shape[0]
    col = pl.BlockSpec((tm, 1), lambda i: (i, 0))
    return pl.pallas_call(
        _router_kernel,
        grid=(t // tm,),
        in_specs=[pl.BlockSpec((tm, LANES), lambda i: (i, 0))],
        out_specs=[col, col, col, col],
        out_shape=[jax.ShapeDtypeStruct((t, 1), I32), jax.ShapeDtypeStruct((t, 1), I32),
                   jax.ShapeDtypeStruct((t, 1), F32), jax.ShapeDtypeStruct((t, 1), F32)],
        compiler_params=_cparams(("parallel",)),
        name="moe_router",
    )(logits)


HI16 = -65536


def _pack_halves(x):
    d = x.shape[1] // 2
    lo = pltpu.bitcast(x[:, :d].astype(BF16).astype(F32), I32)
    hi = pltpu.bitcast(x[:, d:].astype(BF16).astype(F32), I32)
    return lax.shift_right_logical(lo, 16) | (hi & HI16)


def _unpack_halves(u):
    return pltpu.bitcast(lax.shift_left(u, 16), F32), pltpu.bitcast(u & HI16, F32)


def _moe_rank_kernel(e1_ref, e2_ref, r1_ref, r2_ref, cnt_ref, base):
    i = pl.program_id(0)
    tm = e1_ref.shape[0]

    @pl.when(i == 0)
    def _():
        base[...] = jnp.zeros(base.shape, F32)

    lane = lax.broadcasted_iota(I32, (tm, LANES), 1)
    rr = lax.broadcasted_iota(I32, (tm, tm), 0)
    cc = lax.broadcasted_iota(I32, (tm, tm), 1)
    earlier = jnp.where(cc < rr, 1.0, 0.0).astype(BF16)
    for slot, (e_ref, r_ref) in enumerate(((e1_ref, r1_ref), (e2_ref, r2_ref))):
        oh = jnp.where(lane == e_ref[...], 1.0, 0.0)
        before = _dot(earlier, oh.astype(BF16)) + base[slot:slot + 1, :]
        r_ref[...] = jnp.sum(oh * before, axis=1, keepdims=True).astype(I32)
        base[slot:slot + 1, :] = base[slot:slot + 1, :] + jnp.sum(oh, axis=0, keepdims=True)
    cnt_ref[...] = base[...]


def _moe_rank(e1, e2, tm):
    t = e1.shape[0]
    col = pl.BlockSpec((tm, 1), lambda i: (i, 0))
    return pl.pallas_call(
        _moe_rank_kernel,
        grid=(t // tm,),
        in_specs=[col, col],
        out_specs=[col, col, pl.BlockSpec((8, LANES), lambda i: (0, 0))],
        out_shape=[jax.ShapeDtypeStruct((t, 1), I32), jax.ShapeDtypeStruct((t, 1), I32),
                   jax.ShapeDtypeStruct((8, LANES), F32)],
        scratch_shapes=[pltpu.VMEM((8, LANES), F32)],
        compiler_params=_cparams(("arbitrary",)),
        name="moe_rank",
    )(e1, e2)


def _moe_place_kernel(e1_ref, e2_ref, r1_ref, r2_ref, cnt_ref, p1_ref, p2_ref, be_ref, nu_ref):
    tm = e1_ref.shape[0]
    nbp = be_ref.shape[0]
    lane8 = lax.broadcasted_iota(I32, (8, LANES), 1)
    cnt = cnt_ref[...].astype(I32)
    c0 = jnp.broadcast_to(cnt[0:1], (8, LANES))
    c1 = jnp.broadcast_to(cnt[1:2], (8, LANES))
    blk_shift = MOE_BLOCK.bit_length() - 1
    padded = lax.shift_left(lax.shift_right_logical(c0 + c1 + (MOE_BLOCK - 1), blk_shift), blk_shift)
    pad_end = padded
    s = 1
    while s < LANES:
        pad_end = pad_end + jnp.where(lane8 >= s, pltpu.roll(pad_end, s, 1), 0)
        s *= 2
    start0 = (pad_end - padded).astype(F32)
    start1 = (pad_end - padded + c0).astype(F32)

    lane = lax.broadcasted_iota(I32, (tm, LANES), 1)
    for e_ref, r_ref, p_ref, start in ((e1_ref, r1_ref, p1_ref, start0), (e2_ref, r2_ref, p2_ref, start1)):
        seg = jnp.sum(jnp.where(lane == e_ref[...], start[0:1], 0.0), axis=1, keepdims=True)
        p_ref[...] = seg.astype(I32) + r_ref[...]

    block_row = lax.broadcasted_iota(I32, (nbp, LANES), 0) * MOE_BLOCK
    lane_b = lax.broadcasted_iota(I32, (nbp, LANES), 1)
    ended = (pad_end[0:1] <= block_row) & (lane_b < N_EXPERTS)
    be = jnp.sum(jnp.where(ended, 1.0, 0.0), axis=1, keepdims=True)
    be_ref[...] = jnp.minimum(be, N_EXPERTS - 1.0).astype(I32)
    total = jnp.max(pad_end, axis=1, keepdims=True)
    sub = lax.broadcasted_iota(I32, (8, LANES), 0)
    n_used = jnp.broadcast_to(lax.shift_right_logical(total, blk_shift), (8, LANES))
    nu_ref[...] = jnp.where(sub == 0, n_used, jnp.where(sub == 1, pad_end, jnp.where(sub == 2, padded, 0)))


def _moe_place(e1, e2, r1, r2, cnt, nb, tm):
    t = e1.shape[0]
    col = pl.BlockSpec((tm, 1), lambda i: (i, 0))
    return pl.pallas_call(
        _moe_place_kernel,
        grid=(t // tm,),
        in_specs=[col, col, col, col, pl.BlockSpec((8, LANES), lambda i: (0, 0))],
        out_specs=[col, col, pl.BlockSpec((nb, 1), lambda i: (0, 0)),
                   pl.BlockSpec((8, LANES), lambda i: (0, 0))],
        out_shape=[jax.ShapeDtypeStruct((t, 1), I32), jax.ShapeDtypeStruct((t, 1), I32),
                   jax.ShapeDtypeStruct((nb, 1), I32), jax.ShapeDtypeStruct((8, LANES), I32)],
        compiler_params=_cparams(("arbitrary",)),
        name="moe_place",
    )(e1, e2, r1, r2, cnt)


def _moe_dispatch_kernel(seg_ref, p1_ref, p2_ref, x_ref, xs_hbm, pk, zbuf, sem, zsem):
    i = pl.program_id(0)
    n = pl.num_programs(0)
    slot = lax.rem(i, 2)
    blk = x_ref.shape[0]
    nb = xs_hbm.shape[0] // blk

    def row_copy(s, r, pos):
        return pltpu.make_async_copy(pk.at[s, pl.ds(r, 1)], xs_hbm.at[pl.ds(pos, 1)], sem.at[s])

    def drain(s):
        for _ in range(2):
            pltpu.make_async_copy(pk.at[s], xs_hbm.at[pl.ds(0, blk)], sem.at[s]).wait()

    def zero_copy(row0):
        return pltpu.make_async_copy(zbuf, xs_hbm.at[pl.ds(pl.multiple_of(row0, blk), blk)], zsem)

    @pl.when(i == 0)
    def _():
        zbuf[...] = jnp.zeros(zbuf.shape, zbuf.dtype)
        n_used = seg_ref[0, 0]

        def seg_start(e, c):
            @pl.when(seg_ref[2, e] > 0)
            def _():
                zero_copy(seg_ref[1, e] - blk).start()
            return c

        def tail_start(b, c):
            zero_copy(b * blk).start()
            return c

        def seg_wait(e, c):
            @pl.when(seg_ref[2, e] > 0)
            def _():
                zero_copy(0).wait()
            return c

        def tail_wait(b, c):
            zero_copy(0).wait()
            return c

        lax.fori_loop(0, N_EXPERTS, seg_start, 0)
        lax.fori_loop(n_used, nb, tail_start, 0)
        lax.fori_loop(0, N_EXPERTS, seg_wait, 0)
        lax.fori_loop(n_used, nb, tail_wait, 0)

    @pl.when(i >= 2)
    def _():
        drain(slot)

    pk[slot] = _pack_halves(x_ref[...])

    def issue(r, c):
        row_copy(slot, r, p1_ref[0, 0, r]).start()
        row_copy(slot, r, p2_ref[0, 0, r]).start()
        return c

    lax.fori_loop(0, blk, issue, 0, unroll=8)

    @pl.when(i == n - 1)
    def _():
        drain(slot)

    @pl.when((i == n - 1) & (i >= 1))
    def _():
        drain(1 - slot)


def _moe_dispatch(x2, p1, p2, seg, nb):
    t, d = x2.shape
    nt = t // MOE_BLOCK
    rows = nb * MOE_BLOCK
    pos = pl.BlockSpec((1, 1, MOE_BLOCK), lambda i: (i, 0, 0), memory_space=pltpu.SMEM)
    return pl.pallas_call(
        _moe_dispatch_kernel,
        grid=(nt,),
        in_specs=[pl.BlockSpec(memory_space=pltpu.SMEM), pos, pos,
                  pl.BlockSpec((MOE_BLOCK, d), lambda i: (i, 0))],
        out_specs=pl.BlockSpec(memory_space=pl.ANY),
        out_shape=jax.ShapeDtypeStruct((rows, d // 2), I32),
        scratch_shapes=[pltpu.VMEM((2, MOE_BLOCK, d // 2), I32),
                        pltpu.VMEM((MOE_BLOCK, d // 2), I32),
                        pltpu.SemaphoreType.DMA((2,)),
                        pltpu.SemaphoreType.DMA(())],
        compiler_params=_cparams(("arbitrary",)),
        name="moe_dispatch",
    )(seg, p1.reshape(nt, 1, MOE_BLOCK), p2.reshape(nt, 1, MOE_BLOCK), x2)


MAT_PIECES = 4
N_PIECES = 3 * MAT_PIECES
PIECES_PER_BLOCK = 3
PIECE_RING = 4


def _moe_ffn_kernel(be_ref, nu_ref, xs_ref, wg_hbm, wu_hbm, wd_hbm, ys_ref,
                    wgb, wub, wdb, sa, sb, sem, st):
    i = pl.program_id(0)
    nb = be_ref.shape[0]
    n_used = nu_ref[0]
    ra = wgb.shape[1] // MAT_PIECES
    rb = wdb.shape[1] // MAT_PIECES

    def piece_copy(p, e):
        k = p % PIECE_RING
        m, r = divmod(p, MAT_PIECES)
        if m == 0:
            return pltpu.make_async_copy(wg_hbm.at[e, pl.ds(r * ra, ra)], sa.at[k], sem.at[k])
        if m == 1:
            return pltpu.make_async_copy(wu_hbm.at[e, pl.ds(r * ra, ra)], sa.at[k], sem.at[k])
        return pltpu.make_async_copy(wd_hbm.at[e, pl.ds(r * rb, rb)], sb.at[k], sem.at[k])

    def piece_round(p, slot):
        k = p % PIECE_RING
        m, r = divmod(p, MAT_PIECES)
        if m == 0:
            wgb[slot, pl.ds(r * ra, ra), :] = sa[k].astype(BF16)
        elif m == 1:
            wub[slot, pl.ds(r * ra, ra), :] = sa[k].astype(BF16)
        else:
            wdb[slot, pl.ds(r * rb, rb), :] = sb[k].astype(BF16)

    def start_one():
        e, started, finished = st[1], st[2], st[3]
        can = (started < N_PIECES) & (started - finished < PIECE_RING)
        for p in range(N_PIECES):
            @pl.when(can & (started == p))
            def _():
                piece_copy(p, e).start(priority=p % 2)
        st[2] = started + can.astype(I32)

    def finish_one(slot):
        e, finished = st[1], st[3]

        @pl.when((e >= 0) & (finished < N_PIECES))
        def _():
            for p in range(N_PIECES):
                @pl.when(finished == p)
                def _():
                    piece_copy(p, e).wait()
                    piece_round(p, slot)
            st[3] = finished + 1
            start_one()

    def prepare(e):
        st[1] = e
        st[2] = 0
        st[3] = 0

        @pl.when(e >= 0)
        def _():
            for _ in range(PIECE_RING):
                start_one()

    @pl.when(i < n_used)
    def _():
        e = be_ref[i]

        @pl.when(i == 0)
        def _():
            st[0] = 1
            prepare(e)

        @pl.when((i == 0) | (e != be_ref[jnp.maximum(i - 1, 0)]))
        def _():
            slot = 1 - st[0]

            def fin(_, c):
                finish_one(slot)
                return c

            lax.fori_loop(0, N_PIECES, fin, 0)
            st[0] = slot
            k = lax.while_loop(lambda k: (k < n_used) & (be_ref[jnp.minimum(k, nb - 1)] == e),
                               lambda k: k + 1, i + 1)
            prepare(jnp.where(k < n_used, be_ref[jnp.minimum(k, nb - 1)], -1))

        def ahead(_, c):
            finish_one(1 - st[0])
            return c

        lax.fori_loop(0, PIECES_PER_BLOCK, ahead, 0)

        slot = st[0]
        lo, hi = _unpack_halves(xs_ref[...])
        lo = lo.astype(BF16)
        hi = hi.astype(BF16)
        d2 = lo.shape[1]
        g = _dot(lo, wgb[slot, :d2]) + _dot(hi, wgb[slot, d2:])
        u = _dot(lo, wub[slot, :d2]) + _dot(hi, wub[slot, d2:])
        hmid = (g * jax.nn.sigmoid(g) * u).astype(BF16)
        ys_ref[...] = _pack_halves(_dot(hmid, wdb[slot]))

    @pl.when(i >= n_used)
    def _():
        ys_ref[...] = jnp.zeros(ys_ref.shape, ys_ref.dtype)


def _moe_ffn(xs, w_gate, w_up, w_down, block_expert, n_used):
    rows, d2 = xs.shape
    nb = rows // MOE_BLOCK
    d = 2 * d2
    ff = w_gate.shape[2]
    grid_spec = pltpu.PrefetchScalarGridSpec(
        num_scalar_prefetch=2,
        grid=(nb,),
        in_specs=[pl.BlockSpec((MOE_BLOCK, d2), lambda i, be, nu: (i, 0)),
                  pl.BlockSpec(memory_space=pl.ANY),
                  pl.BlockSpec(memory_space=pl.ANY),
                  pl.BlockSpec(memory_space=pl.ANY)],
        out_specs=pl.BlockSpec((MOE_BLOCK, d2), lambda i, be, nu: (i, 0)),
        scratch_shapes=[pltpu.VMEM((2, d, ff), BF16), pltpu.VMEM((2, d, ff), BF16),
                        pltpu.VMEM((2, ff, d), BF16),
                        pltpu.VMEM((PIECE_RING, d // MAT_PIECES, ff), F32),
                        pltpu.VMEM((PIECE_RING, ff // MAT_PIECES, d), F32),
                        pltpu.SemaphoreType.DMA((PIECE_RING,)),
                        pltpu.SMEM((4,), I32)],
    )
    return pl.pallas_call(
        _moe_ffn_kernel,
        grid_spec=grid_spec,
        out_shape=jax.ShapeDtypeStruct((rows, d2), I32),
        compiler_params=_cparams(("arbitrary",)),
        name="moe_ffn",
    )(block_expert, n_used, xs, w_gate, w_up, w_down)


def _moe_combine_kernel(p1_ref, p2_ref, q1_ref, q2_ref, x_ref, g1_ref, g2_ref, g_ref, b_ref, ys_hbm,
                        o_ref, yb, sem):
    i = pl.program_id(0)
    n = pl.num_programs(0)
    slot = lax.rem(i, 2)
    blk = x_ref.shape[0]

    def row_copy(s, k, r, pos):
        return pltpu.make_async_copy(ys_hbm.at[pl.ds(pos, 1)], yb.at[s, k, pl.ds(r, 1)], sem.at[s])

    def fetch(s, a_ref, b_ref2):
        def body(r, c):
            row_copy(s, 0, r, a_ref[0, 0, r]).start()
            row_copy(s, 1, r, b_ref2[0, 0, r]).start()
            return c
        lax.fori_loop(0, blk, body, 0, unroll=8)

    @pl.when(i == 0)
    def _():
        fetch(0, p1_ref, p2_ref)

    @pl.when(i + 1 < n)
    def _():
        fetch(1 - slot, q1_ref, q2_ref)

    for k in range(2):
        pltpu.make_async_copy(ys_hbm.at[pl.ds(0, blk)], yb.at[slot, k], sem.at[slot]).wait()
    y1 = jnp.concatenate(_unpack_halves(yb[slot, 0]), axis=1)
    y2 = jnp.concatenate(_unpack_halves(yb[slot, 1]), axis=1)
    y = y1 * g1_ref[...] + y2 * g2_ref[...]
    o_ref[...] = _ln_rows(ALPHA * x_ref[...] + y, g_ref[...], b_ref[...])


def _moe_combine(x2, ys, p1, p2, g1, g2, g, b):
    t, d = x2.shape
    nt = t // MOE_BLOCK
    p1 = p1.reshape(nt, 1, MOE_BLOCK)
    p2 = p2.reshape(nt, 1, MOE_BLOCK)
    pos = pl.BlockSpec((1, 1, MOE_BLOCK), lambda i: (i, 0, 0), memory_space=pltpu.SMEM)
    nxt = pl.BlockSpec((1, 1, MOE_BLOCK), lambda i: (jnp.minimum(i + 1, nt - 1), 0, 0),
                       memory_space=pltpu.SMEM)
    col = pl.BlockSpec((MOE_BLOCK, 1), lambda i: (i, 0))
    vec = pl.BlockSpec((1, d), lambda i: (0, 0))
    return pl.pallas_call(
        _moe_combine_kernel,
        grid=(nt,),
        in_specs=[pos, pos, nxt, nxt, pl.BlockSpec((MOE_BLOCK, d), lambda i: (i, 0)), col, col, vec, vec,
                  pl.BlockSpec(memory_space=pl.ANY)],
        out_specs=pl.BlockSpec((MOE_BLOCK, d), lambda i: (i, 0)),
        out_shape=jax.ShapeDtypeStruct((t, d), F32),
        scratch_shapes=[pltpu.VMEM((2, 2, MOE_BLOCK, d // 2), I32),
                        pltpu.SemaphoreType.DMA((2,))],
        compiler_params=_cparams(("arbitrary",)),
        name="moe_combine",
    )(p1, p2, p1, p2, x2, g1, g2, g.reshape(1, d), b.reshape(1, d), ys)


def _tile(n, pref):
    return pref if n % pref == 0 else n


def _layer(x, mem, w_in, conv_w, conv_b, conv_ln_g, conv_ln_b, kv_norm_g, w_uk, w_uv, rel_bias,
           conv_out_g, attn_out_g, w_out, ln1_g, ln1_b, w_mq, w_mk, w_mv, w_mo, ln2_g, ln2_b,
           w_router_grp, w_router_exp, w_gate, w_up, w_down, ln3_g, ln3_b):
    bsz, seq, d = x.shape
    t = bsz * seq
    d_conv = conv_w.shape[1]
    d_attn = N_HEADS * HEAD_DIM
    c_glu = 2 * d_conv
    c_qi = H_IDX * D_IDX
    o_q, o_kv = c_glu, c_glu + d_attn
    o_qi = o_kv + KV_RANK
    o_ki = o_qi + c_qi
    topk = min(TOPK_MAX, seq // 4)

    xf = x.reshape(t, d)
    xb = xf.astype(BF16)
    w_inb = w_in.astype(BF16)
    tail_w = jnp.concatenate([w_inb[:, o_kv:o_qi], w_inb[:, o_ki:]], axis=1)
    tail_w = jnp.pad(tail_w, ((0, 0), (0, KV_RANK + LANES - tail_w.shape[1])))
    tm = _tile(t, 1024)

    u = _matmul_cols(xb, w_inb, c_glu, BF16, tm, 512, "proj_glu")
    q_hm = _matmul_heads(xb, w_inb, o_q, d_attn, BF16, tm, 512, "proj_q")
    qi_hm = _matmul_heads(xb, w_inb, o_qi, c_qi, BF16, tm, 512, "proj_qidx")
    tail = _matmul(xb, tail_w, F32, tm, KV_RANK + LANES, "proj_tail")

    conv_n = _conformer_conv(u, bsz, seq, conv_w, conv_b, conv_ln_g, conv_ln_b, conv_out_g)
    tk = min(ATT_BLOCK, seq)
    ckv_n, ckv_t, kia, kib, kw = _prep_latent(tail, kv_norm_g, tm, tk)
    ckv_t = ckv_t.reshape(bsz, seq // tk, KV_RANK, tk)
    mask = _indexer_mask(qi_hm, kw, kia, kib, bsz, seq, topk)
    w_uk_t = w_uk.transpose(0, 2, 1).reshape(d_attn, KV_RANK).astype(BF16)
    w_uv_all = w_uv.transpose(1, 0, 2).reshape(KV_RANK, d_attn).astype(BF16)
    k_t = _key_heads(w_uk_t, ckv_t)
    v_hm = _matmul_value_heads(ckv_n, w_uv_all, tm, 512, "value_heads")
    attn = _head_attention(q_hm, k_t, v_hm, mask, rel_bias, bsz, seq)

    pre1 = _matmul2_residual(conv_n, attn, attn_out_g, w_out.astype(BF16), xf, tm, 512, "out_proj")

    mem_len = mem.shape[1]
    memb = mem.reshape(bsz * mem_len, d).astype(BF16)
    w_kv = jnp.concatenate([w_mk, w_mv], axis=1).astype(BF16)
    kv = _matmul(memb, w_kv, BF16, _tile(bsz * mem_len, 512), 512, "mem_kv")
    w_router = jnp.concatenate([w_router_grp, w_router_exp], axis=1)
    w_router = jnp.pad(w_router, ((0, 0), (0, LANES - w_router.shape[1]))).astype(BF16)
    x2, logits = _memory_attention(pre1, ln1_g, ln1_b, kv, w_mq.astype(BF16), w_mo.astype(BF16),
                                   ln2_g, ln2_b, w_router, bsz, seq, 256)

    e1, e2, g1, g2 = _router(logits, _tile(t, 1024))
    nb = (2 * t + N_EXPERTS * (MOE_BLOCK - 1) + MOE_BLOCK - 1) // MOE_BLOCK
    r1, r2, cnt = _moe_rank(e1, e2, 512)
    p1, p2, block_expert, n_used = _moe_place(e1, e2, r1, r2, cnt, nb, 512)
    xs = _moe_dispatch(x2, p1, p2, n_used, nb)
    ys = _moe_ffn(xs, w_gate, w_up, w_down, block_expert.reshape(nb), n_used[0, 0:1])
    x3 = _moe_combine(x2, ys, p1, p2, g1, g2, ln3_g, ln3_b)
    return x3.reshape(bsz, seq, d)


def kernel(x, mem, w_in, conv_w, conv_b, conv_ln_g, conv_ln_b, kv_norm_g, w_uk, w_uv, rel_bias, conv_out_g, attn_out_g, w_out, ln1_g, ln1_b, w_mq, w_mk, w_mv, w_mo, ln2_g, ln2_b, w_router_grp, w_router_exp, w_gate, w_up, w_down, ln3_g, ln3_b):
    for l in range(w_in.shape[0]):
        x = _layer(x, mem, w_in[l], conv_w[l], conv_b[l], conv_ln_g[l], conv_ln_b[l], kv_norm_g[l],
                   w_uk[l], w_uv[l], rel_bias, conv_out_g[l], attn_out_g[l], w_out[l], ln1_g[l], ln1_b[l],
                   w_mq[l], w_mk[l], w_mv[l], w_mo[l], ln2_g[l], ln2_b[l], w_router_grp[l],
                   w_router_exp[l], w_gate[l], w_up[l], w_down[l], ln3_g[l], ln3_b[l])
    return x
```

```python
import functools
import math

import jax
import jax.numpy as jnp
from jax import lax
from jax.experimental import pallas as pl
from jax.experimental.pallas import tpu as pltpu

F32 = jnp.float32
BF16 = jnp.bfloat16
I32 = jnp.int32

DEPTH = 1
CONV_WIDTH = 31
N_HEADS = 16
HEAD_DIM = 128
KV_RANK = 512
H_IDX = 32
D_IDX = 64
TOPK_MAX = 256
N_BUCKETS = 32
MAX_DIST = 128
MEM_HEADS = 4
MEM_HEAD_DIM = 128
N_GROUPS = 8
EXP_PER_GROUP = 8
N_EXPERTS = N_GROUPS * EXP_PER_GROUP
MOE_BLOCK = 128
ALPHA = (2.0 * DEPTH) ** 0.25
LN_EPS = 1e-5

LANES = 128
SUBLANES = 8
V7X_VMEM_BYTES = 64 * 1024 * 1024
VMEM_LIMIT = 56 * 1024 * 1024
NEG = -1e30
INT_MIN = -(2 ** 31)

LOG2E = 1.4426950408889634

ATT_BLOCK = 256
ATT_HEAD_GROUP = 8
CONV_TS = 256
CONV_HALO = 32
CONV_CC = 256
CONV_RC = 32


def _cparams(sem):
    return pltpu.CompilerParams(dimension_semantics=sem, vmem_limit_bytes=VMEM_LIMIT)


def _dot(a, b):
    return jnp.dot(a, b, preferred_element_type=F32)


def _dot_nt(a, b):
    return lax.dot_general(a, b, (((1,), (1,)), ((), ())), preferred_element_type=F32)


def _mm_kernel(a_ref, b_ref, o_ref):
    o_ref[...] = _dot(a_ref[...], b_ref[...]).astype(o_ref.dtype)


def _mm_cast_kernel(a_ref, b_ref, o_ref, ab_ref):
    ab = a_ref[...].astype(ab_ref.dtype)
    ab_ref[...] = ab
    o_ref[...] = _dot(ab, b_ref[...]).astype(o_ref.dtype)


def _matmul_cast(a, b, out_dtype, tm, name):
    m, k = a.shape
    n = b.shape[1]
    return pl.pallas_call(
        _mm_cast_kernel,
        grid=(m // tm,),
        in_specs=[pl.BlockSpec((tm, k), lambda i: (i, 0)),
                  pl.BlockSpec((k, n), lambda i: (0, 0))],
        out_specs=[pl.BlockSpec((tm, n), lambda i: (i, 0)),
                   pl.BlockSpec((tm, k), lambda i: (i, 0))],
        out_shape=[jax.ShapeDtypeStruct((m, n), out_dtype), jax.ShapeDtypeStruct((m, k), b.dtype)],
        compiler_params=_cparams(("parallel",)),
        name=name,
    )(a, b)


def _matmul(a, b, out_dtype, tm, tn, name):
    m, k = a.shape
    n = b.shape[1]
    return pl.pallas_call(
        _mm_kernel,
        grid=(m // tm, n // tn),
        in_specs=[pl.BlockSpec((tm, k), lambda i, j: (i, 0)),
                  pl.BlockSpec((k, tn), lambda i, j: (0, j))],
        out_specs=pl.BlockSpec((tm, tn), lambda i, j: (i, j)),
        out_shape=jax.ShapeDtypeStruct((m, n), out_dtype),
        compiler_params=_cparams(("parallel", "parallel")),
        name=name,
    )(a, b)


def _matmul_cols(a, b, n, out_dtype, tm, tn, name):
    m, k = a.shape
    return pl.pallas_call(
        _mm_kernel,
        grid=(m // tm, n // tn),
        in_specs=[pl.BlockSpec((tm, k), lambda i, j: (i, 0)),
                  pl.BlockSpec((k, tn), lambda i, j: (0, j))],
        out_specs=pl.BlockSpec((tm, tn), lambda i, j: (i, j)),
        out_shape=jax.ShapeDtypeStruct((m, n), out_dtype),
        compiler_params=_cparams(("parallel", "parallel")),
        name=name,
    )(a, b)


def _mm_heads_kernel(a_ref, b_ref, o_ref):
    r = _dot(a_ref[...], b_ref[...])
    for p in range(o_ref.shape[0]):
        o_ref[p] = r[:, p * LANES:(p + 1) * LANES].astype(o_ref.dtype)


def _matmul_heads(a, b, col0, n, out_dtype, tm, tn, name):
    m, k = a.shape
    assert col0 % tn == 0
    return pl.pallas_call(
        _mm_heads_kernel,
        grid=(m // tm, n // tn),
        in_specs=[pl.BlockSpec((tm, k), lambda i, j: (i, 0)),
                  pl.BlockSpec((k, tn), lambda i, j: (0, col0 // tn + j))],
        out_specs=pl.BlockSpec((tn // LANES, tm, LANES), lambda i, j: (j, i, 0)),
        out_shape=jax.ShapeDtypeStruct((n // LANES, m, LANES), out_dtype),
        compiler_params=_cparams(("parallel", "parallel")),
        name=name,
    )(a, b)


def _mm_value_heads_kernel(a_ref, b_ref, o_ref):
    r = _dot(a_ref[...], b_ref[...])
    tm = r.shape[0]
    for p in range(o_ref.shape[0]):
        o_ref[p, :, 0:LANES] = r[:, p * LANES:(p + 1) * LANES].astype(o_ref.dtype)
        o_ref[p, :, LANES:2 * LANES] = jnp.ones((tm, LANES), o_ref.dtype)


def _matmul_value_heads(a, b, tm, tn, name):
    m, k = a.shape
    n = b.shape[1]
    return pl.pallas_call(
        _mm_value_heads_kernel,
        grid=(m // tm, n // tn),
        in_specs=[pl.BlockSpec((tm, k), lambda i, j: (i, 0)),
                  pl.BlockSpec((k, tn), lambda i, j: (0, j))],
        out_specs=pl.BlockSpec((tn // LANES, tm, 2 * LANES), lambda i, j: (j, i, 0)),
        out_shape=jax.ShapeDtypeStruct((n // LANES, m, 2 * LANES), BF16),
        compiler_params=_cparams(("parallel", "parallel")),
        name=name,
    )(a, b)


def _mm2_res_kernel(a1_ref, a2_ref, g2_ref, w1_ref, w2_ref, r_ref, o_ref, a2n):
    @pl.when(pl.program_id(1) == 0)
    def _():
        x = a2_ref[...].astype(F32)
        ms = jnp.mean(x * x, axis=-1, keepdims=True)
        a2n[...] = (x * lax.rsqrt(ms + LN_EPS) * g2_ref[...]).astype(a2n.dtype)

    o_ref[...] = (ALPHA * r_ref[...] + _dot(a1_ref[...], w1_ref[...])
                  + _dot(a2n[...], w2_ref[...]))


def _matmul2_residual(a1, a2, g2, w, res, tm, tn, name):
    m, k = a1.shape
    assert a2.shape == a1.shape and w.shape[0] == 2 * k
    n = w.shape[1]
    return pl.pallas_call(
        _mm2_res_kernel,
        grid=(m // tm, n // tn),
        in_specs=[pl.BlockSpec((tm, k), lambda i, j: (i, 0)),
                  pl.BlockSpec((tm, k), lambda i, j: (i, 0)),
                  pl.BlockSpec((1, k), lambda i, j: (0, 0)),
                  pl.BlockSpec((k, tn), lambda i, j: (0, j)),
                  pl.BlockSpec((k, tn), lambda i, j: (1, j)),
                  pl.BlockSpec((tm, tn), lambda i, j: (i, j))],
        out_specs=pl.BlockSpec((tm, tn), lambda i, j: (i, j)),
        out_shape=jax.ShapeDtypeStruct((m, n), F32),
        scratch_shapes=[pltpu.VMEM((tm, k), BF16)],
        compiler_params=_cparams(("parallel", "arbitrary")),
        name=name,
    )(a1, a2, g2.reshape(1, k), w, w, res)


def _ln_rows(x, g, b):
    mu = jnp.mean(x, axis=-1, keepdims=True)
    xc = x - mu
    var = jnp.mean(xc * xc, axis=-1, keepdims=True)
    return xc * lax.rsqrt(var + LN_EPS) * g + b


def _conv_kernel(a_ref, g_ref, cw_ref, cb_ref, lg_ref, lb_ref, og_ref, o_ref, hbuf, ybuf, hs):
    ts = a_ref.shape[0]
    nch = hbuf.shape[0]
    cc = hbuf.shape[2]
    d_conv = nch * cc

    nrow = CONV_HALO + ts

    @pl.when(pl.program_id(1) == 0)
    def _():
        hbuf[:, 0:CONV_HALO, :] = jnp.zeros((nch, CONV_HALO, cc), F32)
        hbuf[:, nrow:nrow + SUBLANES, :] = jnp.zeros((nch, SUBLANES, cc), F32)

    for c in range(nch):
        a = a_ref[:, c * cc:(c + 1) * cc].astype(F32)
        g = g_ref[:, c * cc:(c + 1) * cc].astype(F32)
        hbuf[c, CONV_HALO:nrow, :] = a * jax.nn.sigmoid(g)

    first = CONV_HALO - (CONV_WIDTH - 1)

    def chunk_body(c, carry):
        for o in range(1, SUBLANES):
            hs[o - 1] = hbuf[c, o:o + nrow, :]
        for r0 in range(0, ts, CONV_RC):
            acc = jnp.zeros((CONV_RC, cc), F32)
            for j in range(CONV_WIDTH):
                o = (first + j) % SUBLANES
                base = r0 + first + j - o
                rows = hbuf[c, base:base + CONV_RC, :] if o == 0 else hs[o - 1, base:base + CONV_RC, :]
                acc = acc + cw_ref[c, j:j + 1, :] * rows
            ybuf[c, r0:r0 + CONV_RC, :] = acc + cb_ref[c]
        hbuf[c, 0:CONV_HALO, :] = hbuf[c, ts:nrow, :]
        return carry

    lax.fori_loop(0, nch, chunk_body, 0)

    s1 = jnp.zeros((ts, 1), F32)
    for c in range(nch):
        s1 = s1 + jnp.sum(ybuf[c], axis=1, keepdims=True)
    mu = s1 * (1.0 / d_conv)
    s2 = jnp.zeros((ts, 1), F32)
    for c in range(nch):
        yc = ybuf[c] - mu
        s2 = s2 + jnp.sum(yc * yc, axis=1, keepdims=True)
    rstd = lax.rsqrt(s2 * (1.0 / d_conv) + LN_EPS)
    s3 = jnp.zeros((ts, 1), F32)
    for c in range(nch):
        z = (ybuf[c] - mu) * rstd * lg_ref[:, c * cc:(c + 1) * cc] + lb_ref[:, c * cc:(c + 1) * cc]
        z = z * jax.nn.sigmoid(z)
        ybuf[c] = z
        s3 = s3 + jnp.sum(z * z, axis=1, keepdims=True)
    rr = lax.rsqrt(s3 * (1.0 / d_conv) + LN_EPS)
    for c in range(nch):
        o_ref[:, c * cc:(c + 1) * cc] = (ybuf[c] * rr * og_ref[:, c * cc:(c + 1) * cc]).astype(o_ref.dtype)


def _conformer_conv(u, bsz, seq, conv_w, conv_b, ln_g, ln_b, out_g):
    d_conv = u.shape[1] // 2
    ts = min(CONV_TS, seq)
    nch = d_conv // CONV_CC
    nt = seq // ts
    cw = conv_w.reshape(CONV_WIDTH, nch, CONV_CC).transpose(1, 0, 2)
    cb = conv_b.reshape(nch, 1, CONV_CC)
    vec = pl.BlockSpec((1, d_conv), lambda b, i: (0, 0))
    return pl.pallas_call(
        _conv_kernel,
        grid=(bsz, nt),
        in_specs=[pl.BlockSpec((ts, d_conv), lambda b, i: (b * nt + i, 0)),
                  pl.BlockSpec((ts, d_conv), lambda b, i: (b * nt + i, 1)),
                  pl.BlockSpec((nch, CONV_WIDTH, CONV_CC), lambda b, i: (0, 0, 0)),
                  pl.BlockSpec((nch, 1, CONV_CC), lambda b, i: (0, 0, 0)),
                  vec, vec, vec],
        out_specs=pl.BlockSpec((ts, d_conv), lambda b, i: (b * nt + i, 0)),
        out_shape=jax.ShapeDtypeStruct((bsz * seq, d_conv), BF16),
        scratch_shapes=[pltpu.VMEM((nch, CONV_HALO + ts + SUBLANES, CONV_CC), F32),
                        pltpu.VMEM((nch, ts, CONV_CC), F32),
                        pltpu.VMEM((SUBLANES - 1, CONV_HALO + ts, CONV_CC), F32)],
        compiler_params=_cparams(("arbitrary", "arbitrary")),
        name="conformer_conv",
    )(u, u, cw, cb, ln_g.reshape(1, d_conv), ln_b.reshape(1, d_conv), out_g.reshape(1, d_conv))


def _prep_kernel(t_ref, g_ref, ckv_ref, ckvt_ref, kia_ref, kib_ref, kw_ref):
    ckv = t_ref[:, 0:KV_RANK]
    ms = jnp.mean(ckv * ckv, axis=-1, keepdims=True)
    ckv_n = ckv * lax.rsqrt(ms + LN_EPS) * g_ref[...]
    ckv_ref[...] = ckv_n.astype(ckv_ref.dtype)
    tk = ckvt_ref.shape[3]
    for c in range(ckvt_ref.shape[1]):
        ckvt_ref[0, c] = ckv_n[c * tk:(c + 1) * tk, :].T.astype(ckvt_ref.dtype)
    kw = t_ref[:, KV_RANK:KV_RANK + LANES]
    kw_ref[...] = kw
    lane = lax.broadcasted_iota(I32, kw.shape, 1)
    kia_ref[...] = jnp.where(lane < D_IDX, kw, 0.0).astype(kia_ref.dtype)
    kib_ref[...] = jnp.where(lane >= D_IDX, pltpu.roll(kw, D_IDX, 1), 0.0).astype(kib_ref.dtype)


def _prep_latent(tail, kv_norm_g, tm, tk):
    m, w = tail.shape
    return pl.pallas_call(
        _prep_kernel,
        grid=(m // tm,),
        in_specs=[pl.BlockSpec((tm, w), lambda i: (i, 0)),
                  pl.BlockSpec((1, KV_RANK), lambda i: (0, 0))],
        out_specs=[pl.BlockSpec((tm, KV_RANK), lambda i: (i, 0)),
                   pl.BlockSpec((1, tm // tk, KV_RANK, tk), lambda i: (i, 0, 0, 0)),
                   pl.BlockSpec((tm, LANES), lambda i: (i, 0)),
                   pl.BlockSpec((tm, LANES), lambda i: (i, 0)),
                   pl.BlockSpec((tm, LANES), lambda i: (i, 0))],
        out_shape=[jax.ShapeDtypeStruct((m, KV_RANK), BF16),
                   jax.ShapeDtypeStruct((m // tm, tm // tk, KV_RANK, tk), BF16),
                   jax.ShapeDtypeStruct((m, LANES), BF16),
                   jax.ShapeDtypeStruct((m, LANES), BF16),
                   jax.ShapeDtypeStruct((m, LANES), F32)],
        compiler_params=_cparams(("parallel",)),
        name="prep_latent",
    )(tail, kv_norm_g.reshape(1, KV_RANK))


def _indexer_t_kernel(qi_ref, kw_ref, kia_ref, kib_ref, o_ref, keybuf, *, topk):
    i = pl.program_id(1)
    npairs, tq, _ = qi_ref.shape
    nk = o_ref.shape[1]
    tk = o_ref.shape[3]
    kf = float(topk)
    group = 4
    slabs = tk // SUBLANES

    w_t = kw_ref[...].T
    key_idx = lax.broadcasted_iota(I32, (tk, tq), 0)
    qry_idx = lax.broadcasted_iota(I32, (tk, tq), 1) + i * tq

    def score_chunk(j, carry):
        k0 = pl.multiple_of(j * tk, tk)
        kd = jnp.concatenate([kia_ref[pl.ds(k0, tk), :], kib_ref[pl.ds(k0, tk), :]], axis=0)
        acc = jnp.zeros((tk, tq), F32)
        for p0 in range(0, npairs, group):
            rhs = qi_ref[p0:p0 + group].reshape(group * tq, LANES)
            zz = _dot_nt(kd, rhs)
            for p in range(group):
                h = 2 * (p0 + p)
                z = zz[:, p * tq:(p + 1) * tq]
                acc = (acc + w_t[D_IDX + h:D_IDX + h + 1, :] * jnp.maximum(z[0:tk], 0.0)
                       + w_t[D_IDX + h + 1:D_IDX + h + 2, :] * jnp.maximum(z[tk:2 * tk], 0.0))
        bits = pltpu.bitcast(acc, I32)
        key = jnp.where(bits >= 0, bits, bits ^ jnp.int32(0x7FFFFFFF))
        keybuf[j] = jnp.where(key_idx + j * tk <= qry_idx, key, INT_MIN)
        return carry

    def chunk_pairs(fn):
        def pair(p, carry):
            fn(2 * p, carry)
            fn(2 * p + 1, carry)
            return carry

        lax.fori_loop(0, lax.shift_right_logical(i + 1, 1), pair, 0)

        @pl.when((i & 1) == 0)
        def _():
            fn(i, 0)

    chunk_pairs(score_chunk)

    def count(pred):
        def one(j, c):
            hit = jnp.where(pred(keybuf[j], key_idx + j * tk), 1.0, 0.0)
            return c + jnp.sum(hit.reshape(slabs, SUBLANES, tq), axis=0)

        def pair(p, c):
            return one(2 * p + 1, one(2 * p, c))

        c = lax.fori_loop(0, lax.shift_right_logical(i + 1, 1), pair, jnp.zeros((SUBLANES, tq), F32))
        c = lax.cond((i & 1) == 0, lambda c: one(i, c), lambda c: c, c)
        return jnp.broadcast_to(jnp.sum(c, axis=0, keepdims=True), (SUBLANES, tq))

    def tile_rows(v):
        return jnp.concatenate([v] * slabs, axis=0)

    def count_ge(cand):
        cb = tile_rows(cand)
        return count(lambda kk, idx: kk >= cb)

    tau = jnp.where(count_ge(jnp.zeros((SUBLANES, tq), I32)) >= kf, 0, INT_MIN).astype(I32)

    def bit_body(it, tau):
        cand = tau | jnp.left_shift(jnp.int32(1), 30 - it)
        return jnp.where(count_ge(cand) >= kf, cand, tau)

    tau = lax.fori_loop(0, 31, bit_body, tau)
    n_ge = count_ge(tau)
    taub = tile_rows(tau)

    def tie_cut():
        need = kf - count(lambda kk, idx: kk > taub)

        def cut_body(it, cut):
            cand = cut + jnp.left_shift(jnp.int32(1), 30 - it)
            cb = tile_rows(cand)
            below = count(lambda kk, idx: (kk == taub) & (idx < cb))
            return jnp.where(below < need, cand, cut)

        return lax.fori_loop(0, 31, cut_body, jnp.zeros((SUBLANES, tq), I32))

    has_ties = jnp.max(n_ge) > kf
    cut = lax.cond(has_ties, tie_cut, lambda: jnp.full((SUBLANES, tq), 2 ** 30, I32))
    cutb = tile_rows(cut)
    eye = jnp.where(lax.broadcasted_iota(I32, (tq, tq), 0) == lax.broadcasted_iota(I32, (tq, tq), 1),
                    1.0, 0.0).astype(BF16)

    def write_chunk(j, carry):
        kk = keybuf[j]
        sel = ((kk > taub) | ((kk == taub) & (key_idx + j * tk <= cutb))) & (kk != INT_MIN)
        sel_qk = _dot_nt(eye, jnp.where(sel, 1.0, 0.0).astype(BF16))
        o_ref[0, j] = jnp.where(sel_qk > 0.5, 0.0, NEG).astype(o_ref.dtype)
        return carry

    chunk_pairs(write_chunk)

    def write_rest(j, carry):
        o_ref[0, j] = jnp.full((tq, tk), NEG, o_ref.dtype)
        return carry

    lax.fori_loop(i + 1, nk, write_rest, 0)


def _indexer_mask(qi_hm, kw, kia, kib, bsz, seq, topk):
    tq = min(ATT_BLOCK, seq)
    nq = seq // tq
    npairs = qi_hm.shape[0]
    return pl.pallas_call(
        functools.partial(_indexer_t_kernel, topk=topk),
        grid=(bsz, nq),
        in_specs=[pl.BlockSpec((npairs, tq, LANES), lambda b, i: (0, b * nq + i, 0)),
                  pl.BlockSpec((tq, LANES), lambda b, i: (b * nq + i, 0)),
                  pl.BlockSpec((seq, LANES), lambda b, i: (b, 0)),
                  pl.BlockSpec((seq, LANES), lambda b, i: (b, 0))],
        out_specs=pl.BlockSpec((1, nq, tq, tq), lambda b, i: (b * nq + i, 0, 0, 0)),
        out_shape=jax.ShapeDtypeStruct((bsz * nq, nq, tq, tq), BF16),
        scratch_shapes=[pltpu.VMEM((nq, tq, tq), I32)],
        compiler_params=_cparams(("parallel", "parallel")),
        name="indexer_mask",
    )(qi_hm, kw, kia, kib)


def _t5_bucket(dist):
    n = jnp.maximum(dist, 0)
    max_exact = N_BUCKETS // 2
    nf = jnp.maximum(n, 1).astype(F32)
    large = max_exact + (jnp.log(nf / max_exact) / math.log(MAX_DIST / max_exact)
                         * (N_BUCKETS - max_exact)).astype(I32)
    large = jnp.minimum(large, N_BUCKETS - 1)
    return jnp.where(n < max_exact, n, large)


def _key_heads_kernel(wt_ref, ct_ref, o_ref):
    r = _dot(wt_ref[...], ct_ref[0, 0]) * (HEAD_DIM ** -0.5 * LOG2E)
    for h in range(o_ref.shape[1]):
        o_ref[0, h, 0] = r[h * HEAD_DIM:(h + 1) * HEAD_DIM].astype(o_ref.dtype)


def _key_heads(w_uk_t, ckv_t):
    bsz, nk, r_lat, tk = ckv_t.shape
    nh = w_uk_t.shape[0] // HEAD_DIM
    return pl.pallas_call(
        _key_heads_kernel,
        grid=(bsz, nk),
        in_specs=[pl.BlockSpec((nh * HEAD_DIM, r_lat), lambda b, j: (0, 0)),
                  pl.BlockSpec((1, 1, r_lat, tk), lambda b, j: (b, j, 0, 0))],
        out_specs=pl.BlockSpec((1, nh, 1, HEAD_DIM, tk), lambda b, j: (b, 0, j, 0, 0)),
        out_shape=jax.ShapeDtypeStruct((bsz, nh, nk, HEAD_DIM, tk), BF16),
        compiler_params=_cparams(("parallel", "parallel")),
        name="key_heads",
    )(w_uk_t, ckv_t)


def _attn_kernel(relb_ref, q_ref, kt_ref, v_ref, mask_ref, o_ref, acc, m_s, btab):
    b = pl.program_id(0)
    g = pl.program_id(1)
    i = pl.program_id(2)
    hg, tq, _ = q_ref.shape
    nh = btab.shape[1]
    tk = mask_ref.shape[3]

    @pl.when((b == 0) & (g == 0) & (i == 0))
    def _():
        r = lax.broadcasted_iota(I32, (tq, tk), 0)
        c = lax.broadcasted_iota(I32, (tq, tk), 1)
        for t in range(2):
            bk = _t5_bucket(r - c + t * tk)

            def fill(h, carry):
                far = relb_ref[N_BUCKETS - 1, h]
                v = jnp.zeros((tq, tk), F32)
                for k in range(N_BUCKETS):
                    v = jnp.where(bk == k, (relb_ref[k, h] - far) * LOG2E, v)
                btab[t, h] = v
                return carry

            lax.fori_loop(0, nh, fill, 0)

    acc[...] = jnp.zeros(acc.shape, F32)
    m_s[...] = jnp.full(m_s.shape, NEG, F32)

    def lane_tile(v, n):
        return jnp.concatenate([v] * n, axis=1)

    def chunks(js, near):
        mks = [mask_ref[0, j].astype(F32) for j in js]
        for h in range(hg):
            for j, mk in zip(js, mks):
                s = _dot(q_ref[h], kt_ref[0, h, j]) + mk
                if near:
                    s = s + btab[i - j, g * hg + h]
                m_old = m_s[h]
                m_new = jnp.maximum(m_old, jnp.max(s, axis=1, keepdims=True))
                a = jnp.exp2(m_old - m_new)
                p = jnp.exp2(s - lane_tile(m_new, tk // LANES))
                rows = pl.ds(pl.multiple_of(j * tk, tk), tk)
                acc[h] = lane_tile(a, 2) * acc[h] + _dot(p.astype(BF16), v_ref[h, rows, :])
                m_s[h] = m_new

    n_far = jnp.maximum(i - 1, 0)

    def far_quad(p, carry):
        chunks([4 * p, 4 * p + 1, 4 * p + 2, 4 * p + 3], False)
        return carry

    lax.fori_loop(0, lax.shift_right_logical(n_far, 2), far_quad, 0)
    done = n_far & ~3

    @pl.when((n_far & 2) == 2)
    def _():
        chunks([done, done + 1], False)

    @pl.when((n_far & 1) == 1)
    def _():
        chunks([n_far - 1], False)

    @pl.when(i >= 1)
    def _():
        chunks([i - 1, i], True)

    @pl.when(i == 0)
    def _():
        chunks([i], True)

    for h in range(hg):
        o_ref[:, h * HEAD_DIM:(h + 1) * HEAD_DIM] = (
            acc[h, :, :HEAD_DIM] / acc[h, :, HEAD_DIM:]).astype(o_ref.dtype)


def _head_attention(q_hm, k_t, v_hm, mask, rel_bias, bsz, seq):
    nh = q_hm.shape[0]
    hg = ATT_HEAD_GROUP
    tq = mask.shape[2]
    nq = seq // tq
    assert tq == mask.shape[3] and tq >= MAX_DIST
    once = pl.Buffered(1)
    return pl.pallas_call(
        _attn_kernel,
        grid=(bsz, nh // hg, nq),
        in_specs=[pl.BlockSpec(memory_space=pltpu.SMEM),
                  pl.BlockSpec((hg, tq, HEAD_DIM), lambda b, g, i: (g, b * nq + i, 0)),
                  pl.BlockSpec((1, hg, nq, HEAD_DIM, tq), lambda b, g, i: (b, g, 0, 0, 0), pipeline_mode=once),
                  pl.BlockSpec((hg, seq, 2 * HEAD_DIM), lambda b, g, i: (g, b, 0), pipeline_mode=once),
                  pl.BlockSpec((1, nq, tq, tq), lambda b, g, i: (b * nq + i, 0, 0, 0))],
        out_specs=pl.BlockSpec((tq, hg * HEAD_DIM), lambda b, g, i: (b * nq + i, g)),
        out_shape=jax.ShapeDtypeStruct((bsz * seq, nh * HEAD_DIM), BF16),
        scratch_shapes=[pltpu.VMEM((hg, tq, 2 * HEAD_DIM), F32),
                        pltpu.VMEM((hg, tq, LANES), F32),
                        pltpu.VMEM((2, nh, tq, tq), F32)],
        compiler_params=_cparams(("arbitrary", "arbitrary", "arbitrary")),
        name="head_attention",
    )(rel_bias, q_hm, k_t, v_hm, mask)


def _memattn_kernel(x_ref, g1_ref, b1_ref, kv_ref, wq_ref, wo_ref, g_ref, b_ref, wr_ref, o_ref, lg_ref):
    x = _ln_rows(x_ref[...], g1_ref[...], b1_ref[...])
    d_mem = MEM_HEADS * MEM_HEAD_DIM
    q = (_dot(x.astype(BF16), wq_ref[...]) * (MEM_HEAD_DIM ** -0.5)).astype(BF16)
    outs = []
    for h in range(MEM_HEADS):
        lo = h * MEM_HEAD_DIM
        k = kv_ref[:, lo:lo + MEM_HEAD_DIM]
        v = kv_ref[:, d_mem + lo:d_mem + lo + MEM_HEAD_DIM]
        s = _dot_nt(q[:, lo:lo + MEM_HEAD_DIM], k)
        p = jnp.exp(s - jnp.max(s, axis=1, keepdims=True))
        p = p / jnp.sum(p, axis=1, keepdims=True)
        outs.append(_dot(p.astype(BF16), v).astype(BF16))
    o = jnp.concatenate(outs, axis=1)
    x2 = _ln_rows(ALPHA * x + _dot(o, wo_ref[...]), g_ref[...], b_ref[...])
    o_ref[...] = x2
    lg_ref[...] = _dot(x2.astype(BF16), wr_ref[...])


def _memory_attention(pre1, g1, b1, kv, w_mq, w_mo, g, b, w_router, bsz, seq, tm):
    t, d = pre1.shape
    nt = seq // tm
    mem_len = kv.shape[0] // bsz
    d_mem = w_mq.shape[1]
    return pl.pallas_call(
        _memattn_kernel,
        grid=(bsz, nt),
        in_specs=[pl.BlockSpec((tm, d), lambda bi, i: (bi * nt + i, 0)),
                  pl.BlockSpec((1, d), lambda bi, i: (0, 0)),
                  pl.BlockSpec((1, d), lambda bi, i: (0, 0)),
                  pl.BlockSpec((mem_len, 2 * d_mem), lambda bi, i: (bi, 0)),
                  pl.BlockSpec((d, d_mem), lambda bi, i: (0, 0), pipeline_mode=pl.Buffered(1)),
                  pl.BlockSpec((d_mem, d), lambda bi, i: (0, 0), pipeline_mode=pl.Buffered(1)),
                  pl.BlockSpec((1, d), lambda bi, i: (0, 0)),
                  pl.BlockSpec((1, d), lambda bi, i: (0, 0)),
                  pl.BlockSpec((d, LANES), lambda bi, i: (0, 0), pipeline_mode=pl.Buffered(1))],
        out_specs=[pl.BlockSpec((tm, d), lambda bi, i: (bi * nt + i, 0)),
                   pl.BlockSpec((tm, LANES), lambda bi, i: (bi * nt + i, 0))],
        out_shape=[jax.ShapeDtypeStruct((t, d), F32),
                   jax.ShapeDtypeStruct((t, LANES), F32)],
        compiler_params=_cparams(("parallel", "parallel")),
        name="memory_attention",
    )(pre1, g1.reshape(1, d), b1.reshape(1, d), kv, w_mq, w_mo, g.reshape(1, d), b.reshape(1, d), w_router)


def _router_kernel(lg_ref, e1_ref, e2_ref, g1_ref, g2_ref):
    x = lg_ref[...]
    lane = lax.broadcasted_iota(I32, x.shape, 1)
    lane_f = lane.astype(F32)

    def argmax(mask):
        v = jnp.where(mask, x, -jnp.inf)
        mx = jnp.max(v, axis=1, keepdims=True)
        idx = jnp.min(jnp.where(mask & (v == mx), lane_f, float(LANES)), axis=1, keepdims=True)
        return mx, idx.astype(I32)

    gmask = lane < N_GROUPS
    gmax, gsel = argmax(gmask)
    gsum = jnp.sum(jnp.where(gmask, jnp.exp(x - gmax), 0.0), axis=1, keepdims=True)
    g_p = 1.0 / gsum
    lo = N_GROUPS + gsel * EXP_PER_GROUP
    emask = (lane >= lo) & (lane < lo + EXP_PER_GROUP)
    m1, i1 = argmax(emask)
    m2, i2 = argmax(emask & (lane != i1))
    esum = jnp.sum(jnp.where(emask, jnp.exp(x - m1), 0.0), axis=1, keepdims=True)
    p1 = 1.0 / esum
    p2 = jnp.exp(m2 - m1) / esum
    e1_ref[...] = i1 - N_GROUPS
    e2_ref[...] = i2 - N_GROUPS
    g1_ref[...] = g_p * (p1 / (p1 + p2))
    g2_ref[...] = g_p * (p2 / (p1 + p2))


def _router(logits, tm):
    t = logits.shape[0]
    col = pl.BlockSpec((tm, 1), lambda i: (i, 0))
    return pl.pallas_call(
        _router_kernel,
        grid=(t // tm,),
        in_specs=[pl.BlockSpec((tm, LANES), lambda i: (i, 0))],
        out_specs=[col, col, col, col],
        out_shape=[jax.ShapeDtypeStruct((t, 1), I32), jax.ShapeDtypeStruct((t, 1), I32),
                   jax.ShapeDtypeStruct((t, 1), F32), jax.ShapeDtypeStruct((t, 1), F32)],
        compiler_params=_cparams(("parallel",)),
        name="moe_router",
    )(logits)


HI16 = -65536


def _pack_halves(x):
    d = x.shape[1] // 2
    lo = pltpu.bitcast(x[:, :d].astype(BF16).astype(F32), I32)
    hi = pltpu.bitcast(x[:, d:].astype(BF16).astype(F32), I32)
    return lax.shift_right_logical(lo, 16) | (hi & HI16)


def _unpack_halves(u):
    return pltpu.bitcast(lax.shift_left(u, 16), F32), pltpu.bitcast(u & HI16, F32)


def _moe_rank_kernel(e1_ref, e2_ref, r1_ref, r2_ref, cnt_ref, base):
    i = pl.program_id(0)
    tm = e1_ref.shape[0]

    @pl.when(i == 0)
    def _():
        base[...] = jnp.zeros(base.shape, F32)

    lane = lax.broadcasted_iota(I32, (tm, LANES), 1)
    rr = lax.broadcasted_iota(I32, (tm, tm), 0)
    cc = lax.broadcasted_iota(I32, (tm, tm), 1)
    earlier = jnp.where(cc < rr, 1.0, 0.0).astype(BF16)
    for slot, (e_ref, r_ref) in enumerate(((e1_ref, r1_ref), (e2_ref, r2_ref))):
        oh = jnp.where(lane == e_ref[...], 1.0, 0.0)
        before = _dot(earlier, oh.astype(BF16)) + base[slot:slot + 1, :]
        r_ref[...] = jnp.sum(oh * before, axis=1, keepdims=True).astype(I32)
        base[slot:slot + 1, :] = base[slot:slot + 1, :] + jnp.sum(oh, axis=0, keepdims=True)
    cnt_ref[...] = base[...]


def _moe_rank(e1, e2, tm):
    t = e1.shape[0]
    col = pl.BlockSpec((tm, 1), lambda i: (i, 0))
    return pl.pallas_call(
        _moe_rank_kernel,
        grid=(t // tm,),
        in_specs=[col, col],
        out_specs=[col, col, pl.BlockSpec((8, LANES), lambda i: (0, 0))],
        out_shape=[jax.ShapeDtypeStruct((t, 1), I32), jax.ShapeDtypeStruct((t, 1), I32),
                   jax.ShapeDtypeStruct((8, LANES), F32)],
        scratch_shapes=[pltpu.VMEM((8, LANES), F32)],
        compiler_params=_cparams(("arbitrary",)),
        name="moe_rank",
    )(e1, e2)


def _moe_place_kernel(e1_ref, e2_ref, r1_ref, r2_ref, cnt_ref, p1_ref, p2_ref, be_ref, nu_ref):
    tm = e1_ref.shape[0]
    nbp = be_ref.shape[0]
    lane8 = lax.broadcasted_iota(I32, (8, LANES), 1)
    cnt = cnt_ref[...].astype(I32)
    c0 = jnp.broadcast_to(cnt[0:1], (8, LANES))
    c1 = jnp.broadcast_to(cnt[1:2], (8, LANES))
    blk_shift = MOE_BLOCK.bit_length() - 1
    padded = lax.shift_left(lax.shift_right_logical(c0 + c1 + (MOE_BLOCK - 1), blk_shift), blk_shift)
    pad_end = padded
    s = 1
    while s < LANES:
        pad_end = pad_end + jnp.where(lane8 >= s, pltpu.roll(pad_end, s, 1), 0)
        s *= 2
    start0 = (pad_end - padded).astype(F32)
    start1 = (pad_end - padded + c0).astype(F32)

    lane = lax.broadcasted_iota(I32, (tm, LANES), 1)
    for e_ref, r_ref, p_ref, start in ((e1_ref, r1_ref, p1_ref, start0), (e2_ref, r2_ref, p2_ref, start1)):
        seg = jnp.sum(jnp.where(lane == e_ref[...], start[0:1], 0.0), axis=1, keepdims=True)
        p_ref[...] = seg.astype(I32) + r_ref[...]

    block_row = lax.broadcasted_iota(I32, (nbp, LANES), 0) * MOE_BLOCK
    lane_b = lax.broadcasted_iota(I32, (nbp, LANES), 1)
    ended = (pad_end[0:1] <= block_row) & (lane_b < N_EXPERTS)
    be = jnp.sum(jnp.where(ended, 1.0, 0.0), axis=1, keepdims=True)
    be_ref[...] = jnp.minimum(be, N_EXPERTS - 1.0).astype(I32)
    total = jnp.max(pad_end, axis=1, keepdims=True)
    sub = lax.broadcasted_iota(I32, (8, LANES), 0)
    n_used = jnp.broadcast_to(lax.shift_right_logical(total, blk_shift), (8, LANES))
    nu_ref[...] = jnp.where(sub == 0, n_used, jnp.where(sub == 1, pad_end, jnp.where(sub == 2, padded, 0)))


def _moe_place(e1, e2, r1, r2, cnt, nb, tm):
    t = e1.shape[0]
    col = pl.BlockSpec((tm, 1), lambda i: (i, 0))
    return pl.pallas_call(
        _moe_place_kernel,
        grid=(t // tm,),
        in_specs=[col, col, col, col, pl.BlockSpec((8, LANES), lambda i: (0, 0))],
        out_specs=[col, col, pl.BlockSpec((nb, 1), lambda i: (0, 0)),
                   pl.BlockSpec((8, LANES), lambda i: (0, 0))],
        out_shape=[jax.ShapeDtypeStruct((t, 1), I32), jax.ShapeDtypeStruct((t, 1), I32),
                   jax.ShapeDtypeStruct((nb, 1), I32), jax.ShapeDtypeStruct((8, LANES), I32)],
        compiler_params=_cparams(("arbitrary",)),
        name="moe_place",
    )(e1, e2, r1, r2, cnt)


def _moe_dispatch_kernel(seg_ref, p1_ref, p2_ref, x_ref, xs_hbm, pk, zbuf, sem, zsem):
    i = pl.program_id(0)
    n = pl.num_programs(0)
    slot = lax.rem(i, 2)
    blk = x_ref.shape[0]
    nb = xs_hbm.shape[0] // blk

    def row_copy(s, r, pos):
        return pltpu.make_async_copy(pk.at[s, pl.ds(r, 1)], xs_hbm.at[pl.ds(pos, 1)], sem.at[s])

    def drain(s):
        for _ in range(2):
            pltpu.make_async_copy(pk.at[s], xs_hbm.at[pl.ds(0, blk)], sem.at[s]).wait()

    def zero_copy(row0):
        return pltpu.make_async_copy(zbuf, xs_hbm.at[pl.ds(pl.multiple_of(row0, blk), blk)], zsem)

    @pl.when(i == 0)
    def _():
        zbuf[...] = jnp.zeros(zbuf.shape, zbuf.dtype)
        n_used = seg_ref[0, 0]

        def seg_start(e, c):
            @pl.when(seg_ref[2, e] > 0)
            def _():
                zero_copy(seg_ref[1, e] - blk).start()
            return c

        def tail_start(b, c):
            zero_copy(b * blk).start()
            return c

        def seg_wait(e, c):
            @pl.when(seg_ref[2, e] > 0)
            def _():
                zero_copy(0).wait()
            return c

        def tail_wait(b, c):
            zero_copy(0).wait()
            return c

        lax.fori_loop(0, N_EXPERTS, seg_start, 0)
        lax.fori_loop(n_used, nb, tail_start, 0)
        lax.fori_loop(0, N_EXPERTS, seg_wait, 0)
        lax.fori_loop(n_used, nb, tail_wait, 0)

    @pl.when(i >= 2)
    def _():
        drain(slot)

    pk[slot] = _pack_halves(x_ref[...])

    def issue(r, c):
        row_copy(slot, r, p1_ref[0, 0, r]).start()
        row_copy(slot, r, p2_ref[0, 0, r]).start()
        return c

    lax.fori_loop(0, blk, issue, 0, unroll=8)

    @pl.when(i == n - 1)
    def _():
        drain(slot)

    @pl.when((i == n - 1) & (i >= 1))
    def _():
        drain(1 - slot)


def _moe_dispatch(x2, p1, p2, seg, nb):
    t, d = x2.shape
    nt = t // MOE_BLOCK
    rows = nb * MOE_BLOCK
    pos = pl.BlockSpec((1, 1, MOE_BLOCK), lambda i: (i, 0, 0), memory_space=pltpu.SMEM)
    return pl.pallas_call(
        _moe_dispatch_kernel,
        grid=(nt,),
        in_specs=[pl.BlockSpec(memory_space=pltpu.SMEM), pos, pos,
                  pl.BlockSpec((MOE_BLOCK, d), lambda i: (i, 0))],
        out_specs=pl.BlockSpec(memory_space=pl.ANY),
        out_shape=jax.ShapeDtypeStruct((rows, d // 2), I32),
        scratch_shapes=[pltpu.VMEM((2, MOE_BLOCK, d // 2), I32),
                        pltpu.VMEM((MOE_BLOCK, d // 2), I32),
                        pltpu.SemaphoreType.DMA((2,)),
                        pltpu.SemaphoreType.DMA(())],
        compiler_params=_cparams(("arbitrary",)),
        name="moe_dispatch",
    )(seg, p1.reshape(nt, 1, MOE_BLOCK), p2.reshape(nt, 1, MOE_BLOCK), x2)


MAT_PIECES = 4
N_PIECES = 3 * MAT_PIECES
PIECES_PER_BLOCK = 3
PIECE_RING = 4


def _moe_ffn_kernel(be_ref, nu_ref, xs_ref, wg_hbm, wu_hbm, wd_hbm, ys_ref,
                    wgb, wub, wdb, sa, sb, sem, st):
    i = pl.program_id(0)
    nb = be_ref.shape[0]
    n_used = nu_ref[0]
    ra = wgb.shape[1] // MAT_PIECES
    rb = wdb.shape[1] // MAT_PIECES

    def piece_copy(p, e):
        k = p % PIECE_RING
        m, r = divmod(p, MAT_PIECES)
        if m == 0:
            return pltpu.make_async_copy(wg_hbm.at[e, pl.ds(r * ra, ra)], sa.at[k], sem.at[k])
        if m == 1:
            return pltpu.make_async_copy(wu_hbm.at[e, pl.ds(r * ra, ra)], sa.at[k], sem.at[k])
        return pltpu.make_async_copy(wd_hbm.at[e, pl.ds(r * rb, rb)], sb.at[k], sem.at[k])

    def piece_round(p, slot):
        k = p % PIECE_RING
        m, r = divmod(p, MAT_PIECES)
        if m == 0:
            wgb[slot, pl.ds(r * ra, ra), :] = sa[k].astype(BF16)
        elif m == 1:
            wub[slot, pl.ds(r * ra, ra), :] = sa[k].astype(BF16)
        else:
            wdb[slot, pl.ds(r * rb, rb), :] = sb[k].astype(BF16)

    def start_one():
        e, started, finished = st[1], st[2], st[3]
        can = (started < N_PIECES) & (started - finished < PIECE_RING)
        for p in range(N_PIECES):
            @pl.when(can & (started == p))
            def _():
                piece_copy(p, e).start(priority=p % 2)
        st[2] = started + can.astype(I32)

    def finish_one(slot):
        e, finished = st[1], st[3]

        @pl.when((e >= 0) & (finished < N_PIECES))
        def _():
            for p in range(N_PIECES):
                @pl.when(finished == p)
                def _():
                    piece_copy(p, e).wait()
                    piece_round(p, slot)
            st[3] = finished + 1
            start_one()

    def prepare(e):
        st[1] = e
        st[2] = 0
        st[3] = 0

        @pl.when(e >= 0)
        def _():
            for _ in range(PIECE_RING):
                start_one()

    @pl.when(i < n_used)
    def _():
        e = be_ref[i]

        @pl.when(i == 0)
        def _():
            st[0] = 1
            prepare(e)

        @pl.when((i == 0) | (e != be_ref[jnp.maximum(i - 1, 0)]))
        def _():
            slot = 1 - st[0]

            def fin(_, c):
                finish_one(slot)
                return c

            lax.fori_loop(0, N_PIECES, fin, 0)
            st[0] = slot
            k = lax.while_loop(lambda k: (k < n_used) & (be_ref[jnp.minimum(k, nb - 1)] == e),
                               lambda k: k + 1, i + 1)
            prepare(jnp.where(k < n_used, be_ref[jnp.minimum(k, nb - 1)], -1))

        def ahead(_, c):
            finish_one(1 - st[0])
            return c

        lax.fori_loop(0, PIECES_PER_BLOCK, ahead, 0)

        slot = st[0]
        lo, hi = _unpack_halves(xs_ref[...])
        lo = lo.astype(BF16)
        hi = hi.astype(BF16)
        d2 = lo.shape[1]
        g = _dot(lo, wgb[slot, :d2]) + _dot(hi, wgb[slot, d2:])
        u = _dot(lo, wub[slot, :d2]) + _dot(hi, wub[slot, d2:])
        hmid = (g * jax.nn.sigmoid(g) * u).astype(BF16)
        ys_ref[...] = _pack_halves(_dot(hmid, wdb[slot]))

    @pl.when(i >= n_used)
    def _():
        ys_ref[...] = jnp.zeros(ys_ref.shape, ys_ref.dtype)


def _moe_ffn(xs, w_gate, w_up, w_down, block_expert, n_used):
    rows, d2 = xs.shape
    nb = rows // MOE_BLOCK
    d = 2 * d2
    ff = w_gate.shape[2]
    grid_spec = pltpu.PrefetchScalarGridSpec(
        num_scalar_prefetch=2,
        grid=(nb,),
        in_specs=[pl.BlockSpec((MOE_BLOCK, d2), lambda i, be, nu: (i, 0)),
                  pl.BlockSpec(memory_space=pl.ANY),
                  pl.BlockSpec(memory_space=pl.ANY),
                  pl.BlockSpec(memory_space=pl.ANY)],
        out_specs=pl.BlockSpec((MOE_BLOCK, d2), lambda i, be, nu: (i, 0)),
        scratch_shapes=[pltpu.VMEM((2, d, ff), BF16), pltpu.VMEM((2, d, ff), BF16),
                        pltpu.VMEM((2, ff, d), BF16),
                        pltpu.VMEM((PIECE_RING, d // MAT_PIECES, ff), F32),
                        pltpu.VMEM((PIECE_RING, ff // MAT_PIECES, d), F32),
                        pltpu.SemaphoreType.DMA((PIECE_RING,)),
                        pltpu.SMEM((4,), I32)],
    )
    return pl.pallas_call(
        _moe_ffn_kernel,
        grid_spec=grid_spec,
        out_shape=jax.ShapeDtypeStruct((rows, d2), I32),
        compiler_params=_cparams(("arbitrary",)),
        name="moe_ffn",
    )(block_expert, n_used, xs, w_gate, w_up, w_down)


def _moe_combine_kernel(p1_ref, p2_ref, q1_ref, q2_ref, x_ref, g1_ref, g2_ref, g_ref, b_ref, ys_hbm,
                        o_ref, yb, sem):
    i = pl.program_id(0)
    n = pl.num_programs(0)
    slot = lax.rem(i, 2)
    blk = x_ref.shape[0]

    def row_copy(s, k, r, pos):
        return pltpu.make_async_copy(ys_hbm.at[pl.ds(pos, 1)], yb.at[s, k, pl.ds(r, 1)], sem.at[s])

    def fetch(s, a_ref, b_ref2):
        def body(r, c):
            row_copy(s, 0, r, a_ref[0, 0, r]).start()
            row_copy(s, 1, r, b_ref2[0, 0, r]).start()
            return c
        lax.fori_loop(0, blk, body, 0, unroll=8)

    @pl.when(i == 0)
    def _():
        fetch(0, p1_ref, p2_ref)

    @pl.when(i + 1 < n)
    def _():
        fetch(1 - slot, q1_ref, q2_ref)

    for k in range(2):
        pltpu.make_async_copy(ys_hbm.at[pl.ds(0, blk)], yb.at[slot, k], sem.at[slot]).wait()
    y1 = jnp.concatenate(_unpack_halves(yb[slot, 0]), axis=1)
    y2 = jnp.concatenate(_unpack_halves(yb[slot, 1]), axis=1)
    y = y1 * g1_ref[...] + y2 * g2_ref[...]
    o_ref[...] = _ln_rows(ALPHA * x_ref[...] + y, g_ref[...], b_ref[...])


def _moe_combine(x2, ys, p1, p2, g1, g2, g, b):
    t, d = x2.shape
    nt = t // MOE_BLOCK
    p1 = p1.reshape(nt, 1, MOE_BLOCK)
    p2 = p2.reshape(nt, 1, MOE_BLOCK)
    pos = pl.BlockSpec((1, 1, MOE_BLOCK), lambda i: (i, 0, 0), memory_space=pltpu.SMEM)
    nxt = pl.BlockSpec((1, 1, MOE_BLOCK), lambda i: (jnp.minimum(i + 1, nt - 1), 0, 0),
                       memory_space=pltpu.SMEM)
    col = pl.BlockSpec((MOE_BLOCK, 1), lambda i: (i, 0))
    vec = pl.BlockSpec((1, d), lambda i: (0, 0))
    return pl.pallas_call(
        _moe_combine_kernel,
        grid=(nt,),
        in_specs=[pos, pos, nxt, nxt, pl.BlockSpec((MOE_BLOCK, d), lambda i: (i, 0)), col, col, vec, vec,
                  pl.BlockSpec(memory_space=pl.ANY)],
        out_specs=pl.BlockSpec((MOE_BLOCK, d), lambda i: (i, 0)),
        out_shape=jax.ShapeDtypeStruct((t, d), F32),
        scratch_shapes=[pltpu.VMEM((2, 2, MOE_BLOCK, d // 2), I32),
                        pltpu.SemaphoreType.DMA((2,))],
        compiler_params=_cparams(("arbitrary",)),
        name="moe_combine",
    )(p1, p2, p1, p2, x2, g1, g2, g.reshape(1, d), b.reshape(1, d), ys)


def _tile(n, pref):
    return pref if n % pref == 0 else n


def _layer(x, mem, w_in, conv_w, conv_b, conv_ln_g, conv_ln_b, kv_norm_g, w_uk, w_uv, rel_bias,
           conv_out_g, attn_out_g, w_out, ln1_g, ln1_b, w_mq, w_mk, w_mv, w_mo, ln2_g, ln2_b,
           w_router_grp, w_router_exp, w_gate, w_up, w_down, ln3_g, ln3_b):
    bsz, seq, d = x.shape
    t = bsz * seq
    d_conv = conv_w.shape[1]
    d_attn = N_HEADS * HEAD_DIM
    c_glu = 2 * d_conv
    c_qi = H_IDX * D_IDX
    o_q, o_kv = c_glu, c_glu + d_attn
    o_qi = o_kv + KV_RANK
    o_ki = o_qi + c_qi
    topk = min(TOPK_MAX, seq // 4)

    xf = x.reshape(t, d)
    w_inb = w_in.astype(BF16)
    tail_w = jnp.concatenate([w_inb[:, o_kv:o_qi], w_inb[:, o_ki:]], axis=1)
    tail_w = jnp.pad(tail_w, ((0, 0), (0, KV_RANK + LANES - tail_w.shape[1])))
    tm = _tile(t, 1024)

    tail, xb = _matmul_cast(xf, tail_w, F32, _tile(t, 512), "proj_tail")
    u = _matmul_cols(xb, w_inb, c_glu, BF16, tm, 512, "proj_glu")
    q_hm = _matmul_heads(xb, w_inb, o_q, d_attn, BF16, tm, 512, "proj_q")
    qi_hm = _matmul_heads(xb, w_inb, o_qi, c_qi, BF16, tm, 512, "proj_qidx")

    conv_n = _conformer_conv(u, bsz, seq, conv_w, conv_b, conv_ln_g, conv_ln_b, conv_out_g)
    tk = min(ATT_BLOCK, seq)
    ckv_n, ckv_t, kia, kib, kw = _prep_latent(tail, kv_norm_g, tm, tk)
    ckv_t = ckv_t.reshape(bsz, seq // tk, KV_RANK, tk)
    mask = _indexer_mask(qi_hm, kw, kia, kib, bsz, seq, topk)
    w_uk_t = w_uk.transpose(0, 2, 1).reshape(d_attn, KV_RANK).astype(BF16)
    w_uv_all = w_uv.transpose(1, 0, 2).reshape(KV_RANK, d_attn).astype(BF16)
    k_t = _key_heads(w_uk_t, ckv_t)
    v_hm = _matmul_value_heads(ckv_n, w_uv_all, tm, 512, "value_heads")
    attn = _head_attention(q_hm, k_t, v_hm, mask, rel_bias, bsz, seq)

    pre1 = _matmul2_residual(conv_n, attn, attn_out_g, w_out.astype(BF16), xf, tm, 512, "out_proj")

    mem_len = mem.shape[1]
    memb = mem.reshape(bsz * mem_len, d).astype(BF16)
    w_kv = jnp.concatenate([w_mk, w_mv], axis=1).astype(BF16)
    kv = _matmul(memb, w_kv, BF16, _tile(bsz * mem_len, 512), 512, "mem_kv")
    w_router = jnp.concatenate([w_router_grp, w_router_exp], axis=1)
    w_router = jnp.pad(w_router, ((0, 0), (0, LANES - w_router.shape[1]))).astype(BF16)
    x2, logits = _memory_attention(pre1, ln1_g, ln1_b, kv, w_mq.astype(BF16), w_mo.astype(BF16),
                                   ln2_g, ln2_b, w_router, bsz, seq, 256)

    e1, e2, g1, g2 = _router(logits, _tile(t, 1024))
    nb = (2 * t + N_EXPERTS * (MOE_BLOCK - 1) + MOE_BLOCK - 1) // MOE_BLOCK
    r1, r2, cnt = _moe_rank(e1, e2, 512)
    p1, p2, block_expert, n_used = _moe_place(e1, e2, r1, r2, cnt, nb, 512)
    xs = _moe_dispatch(x2, p1, p2, n_used, nb)
    ys = _moe_ffn(xs, w_gate, w_up, w_down, block_expert.reshape(nb), n_used[0, 0:1])
    x3 = _moe_combine(x2, ys, p1, p2, g1, g2, ln3_g, ln3_b)
    return x3.reshape(bsz, seq, d)


def kernel(x, mem, w_in, conv_w, conv_b, conv_ln_g, conv_ln_b, kv_norm_g, w_uk, w_uv, rel_bias, conv_out_g, attn_out_g, w_out, ln1_g, ln1_b, w_mq, w_mk, w_mv, w_mo, ln2_g, ln2_b, w_router_grp, w_router_exp, w_gate, w_up, w_down, ln3_g, ln3_b):
    for l in range(w_in.shape[0]):
        x = _layer(x, mem, w_in[l], conv_w[l], conv_b[l], conv_ln_g[l], conv_ln_b[l], kv_norm_g[l],
                   w_uk[l], w_uv[l], rel_bias, conv_out_g[l], attn_out_g[l], w_out[l], ln1_g[l], ln1_b[l],
                   w_mq[l], w_mk[l], w_mv[l], w_mo[l], ln2_g[l], ln2_b[l], w_router_grp[l],
                   w_router_exp[l], w_gate[l], w_up[l], w_down[l], ln3_g[l], ln3_b[l])
    return x
```

```python
import functools
import math

import jax
import jax.numpy as jnp
from jax import lax
from jax.experimental import pallas as pl
from jax.experimental.pallas import tpu as pltpu

F32 = jnp.float32
BF16 = jnp.bfloat16
I32 = jnp.int32

DEPTH = 1
CONV_WIDTH = 31
N_HEADS = 16
HEAD_DIM = 128
KV_RANK = 512
H_IDX = 32
D_IDX = 64
TOPK_MAX = 256
N_BUCKETS = 32
MAX_DIST = 128
MEM_HEADS = 4
MEM_HEAD_DIM = 128
N_GROUPS = 8
EXP_PER_GROUP = 8
N_EXPERTS = N_GROUPS * EXP_PER_GROUP
MOE_BLOCK = 128
ALPHA = (2.0 * DEPTH) ** 0.25
LN_EPS = 1e-5

LANES = 128
SUBLANES = 8
V7X_VMEM_BYTES = 64 * 1024 * 1024
VMEM_LIMIT = V7X_VMEM_BYTES * 7 // 8
NEG = -1e30
INT_MIN = -(2 ** 31)

LOG2E = 1.4426950408889634

ATT_BLOCK = 256
ATT_HEAD_GROUP = 8
CONV_TS = 256
CONV_HALO = 32
CONV_CC = 256
CONV_RC = 32


def _cparams(sem):
    return pltpu.CompilerParams(dimension_semantics=sem, vmem_limit_bytes=VMEM_LIMIT)


def _dot(a, b):
    return jnp.dot(a, b, preferred_element_type=F32)


def _dot_nt(a, b):
    return lax.dot_general(a, b, (((1,), (1,)), ((), ())), preferred_element_type=F32)


def _mm_kernel(a_ref, b_ref, o_ref):
    o_ref[...] = _dot(a_ref[...], b_ref[...]).astype(o_ref.dtype)


def _mm_cast_kernel(a_ref, b_ref, o_ref, ab_ref):
    ab = a_ref[...].astype(ab_ref.dtype)
    ab_ref[...] = ab
    o_ref[...] = _dot(ab, b_ref[...]).astype(o_ref.dtype)


def _matmul_cast(a, b, out_dtype, tm, name):
    m, k = a.shape
    n = b.shape[1]
    return pl.pallas_call(
        _mm_cast_kernel,
        grid=(m // tm,),
        in_specs=[pl.BlockSpec((tm, k), lambda i: (i, 0)),
                  pl.BlockSpec((k, n), lambda i: (0, 0))],
        out_specs=[pl.BlockSpec((tm, n), lambda i: (i, 0)),
                   pl.BlockSpec((tm, k), lambda i: (i, 0))],
        out_shape=[jax.ShapeDtypeStruct((m, n), out_dtype), jax.ShapeDtypeStruct((m, k), b.dtype)],
        compiler_params=_cparams(("parallel",)),
        name=name,
    )(a, b)


def _matmul(a, b, out_dtype, tm, tn, name):
    m, k = a.shape
    n = b.shape[1]
    return pl.pallas_call(
        _mm_kernel,
        grid=(m // tm, n // tn),
        in_specs=[pl.BlockSpec((tm, k), lambda i, j: (i, 0)),
                  pl.BlockSpec((k, tn), lambda i, j: (0, j))],
        out_specs=pl.BlockSpec((tm, tn), lambda i, j: (i, j)),
        out_shape=jax.ShapeDtypeStruct((m, n), out_dtype),
        compiler_params=_cparams(("parallel", "parallel")),
        name=name,
    )(a, b)


def _matmul_cols(a, b, n, out_dtype, tm, tn, name):
    m, k = a.shape
    return pl.pallas_call(
        _mm_kernel,
        grid=(m // tm, n // tn),
        in_specs=[pl.BlockSpec((tm, k), lambda i, j: (i, 0)),
                  pl.BlockSpec((k, tn), lambda i, j: (0, j))],
        out_specs=pl.BlockSpec((tm, tn), lambda i, j: (i, j)),
        out_shape=jax.ShapeDtypeStruct((m, n), out_dtype),
        compiler_params=_cparams(("parallel", "parallel")),
        name=name,
    )(a, b)


def _mm_heads_kernel(a_ref, b_ref, o_ref):
    r = _dot(a_ref[...], b_ref[...])
    for p in range(o_ref.shape[0]):
        o_ref[p] = r[:, p * LANES:(p + 1) * LANES].astype(o_ref.dtype)


def _matmul_heads(a, b, col0, n, out_dtype, tm, tn, name):
    m, k = a.shape
    assert col0 % tn == 0
    return pl.pallas_call(
        _mm_heads_kernel,
        grid=(m // tm, n // tn),
        in_specs=[pl.BlockSpec((tm, k), lambda i, j: (i, 0)),
                  pl.BlockSpec((k, tn), lambda i, j: (0, col0 // tn + j))],
        out_specs=pl.BlockSpec((tn // LANES, tm, LANES), lambda i, j: (j, i, 0)),
        out_shape=jax.ShapeDtypeStruct((n // LANES, m, LANES), out_dtype),
        compiler_params=_cparams(("parallel", "parallel")),
        name=name,
    )(a, b)


def _mm_value_heads_kernel(a_ref, b_ref, o_ref):
    r = _dot(a_ref[...], b_ref[...])
    tm = r.shape[0]
    for p in range(o_ref.shape[0]):
        o_ref[p, :, 0:LANES] = r[:, p * LANES:(p + 1) * LANES].astype(o_ref.dtype)
        o_ref[p, :, LANES:2 * LANES] = jnp.ones((tm, LANES), o_ref.dtype)


def _matmul_value_heads(a, b, tm, tn, name):
    m, k = a.shape
    n = b.shape[1]
    return pl.pallas_call(
        _mm_value_heads_kernel,
        grid=(m // tm, n // tn),
        in_specs=[pl.BlockSpec((tm, k), lambda i, j: (i, 0)),
                  pl.BlockSpec((k, tn), lambda i, j: (0, j))],
        out_specs=pl.BlockSpec((tn // LANES, tm, 2 * LANES), lambda i, j: (j, i, 0)),
        out_shape=jax.ShapeDtypeStruct((n // LANES, m, 2 * LANES), BF16),
        compiler_params=_cparams(("parallel", "parallel")),
        name=name,
    )(a, b)


def _mm2_res_kernel(a1_ref, a2_ref, g2_ref, w1_ref, w2_ref, r_ref, o_ref, a2n):
    @pl.when(pl.program_id(1) == 0)
    def _():
        x = a2_ref[...].astype(F32)
        ms = jnp.mean(x * x, axis=-1, keepdims=True)
        a2n[...] = (x * lax.rsqrt(ms + LN_EPS) * g2_ref[...]).astype(a2n.dtype)

    o_ref[...] = (ALPHA * r_ref[...] + _dot(a1_ref[...], w1_ref[...])
                  + _dot(a2n[...], w2_ref[...]))


def _matmul2_residual(a1, a2, g2, w, res, tm, tn, name):
    m, k = a1.shape
    assert a2.shape == a1.shape and w.shape[0] == 2 * k
    n = w.shape[1]
    return pl.pallas_call(
        _mm2_res_kernel,
        grid=(m // tm, n // tn),
        in_specs=[pl.BlockSpec((tm, k), lambda i, j: (i, 0)),
                  pl.BlockSpec((tm, k), lambda i, j: (i, 0)),
                  pl.BlockSpec((1, k), lambda i, j: (0, 0)),
                  pl.BlockSpec((k, tn), lambda i, j: (0, j)),
                  pl.BlockSpec((k, tn), lambda i, j: (1, j)),
                  pl.BlockSpec((tm, tn), lambda i, j: (i, j))],
        out_specs=pl.BlockSpec((tm, tn), lambda i, j: (i, j)),
        out_shape=jax.ShapeDtypeStruct((m, n), F32),
        scratch_shapes=[pltpu.VMEM((tm, k), BF16)],
        compiler_params=_cparams(("parallel", "arbitrary")),
        name=name,
    )(a1, a2, g2.reshape(1, k), w, w, res)


def _ln_rows(x, g, b):
    mu = jnp.mean(x, axis=-1, keepdims=True)
    xc = x - mu
    var = jnp.mean(xc * xc, axis=-1, keepdims=True)
    return xc * lax.rsqrt(var + LN_EPS) * g + b


def _conv_kernel(a_ref, g_ref, cw_ref, cb_ref, lg_ref, lb_ref, og_ref, o_ref, hbuf, ybuf, hs):
    ts = a_ref.shape[0]
    nch = hbuf.shape[0]
    cc = hbuf.shape[2]
    d_conv = nch * cc

    nrow = CONV_HALO + ts

    @pl.when(pl.program_id(1) == 0)
    def _():
        hbuf[:, 0:CONV_HALO, :] = jnp.zeros((nch, CONV_HALO, cc), F32)
        hbuf[:, nrow:nrow + SUBLANES, :] = jnp.zeros((nch, SUBLANES, cc), F32)

    for c in range(nch):
        a = a_ref[:, c * cc:(c + 1) * cc].astype(F32)
        g = g_ref[:, c * cc:(c + 1) * cc].astype(F32)
        hbuf[c, CONV_HALO:nrow, :] = a * jax.nn.sigmoid(g)

    first = CONV_HALO - (CONV_WIDTH - 1)

    def chunk_body(c, carry):
        for o in range(1, SUBLANES):
            hs[o - 1] = hbuf[c, o:o + nrow, :]
        for r0 in range(0, ts, CONV_RC):
            acc = jnp.zeros((CONV_RC, cc), F32)
            for j in range(CONV_WIDTH):
                o = (first + j) % SUBLANES
                base = r0 + first + j - o
                rows = hbuf[c, base:base + CONV_RC, :] if o == 0 else hs[o - 1, base:base + CONV_RC, :]
                acc = acc + cw_ref[c, j:j + 1, :] * rows
            ybuf[c, r0:r0 + CONV_RC, :] = acc + cb_ref[c]
        hbuf[c, 0:CONV_HALO, :] = hbuf[c, ts:nrow, :]
        return carry

    lax.fori_loop(0, nch, chunk_body, 0)

    s1 = jnp.zeros((ts, 1), F32)
    for c in range(nch):
        s1 = s1 + jnp.sum(ybuf[c], axis=1, keepdims=True)
    mu = s1 * (1.0 / d_conv)
    s2 = jnp.zeros((ts, 1), F32)
    for c in range(nch):
        yc = ybuf[c] - mu
        s2 = s2 + jnp.sum(yc * yc, axis=1, keepdims=True)
    rstd = lax.rsqrt(s2 * (1.0 / d_conv) + LN_EPS)
    s3 = jnp.zeros((ts, 1), F32)
    for c in range(nch):
        z = (ybuf[c] - mu) * rstd * lg_ref[:, c * cc:(c + 1) * cc] + lb_ref[:, c * cc:(c + 1) * cc]
        z = z * jax.nn.sigmoid(z)
        ybuf[c] = z
        s3 = s3 + jnp.sum(z * z, axis=1, keepdims=True)
    rr = lax.rsqrt(s3 * (1.0 / d_conv) + LN_EPS)
    for c in range(nch):
        o_ref[:, c * cc:(c + 1) * cc] = (ybuf[c] * rr * og_ref[:, c * cc:(c + 1) * cc]).astype(o_ref.dtype)


def _conformer_conv(u, bsz, seq, conv_w, conv_b, ln_g, ln_b, out_g):
    d_conv = u.shape[1] // 2
    ts = min(CONV_TS, seq)
    nch = d_conv // CONV_CC
    nt = seq // ts
    cw = conv_w.reshape(CONV_WIDTH, nch, CONV_CC).transpose(1, 0, 2)
    cb = conv_b.reshape(nch, 1, CONV_CC)
    vec = pl.BlockSpec((1, d_conv), lambda b, i: (0, 0))
    return pl.pallas_call(
        _conv_kernel,
        grid=(bsz, nt),
        in_specs=[pl.BlockSpec((ts, d_conv), lambda b, i: (b * nt + i, 0)),
                  pl.BlockSpec((ts, d_conv), lambda b, i: (b * nt + i, 1)),
                  pl.BlockSpec((nch, CONV_WIDTH, CONV_CC), lambda b, i: (0, 0, 0)),
                  pl.BlockSpec((nch, 1, CONV_CC), lambda b, i: (0, 0, 0)),
                  vec, vec, vec],
        out_specs=pl.BlockSpec((ts, d_conv), lambda b, i: (b * nt + i, 0)),
        out_shape=jax.ShapeDtypeStruct((bsz * seq, d_conv), BF16),
        scratch_shapes=[pltpu.VMEM((nch, CONV_HALO + ts + SUBLANES, CONV_CC), F32),
                        pltpu.VMEM((nch, ts, CONV_CC), F32),
                        pltpu.VMEM((SUBLANES - 1, CONV_HALO + ts, CONV_CC), F32)],
        compiler_params=_cparams(("arbitrary", "arbitrary")),
        name="conformer_conv",
    )(u, u, cw, cb, ln_g.reshape(1, d_conv), ln_b.reshape(1, d_conv), out_g.reshape(1, d_conv))


def _prep_kernel(t_ref, g_ref, ckv_ref, ckvt_ref, kia_ref, kib_ref, kw_ref):
    ckv = t_ref[:, 0:KV_RANK]
    ms = jnp.mean(ckv * ckv, axis=-1, keepdims=True)
    ckv_n = ckv * lax.rsqrt(ms + LN_EPS) * g_ref[...]
    ckv_ref[...] = ckv_n.astype(ckv_ref.dtype)
    tk = ckvt_ref.shape[3]
    for c in range(ckvt_ref.shape[1]):
        ckvt_ref[0, c] = ckv_n[c * tk:(c + 1) * tk, :].T.astype(ckvt_ref.dtype)
    kw = t_ref[:, KV_RANK:KV_RANK + LANES]
    kw_ref[...] = kw
    lane = lax.broadcasted_iota(I32, kw.shape, 1)
    kia_ref[...] = jnp.where(lane < D_IDX, kw, 0.0).astype(kia_ref.dtype)
    kib_ref[...] = jnp.where(lane >= D_IDX, pltpu.roll(kw, D_IDX, 1), 0.0).astype(kib_ref.dtype)


def _prep_latent(tail, kv_norm_g, tm, tk):
    m, w = tail.shape
    return pl.pallas_call(
        _prep_kernel,
        grid=(m // tm,),
        in_specs=[pl.BlockSpec((tm, w), lambda i: (i, 0)),
                  pl.BlockSpec((1, KV_RANK), lambda i: (0, 0))],
        out_specs=[pl.BlockSpec((tm, KV_RANK), lambda i: (i, 0)),
                   pl.BlockSpec((1, tm // tk, KV_RANK, tk), lambda i: (i, 0, 0, 0)),
                   pl.BlockSpec((tm, LANES), lambda i: (i, 0)),
                   pl.BlockSpec((tm, LANES), lambda i: (i, 0)),
                   pl.BlockSpec((tm, LANES), lambda i: (i, 0))],
        out_shape=[jax.ShapeDtypeStruct((m, KV_RANK), BF16),
                   jax.ShapeDtypeStruct((m // tm, tm // tk, KV_RANK, tk), BF16),
                   jax.ShapeDtypeStruct((m, LANES), BF16),
                   jax.ShapeDtypeStruct((m, LANES), BF16),
                   jax.ShapeDtypeStruct((m, LANES), F32)],
        compiler_params=_cparams(("parallel",)),
        name="prep_latent",
    )(tail, kv_norm_g.reshape(1, KV_RANK))


def _indexer_t_kernel(qi_ref, kw_ref, kia_ref, kib_ref, o_ref, keybuf, *, topk):
    i = pl.program_id(1)
    npairs, tq, _ = qi_ref.shape
    nk = o_ref.shape[1]
    tk = o_ref.shape[3]
    kf = float(topk)
    group = 4
    slabs = tk // SUBLANES

    w_t = kw_ref[...].T
    key_idx = lax.broadcasted_iota(I32, (tk, tq), 0)
    qry_idx = lax.broadcasted_iota(I32, (tk, tq), 1) + i * tq

    def score_chunk(j, carry):
        k0 = pl.multiple_of(j * tk, tk)
        kd = jnp.concatenate([kia_ref[pl.ds(k0, tk), :], kib_ref[pl.ds(k0, tk), :]], axis=0)
        acc = jnp.zeros((tk, tq), F32)
        for p0 in range(0, npairs, group):
            rhs = qi_ref[p0:p0 + group].reshape(group * tq, LANES)
            zz = _dot_nt(kd, rhs)
            for p in range(group):
                h = 2 * (p0 + p)
                z = zz[:, p * tq:(p + 1) * tq]
                acc = (acc + w_t[D_IDX + h:D_IDX + h + 1, :] * jnp.maximum(z[0:tk], 0.0)
                       + w_t[D_IDX + h + 1:D_IDX + h + 2, :] * jnp.maximum(z[tk:2 * tk], 0.0))
        bits = pltpu.bitcast(acc, I32)
        key = jnp.where(bits >= 0, bits, bits ^ jnp.int32(0x7FFFFFFF))
        keybuf[j] = jnp.where(key_idx + j * tk <= qry_idx, key, INT_MIN)
        return carry

    def chunk_pairs(fn):
        def pair(p, carry):
            fn(2 * p, carry)
            fn(2 * p + 1, carry)
            return carry

        lax.fori_loop(0, lax.shift_right_logical(i + 1, 1), pair, 0)

        @pl.when((i & 1) == 0)
        def _():
            fn(i, 0)

    chunk_pairs(score_chunk)

    def count(pred):
        def one(j, c):
            hit = jnp.where(pred(keybuf[j], key_idx + j * tk), 1.0, 0.0)
            return c + jnp.sum(hit.reshape(slabs, SUBLANES, tq), axis=0)

        def pair(p, c):
            return one(2 * p + 1, one(2 * p, c))

        c = lax.fori_loop(0, lax.shift_right_logical(i + 1, 1), pair, jnp.zeros((SUBLANES, tq), F32))
        c = lax.cond((i & 1) == 0, lambda c: one(i, c), lambda c: c, c)
        return jnp.broadcast_to(jnp.sum(c, axis=0, keepdims=True), (SUBLANES, tq))

    def tile_rows(v):
        return jnp.concatenate([v] * slabs, axis=0)

    def count_ge(cand):
        cb = tile_rows(cand)
        return count(lambda kk, idx: kk >= cb)

    tau = jnp.where(count_ge(jnp.zeros((SUBLANES, tq), I32)) >= kf, 0, INT_MIN).astype(I32)

    def bit_body(it, tau):
        cand = tau | jnp.left_shift(jnp.int32(1), 30 - it)
        return jnp.where(count_ge(cand) >= kf, cand, tau)

    tau = lax.fori_loop(0, 31, bit_body, tau)
    n_ge = count_ge(tau)
    taub = tile_rows(tau)

    def tie_cut():
        need = kf - count(lambda kk, idx: kk > taub)

        def cut_body(it, cut):
            cand = cut + jnp.left_shift(jnp.int32(1), 30 - it)
            cb = tile_rows(cand)
            below = count(lambda kk, idx: (kk == taub) & (idx < cb))
            return jnp.where(below < need, cand, cut)

        return lax.fori_loop(0, 31, cut_body, jnp.zeros((SUBLANES, tq), I32))

    has_ties = jnp.max(n_ge) > kf
    cut = lax.cond(has_ties, tie_cut, lambda: jnp.full((SUBLANES, tq), 2 ** 30, I32))
    cutb = tile_rows(cut)
    eye = jnp.where(lax.broadcasted_iota(I32, (tq, tq), 0) == lax.broadcasted_iota(I32, (tq, tq), 1),
                    1.0, 0.0).astype(BF16)

    def write_chunk(j, carry):
        kk = keybuf[j]
        sel = ((kk > taub) | ((kk == taub) & (key_idx + j * tk <= cutb))) & (kk != INT_MIN)
        sel_qk = _dot_nt(eye, jnp.where(sel, 1.0, 0.0).astype(BF16))
        o_ref[0, j] = jnp.where(sel_qk > 0.5, 0.0, NEG).astype(o_ref.dtype)
        return carry

    chunk_pairs(write_chunk)

    def write_rest(j, carry):
        o_ref[0, j] = jnp.full((tq, tk), NEG, o_ref.dtype)
        return carry

    lax.fori_loop(i + 1, nk, write_rest, 0)


def _indexer_mask(qi_hm, kw, kia, kib, bsz, seq, topk):
    tq = min(ATT_BLOCK, seq)
    nq = seq // tq
    npairs = qi_hm.shape[0]
    return pl.pallas_call(
        functools.partial(_indexer_t_kernel, topk=topk),
        grid=(bsz, nq),
        in_specs=[pl.BlockSpec((npairs, tq, LANES), lambda b, i: (0, b * nq + i, 0)),
                  pl.BlockSpec((tq, LANES), lambda b, i: (b * nq + i, 0)),
                  pl.BlockSpec((seq, LANES), lambda b, i: (b, 0)),
                  pl.BlockSpec((seq, LANES), lambda b, i: (b, 0))],
        out_specs=pl.BlockSpec((1, nq, tq, tq), lambda b, i: (b * nq + i, 0, 0, 0)),
        out_shape=jax.ShapeDtypeStruct((bsz * nq, nq, tq, tq), BF16),
        scratch_shapes=[pltpu.VMEM((nq, tq, tq), I32)],
        compiler_params=_cparams(("parallel", "parallel")),
        name="indexer_mask",
    )(qi_hm, kw, kia, kib)


def _t5_bucket(dist):
    n = jnp.maximum(dist, 0)
    max_exact = N_BUCKETS // 2
    nf = jnp.maximum(n, 1).astype(F32)
    large = max_exact + (jnp.log(nf / max_exact) / math.log(MAX_DIST / max_exact)
                         * (N_BUCKETS - max_exact)).astype(I32)
    large = jnp.minimum(large, N_BUCKETS - 1)
    return jnp.where(n < max_exact, n, large)


def _key_heads_kernel(wt_ref, ct_ref, o_ref):
    r = _dot(wt_ref[...], ct_ref[0, 0]) * (HEAD_DIM ** -0.5 * LOG2E)
    for h in range(o_ref.shape[1]):
        o_ref[0, h, 0] = r[h * HEAD_DIM:(h + 1) * HEAD_DIM].astype(o_ref.dtype)


def _key_heads(w_uk_t, ckv_t):
    bsz, nk, r_lat, tk = ckv_t.shape
    nh = w_uk_t.shape[0] // HEAD_DIM
    return pl.pallas_call(
        _key_heads_kernel,
        grid=(bsz, nk),
        in_specs=[pl.BlockSpec((nh * HEAD_DIM, r_lat), lambda b, j: (0, 0)),
                  pl.BlockSpec((1, 1, r_lat, tk), lambda b, j: (b, j, 0, 0))],
        out_specs=pl.BlockSpec((1, nh, 1, HEAD_DIM, tk), lambda b, j: (b, 0, j, 0, 0)),
        out_shape=jax.ShapeDtypeStruct((bsz, nh, nk, HEAD_DIM, tk), BF16),
        compiler_params=_cparams(("parallel", "parallel")),
        name="key_heads",
    )(w_uk_t, ckv_t)


def _attn_kernel(relb_ref, q_ref, kt_ref, v_ref, mask_ref, o_ref, acc, m_s, btab):
    b = pl.program_id(0)
    g = pl.program_id(1)
    i = pl.program_id(2)
    hg, tq, _ = q_ref.shape
    nh = btab.shape[1]
    tk = mask_ref.shape[3]

    @pl.when((b == 0) & (g == 0) & (i == 0))
    def _():
        r = lax.broadcasted_iota(I32, (tq, tk), 0)
        c = lax.broadcasted_iota(I32, (tq, tk), 1)
        for t in range(2):
            bk = _t5_bucket(r - c + t * tk)

            def fill(h, carry):
                far = relb_ref[N_BUCKETS - 1, h]
                v = jnp.zeros((tq, tk), F32)
                for k in range(N_BUCKETS):
                    v = jnp.where(bk == k, (relb_ref[k, h] - far) * LOG2E, v)
                btab[t, h] = v
                return carry

            lax.fori_loop(0, nh, fill, 0)

    acc[...] = jnp.zeros(acc.shape, F32)
    m_s[...] = jnp.full(m_s.shape, NEG, F32)

    def lane_tile(v, n):
        return jnp.concatenate([v] * n, axis=1)

    def chunks(js, near):
        mks = [mask_ref[0, j].astype(F32) for j in js]
        for h in range(hg):
            for j, mk in zip(js, mks):
                s = _dot(q_ref[h], kt_ref[0, h, j]) + mk
                if near:
                    s = s + btab[i - j, g * hg + h]
                m_old = m_s[h]
                m_new = jnp.maximum(m_old, jnp.max(s, axis=1, keepdims=True))
                a = jnp.exp2(m_old - m_new)
                p = jnp.exp2(s - lane_tile(m_new, tk // LANES))
                rows = pl.ds(pl.multiple_of(j * tk, tk), tk)
                acc[h] = lane_tile(a, 2) * acc[h] + _dot(p.astype(BF16), v_ref[h, rows, :])
                m_s[h] = m_new

    n_far = jnp.maximum(i - 1, 0)

    def far_quad(p, carry):
        chunks([4 * p, 4 * p + 1, 4 * p + 2, 4 * p + 3], False)
        return carry

    lax.fori_loop(0, lax.shift_right_logical(n_far, 2), far_quad, 0)
    done = n_far & ~3

    @pl.when((n_far & 2) == 2)
    def _():
        chunks([done, done + 1], False)

    @pl.when((n_far & 1) == 1)
    def _():
        chunks([n_far - 1], False)

    @pl.when(i >= 1)
    def _():
        chunks([i - 1, i], True)

    @pl.when(i == 0)
    def _():
        chunks([i], True)

    for h in range(hg):
        o_ref[:, h * HEAD_DIM:(h + 1) * HEAD_DIM] = (
            acc[h, :, :HEAD_DIM] / acc[h, :, HEAD_DIM:]).astype(o_ref.dtype)


def _head_attention(q_hm, k_t, v_hm, mask, rel_bias, bsz, seq):
    nh = q_hm.shape[0]
    hg = ATT_HEAD_GROUP
    tq = mask.shape[2]
    nq = seq // tq
    assert tq == mask.shape[3] and tq >= MAX_DIST
    once = pl.Buffered(1)
    return pl.pallas_call(
        _attn_kernel,
        grid=(bsz, nh // hg, nq),
        in_specs=[pl.BlockSpec(memory_space=pltpu.SMEM),
                  pl.BlockSpec((hg, tq, HEAD_DIM), lambda b, g, i: (g, b * nq + i, 0)),
                  pl.BlockSpec((1, hg, nq, HEAD_DIM, tq), lambda b, g, i: (b, g, 0, 0, 0), pipeline_mode=once),
                  pl.BlockSpec((hg, seq, 2 * HEAD_DIM), lambda b, g, i: (g, b, 0), pipeline_mode=once),
                  pl.BlockSpec((1, nq, tq, tq), lambda b, g, i: (b * nq + i, 0, 0, 0))],
        out_specs=pl.BlockSpec((tq, hg * HEAD_DIM), lambda b, g, i: (b * nq + i, g)),
        out_shape=jax.ShapeDtypeStruct((bsz * seq, nh * HEAD_DIM), BF16),
        scratch_shapes=[pltpu.VMEM((hg, tq, 2 * HEAD_DIM), F32),
                        pltpu.VMEM((hg, tq, LANES), F32),
                        pltpu.VMEM((2, nh, tq, tq), F32)],
        compiler_params=_cparams(("arbitrary", "arbitrary", "arbitrary")),
        name="head_attention",
    )(rel_bias, q_hm, k_t, v_hm, mask)


def _memattn_kernel(x_ref, g1_ref, b1_ref, kv_ref, wq_ref, wo_ref, g_ref, b_ref, wr_ref, o_ref, lg_ref):
    x = _ln_rows(x_ref[...], g1_ref[...], b1_ref[...])
    d_mem = MEM_HEADS * MEM_HEAD_DIM
    q = (_dot(x.astype(BF16), wq_ref[...]) * (MEM_HEAD_DIM ** -0.5)).astype(BF16)
    outs = []
    for h in range(MEM_HEADS):
        lo = h * MEM_HEAD_DIM
        k = kv_ref[:, lo:lo + MEM_HEAD_DIM]
        v = kv_ref[:, d_mem + lo:d_mem + lo + MEM_HEAD_DIM]
        s = _dot_nt(q[:, lo:lo + MEM_HEAD_DIM], k)
        p = jnp.exp(s - jnp.max(s, axis=1, keepdims=True))
        p = p / jnp.sum(p, axis=1, keepdims=True)
        outs.append(_dot(p.astype(BF16), v).astype(BF16))
    o = jnp.concatenate(outs, axis=1)
    x2 = _ln_rows(ALPHA * x + _dot(o, wo_ref[...]), g_ref[...], b_ref[...])
    o_ref[...] = x2
    lg_ref[...] = _dot(x2.astype(BF16), wr_ref[...])


def _memory_attention(pre1, g1, b1, kv, w_mq, w_mo, g, b, w_router, bsz, seq, tm):
    t, d = pre1.shape
    nt = seq // tm
    mem_len = kv.shape[0] // bsz
    d_mem = w_mq.shape[1]
    return pl.pallas_call(
        _memattn_kernel,
        grid=(bsz, nt),
        in_specs=[pl.BlockSpec((tm, d), lambda bi, i: (bi * nt + i, 0)),
                  pl.BlockSpec((1, d), lambda bi, i: (0, 0)),
                  pl.BlockSpec((1, d), lambda bi, i: (0, 0)),
                  pl.BlockSpec((mem_len, 2 * d_mem), lambda bi, i: (bi, 0)),
                  pl.BlockSpec((d, d_mem), lambda bi, i: (0, 0), pipeline_mode=pl.Buffered(1)),
                  pl.BlockSpec((d_mem, d), lambda bi, i: (0, 0), pipeline_mode=pl.Buffered(1)),
                  pl.BlockSpec((1, d), lambda bi, i: (0, 0)),
                  pl.BlockSpec((1, d), lambda bi, i: (0, 0)),
                  pl.BlockSpec((d, LANES), lambda bi, i: (0, 0), pipeline_mode=pl.Buffered(1))],
        out_specs=[pl.BlockSpec((tm, d), lambda bi, i: (bi * nt + i, 0)),
                   pl.BlockSpec((tm, LANES), lambda bi, i: (bi * nt + i, 0))],
        out_shape=[jax.ShapeDtypeStruct((t, d), F32),
                   jax.ShapeDtypeStruct((t, LANES), F32)],
        compiler_params=_cparams(("parallel", "parallel")),
        name="memory_attention",
    )(pre1, g1.reshape(1, d), b1.reshape(1, d), kv, w_mq, w_mo, g.reshape(1, d), b.reshape(1, d), w_router)


def _router_kernel(lg_ref, e1_ref, e2_ref, g1_ref, g2_ref):
    x = lg_ref[...]
    lane = lax.broadcasted_iota(I32, x.shape, 1)
    lane_f = lane.astype(F32)

    def argmax(mask):
        v = jnp.where(mask, x, -jnp.inf)
        mx = jnp.max(v, axis=1, keepdims=True)
        idx = jnp.min(jnp.where(mask & (v == mx), lane_f, float(LANES)), axis=1, keepdims=True)
        return mx, idx.astype(I32)

    gmask = lane < N_GROUPS
    gmax, gsel = argmax(gmask)
    gsum = jnp.sum(jnp.where(gmask, jnp.exp(x - gmax), 0.0), axis=1, keepdims=True)
    g_p = 1.0 / gsum
    lo = N_GROUPS + gsel * EXP_PER_GROUP
    emask = (lane >= lo) & (lane < lo + EXP_PER_GROUP)
    m1, i1 = argmax(emask)
    m2, i2 = argmax(emask & (lane != i1))
    esum = jnp.sum(jnp.where(emask, jnp.exp(x - m1), 0.0), axis=1, keepdims=True)
    p1 = 1.0 / esum
    p2 = jnp.exp(m2 - m1) / esum
    e1_ref[...] = i1 - N_GROUPS
    e2_ref[...] = i2 - N_GROUPS
    g1_ref[...] = g_p * (p1 / (p1 + p2))
    g2_ref[...] = g_p * (p2 / (p1 + p2))


def _router(logits, tm):
    t = logits.shape[0]
    col = pl.BlockSpec((tm, 1), lambda i: (i, 0))
    return pl.pallas_call(
        _router_kernel,
        grid=(t // tm,),
        in_specs=[pl.BlockSpec((tm, LANES), lambda i: (i, 0))],
        out_specs=[col, col, col, col],
        out_shape=[jax.ShapeDtypeStruct((t, 1), I32), jax.ShapeDtypeStruct((t, 1), I32),
                   jax.ShapeDtypeStruct((t, 1), F32), jax.ShapeDtypeStruct((t, 1), F32)],
        compiler_params=_cparams(("parallel",)),
        name="moe_router",
    )(logits)


HI16 = -65536


def _pack_halves(x):
    d = x.shape[1] // 2
    lo = pltpu.bitcast(x[:, :d].astype(BF16).astype(F32), I32)
    hi = pltpu.bitcast(x[:, d:].astype(BF16).astype(F32), I32)
    return lax.shift_right_logical(lo, 16) | (hi & HI16)


def _unpack_halves(u):
    return pltpu.bitcast(lax.shift_left(u, 16), F32), pltpu.bitcast(u & HI16, F32)


def _moe_rank_kernel(e1_ref, e2_ref, r1_ref, r2_ref, cnt_ref, base):
    i = pl.program_id(0)
    tm = e1_ref.shape[0]

    @pl.when(i == 0)
    def _():
        base[...] = jnp.zeros(base.shape, F32)

    lane = lax.broadcasted_iota(I32, (tm, LANES), 1)
    rr = lax.broadcasted_iota(I32, (tm, tm), 0)
    cc = lax.broadcasted_iota(I32, (tm, tm), 1)
    earlier = jnp.where(cc < rr, 1.0, 0.0).astype(BF16)
    for slot, (e_ref, r_ref) in enumerate(((e1_ref, r1_ref), (e2_ref, r2_ref))):
        oh = jnp.where(lane == e_ref[...], 1.0, 0.0)
        before = _dot(earlier, oh.astype(BF16)) + base[slot:slot + 1, :]
        r_ref[...] = jnp.sum(oh * before, axis=1, keepdims=True).astype(I32)
        base[slot:slot + 1, :] = base[slot:slot + 1, :] + jnp.sum(oh, axis=0, keepdims=True)
    cnt_ref[...] = base[...]


def _moe_rank(e1, e2, tm):
    t = e1.shape[0]
    col = pl.BlockSpec((tm, 1), lambda i: (i, 0))
    return pl.pallas_call(
        _moe_rank_kernel,
        grid=(t // tm,),
        in_specs=[col, col],
        out_specs=[col, col, pl.BlockSpec((8, LANES), lambda i: (0, 0))],
        out_shape=[jax.ShapeDtypeStruct((t, 1), I32), jax.ShapeDtypeStruct((t, 1), I32),
                   jax.ShapeDtypeStruct((8, LANES), F32)],
        scratch_shapes=[pltpu.VMEM((8, LANES), F32)],
        compiler_params=_cparams(("arbitrary",)),
        name="moe_rank",
    )(e1, e2)


def _moe_place_kernel(e1_ref, e2_ref, r1_ref, r2_ref, cnt_ref, p1_ref, p2_ref, be_ref, nu_ref):
    tm = e1_ref.shape[0]
    nbp = be_ref.shape[0]
    lane8 = lax.broadcasted_iota(I32, (8, LANES), 1)
    cnt = cnt_ref[...].astype(I32)
    c0 = jnp.broadcast_to(cnt[0:1], (8, LANES))
    c1 = jnp.broadcast_to(cnt[1:2], (8, LANES))
    blk_shift = MOE_BLOCK.bit_length() - 1
    padded = lax.shift_left(lax.shift_right_logical(c0 + c1 + (MOE_BLOCK - 1), blk_shift), blk_shift)
    pad_end = padded
    s = 1
    while s < LANES:
        pad_end = pad_end + jnp.where(lane8 >= s, pltpu.roll(pad_end, s, 1), 0)
        s *= 2
    start0 = (pad_end - padded).astype(F32)
    start1 = (pad_end - padded + c0).astype(F32)

    lane = lax.broadcasted_iota(I32, (tm, LANES), 1)
    for e_ref, r_ref, p_ref, start in ((e1_ref, r1_ref, p1_ref, start0), (e2_ref, r2_ref, p2_ref, start1)):
        seg = jnp.sum(jnp.where(lane == e_ref[...], start[0:1], 0.0), axis=1, keepdims=True)
        p_ref[...] = seg.astype(I32) + r_ref[...]

    block_row = lax.broadcasted_iota(I32, (nbp, LANES), 0) * MOE_BLOCK
    lane_b = lax.broadcasted_iota(I32, (nbp, LANES), 1)
    ended = (pad_end[0:1] <= block_row) & (lane_b < N_EXPERTS)
    be = jnp.sum(jnp.where(ended, 1.0, 0.0), axis=1, keepdims=True)
    be_ref[...] = jnp.minimum(be, N_EXPERTS - 1.0).astype(I32)
    total = jnp.max(pad_end, axis=1, keepdims=True)
    sub = lax.broadcasted_iota(I32, (8, LANES), 0)
    n_used = jnp.broadcast_to(lax.shift_right_logical(total, blk_shift), (8, LANES))
    nu_ref[...] = jnp.where(sub == 0, n_used, jnp.where(sub == 1, pad_end, jnp.where(sub == 2, padded, 0)))


def _moe_place(e1, e2, r1, r2, cnt, nb, tm):
    t = e1.shape[0]
    col = pl.BlockSpec((tm, 1), lambda i: (i, 0))
    return pl.pallas_call(
        _moe_place_kernel,
        grid=(t // tm,),
        in_specs=[col, col, col, col, pl.BlockSpec((8, LANES), lambda i: (0, 0))],
        out_specs=[col, col, pl.BlockSpec((nb, 1), lambda i: (0, 0)),
                   pl.BlockSpec((8, LANES), lambda i: (0, 0))],
        out_shape=[jax.ShapeDtypeStruct((t, 1), I32), jax.ShapeDtypeStruct((t, 1), I32),
                   jax.ShapeDtypeStruct((nb, 1), I32), jax.ShapeDtypeStruct((8, LANES), I32)],
        compiler_params=_cparams(("arbitrary",)),
        name="moe_place",
    )(e1, e2, r1, r2, cnt)


def _moe_dispatch_kernel(seg_ref, p1_ref, p2_ref, x_ref, xs_hbm, pk, zbuf, sem, zsem):
    i = pl.program_id(0)
    n = pl.num_programs(0)
    slot = lax.rem(i, 2)
    blk = x_ref.shape[0]
    nb = xs_hbm.shape[0] // blk

    def row_copy(s, r, pos):
        return pltpu.make_async_copy(pk.at[s, pl.ds(r, 1)], xs_hbm.at[pl.ds(pos, 1)], sem.at[s])

    def drain(s):
        for _ in range(2):
            pltpu.make_async_copy(pk.at[s], xs_hbm.at[pl.ds(0, blk)], sem.at[s]).wait()

    def zero_copy(row0):
        return pltpu.make_async_copy(zbuf, xs_hbm.at[pl.ds(pl.multiple_of(row0, blk), blk)], zsem)

    @pl.when(i == 0)
    def _():
        zbuf[...] = jnp.zeros(zbuf.shape, zbuf.dtype)
        n_used = seg_ref[0, 0]

        def seg_start(e, c):
            @pl.when(seg_ref[2, e] > 0)
            def _():
                zero_copy(seg_ref[1, e] - blk).start()
            return c

        def tail_start(b, c):
            zero_copy(b * blk).start()
            return c

        def seg_wait(e, c):
            @pl.when(seg_ref[2, e] > 0)
            def _():
                zero_copy(0).wait()
            return c

        def tail_wait(b, c):
            zero_copy(0).wait()
            return c

        lax.fori_loop(0, N_EXPERTS, seg_start, 0)
        lax.fori_loop(n_used, nb, tail_start, 0)
        lax.fori_loop(0, N_EXPERTS, seg_wait, 0)
        lax.fori_loop(n_used, nb, tail_wait, 0)

    @pl.when(i >= 2)
    def _():
        drain(slot)

    pk[slot] = _pack_halves(x_ref[...])

    def issue(r, c):
        row_copy(slot, r, p1_ref[0, 0, r]).start()
        row_copy(slot, r, p2_ref[0, 0, r]).start()
        return c

    lax.fori_loop(0, blk, issue, 0, unroll=8)

    @pl.when(i == n - 1)
    def _():
        drain(slot)

    @pl.when((i == n - 1) & (i >= 1))
    def _():
        drain(1 - slot)


def _moe_dispatch(x2, p1, p2, seg, nb):
    t, d = x2.shape
    nt = t // MOE_BLOCK
    rows = nb * MOE_BLOCK
    pos = pl.BlockSpec((1, 1, MOE_BLOCK), lambda i: (i, 0, 0), memory_space=pltpu.SMEM)
    return pl.pallas_call(
        _moe_dispatch_kernel,
        grid=(nt,),
        in_specs=[pl.BlockSpec(memory_space=pltpu.SMEM), pos, pos,
                  pl.BlockSpec((MOE_BLOCK, d), lambda i: (i, 0))],
        out_specs=pl.BlockSpec(memory_space=pl.ANY),
        out_shape=jax.ShapeDtypeStruct((rows, d // 2), I32),
        scratch_shapes=[pltpu.VMEM((2, MOE_BLOCK, d // 2), I32),
                        pltpu.VMEM((MOE_BLOCK, d // 2), I32),
                        pltpu.SemaphoreType.DMA((2,)),
                        pltpu.SemaphoreType.DMA(())],
        compiler_params=_cparams(("arbitrary",)),
        name="moe_dispatch",
    )(seg, p1.reshape(nt, 1, MOE_BLOCK), p2.reshape(nt, 1, MOE_BLOCK), x2)


MAT_PIECES = 4
N_PIECES = 3 * MAT_PIECES
PIECES_PER_BLOCK = 3
PIECE_RING = 4


def _moe_ffn_kernel(be_ref, nu_ref, xs_ref, wg_hbm, wu_hbm, wd_hbm, ys_ref,
                    wgb, wub, wdb, sa, sb, sem, st):
    i = pl.program_id(0)
    nb = be_ref.shape[0]
    n_used = nu_ref[0]
    ra = wgb.shape[1] // MAT_PIECES
    rb = wdb.shape[1] // MAT_PIECES

    def piece_copy(p, e):
        k = p % PIECE_RING
        m, r = divmod(p, MAT_PIECES)
        if m == 0:
            return pltpu.make_async_copy(wg_hbm.at[e, pl.ds(r * ra, ra)], sa.at[k], sem.at[k])
        if m == 1:
            return pltpu.make_async_copy(wu_hbm.at[e, pl.ds(r * ra, ra)], sa.at[k], sem.at[k])
        return pltpu.make_async_copy(wd_hbm.at[e, pl.ds(r * rb, rb)], sb.at[k], sem.at[k])

    def piece_round(p, slot):
        k = p % PIECE_RING
        m, r = divmod(p, MAT_PIECES)
        if m == 0:
            wgb[slot, pl.ds(r * ra, ra), :] = sa[k].astype(BF16)
        elif m == 1:
            wub[slot, pl.ds(r * ra, ra), :] = sa[k].astype(BF16)
        else:
            wdb[slot, pl.ds(r * rb, rb), :] = sb[k].astype(BF16)

    def start_one():
        e, started, finished = st[1], st[2], st[3]
        can = (started < N_PIECES) & (started - finished < PIECE_RING)
        for p in range(N_PIECES):
            @pl.when(can & (started == p))
            def _():
                piece_copy(p, e).start(priority=p % 2)
        st[2] = started + can.astype(I32)

    def finish_one(slot):
        e, finished = st[1], st[3]

        @pl.when((e >= 0) & (finished < N_PIECES))
        def _():
            for p in range(N_PIECES):
                @pl.when(finished == p)
                def _():
                    piece_copy(p, e).wait()
                    piece_round(p, slot)
            st[3] = finished + 1
            start_one()

    def prepare(e):
        st[1] = e
        st[2] = 0
        st[3] = 0

        @pl.when(e >= 0)
        def _():
            for _ in range(PIECE_RING):
                start_one()

    @pl.when(i < n_used)
    def _():
        e = be_ref[i]

        @pl.when(i == 0)
        def _():
            st[0] = 1
            prepare(e)

        @pl.when((i == 0) | (e != be_ref[jnp.maximum(i - 1, 0)]))
        def _():
            slot = 1 - st[0]

            def fin(_, c):
                finish_one(slot)
                return c

            lax.fori_loop(0, N_PIECES, fin, 0)
            st[0] = slot
            k = lax.while_loop(lambda k: (k < n_used) & (be_ref[jnp.minimum(k, nb - 1)] == e),
                               lambda k: k + 1, i + 1)
            prepare(jnp.where(k < n_used, be_ref[jnp.minimum(k, nb - 1)], -1))

        def ahead(_, c):
            finish_one(1 - st[0])
            return c

        lax.fori_loop(0, PIECES_PER_BLOCK, ahead, 0)

        slot = st[0]
        lo, hi = _unpack_halves(xs_ref[...])
        lo = lo.astype(BF16)
        hi = hi.astype(BF16)
        d2 = lo.shape[1]
        g = _dot(lo, wgb[slot, :d2]) + _dot(hi, wgb[slot, d2:])
        u = _dot(lo, wub[slot, :d2]) + _dot(hi, wub[slot, d2:])
        hmid = (g * jax.nn.sigmoid(g) * u).astype(BF16)
        ys_ref[...] = _pack_halves(_dot(hmid, wdb[slot]))

    @pl.when(i >= n_used)
    def _():
        ys_ref[...] = jnp.zeros(ys_ref.shape, ys_ref.dtype)


def _moe_ffn(xs, w_gate, w_up, w_down, block_expert, n_used):
    rows, d2 = xs.shape
    nb = rows // MOE_BLOCK
    d = 2 * d2
    ff = w_gate.shape[2]
    grid_spec = pltpu.PrefetchScalarGridSpec(
        num_scalar_prefetch=2,
        grid=(nb,),
        in_specs=[pl.BlockSpec((MOE_BLOCK, d2), lambda i, be, nu: (i, 0)),
                  pl.BlockSpec(memory_space=pl.ANY),
                  pl.BlockSpec(memory_space=pl.ANY),
                  pl.BlockSpec(memory_space=pl.ANY)],
        out_specs=pl.BlockSpec((MOE_BLOCK, d2), lambda i, be, nu: (i, 0)),
        scratch_shapes=[pltpu.VMEM((2, d, ff), BF16), pltpu.VMEM((2, d, ff), BF16),
                        pltpu.VMEM((2, ff, d), BF16),
                        pltpu.VMEM((PIECE_RING, d // MAT_PIECES, ff), F32),
                        pltpu.VMEM((PIECE_RING, ff // MAT_PIECES, d), F32),
                        pltpu.SemaphoreType.DMA((PIECE_RING,)),
                        pltpu.SMEM((4,), I32)],
    )
    return pl.pallas_call(
        _moe_ffn_kernel,
        grid_spec=grid_spec,
        out_shape=jax.ShapeDtypeStruct((rows, d2), I32),
        compiler_params=_cparams(("arbitrary",)),
        name="moe_ffn",
    )(block_expert, n_used, xs, w_gate, w_up, w_down)


def _moe_combine_kernel(p1_ref, p2_ref, q1_ref, q2_ref, x_ref, g1_ref, g2_ref, g_ref, b_ref, ys_hbm,
                        o_ref, yb, sem):
    i = pl.program_id(0)
    n = pl.num_programs(0)
    slot = lax.rem(i, 2)
    blk = x_ref.shape[0]

    def row_copy(s, k, r, pos):
        return pltpu.make_async_copy(ys_hbm.at[pl.ds(pos, 1)], yb.at[s, k, pl.ds(r, 1)], sem.at[s])

    def fetch(s, a_ref, b_ref2):
        def body(r, c):
            row_copy(s, 0, r, a_ref[0, 0, r]).start()
            row_copy(s, 1, r, b_ref2[0, 0, r]).start()
            return c
        lax.fori_loop(0, blk, body, 0, unroll=8)

    @pl.when(i == 0)
    def _():
        fetch(0, p1_ref, p2_ref)

    @pl.when(i + 1 < n)
    def _():
        fetch(1 - slot, q1_ref, q2_ref)

    for k in range(2):
        pltpu.make_async_copy(ys_hbm.at[pl.ds(0, blk)], yb.at[slot, k], sem.at[slot]).wait()
    y1 = jnp.concatenate(_unpack_halves(yb[slot, 0]), axis=1)
    y2 = jnp.concatenate(_unpack_halves(yb[slot, 1]), axis=1)
    y = y1 * g1_ref[...] + y2 * g2_ref[...]
    o_ref[...] = _ln_rows(ALPHA * x_ref[...] + y, g_ref[...], b_ref[...])


def _moe_combine(x2, ys, p1, p2, g1, g2, g, b):
    t, d = x2.shape
    nt = t // MOE_BLOCK
    p1 = p1.reshape(nt, 1, MOE_BLOCK)
    p2 = p2.reshape(nt, 1, MOE_BLOCK)
    pos = pl.BlockSpec((1, 1, MOE_BLOCK), lambda i: (i, 0, 0), memory_space=pltpu.SMEM)
    nxt = pl.BlockSpec((1, 1, MOE_BLOCK), lambda i: (jnp.minimum(i + 1, nt - 1), 0, 0),
                       memory_space=pltpu.SMEM)
    col = pl.BlockSpec((MOE_BLOCK, 1), lambda i: (i, 0))
    vec = pl.BlockSpec((1, d), lambda i: (0, 0))
    return pl.pallas_call(
        _moe_combine_kernel,
        grid=(nt,),
        in_specs=[pos, pos, nxt, nxt, pl.BlockSpec((MOE_BLOCK, d), lambda i: (i, 0)), col, col, vec, vec,
                  pl.BlockSpec(memory_space=pl.ANY)],
        out_specs=pl.BlockSpec((MOE_BLOCK, d), lambda i: (i, 0)),
        out_shape=jax.ShapeDtypeStruct((t, d), F32),
        scratch_shapes=[pltpu.VMEM((2, 2, MOE_BLOCK, d // 2), I32),
                        pltpu.SemaphoreType.DMA((2,))],
        compiler_params=_cparams(("arbitrary",)),
        name="moe_combine",
    )(p1, p2, p1, p2, x2, g1, g2, g.reshape(1, d), b.reshape(1, d), ys)


def _tile(n, pref):
    return pref if n % pref == 0 else n


def _layer(x, mem, w_in, conv_w, conv_b, conv_ln_g, conv_ln_b, kv_norm_g, w_uk, w_uv, rel_bias,
           conv_out_g, attn_out_g, w_out, ln1_g, ln1_b, w_mq, w_mk, w_mv, w_mo, ln2_g, ln2_b,
           w_router_grp, w_router_exp, w_gate, w_up, w_down, ln3_g, ln3_b):
    bsz, seq, d = x.shape
    t = bsz * seq
    d_conv = conv_w.shape[1]
    d_attn = N_HEADS * HEAD_DIM
    c_glu = 2 * d_conv
    c_qi = H_IDX * D_IDX
    o_q, o_kv = c_glu, c_glu + d_attn
    o_qi = o_kv + KV_RANK
    o_ki = o_qi + c_qi
    topk = min(TOPK_MAX, seq // 4)

    xf = x.reshape(t, d)
    w_inb = w_in.astype(BF16)
    tail_w = jnp.concatenate([w_inb[:, o_kv:o_qi], w_inb[:, o_ki:]], axis=1)
    tail_w = jnp.pad(tail_w, ((0, 0), (0, KV_RANK + LANES - tail_w.shape[1])))
    tm = _tile(t, 1024)

    tail, xb = _matmul_cast(xf, tail_w, F32, _tile(t, 512), "proj_tail")
    u = _matmul_cols(xb, w_inb, c_glu, BF16, tm, 512, "proj_glu")
    q_hm = _matmul_heads(xb, w_inb, o_q, d_attn, BF16, tm, 512, "proj_q")
    qi_hm = _matmul_heads(xb, w_inb, o_qi, c_qi, BF16, tm, 512, "proj_qidx")

    conv_n = _conformer_conv(u, bsz, seq, conv_w, conv_b, conv_ln_g, conv_ln_b, conv_out_g)
    tk = min(ATT_BLOCK, seq)
    ckv_n, ckv_t, kia, kib, kw = _prep_latent(tail, kv_norm_g, tm, tk)
    ckv_t = ckv_t.reshape(bsz, seq // tk, KV_RANK, tk)
    mask = _indexer_mask(qi_hm, kw, kia, kib, bsz, seq, topk)
    w_uk_t = w_uk.transpose(0, 2, 1).reshape(d_attn, KV_RANK).astype(BF16)
    w_uv_all = w_uv.transpose(1, 0, 2).reshape(KV_RANK, d_attn).astype(BF16)
    k_t = _key_heads(w_uk_t, ckv_t)
    v_hm = _matmul_value_heads(ckv_n, w_uv_all, tm, 512, "value_heads")
    attn = _head_attention(q_hm, k_t, v_hm, mask, rel_bias, bsz, seq)

    pre1 = _matmul2_residual(conv_n, attn, attn_out_g, w_out.astype(BF16), xf, tm, 512, "out_proj")

    mem_len = mem.shape[1]
    memb = mem.reshape(bsz * mem_len, d).astype(BF16)
    w_kv = jnp.concatenate([w_mk, w_mv], axis=1).astype(BF16)
    kv = _matmul(memb, w_kv, BF16, _tile(bsz * mem_len, 512), 512, "mem_kv")
    w_router = jnp.concatenate([w_router_grp, w_router_exp], axis=1)
    w_router = jnp.pad(w_router, ((0, 0), (0, LANES - w_router.shape[1]))).astype(BF16)
    x2, logits = _memory_attention(pre1, ln1_g, ln1_b, kv, w_mq.astype(BF16), w_mo.astype(BF16),
                                   ln2_g, ln2_b, w_router, bsz, seq, 256)

    e1, e2, g1, g2 = _router(logits, _tile(t, 1024))
    nb = (2 * t + N_EXPERTS * (MOE_BLOCK - 1) + MOE_BLOCK - 1) // MOE_BLOCK
    r1, r2, cnt = _moe_rank(e1, e2, 512)
    p1, p2, block_expert, n_used = _moe_place(e1, e2, r1, r2, cnt, nb, 512)
    xs = _moe_dispatch(x2, p1, p2, n_used, nb)
    ys = _moe_ffn(xs, w_gate, w_up, w_down, block_expert.reshape(nb), n_used[0, 0:1])
    x3 = _moe_combine(x2, ys, p1, p2, g1, g2, ln3_g, ln3_b)
    return x3.reshape(bsz, seq, d)


def kernel(x, mem, w_in, conv_w, conv_b, conv_ln_g, conv_ln_b, kv_norm_g, w_uk, w_uv, rel_bias, conv_out_g, attn_out_g, w_out, ln1_g, ln1_b, w_mq, w_mk, w_mv, w_mo, ln2_g, ln2_b, w_router_grp, w_router_exp, w_gate, w_up, w_down, ln3_g, ln3_b):
    for l in range(w_in.shape[0]):
        x = _layer(x, mem, w_in[l], conv_w[l], conv_b[l], conv_ln_g[l], conv_ln_b[l], kv_norm_g[l],
                   w_uk[l], w_uv[l], rel_bias, conv_out_g[l], attn_out_g[l], w_out[l], ln1_g[l], ln1_b[l],
                   w_mq[l], w_mk[l], w_mv[l], w_mo[l], ln2_g[l], ln2_b[l], w_router_grp[l],
                   w_router_exp[l], w_gate[l], w_up[l], w_down[l], ln3_g[l], ln3_b[l])
    return x
```

```python
import functools
import math

import jax
import jax.numpy as jnp
from jax import lax
from jax.experimental import pallas as pl
from jax.experimental.pallas import tpu as pltpu

F32 = jnp.float32
BF16 = jnp.bfloat16
I32 = jnp.int32

DEPTH = 1
CONV_WIDTH = 31
N_HEADS = 16
HEAD_DIM = 128
KV_RANK = 512
H_IDX = 32
D_IDX = 64
TOPK_MAX = 256
N_BUCKETS = 32
MAX_DIST = 128
MEM_HEADS = 4
MEM_HEAD_DIM = 128
N_GROUPS = 8
EXP_PER_GROUP = 8
N_EXPERTS = N_GROUPS * EXP_PER_GROUP
MOE_BLOCK = 128
ALPHA = (2.0 * DEPTH) ** 0.25
LN_EPS = 1e-5

LANES = 128
SUBLANES = 8
V7X_VMEM_BYTES = 64 * 1024 * 1024
VMEM_LIMIT = V7X_VMEM_BYTES * 7 // 8
NEG = -1e30
INT_MIN = -(2 ** 31)

LOG2E = 1.4426950408889634

ATT_BLOCK = 256
ATT_HEAD_GROUP = 8
CONV_TS = 256
CONV_HALO = 32
CONV_CC = 256
CONV_RC = 32


def _cparams(sem):
    return pltpu.CompilerParams(dimension_semantics=sem, vmem_limit_bytes=VMEM_LIMIT)


def _dot(a, b):
    return jnp.dot(a, b, preferred_element_type=F32)


def _dot_nt(a, b):
    return lax.dot_general(a, b, (((1,), (1,)), ((), ())), preferred_element_type=F32)


def _mm_kernel(a_ref, b_ref, o_ref):
    o_ref[...] = _dot(a_ref[...], b_ref[...]).astype(o_ref.dtype)


def _mm_cast_kernel(a_ref, b_ref, o_ref, ab_ref):
    ab = a_ref[...].astype(ab_ref.dtype)
    ab_ref[...] = ab
    o_ref[...] = _dot(ab, b_ref[...]).astype(o_ref.dtype)


def _matmul_cast(a, b, out_dtype, tm, name):
    m, k = a.shape
    n = b.shape[1]
    return pl.pallas_call(
        _mm_cast_kernel,
        grid=(m // tm,),
        in_specs=[pl.BlockSpec((tm, k), lambda i: (i, 0)),
                  pl.BlockSpec((k, n), lambda i: (0, 0))],
        out_specs=[pl.BlockSpec((tm, n), lambda i: (i, 0)),
                   pl.BlockSpec((tm, k), lambda i: (i, 0))],
        out_shape=[jax.ShapeDtypeStruct((m, n), out_dtype), jax.ShapeDtypeStruct((m, k), b.dtype)],
        compiler_params=_cparams(("parallel",)),
        name=name,
    )(a, b)


def _matmul(a, b, out_dtype, tm, tn, name):
    m, k = a.shape
    n = b.shape[1]
    return pl.pallas_call(
        _mm_kernel,
        grid=(m // tm, n // tn),
        in_specs=[pl.BlockSpec((tm, k), lambda i, j: (i, 0)),
                  pl.BlockSpec((k, tn), lambda i, j: (0, j))],
        out_specs=pl.BlockSpec((tm, tn), lambda i, j: (i, j)),
        out_shape=jax.ShapeDtypeStruct((m, n), out_dtype),
        compiler_params=_cparams(("parallel", "parallel")),
        name=name,
    )(a, b)


def _matmul_cols(a, b, n, out_dtype, tm, tn, name):
    m, k = a.shape
    return pl.pallas_call(
        _mm_kernel,
        grid=(m // tm, n // tn),
        in_specs=[pl.BlockSpec((tm, k), lambda i, j: (i, 0)),
                  pl.BlockSpec((k, tn), lambda i, j: (0, j))],
        out_specs=pl.BlockSpec((tm, tn), lambda i, j: (i, j)),
        out_shape=jax.ShapeDtypeStruct((m, n), out_dtype),
        compiler_params=_cparams(("parallel", "parallel")),
        name=name,
    )(a, b)


def _mm_heads_kernel(a_ref, b_ref, o_ref):
    r = _dot(a_ref[...], b_ref[...])
    for p in range(o_ref.shape[0]):
        o_ref[p] = r[:, p * LANES:(p + 1) * LANES].astype(o_ref.dtype)


def _matmul_heads(a, b, col0, n, out_dtype, tm, tn, name):
    m, k = a.shape
    assert col0 % tn == 0
    return pl.pallas_call(
        _mm_heads_kernel,
        grid=(m // tm, n // tn),
        in_specs=[pl.BlockSpec((tm, k), lambda i, j: (i, 0)),
                  pl.BlockSpec((k, tn), lambda i, j: (0, col0 // tn + j))],
        out_specs=pl.BlockSpec((tn // LANES, tm, LANES), lambda i, j: (j, i, 0)),
        out_shape=jax.ShapeDtypeStruct((n // LANES, m, LANES), out_dtype),
        compiler_params=_cparams(("parallel", "parallel")),
        name=name,
    )(a, b)


def _mm_value_heads_kernel(a_ref, b_ref, o_ref):
    r = _dot(a_ref[...], b_ref[...])
    tm = r.shape[0]
    for p in range(o_ref.shape[0]):
        o_ref[p, :, 0:LANES] = r[:, p * LANES:(p + 1) * LANES].astype(o_ref.dtype)
        o_ref[p, :, LANES:2 * LANES] = jnp.ones((tm, LANES), o_ref.dtype)


def _matmul_value_heads(a, b, tm, tn, name):
    m, k = a.shape
    n = b.shape[1]
    return pl.pallas_call(
        _mm_value_heads_kernel,
        grid=(m // tm, n // tn),
        in_specs=[pl.BlockSpec((tm, k), lambda i, j: (i, 0)),
                  pl.BlockSpec((k, tn), lambda i, j: (0, j))],
        out_specs=pl.BlockSpec((tn // LANES, tm, 2 * LANES), lambda i, j: (j, i, 0)),
        out_shape=jax.ShapeDtypeStruct((n // LANES, m, 2 * LANES), BF16),
        compiler_params=_cparams(("parallel", "parallel")),
        name=name,
    )(a, b)


def _mm2_res_kernel(a1_ref, a2_ref, g2_ref, w1_ref, w2_ref, r_ref, o_ref, a2n):
    @pl.when(pl.program_id(1) == 0)
    def _():
        x = a2_ref[...].astype(F32)
        ms = jnp.mean(x * x, axis=-1, keepdims=True)
        a2n[...] = (x * lax.rsqrt(ms + LN_EPS) * g2_ref[...]).astype(a2n.dtype)

    o_ref[...] = (ALPHA * r_ref[...] + _dot(a1_ref[...], w1_ref[...])
                  + _dot(a2n[...], w2_ref[...]))


def _matmul2_residual(a1, a2, g2, w, res, tm, tn, name):
    m, k = a1.shape
    assert a2.shape == a1.shape and w.shape[0] == 2 * k
    n = w.shape[1]
    return pl.pallas_call(
        _mm2_res_kernel,
        grid=(m // tm, n // tn),
        in_specs=[pl.BlockSpec((tm, k), lambda i, j: (i, 0)),
                  pl.BlockSpec((tm, k), lambda i, j: (i, 0)),
                  pl.BlockSpec((1, k), lambda i, j: (0, 0)),
                  pl.BlockSpec((k, tn), lambda i, j: (0, j)),
                  pl.BlockSpec((k, tn), lambda i, j: (1, j)),
                  pl.BlockSpec((tm, tn), lambda i, j: (i, j))],
        out_specs=pl.BlockSpec((tm, tn), lambda i, j: (i, j)),
        out_shape=jax.ShapeDtypeStruct((m, n), F32),
        scratch_shapes=[pltpu.VMEM((tm, k), BF16)],
        compiler_params=_cparams(("parallel", "arbitrary")),
        name=name,
    )(a1, a2, g2.reshape(1, k), w, w, res)


def _ln_rows(x, g, b):
    mu = jnp.mean(x, axis=-1, keepdims=True)
    xc = x - mu
    var = jnp.mean(xc * xc, axis=-1, keepdims=True)
    return xc * lax.rsqrt(var + LN_EPS) * g + b


def _conv_kernel(a_ref, g_ref, cw_ref, cb_ref, lg_ref, lb_ref, og_ref, o_ref, hbuf, ybuf, hs):
    ts = a_ref.shape[0]
    nch = hbuf.shape[0]
    cc = hbuf.shape[2]
    d_conv = nch * cc

    nrow = CONV_HALO + ts

    @pl.when(pl.program_id(1) == 0)
    def _():
        hbuf[:, 0:CONV_HALO, :] = jnp.zeros((nch, CONV_HALO, cc), F32)
        hbuf[:, nrow:nrow + SUBLANES, :] = jnp.zeros((nch, SUBLANES, cc), F32)

    for c in range(nch):
        a = a_ref[:, c * cc:(c + 1) * cc].astype(F32)
        g = g_ref[:, c * cc:(c + 1) * cc].astype(F32)
        hbuf[c, CONV_HALO:nrow, :] = a * jax.nn.sigmoid(g)

    first = CONV_HALO - (CONV_WIDTH - 1)

    def chunk_body(c, carry):
        for o in range(1, SUBLANES):
            hs[o - 1] = hbuf[c, o:o + nrow, :]
        for r0 in range(0, ts, CONV_RC):
            acc = jnp.zeros((CONV_RC, cc), F32)
            for j in range(CONV_WIDTH):
                o = (first + j) % SUBLANES
                base = r0 + first + j - o
                rows = hbuf[c, base:base + CONV_RC, :] if o == 0 else hs[o - 1, base:base + CONV_RC, :]
                acc = acc + cw_ref[c, j:j + 1, :] * rows
            ybuf[c, r0:r0 + CONV_RC, :] = acc + cb_ref[c]
        hbuf[c, 0:CONV_HALO, :] = hbuf[c, ts:nrow, :]
        return carry

    lax.fori_loop(0, nch, chunk_body, 0)

    s1 = jnp.zeros((ts, 1), F32)
    for c in range(nch):
        s1 = s1 + jnp.sum(ybuf[c], axis=1, keepdims=True)
    mu = s1 * (1.0 / d_conv)
    s2 = jnp.zeros((ts, 1), F32)
    for c in range(nch):
        yc = ybuf[c] - mu
        s2 = s2 + jnp.sum(yc * yc, axis=1, keepdims=True)
    rstd = lax.rsqrt(s2 * (1.0 / d_conv) + LN_EPS)
    s3 = jnp.zeros((ts, 1), F32)
    for c in range(nch):
        z = (ybuf[c] - mu) * rstd * lg_ref[:, c * cc:(c + 1) * cc] + lb_ref[:, c * cc:(c + 1) * cc]
        z = z * jax.nn.sigmoid(z)
        ybuf[c] = z
        s3 = s3 + jnp.sum(z * z, axis=1, keepdims=True)
    rr = lax.rsqrt(s3 * (1.0 / d_conv) + LN_EPS)
    for c in range(nch):
        o_ref[:, c * cc:(c + 1) * cc] = (ybuf[c] * rr * og_ref[:, c * cc:(c + 1) * cc]).astype(o_ref.dtype)


def _conformer_conv(u, bsz, seq, conv_w, conv_b, ln_g, ln_b, out_g):
    d_conv = u.shape[1] // 2
    ts = min(CONV_TS, seq)
    nch = d_conv // CONV_CC
    nt = seq // ts
    cw = conv_w.reshape(CONV_WIDTH, nch, CONV_CC).transpose(1, 0, 2)
    cb = conv_b.reshape(nch, 1, CONV_CC)
    vec = pl.BlockSpec((1, d_conv), lambda b, i: (0, 0))
    return pl.pallas_call(
        _conv_kernel,
        grid=(bsz, nt),
        in_specs=[pl.BlockSpec((ts, d_conv), lambda b, i: (b * nt + i, 0)),
                  pl.BlockSpec((ts, d_conv), lambda b, i: (b * nt + i, 1)),
                  pl.BlockSpec((nch, CONV_WIDTH, CONV_CC), lambda b, i: (0, 0, 0)),
                  pl.BlockSpec((nch, 1, CONV_CC), lambda b, i: (0, 0, 0)),
                  vec, vec, vec],
        out_specs=pl.BlockSpec((ts, d_conv), lambda b, i: (b * nt + i, 0)),
        out_shape=jax.ShapeDtypeStruct((bsz * seq, d_conv), BF16),
        scratch_shapes=[pltpu.VMEM((nch, CONV_HALO + ts + SUBLANES, CONV_CC), F32),
                        pltpu.VMEM((nch, ts, CONV_CC), F32),
                        pltpu.VMEM((SUBLANES - 1, CONV_HALO + ts, CONV_CC), F32)],
        compiler_params=_cparams(("arbitrary", "arbitrary")),
        name="conformer_conv",
    )(u, u, cw, cb, ln_g.reshape(1, d_conv), ln_b.reshape(1, d_conv), out_g.reshape(1, d_conv))


def _prep_kernel(t_ref, g_ref, ckv_ref, ckvt_ref, kia_ref, kib_ref, kw_ref):
    ckv = t_ref[:, 0:KV_RANK]
    ms = jnp.mean(ckv * ckv, axis=-1, keepdims=True)
    ckv_n = ckv * lax.rsqrt(ms + LN_EPS) * g_ref[...]
    ckv_ref[...] = ckv_n.astype(ckv_ref.dtype)
    tk = ckvt_ref.shape[3]
    for c in range(ckvt_ref.shape[1]):
        ckvt_ref[0, c] = ckv_n[c * tk:(c + 1) * tk, :].T.astype(ckvt_ref.dtype)
    kw = t_ref[:, KV_RANK:KV_RANK + LANES]
    kw_ref[...] = kw
    lane = lax.broadcasted_iota(I32, kw.shape, 1)
    kia_ref[...] = jnp.where(lane < D_IDX, kw, 0.0).astype(kia_ref.dtype)
    kib_ref[...] = jnp.where(lane >= D_IDX, pltpu.roll(kw, D_IDX, 1), 0.0).astype(kib_ref.dtype)


def _prep_latent(tail, kv_norm_g, tm, tk):
    m, w = tail.shape
    return pl.pallas_call(
        _prep_kernel,
        grid=(m // tm,),
        in_specs=[pl.BlockSpec((tm, w), lambda i: (i, 0)),
                  pl.BlockSpec((1, KV_RANK), lambda i: (0, 0))],
        out_specs=[pl.BlockSpec((tm, KV_RANK), lambda i: (i, 0)),
                   pl.BlockSpec((1, tm // tk, KV_RANK, tk), lambda i: (i, 0, 0, 0)),
                   pl.BlockSpec((tm, LANES), lambda i: (i, 0)),
                   pl.BlockSpec((tm, LANES), lambda i: (i, 0)),
                   pl.BlockSpec((tm, LANES), lambda i: (i, 0))],
        out_shape=[jax.ShapeDtypeStruct((m, KV_RANK), BF16),
                   jax.ShapeDtypeStruct((m // tm, tm // tk, KV_RANK, tk), BF16),
                   jax.ShapeDtypeStruct((m, LANES), BF16),
                   jax.ShapeDtypeStruct((m, LANES), BF16),
                   jax.ShapeDtypeStruct((m, LANES), F32)],
        compiler_params=_cparams(("parallel",)),
        name="prep_latent",
    )(tail, kv_norm_g.reshape(1, KV_RANK))


def _indexer_t_kernel(qi_ref, kw_ref, kia_ref, kib_ref, o_ref, keybuf, *, topk):
    i = pl.program_id(1)
    npairs, tq, _ = qi_ref.shape
    nk = o_ref.shape[1]
    tk = o_ref.shape[3]
    kf = float(topk)
    group = 4
    slabs = tk // SUBLANES

    w_t = kw_ref[...].T
    key_idx = lax.broadcasted_iota(I32, (tk, tq), 0)
    qry_idx = lax.broadcasted_iota(I32, (tk, tq), 1) + i * tq

    def score_chunk(j, carry):
        k0 = pl.multiple_of(j * tk, tk)
        kd = jnp.concatenate([kia_ref[pl.ds(k0, tk), :], kib_ref[pl.ds(k0, tk), :]], axis=0)
        acc = jnp.zeros((tk, tq), F32)
        for p0 in range(0, npairs, group):
            rhs = qi_ref[p0:p0 + group].reshape(group * tq, LANES)
            zz = _dot_nt(kd, rhs)
            for p in range(group):
                h = 2 * (p0 + p)
                z = zz[:, p * tq:(p + 1) * tq]
                acc = (acc + w_t[D_IDX + h:D_IDX + h + 1, :] * jnp.maximum(z[0:tk], 0.0)
                       + w_t[D_IDX + h + 1:D_IDX + h + 2, :] * jnp.maximum(z[tk:2 * tk], 0.0))
        bits = pltpu.bitcast(acc, I32)
        key = jnp.where(bits >= 0, bits, bits ^ jnp.int32(0x7FFFFFFF))
        keybuf[j] = jnp.where(key_idx + j * tk <= qry_idx, key, INT_MIN)
        return carry

    def chunk_pairs(fn):
        def pair(p, carry):
            fn(2 * p, carry)
            fn(2 * p + 1, carry)
            return carry

        lax.fori_loop(0, lax.shift_right_logical(i + 1, 1), pair, 0)

        @pl.when((i & 1) == 0)
        def _():
            fn(i, 0)

    chunk_pairs(score_chunk)

    def count(pred):
        def one(j, c):
            hit = jnp.where(pred(keybuf[j], key_idx + j * tk), 1.0, 0.0)
            return c + jnp.sum(hit.reshape(slabs, SUBLANES, tq), axis=0)

        def pair(p, c):
            return one(2 * p + 1, one(2 * p, c))

        c = lax.fori_loop(0, lax.shift_right_logical(i + 1, 1), pair, jnp.zeros((SUBLANES, tq), F32))
        c = lax.cond((i & 1) == 0, lambda c: one(i, c), lambda c: c, c)
        return jnp.broadcast_to(jnp.sum(c, axis=0, keepdims=True), (SUBLANES, tq))

    def tile_rows(v):
        return jnp.concatenate([v] * slabs, axis=0)

    def count_ge(cand):
        cb = tile_rows(cand)
        return count(lambda kk, idx: kk >= cb)

    tau = jnp.where(count_ge(jnp.zeros((SUBLANES, tq), I32)) >= kf, 0, INT_MIN).astype(I32)

    def bit_body(it, tau):
        cand = tau | jnp.left_shift(jnp.int32(1), 30 - it)
        return jnp.where(count_ge(cand) >= kf, cand, tau)

    tau = lax.fori_loop(0, 31, bit_body, tau)
    n_ge = count_ge(tau)
    taub = tile_rows(tau)

    def tie_cut():
        need = kf - count(lambda kk, idx: kk > taub)

        def cut_body(it, cut):
            cand = cut + jnp.left_shift(jnp.int32(1), 30 - it)
            cb = tile_rows(cand)
            below = count(lambda kk, idx: (kk == taub) & (idx < cb))
            return jnp.where(below < need, cand, cut)

        return lax.fori_loop(0, 31, cut_body, jnp.zeros((SUBLANES, tq), I32))

    has_ties = jnp.max(n_ge) > kf
    cut = lax.cond(has_ties, tie_cut, lambda: jnp.full((SUBLANES, tq), 2 ** 30, I32))
    cutb = tile_rows(cut)
    eye = jnp.where(lax.broadcasted_iota(I32, (tq, tq), 0) == lax.broadcasted_iota(I32, (tq, tq), 1),
                    1.0, 0.0).astype(BF16)

    def write_chunk(j, carry):
        kk = keybuf[j]
        sel = ((kk > taub) | ((kk == taub) & (key_idx + j * tk <= cutb))) & (kk != INT_MIN)
        sel_qk = _dot_nt(eye, jnp.where(sel, 1.0, 0.0).astype(BF16))
        o_ref[0, j] = jnp.where(sel_qk > 0.5, 0.0, NEG).astype(o_ref.dtype)
        return carry

    chunk_pairs(write_chunk)

    def write_rest(j, carry):
        o_ref[0, j] = jnp.full((tq, tk), NEG, o_ref.dtype)
        return carry

    lax.fori_loop(i + 1, nk, write_rest, 0)


def _indexer_mask(qi_hm, kw, kia, kib, bsz, seq, topk):
    tq = min(ATT_BLOCK, seq)
    nq = seq // tq
    npairs = qi_hm.shape[0]
    return pl.pallas_call(
        functools.partial(_indexer_t_kernel, topk=topk),
        grid=(bsz, nq),
        in_specs=[pl.BlockSpec((npairs, tq, LANES), lambda b, i: (0, b * nq + i, 0)),
                  pl.BlockSpec((tq, LANES), lambda b, i: (b * nq + i, 0)),
                  pl.BlockSpec((seq, LANES), lambda b, i: (b, 0)),
                  pl.BlockSpec((seq, LANES), lambda b, i: (b, 0))],
        out_specs=pl.BlockSpec((1, nq, tq, tq), lambda b, i: (b * nq + i, 0, 0, 0)),
        out_shape=jax.ShapeDtypeStruct((bsz * nq, nq, tq, tq), BF16),
        scratch_shapes=[pltpu.VMEM((nq, tq, tq), I32)],
        compiler_params=_cparams(("parallel", "parallel")),
        name="indexer_mask",
    )(qi_hm, kw, kia, kib)


def _t5_bucket(dist):
    n = jnp.maximum(dist, 0)
    max_exact = N_BUCKETS // 2
    nf = jnp.maximum(n, 1).astype(F32)
    large = max_exact + (jnp.log(nf / max_exact) / math.log(MAX_DIST / max_exact)
                         * (N_BUCKETS - max_exact)).astype(I32)
    large = jnp.minimum(large, N_BUCKETS - 1)
    return jnp.where(n < max_exact, n, large)


def _key_heads_kernel(wt_ref, ct_ref, o_ref):
    r = _dot(wt_ref[...], ct_ref[0, 0]) * (HEAD_DIM ** -0.5 * LOG2E)
    for h in range(o_ref.shape[1]):
        o_ref[0, h, 0] = r[h * HEAD_DIM:(h + 1) * HEAD_DIM].astype(o_ref.dtype)


def _key_heads(w_uk_t, ckv_t):
    bsz, nk, r_lat, tk = ckv_t.shape
    nh = w_uk_t.shape[0] // HEAD_DIM
    return pl.pallas_call(
        _key_heads_kernel,
        grid=(bsz, nk),
        in_specs=[pl.BlockSpec((nh * HEAD_DIM, r_lat), lambda b, j: (0, 0)),
                  pl.BlockSpec((1, 1, r_lat, tk), lambda b, j: (b, j, 0, 0))],
        out_specs=pl.BlockSpec((1, nh, 1, HEAD_DIM, tk), lambda b, j: (b, 0, j, 0, 0)),
        out_shape=jax.ShapeDtypeStruct((bsz, nh, nk, HEAD_DIM, tk), BF16),
        compiler_params=_cparams(("parallel", "parallel")),
        name="key_heads",
    )(w_uk_t, ckv_t)


def _attn_kernel(relb_ref, q_ref, kt_ref, v_ref, mask_ref, o_ref, acc, m_s, btab):
    b = pl.program_id(0)
    g = pl.program_id(1)
    i = pl.program_id(2)
    hg, tq, _ = q_ref.shape
    nh = btab.shape[1]
    tk = mask_ref.shape[3]

    @pl.when((b == 0) & (g == 0) & (i == 0))
    def _():
        r = lax.broadcasted_iota(I32, (tq, tk), 0)
        c = lax.broadcasted_iota(I32, (tq, tk), 1)
        for t in range(2):
            bk = _t5_bucket(r - c + t * tk)

            def fill(h, carry):
                far = relb_ref[N_BUCKETS - 1, h]
                v = jnp.zeros((tq, tk), F32)
                for k in range(N_BUCKETS):
                    v = jnp.where(bk == k, (relb_ref[k, h] - far) * LOG2E, v)
                btab[t, h] = v
                return carry

            lax.fori_loop(0, nh, fill, 0)

    acc[...] = jnp.zeros(acc.shape, F32)
    m_s[...] = jnp.full(m_s.shape, NEG, F32)

    def lane_tile(v, n):
        return jnp.concatenate([v] * n, axis=1)

    def chunks(js, near):
        mks = [mask_ref[0, j].astype(F32) for j in js]
        for h in range(hg):
            for j, mk in zip(js, mks):
                s = _dot(q_ref[h], kt_ref[0, h, j]) + mk
                if near:
                    s = s + btab[i - j, g * hg + h]
                m_old = m_s[h]
                m_new = jnp.maximum(m_old, jnp.max(s, axis=1, keepdims=True))
                a = jnp.exp2(m_old - m_new)
                p = jnp.exp2(s - lane_tile(m_new, tk // LANES))
                rows = pl.ds(pl.multiple_of(j * tk, tk), tk)
                acc[h] = lane_tile(a, 2) * acc[h] + _dot(p.astype(BF16), v_ref[h, rows, :])
                m_s[h] = m_new

    n_far = jnp.maximum(i - 1, 0)

    def far_quad(p, carry):
        chunks([4 * p, 4 * p + 1, 4 * p + 2, 4 * p + 3], False)
        return carry

    lax.fori_loop(0, lax.shift_right_logical(n_far, 2), far_quad, 0)
    done = n_far & ~3

    @pl.when((n_far & 2) == 2)
    def _():
        chunks([done, done + 1], False)

    @pl.when((n_far & 1) == 1)
    def _():
        chunks([n_far - 1], False)

    @pl.when(i >= 1)
    def _():
        chunks([i - 1, i], True)

    @pl.when(i == 0)
    def _():
        chunks([i], True)

    for h in range(hg):
        o_ref[:, h * HEAD_DIM:(h + 1) * HEAD_DIM] = (
            acc[h, :, :HEAD_DIM] / acc[h, :, HEAD_DIM:]).astype(o_ref.dtype)


def _head_attention(q_hm, k_t, v_hm, mask, rel_bias, bsz, seq):
    nh = q_hm.shape[0]
    hg = ATT_HEAD_GROUP
    tq = mask.shape[2]
    nq = seq // tq
    assert tq == mask.shape[3] and tq >= MAX_DIST
    once = pl.Buffered(1)
    return pl.pallas_call(
        _attn_kernel,
        grid=(bsz, nh // hg, nq),
        in_specs=[pl.BlockSpec(memory_space=pltpu.SMEM),
                  pl.BlockSpec((hg, tq, HEAD_DIM), lambda b, g, i: (g, b * nq + i, 0)),
                  pl.BlockSpec((1, hg, nq, HEAD_DIM, tq), lambda b, g, i: (b, g, 0, 0, 0), pipeline_mode=once),
                  pl.BlockSpec((hg, seq, 2 * HEAD_DIM), lambda b, g, i: (g, b, 0), pipeline_mode=once),
                  pl.BlockSpec((1, nq, tq, tq), lambda b, g, i: (b * nq + i, 0, 0, 0))],
        out_specs=pl.BlockSpec((tq, hg * HEAD_DIM), lambda b, g, i: (b * nq + i, g)),
        out_shape=jax.ShapeDtypeStruct((bsz * seq, nh * HEAD_DIM), BF16),
        scratch_shapes=[pltpu.VMEM((hg, tq, 2 * HEAD_DIM), F32),
                        pltpu.VMEM((hg, tq, LANES), F32),
                        pltpu.VMEM((2, nh, tq, tq), F32)],
        compiler_params=_cparams(("arbitrary", "arbitrary", "arbitrary")),
        name="head_attention",
    )(rel_bias, q_hm, k_t, v_hm, mask)


def _memattn_kernel(x_ref, g1_ref, b1_ref, kv_ref, wq_ref, wo_ref, g_ref, b_ref, wr_ref, o_ref, lg_ref):
    x = _ln_rows(x_ref[...], g1_ref[...], b1_ref[...])
    d_mem = MEM_HEADS * MEM_HEAD_DIM
    q = (_dot(x.astype(BF16), wq_ref[...]) * (MEM_HEAD_DIM ** -0.5)).astype(BF16)
    outs = []
    for h in range(MEM_HEADS):
        lo = h * MEM_HEAD_DIM
        k = kv_ref[:, lo:lo + MEM_HEAD_DIM]
        v = kv_ref[:, d_mem + lo:d_mem + lo + MEM_HEAD_DIM]
        s = _dot_nt(q[:, lo:lo + MEM_HEAD_DIM], k)
        p = jnp.exp(s - jnp.max(s, axis=1, keepdims=True))
        p = p / jnp.sum(p, axis=1, keepdims=True)
        outs.append(_dot(p.astype(BF16), v).astype(BF16))
    o = jnp.concatenate(outs, axis=1)
    x2 = _ln_rows(ALPHA * x + _dot(o, wo_ref[...]), g_ref[...], b_ref[...])
    o_ref[...] = x2
    lg_ref[...] = _dot(x2.astype(BF16), wr_ref[...])


def _memory_attention(pre1, g1, b1, kv, w_mq, w_mo, g, b, w_router, bsz, seq, tm):
    t, d = pre1.shape
    nt = seq // tm
    mem_len = kv.shape[0] // bsz
    d_mem = w_mq.shape[1]
    return pl.pallas_call(
        _memattn_kernel,
        grid=(bsz, nt),
        in_specs=[pl.BlockSpec((tm, d), lambda bi, i: (bi * nt + i, 0)),
                  pl.BlockSpec((1, d), lambda bi, i: (0, 0)),
                  pl.BlockSpec((1, d), lambda bi, i: (0, 0)),
                  pl.BlockSpec((mem_len, 2 * d_mem), lambda bi, i: (bi, 0)),
                  pl.BlockSpec((d, d_mem), lambda bi, i: (0, 0), pipeline_mode=pl.Buffered(1)),
                  pl.BlockSpec((d_mem, d), lambda bi, i: (0, 0), pipeline_mode=pl.Buffered(1)),
                  pl.BlockSpec((1, d), lambda bi, i: (0, 0)),
                  pl.BlockSpec((1, d), lambda bi, i: (0, 0)),
                  pl.BlockSpec((d, LANES), lambda bi, i: (0, 0), pipeline_mode=pl.Buffered(1))],
        out_specs=[pl.BlockSpec((tm, d), lambda bi, i: (bi * nt + i, 0)),
                   pl.BlockSpec((tm, LANES), lambda bi, i: (bi * nt + i, 0))],
        out_shape=[jax.ShapeDtypeStruct((t, d), F32),
                   jax.ShapeDtypeStruct((t, LANES), F32)],
        compiler_params=_cparams(("parallel", "parallel")),
        name="memory_attention",
    )(pre1, g1.reshape(1, d), b1.reshape(1, d), kv, w_mq, w_mo, g.reshape(1, d), b.reshape(1, d), w_router)


def _router_kernel(lg_ref, e1_ref, e2_ref, g1_ref, g2_ref):
    x = lg_ref[...]
    lane = lax.broadcasted_iota(I32, x.shape, 1)
    lane_f = lane.astype(F32)

    def argmax(mask):
        v = jnp.where(mask, x, -jnp.inf)
        mx = jnp.max(v, axis=1, keepdims=True)
        idx = jnp.min(jnp.where(mask & (v == mx), lane_f, float(LANES)), axis=1, keepdims=True)
        return mx, idx.astype(I32)

    gmask = lane < N_GROUPS
    gmax, gsel = argmax(gmask)
    gsum = jnp.sum(jnp.where(gmask, jnp.exp(x - gmax), 0.0), axis=1, keepdims=True)
    g_p = 1.0 / gsum
    lo = N_GROUPS + gsel * EXP_PER_GROUP
    emask = (lane >= lo) & (lane < lo + EXP_PER_GROUP)
    m1, i1 = argmax(emask)
    m2, i2 = argmax(emask & (lane != i1))
    esum = jnp.sum(jnp.where(emask, jnp.exp(x - m1), 0.0), axis=1, keepdims=True)
    p1 = 1.0 / esum
    p2 = jnp.exp(m2 - m1) / esum
    e1_ref[...] = i1 - N_GROUPS
    e2_ref[...] = i2 - N_GROUPS
    g1_ref[...] = g_p * (p1 / (p1 + p2))
    g2_ref[...] = g_p * (p2 / (p1 + p2))


def _router(logits, tm):
    t = logits.shape[0]
    col = pl.BlockSpec((tm, 1), lambda i: (i, 0))
    return pl.pallas_call(
        _router_kernel,
        grid=(t // tm,),
        in_specs=[pl.BlockSpec((tm, LANES), lambda i: (i, 0))],
        out_specs=[col, col, col, col],
        out_shape=[jax.ShapeDtypeStruct((t, 1), I32), jax.ShapeDtypeStruct((t, 1), I32),
                   jax.ShapeDtypeStruct((t, 1), F32), jax.ShapeDtypeStruct((t, 1), F32)],
        compiler_params=_cparams(("parallel",)),
        name="moe_router",
    )(logits)


HI16 = -65536


def _pack_halves(x):
    d = x.shape[1] // 2
    lo = pltpu.bitcast(x[:, :d].astype(BF16).astype(F32), I32)
    hi = pltpu.bitcast(x[:, d:].astype(BF16).astype(F32), I32)
    return lax.shift_right_logical(lo, 16) | (hi & HI16)


def _unpack_halves(u):
    return pltpu.bitcast(lax.shift_left(u, 16), F32), pltpu.bitcast(u & HI16, F32)


def _moe_rank_kernel(e1_ref, e2_ref, r1_ref, r2_ref, cnt_ref, base):
    i = pl.program_id(0)
    tm = e1_ref.shape[0]

    @pl.when(i == 0)
    def _():
        base[...] = jnp.zeros(base.shape, F32)

    lane = lax.broadcasted_iota(I32, (tm, LANES), 1)
    rr = lax.broadcasted_iota(I32, (tm, tm), 0)
    cc = lax.broadcasted_iota(I32, (tm, tm), 1)
    earlier = jnp.where(cc < rr, 1.0, 0.0).astype(BF16)
    for slot, (e_ref, r_ref) in enumerate(((e1_ref, r1_ref), (e2_ref, r2_ref))):
        oh = jnp.where(lane == e_ref[...], 1.0, 0.0)
        before = _dot(earlier, oh.astype(BF16)) + base[slot:slot + 1, :]
        r_ref[...] = jnp.sum(oh * before, axis=1, keepdims=True).astype(I32)
        base[slot:slot + 1, :] = base[slot:slot + 1, :] + jnp.sum(oh, axis=0, keepdims=True)
    cnt_ref[...] = base[...]


def _moe_rank(e1, e2, tm):
    t = e1.shape[0]
    col = pl.BlockSpec((tm, 1), lambda i: (i, 0))
    return pl.pallas_call(
        _moe_rank_kernel,
        grid=(t // tm,),
        in_specs=[col, col],
        out_specs=[col, col, pl.BlockSpec((8, LANES), lambda i: (0, 0))],
        out_shape=[jax.ShapeDtypeStruct((t, 1), I32), jax.ShapeDtypeStruct((t, 1), I32),
                   jax.ShapeDtypeStruct((8, LANES), F32)],
        scratch_shapes=[pltpu.VMEM((8, LANES), F32)],
        compiler_params=_cparams(("arbitrary",)),
        name="moe_rank",
    )(e1, e2)


def _moe_place_kernel(e1_ref, e2_ref, r1_ref, r2_ref, cnt_ref, p1_ref, p2_ref, be_ref, nu_ref):
    tm = e1_ref.shape[0]
    nbp = be_ref.shape[0]
    lane8 = lax.broadcasted_iota(I32, (8, LANES), 1)
    cnt = cnt_ref[...].astype(I32)
    c0 = jnp.broadcast_to(cnt[0:1], (8, LANES))
    c1 = jnp.broadcast_to(cnt[1:2], (8, LANES))
    blk_shift = MOE_BLOCK.bit_length() - 1
    padded = lax.shift_left(lax.shift_right_logical(c0 + c1 + (MOE_BLOCK - 1), blk_shift), blk_shift)
    pad_end = padded
    s = 1
    while s < LANES:
        pad_end = pad_end + jnp.where(lane8 >= s, pltpu.roll(pad_end, s, 1), 0)
        s *= 2
    start0 = (pad_end - padded).astype(F32)
    start1 = (pad_end - padded + c0).astype(F32)

    lane = lax.broadcasted_iota(I32, (tm, LANES), 1)
    for e_ref, r_ref, p_ref, start in ((e1_ref, r1_ref, p1_ref, start0), (e2_ref, r2_ref, p2_ref, start1)):
        seg = jnp.sum(jnp.where(lane == e_ref[...], start[0:1], 0.0), axis=1, keepdims=True)
        p_ref[...] = seg.astype(I32) + r_ref[...]

    block_row = lax.broadcasted_iota(I32, (nbp, LANES), 0) * MOE_BLOCK
    lane_b = lax.broadcasted_iota(I32, (nbp, LANES), 1)
    ended = (pad_end[0:1] <= block_row) & (lane_b < N_EXPERTS)
    be = jnp.sum(jnp.where(ended, 1.0, 0.0), axis=1, keepdims=True)
    be_ref[...] = jnp.minimum(be, N_EXPERTS - 1.0).astype(I32)
    total = jnp.max(pad_end, axis=1, keepdims=True)
    sub = lax.broadcasted_iota(I32, (8, LANES), 0)
    n_used = jnp.broadcast_to(lax.shift_right_logical(total, blk_shift), (8, LANES))
    nu_ref[...] = jnp.where(sub == 0, n_used, jnp.where(sub == 1, pad_end, jnp.where(sub == 2, padded, 0)))


def _moe_place(e1, e2, r1, r2, cnt, nb, tm):
    t = e1.shape[0]
    col = pl.BlockSpec((tm, 1), lambda i: (i, 0))
    return pl.pallas_call(
        _moe_place_kernel,
        grid=(t // tm,),
        in_specs=[col, col, col, col, pl.BlockSpec((8, LANES), lambda i: (0, 0))],
        out_specs=[col, col, pl.BlockSpec((nb, 1), lambda i: (0, 0)),
                   pl.BlockSpec((8, LANES), lambda i: (0, 0))],
        out_shape=[jax.ShapeDtypeStruct((t, 1), I32), jax.ShapeDtypeStruct((t, 1), I32),
                   jax.ShapeDtypeStruct((nb, 1), I32), jax.ShapeDtypeStruct((8, LANES), I32)],
        compiler_params=_cparams(("arbitrary",)),
        name="moe_place",
    )(e1, e2, r1, r2, cnt)


def _moe_dispatch_kernel(seg_ref, p1_ref, p2_ref, x_ref, xs_hbm, pk, zbuf, sem, zsem):
    i = pl.program_id(0)
    n = pl.num_programs(0)
    slot = lax.rem(i, 2)
    blk = x_ref.shape[0]
    nb = xs_hbm.shape[0] // blk

    def row_copy(s, r, pos):
        return pltpu.make_async_copy(pk.at[s, pl.ds(r, 1)], xs_hbm.at[pl.ds(pos, 1)], sem.at[s])

    def drain(s):
        for _ in range(2):
            pltpu.make_async_copy(pk.at[s], xs_hbm.at[pl.ds(0, blk)], sem.at[s]).wait()

    def zero_copy(row0):
        return pltpu.make_async_copy(zbuf, xs_hbm.at[pl.ds(pl.multiple_of(row0, blk), blk)], zsem)

    @pl.when(i == 0)
    def _():
        zbuf[...] = jnp.zeros(zbuf.shape, zbuf.dtype)
        n_used = seg_ref[0, 0]

        def seg_start(e, c):
            @pl.when(seg_ref[2, e] > 0)
            def _():
                zero_copy(seg_ref[1, e] - blk).start()
            return c

        def tail_start(b, c):
            zero_copy(b * blk).start()
            return c

        def seg_wait(e, c):
            @pl.when(seg_ref[2, e] > 0)
            def _():
                zero_copy(0).wait()
            return c

        def tail_wait(b, c):
            zero_copy(0).wait()
            return c

        lax.fori_loop(0, N_EXPERTS, seg_start, 0)
        lax.fori_loop(n_used, nb, tail_start, 0)
        lax.fori_loop(0, N_EXPERTS, seg_wait, 0)
        lax.fori_loop(n_used, nb, tail_wait, 0)

    @pl.when(i >= 2)
    def _():
        drain(slot)

    pk[slot] = _pack_halves(x_ref[...])

    def issue(r, c):
        row_copy(slot, r, p1_ref[0, 0, r]).start(priority=0)
        row_copy(slot, r, p2_ref[0, 0, r]).start(priority=1)
        return c

    lax.fori_loop(0, blk, issue, 0, unroll=8)

    @pl.when(i == n - 1)
    def _():
        drain(slot)

    @pl.when((i == n - 1) & (i >= 1))
    def _():
        drain(1 - slot)


def _moe_dispatch(x2, p1, p2, seg, nb):
    t, d = x2.shape
    nt = t // MOE_BLOCK
    rows = nb * MOE_BLOCK
    pos = pl.BlockSpec((1, 1, MOE_BLOCK), lambda i: (i, 0, 0), memory_space=pltpu.SMEM)
    return pl.pallas_call(
        _moe_dispatch_kernel,
        grid=(nt,),
        in_specs=[pl.BlockSpec(memory_space=pltpu.SMEM), pos, pos,
                  pl.BlockSpec((MOE_BLOCK, d), lambda i: (i, 0))],
        out_specs=pl.BlockSpec(memory_space=pl.ANY),
        out_shape=jax.ShapeDtypeStruct((rows, d // 2), I32),
        scratch_shapes=[pltpu.VMEM((2, MOE_BLOCK, d // 2), I32),
                        pltpu.VMEM((MOE_BLOCK, d // 2), I32),
                        pltpu.SemaphoreType.DMA((2,)),
                        pltpu.SemaphoreType.DMA(())],
        compiler_params=_cparams(("arbitrary",)),
        name="moe_dispatch",
    )(seg, p1.reshape(nt, 1, MOE_BLOCK), p2.reshape(nt, 1, MOE_BLOCK), x2)


MAT_PIECES = 4
N_PIECES = 3 * MAT_PIECES
PIECES_PER_BLOCK = 3
PIECE_RING = 4


def _moe_ffn_kernel(be_ref, nu_ref, xs_ref, wg_hbm, wu_hbm, wd_hbm, ys_ref,
                    wgb, wub, wdb, sa, sb, sem, st):
    i = pl.program_id(0)
    nb = be_ref.shape[0]
    n_used = nu_ref[0]
    ra = wgb.shape[1] // MAT_PIECES
    rb = wdb.shape[1] // MAT_PIECES

    def piece_copy(p, e):
        k = p % PIECE_RING
        m, r = divmod(p, MAT_PIECES)
        if m == 0:
            return pltpu.make_async_copy(wg_hbm.at[e, pl.ds(r * ra, ra)], sa.at[k], sem.at[k])
        if m == 1:
            return pltpu.make_async_copy(wu_hbm.at[e, pl.ds(r * ra, ra)], sa.at[k], sem.at[k])
        return pltpu.make_async_copy(wd_hbm.at[e, pl.ds(r * rb, rb)], sb.at[k], sem.at[k])

    def piece_round(p, slot):
        k = p % PIECE_RING
        m, r = divmod(p, MAT_PIECES)
        if m == 0:
            wgb[slot, pl.ds(r * ra, ra), :] = sa[k].astype(BF16)
        elif m == 1:
            wub[slot, pl.ds(r * ra, ra), :] = sa[k].astype(BF16)
        else:
            wdb[slot, pl.ds(r * rb, rb), :] = sb[k].astype(BF16)

    def start_one():
        e, started, finished = st[1], st[2], st[3]
        can = (started < N_PIECES) & (started - finished < PIECE_RING)
        for p in range(N_PIECES):
            @pl.when(can & (started == p))
            def _():
                piece_copy(p, e).start(priority=p % 2)
        st[2] = started + can.astype(I32)

    def finish_one(slot):
        e, finished = st[1], st[3]

        @pl.when((e >= 0) & (finished < N_PIECES))
        def _():
            for p in range(N_PIECES):
                @pl.when(finished == p)
                def _():
                    piece_copy(p, e).wait()
                    piece_round(p, slot)
            st[3] = finished + 1
            start_one()

    def prepare(e):
        st[1] = e
        st[2] = 0
        st[3] = 0

        @pl.when(e >= 0)
        def _():
            for _ in range(PIECE_RING):
                start_one()

    @pl.when(i < n_used)
    def _():
        e = be_ref[i]

        @pl.when(i == 0)
        def _():
            st[0] = 1
            prepare(e)

        @pl.when((i == 0) | (e != be_ref[jnp.maximum(i - 1, 0)]))
        def _():
            slot = 1 - st[0]

            def fin(_, c):
                finish_one(slot)
                return c

            lax.fori_loop(0, N_PIECES, fin, 0)
            st[0] = slot
            k = lax.while_loop(lambda k: (k < n_used) & (be_ref[jnp.minimum(k, nb - 1)] == e),
                               lambda k: k + 1, i + 1)
            prepare(jnp.where(k < n_used, be_ref[jnp.minimum(k, nb - 1)], -1))

        def ahead(_, c):
            finish_one(1 - st[0])
            return c

        lax.fori_loop(0, PIECES_PER_BLOCK, ahead, 0)

        slot = st[0]
        lo, hi = _unpack_halves(xs_ref[...])
        lo = lo.astype(BF16)
        hi = hi.astype(BF16)
        d2 = lo.shape[1]
        g = _dot(lo, wgb[slot, :d2]) + _dot(hi, wgb[slot, d2:])
        u = _dot(lo, wub[slot, :d2]) + _dot(hi, wub[slot, d2:])
        hmid = (g * jax.nn.sigmoid(g) * u).astype(BF16)
        ys_ref[...] = _pack_halves(_dot(hmid, wdb[slot]))

    @pl.when(i >= n_used)
    def _():
        ys_ref[...] = jnp.zeros(ys_ref.shape, ys_ref.dtype)


def _moe_ffn(xs, w_gate, w_up, w_down, block_expert, n_used):
    rows, d2 = xs.shape
    nb = rows // MOE_BLOCK
    d = 2 * d2
    ff = w_gate.shape[2]
    grid_spec = pltpu.PrefetchScalarGridSpec(
        num_scalar_prefetch=2,
        grid=(nb,),
        in_specs=[pl.BlockSpec((MOE_BLOCK, d2), lambda i, be, nu: (i, 0)),
                  pl.BlockSpec(memory_space=pl.ANY),
                  pl.BlockSpec(memory_space=pl.ANY),
                  pl.BlockSpec(memory_space=pl.ANY)],
        out_specs=pl.BlockSpec((MOE_BLOCK, d2), lambda i, be, nu: (i, 0)),
        scratch_shapes=[pltpu.VMEM((2, d, ff), BF16), pltpu.VMEM((2, d, ff), BF16),
                        pltpu.VMEM((2, ff, d), BF16),
                        pltpu.VMEM((PIECE_RING, d // MAT_PIECES, ff), F32),
                        pltpu.VMEM((PIECE_RING, ff // MAT_PIECES, d), F32),
                        pltpu.SemaphoreType.DMA((PIECE_RING,)),
                        pltpu.SMEM((4,), I32)],
    )
    return pl.pallas_call(
        _moe_ffn_kernel,
        grid_spec=grid_spec,
        out_shape=jax.ShapeDtypeStruct((rows, d2), I32),
        compiler_params=_cparams(("arbitrary",)),
        name="moe_ffn",
    )(block_expert, n_used, xs, w_gate, w_up, w_down)


def _moe_combine_kernel(p1_ref, p2_ref, q1_ref, q2_ref, x_ref, g1_ref, g2_ref, g_ref, b_ref, ys_hbm,
                        o_ref, yb, sem):
    i = pl.program_id(0)
    n = pl.num_programs(0)
    slot = lax.rem(i, 2)
    blk = x_ref.shape[0]

    def row_copy(s, k, r, pos):
        return pltpu.make_async_copy(ys_hbm.at[pl.ds(pos, 1)], yb.at[s, k, pl.ds(r, 1)], sem.at[s])

    def fetch(s, a_ref, b_ref2):
        def body(r, c):
            row_copy(s, 0, r, a_ref[0, 0, r]).start(priority=0)
            row_copy(s, 1, r, b_ref2[0, 0, r]).start(priority=1)
            return c
        lax.fori_loop(0, blk, body, 0, unroll=8)

    @pl.when(i == 0)
    def _():
        fetch(0, p1_ref, p2_ref)

    @pl.when(i + 1 < n)
    def _():
        fetch(1 - slot, q1_ref, q2_ref)

    for k in range(2):
        pltpu.make_async_copy(ys_hbm.at[pl.ds(0, blk)], yb.at[slot, k], sem.at[slot]).wait()
    y1 = jnp.concatenate(_unpack_halves(yb[slot, 0]), axis=1)
    y2 = jnp.concatenate(_unpack_halves(yb[slot, 1]), axis=1)
    y = y1 * g1_ref[...] + y2 * g2_ref[...]
    o_ref[...] = _ln_rows(ALPHA * x_ref[...] + y, g_ref[...], b_ref[...])


def _moe_combine(x2, ys, p1, p2, g1, g2, g, b):
    t, d = x2.shape
    nt = t // MOE_BLOCK
    p1 = p1.reshape(nt, 1, MOE_BLOCK)
    p2 = p2.reshape(nt, 1, MOE_BLOCK)
    pos = pl.BlockSpec((1, 1, MOE_BLOCK), lambda i: (i, 0, 0), memory_space=pltpu.SMEM)
    nxt = pl.BlockSpec((1, 1, MOE_BLOCK), lambda i: (jnp.minimum(i + 1, nt - 1), 0, 0),
                       memory_space=pltpu.SMEM)
    col = pl.BlockSpec((MOE_BLOCK, 1), lambda i: (i, 0))
    vec = pl.BlockSpec((1, d), lambda i: (0, 0))
    return pl.pallas_call(
        _moe_combine_kernel,
        grid=(nt,),
        in_specs=[pos, pos, nxt, nxt, pl.BlockSpec((MOE_BLOCK, d), lambda i: (i, 0)), col, col, vec, vec,
                  pl.BlockSpec(memory_space=pl.ANY)],
        out_specs=pl.BlockSpec((MOE_BLOCK, d), lambda i: (i, 0)),
        out_shape=jax.ShapeDtypeStruct((t, d), F32),
        scratch_shapes=[pltpu.VMEM((2, 2, MOE_BLOCK, d // 2), I32),
                        pltpu.SemaphoreType.DMA((2,))],
        compiler_params=_cparams(("arbitrary",)),
        name="moe_combine",
    )(p1, p2, p1, p2, x2, g1, g2, g.reshape(1, d), b.reshape(1, d), ys)


def _tile(n, pref):
    return pref if n % pref == 0 else n


def _layer(x, mem, w_in, conv_w, conv_b, conv_ln_g, conv_ln_b, kv_norm_g, w_uk, w_uv, rel_bias,
           conv_out_g, attn_out_g, w_out, ln1_g, ln1_b, w_mq, w_mk, w_mv, w_mo, ln2_g, ln2_b,
           w_router_grp, w_router_exp, w_gate, w_up, w_down, ln3_g, ln3_b):
    bsz, seq, d = x.shape
    t = bsz * seq
    d_conv = conv_w.shape[1]
    d_attn = N_HEADS * HEAD_DIM
    c_glu = 2 * d_conv
    c_qi = H_IDX * D_IDX
    o_q, o_kv = c_glu, c_glu + d_attn
    o_qi = o_kv + KV_RANK
    o_ki = o_qi + c_qi
    topk = min(TOPK_MAX, seq // 4)

    xf = x.reshape(t, d)
    w_inb = w_in.astype(BF16)
    tail_w = jnp.concatenate([w_inb[:, o_kv:o_qi], w_inb[:, o_ki:]], axis=1)
    tail_w = jnp.pad(tail_w, ((0, 0), (0, KV_RANK + LANES - tail_w.shape[1])))
    tm = _tile(t, 1024)

    tail, xb = _matmul_cast(xf, tail_w, F32, _tile(t, 512), "proj_tail")
    u = _matmul_cols(xb, w_inb, c_glu, BF16, tm, 512, "proj_glu")
    q_hm = _matmul_heads(xb, w_inb, o_q, d_attn, BF16, tm, 512, "proj_q")
    qi_hm = _matmul_heads(xb, w_inb, o_qi, c_qi, BF16, tm, 512, "proj_qidx")

    conv_n = _conformer_conv(u, bsz, seq, conv_w, conv_b, conv_ln_g, conv_ln_b, conv_out_g)
    tk = min(ATT_BLOCK, seq)
    ckv_n, ckv_t, kia, kib, kw = _prep_latent(tail, kv_norm_g, tm, tk)
    ckv_t = ckv_t.reshape(bsz, seq // tk, KV_RANK, tk)
    mask = _indexer_mask(qi_hm, kw, kia, kib, bsz, seq, topk)
    w_uk_t = w_uk.transpose(0, 2, 1).reshape(d_attn, KV_RANK).astype(BF16)
    w_uv_all = w_uv.transpose(1, 0, 2).reshape(KV_RANK, d_attn).astype(BF16)
    k_t = _key_heads(w_uk_t, ckv_t)
    v_hm = _matmul_value_heads(ckv_n, w_uv_all, tm, 512, "value_heads")
    attn = _head_attention(q_hm, k_t, v_hm, mask, rel_bias, bsz, seq)

    pre1 = _matmul2_residual(conv_n, attn, attn_out_g, w_out.astype(BF16), xf, tm, 512, "out_proj")

    mem_len = mem.shape[1]
    memb = mem.reshape(bsz * mem_len, d).astype(BF16)
    w_kv = jnp.concatenate([w_mk, w_mv], axis=1).astype(BF16)
    kv = _matmul(memb, w_kv, BF16, _tile(bsz * mem_len, 512), 512, "mem_kv")
    w_router = jnp.concatenate([w_router_grp, w_router_exp], axis=1)
    w_router = jnp.pad(w_router, ((0, 0), (0, LANES - w_router.shape[1]))).astype(BF16)
    x2, logits = _memory_attention(pre1, ln1_g, ln1_b, kv, w_mq.astype(BF16), w_mo.astype(BF16),
                                   ln2_g, ln2_b, w_router, bsz, seq, 256)

    e1, e2, g1, g2 = _router(logits, _tile(t, 1024))
    nb = (2 * t + N_EXPERTS * (MOE_BLOCK - 1) + MOE_BLOCK - 1) // MOE_BLOCK
    r1, r2, cnt = _moe_rank(e1, e2, 512)
    p1, p2, block_expert, n_used = _moe_place(e1, e2, r1, r2, cnt, nb, 512)
    xs = _moe_dispatch(x2, p1, p2, n_used, nb)
    ys = _moe_ffn(xs, w_gate, w_up, w_down, block_expert.reshape(nb), n_used[0, 0:1])
    x3 = _moe_combine(x2, ys, p1, p2, g1, g2, ln3_g, ln3_b)
    return x3.reshape(bsz, seq, d)


def kernel(x, mem, w_in, conv_w, conv_b, conv_ln_g, conv_ln_b, kv_norm_g, w_uk, w_uv, rel_bias, conv_out_g, attn_out_g, w_out, ln1_g, ln1_b, w_mq, w_mk, w_mv, w_mo, ln2_g, ln2_b, w_router_grp, w_router_exp, w_gate, w_up, w_down, ln3_g, ln3_b):
    for l in range(w_in.shape[0]):
        x = _layer(x, mem, w_in[l], conv_w[l], conv_b[l], conv_ln_g[l], conv_ln_b[l], kv_norm_g[l],
                   w_uk[l], w_uv[l], rel_bias, conv_out_g[l], attn_out_g[l], w_out[l], ln1_g[l], ln1_b[l],
                   w_mq[l], w_mk[l], w_mv[l], w_mo[l], ln2_g[l], ln2_b[l], w_router_grp[l],
                   w_router_exp[l], w_gate[l], w_up[l], w_down[l], ln3_g[l], ln3_b[l])
    return x
```
